```python
import math
import jax
import jax.numpy as jnp
from jax import lax
import numpy as np

D_MODEL = 1024
BATCH = 16
SEQ = 2048
DEPTH = 2

CTX_LEN = 256
GRID_W = 64
EPS = 1e-6

HEAD_DIM = 64
N_Q_HEADS = 8
N_KV_HEADS = 2
Q_PER_KV = N_Q_HEADS // N_KV_HEADS
AXIS_DIM = HEAD_DIM // 2
ROPE_THETA = 10000.0
Q_BLOCK = 128
ATTN_WIDTH = N_Q_HEADS * HEAD_DIM
KV_WIDTH = N_KV_HEADS * HEAD_DIM

GMLP_GROUPS = 8
GMLP_GROUP_DIM = 64
GMLP_WIDTH = GMLP_GROUPS * GMLP_GROUP_DIM
GMLP_CHUNK = 128

MIX_IN_WIDTH = ATTN_WIDTH + 2 * KV_WIDTH + 2 * GMLP_WIDTH
MIX_SPLITS = (ATTN_WIDTH, ATTN_WIDTH + KV_WIDTH, ATTN_WIDTH + 2 * KV_WIDTH,
              ATTN_WIDTH + 2 * KV_WIDTH + GMLP_WIDTH)
MIX_OUT_WIDTH = ATTN_WIDTH + GMLP_WIDTH

S5_GROUPS = 32
S5_GROUP_DIM = 16
S5_STATE = 64
S5_WIDTH = S5_GROUPS * S5_GROUP_DIM

N_EXPERT_GROUPS = 4
EXPERTS_PER_GROUP = 8
N_EXPERTS = N_EXPERT_GROUPS * EXPERTS_PER_GROUP
TOP_K_INNER = 2
EXPERT_HIDDEN = 512
MOE_BLOCK = 256

kernel_name = 'hybrid_flow_gqa_gmlp_s5_hmoe'


def rms_norm(x, g):
    x32 = x.astype(jnp.float32)
    y = x32 * lax.rsqrt(jnp.mean(x32 * x32, axis=-1, keepdims=True) + EPS)
    return (y * g.astype(jnp.float32)).astype(x.dtype)


def layer_norm(x, g):
    x32 = x.astype(jnp.float32)
    mu = jnp.mean(x32, axis=-1, keepdims=True)
    xc = x32 - mu
    y = xc * lax.rsqrt(jnp.mean(xc * xc, axis=-1, keepdims=True) + EPS)
    return (y * g.astype(jnp.float32)).astype(x.dtype)


def modulate(x, g, shift, scale):
    return rms_norm(x, g) * (1 + scale) + shift


def adaln(cond, w_mod, b_mod):
    m = jax.nn.silu(cond) @ w_mod + b_mod
    return jnp.split(m, 6, axis=-1)


def axial_rope_tables(n_tokens):
    rows = n_tokens // GRID_W
    row = jnp.repeat(jnp.arange(rows, dtype=jnp.int32), GRID_W).astype(jnp.float32)
    col = jnp.tile(jnp.arange(GRID_W, dtype=jnp.int32), rows).astype(jnp.float32)
    inv_freq = jnp.power(ROPE_THETA, -jnp.arange(0, AXIS_DIM, 2, dtype=jnp.float32) / AXIS_DIM)
    ang_r = row[:, None] * inv_freq[None, :]
    ang_c = col[:, None] * inv_freq[None, :]
    return (jnp.cos(ang_r), jnp.sin(ang_r), jnp.cos(ang_c), jnp.sin(ang_c))


def _rotate_half(x, cos, sin):
    x1, x2 = jnp.split(x, 2, axis=-1)
    return jnp.concatenate([x1 * cos - x2 * sin, x2 * cos + x1 * sin], axis=-1)


def apply_axial_rope(x, tables):
    shape = (x.shape[1],) + (1,) * (x.ndim - 3) + (AXIS_DIM // 2,)
    cr, sr, cc, sc = [t.reshape(shape) for t in tables]
    x32 = x.astype(jnp.float32)
    xr, xcl = jnp.split(x32, 2, axis=-1)
    out = jnp.concatenate([_rotate_half(xr, cr, sr), _rotate_half(xcl, cc, sc)], axis=-1)
    return out.astype(x.dtype)


def gqa_attend(q, k, v):
    s = jnp.einsum('bqhgd,bkhd->bhgqk', q, k).astype(jnp.float32) * (HEAD_DIM ** -0.5)
    p = jax.nn.softmax(s, axis=-1).astype(v.dtype)
    return jnp.einsum('bhgqk,bkhd->bqhgd', p, v)


def attn_gmlp_mixer(h_lat, h_ctx, rope, w_in, w_out, q_g, k_g, ln_g, w_sp, b_sp, with_ctx):
    B, S, _ = h_lat.shape
    C = h_ctx.shape[1]

    def project(h):
        L = h.shape[1]
        q, k, v, u, vg = jnp.split(h @ w_in, MIX_SPLITS, axis=-1)
        q = rms_norm(q.reshape(B, L, N_KV_HEADS, Q_PER_KV, HEAD_DIM), q_g)
        k = rms_norm(k.reshape(B, L, N_KV_HEADS, HEAD_DIM), k_g)
        v = v.reshape(B, L, N_KV_HEADS, HEAD_DIM)
        return q, k, v, u, vg

    def spatial_gate(u, vg):
        L = u.shape[1]
        u = jax.nn.gelu(u)
        vg = layer_norm(jax.nn.gelu(vg), ln_g)
        vc = vg.reshape(B, L // GMLP_CHUNK, GMLP_CHUNK, GMLP_GROUPS, GMLP_GROUP_DIM)
        mixed = jnp.einsum('gpq,bcqgd->bcpgd', w_sp, vc) + b_sp.T[:, :, None]
        return u * mixed.reshape(B, L, GMLP_WIDTH)

    q_l, k_l, v_l, u_l, g_l = project(h_lat)
    q_c, k_c, v_c, u_c, g_c = project(h_ctx)
    q_l = apply_axial_rope(q_l, rope)
    k_l = apply_axial_rope(k_l, rope)
    k_all = jnp.concatenate([k_c, k_l], axis=1)
    v_all = jnp.concatenate([v_c, v_l], axis=1)
    n_blk = S // Q_BLOCK
    q_blocks = q_l.reshape(B, n_blk, Q_BLOCK, N_KV_HEADS, Q_PER_KV, HEAD_DIM).swapaxes(0, 1)
    o_l = lax.map(lambda qb: gqa_attend(qb, k_all, v_all), q_blocks)
    o_l = o_l.swapaxes(0, 1).reshape(B, S, ATTN_WIDTH)
    y_lat = jnp.concatenate([o_l, spatial_gate(u_l, g_l)], axis=-1) @ w_out
    y_ctx = None
    if with_ctx:
        o_c = gqa_attend(q_c, k_c, v_c).reshape(B, C, ATTN_WIDTH)
        y_ctx = jnp.concatenate([o_c, spatial_gate(u_c, g_c)], axis=-1) @ w_out
    return y_lat, y_ctx


def s5_zoh(a_re, a_im, log_dt, b_re, b_im):
    lam = lax.complex(a_re.astype(jnp.float32), a_im.astype(jnp.float32))
    dt = jnp.exp(log_dt.astype(jnp.float32))[:, None]
    lam_bar = jnp.exp(lam * dt)
    b = lax.complex(b_re.astype(jnp.float32), b_im.astype(jnp.float32))
    b_bar = ((lam_bar - 1.0) / lam)[..., None] * b
    return lam_bar, b_bar


def s5_drive(b_bar, u):
    return jnp.einsum('gnp,blgp->blgn', b_bar, u.astype(jnp.float32).astype(jnp.complex64))


def s5_readout(c_mat, h):
    return jnp.real(jnp.einsum('gpn,blgn->blgp', c_mat, h))


def diag_scan(lam_bar, bu, reverse):
    a = jnp.broadcast_to(lam_bar, bu.shape)

    def combine(e1, e2):
        a1, b1 = e1
        a2, b2 = e2
        return a1 * a2, a2 * b1 + b2

    _, h = lax.associative_scan(combine, (a, bu), reverse=reverse, axis=1)
    return h


def s5_mixer(h_lat, h_ctx, w_in, a_re, a_im, log_dt, b_re, b_im, c_re, c_im, d_skip,
             w_glu, b_glu, w_out, with_ctx):
    B, S, _ = h_lat.shape
    C = h_ctx.shape[1]
    u_l = (h_lat @ w_in).reshape(B, S, S5_GROUPS, S5_GROUP_DIM)
    u_c = (h_ctx @ w_in).reshape(B, C, S5_GROUPS, S5_GROUP_DIM)
    d = d_skip.astype(jnp.float32).reshape(S5_GROUPS, S5_GROUP_DIM)
    y_l = d * u_l.astype(jnp.float32)
    y_c = d * u_c.astype(jnp.float32) if with_ctx else None
    for direction in range(2):
        reverse = direction == 1
        lam_bar, b_bar = s5_zoh(a_re[direction], a_im[direction], log_dt[direction],
                                b_re[direction], b_im[direction])
        c_mat = lax.complex(c_re[direction].astype(jnp.float32), c_im[direction].astype(jnp.float32))
        h_c = diag_scan(lam_bar, s5_drive(b_bar, u_c), reverse)
        h0 = h_c[:, 0] if reverse else h_c[:, -1]
        edge = -1 if reverse else 0
        bu_l = s5_drive(b_bar, u_l).at[:, edge].add(lam_bar * h0)
        h_l = diag_scan(lam_bar, bu_l, reverse)
        y_l = y_l + s5_readout(c_mat, h_l)
        if with_ctx:
            y_c = y_c + s5_readout(c_mat, h_c)

    def gated_out(y):
        y = y.reshape(y.shape[0], y.shape[1], S5_WIDTH).astype(h_lat.dtype)
        g = jax.nn.gelu(y)
        return (g * jax.nn.sigmoid(g @ w_glu + b_glu)) @ w_out

    y_ctx = gated_out(y_c) if with_ctx else None
    return gated_out(y_l), y_ctx


def hier_moe(h, w_grp, b_grp, w_rt, b_rt, w1, w3, w2):
    T, D = h.shape
    grp_prob = jax.nn.softmax((h @ w_grp + b_grp).astype(jnp.float32), axis=-1)
    g_w, g_idx = lax.top_k(grp_prob, 1)
    exp_logits = (h @ w_rt + b_rt).astype(jnp.float32).reshape(T, N_EXPERT_GROUPS, EXPERTS_PER_GROUP)
    sel = jnp.take_along_axis(exp_logits, g_idx[:, :, None], axis=1)[:, 0]
    top_v, top_i = lax.top_k(sel, TOP_K_INNER)
    wts = jax.nn.softmax(top_v, axis=-1) * g_w
    eid = (g_idx * EXPERTS_PER_GROUP + top_i).reshape(-1).astype(jnp.int32)
    wt = wts.reshape(-1)
    tok = jnp.repeat(jnp.arange(T, dtype=jnp.int32), TOP_K_INNER)
    n_assign = T * TOP_K_INNER
    order = jnp.argsort(eid)
    eid_s, tok_s, wt_s = eid[order], tok[order], wt[order]
    counts = jnp.bincount(eid, length=N_EXPERTS).astype(jnp.int32)
    start = jnp.cumsum(counts) - counts
    padded = (counts + MOE_BLOCK - 1) // MOE_BLOCK * MOE_BLOCK
    pstart = jnp.cumsum(padded) - padded
    pend = pstart + padded
    dest = pstart[eid_s] + jnp.arange(n_assign, dtype=jnp.int32) - start[eid_s]
    n_slots = (n_assign + MOE_BLOCK - 1) // MOE_BLOCK * MOE_BLOCK + N_EXPERTS * MOE_BLOCK
    n_blocks = n_slots // MOE_BLOCK
    buf_tok = jnp.zeros((n_slots,), jnp.int32).at[dest].set(tok_s)
    buf_w = jnp.zeros((n_slots,), jnp.float32).at[dest].set(wt_s)
    block_start = jnp.arange(n_blocks, dtype=jnp.int32) * MOE_BLOCK
    block_e = jnp.minimum(jnp.sum(block_start[:, None] >= pend[None, :], axis=1), N_EXPERTS - 1)
    xb = h[buf_tok].reshape(n_blocks, MOE_BLOCK, D)

    def expert_block(args):
        xblk, e = args
        return (jax.nn.silu(xblk @ w1[e]) * (xblk @ w3[e])) @ w2[e]

    yb = lax.map(expert_block, (xb, block_e)).reshape(n_slots, D)
    return jnp.zeros((T, D), h.dtype).at[buf_tok].add(yb * buf_w[:, None].astype(h.dtype))


def setup_inputs(seed: int = 0) -> dict:
    key = jax.random.key(seed)
    ks = iter(jax.random.split(key, 48))
    n_even = (DEPTH + 1) // 2
    n_odd = DEPTH // 2
    D = D_MODEL

    def nrm(shape, scale):
        return jax.random.normal(next(ks), shape, jnp.float32) * scale

    def gain(shape):
        return 1.0 + nrm(shape, 0.02)

    inp = {}
    inp['x'] = nrm((BATCH, SEQ, D), 1.0)
    inp['c'] = nrm((BATCH, D), 1.0)
    inp['ctx'] = nrm((BATCH, CTX_LEN, D), 1.0)
    inp['c_ctx'] = nrm((D,), 1.0)
    inp['w_mod'] = nrm((DEPTH, D, 6 * D), 0.5 * D ** -0.5)
    inp['b_mod'] = nrm((DEPTH, 6 * D), 0.02)
    inp['norm_mix_g'] = gain((DEPTH, D))
    inp['norm_ffn_g'] = gain((DEPTH, D))
    inp['mix_w_in'] = nrm((n_even, D, MIX_IN_WIDTH), D ** -0.5)
    inp['mix_w_out'] = nrm((n_even, MIX_OUT_WIDTH, D), MIX_OUT_WIDTH ** -0.5)
    inp['q_norm_g'] = gain((n_even, HEAD_DIM))
    inp['k_norm_g'] = gain((n_even, HEAD_DIM))
    inp['gmlp_norm_g'] = gain((n_even, GMLP_WIDTH))
    inp['gmlp_w_spatial'] = nrm((n_even, GMLP_GROUPS, GMLP_CHUNK, GMLP_CHUNK), GMLP_CHUNK ** -0.5)
    inp['gmlp_b_spatial'] = 1.0 + nrm((n_even, GMLP_GROUPS, GMLP_CHUNK), 0.02)
    inp['s5_w_in'] = nrm((n_odd, D, S5_WIDTH), D ** -0.5)
    inp['s5_a_re'] = -0.5 + nrm((n_odd, 2, S5_GROUPS, S5_STATE), 0.01)
    inp['s5_a_im'] = jnp.pi * jnp.arange(S5_STATE, dtype=jnp.float32) + nrm((n_odd, 2, S5_GROUPS, S5_STATE), 0.01)
    inp['s5_log_dt'] = jax.random.uniform(next(ks), (n_odd, 2, S5_GROUPS), jnp.float32,
                                          math.log(1e-3), math.log(1e-1))
    inp['s5_b_re'] = nrm((n_odd, 2, S5_GROUPS, S5_STATE, S5_GROUP_DIM), (2 * S5_GROUP_DIM) ** -0.5)
    inp['s5_b_im'] = nrm((n_odd, 2, S5_GROUPS, S5_STATE, S5_GROUP_DIM), (2 * S5_GROUP_DIM) ** -0.5)
    inp['s5_c_re'] = nrm((n_odd, 2, S5_GROUPS, S5_GROUP_DIM, S5_STATE), S5_STATE ** -0.5)
    inp['s5_c_im'] = nrm((n_odd, 2, S5_GROUPS, S5_GROUP_DIM, S5_STATE), S5_STATE ** -0.5)
    inp['s5_d'] = nrm((n_odd, S5_WIDTH), 1.0)
    inp['s5_w_glu'] = nrm((n_odd, S5_WIDTH, S5_WIDTH), S5_WIDTH ** -0.5)
    inp['s5_b_glu'] = nrm((n_odd, S5_WIDTH), 0.02)
    inp['s5_w_out'] = nrm((n_odd, S5_WIDTH, D), S5_WIDTH ** -0.5)
    inp['moe_w_group'] = nrm((DEPTH, D, N_EXPERT_GROUPS), D ** -0.5)
    inp['moe_b_group'] = nrm((DEPTH, N_EXPERT_GROUPS), 0.01)
    inp['moe_w_router'] = nrm((DEPTH, D, N_EXPERTS), D ** -0.5)
    inp['moe_b_router'] = nrm((DEPTH, N_EXPERTS), 0.01)
    inp['moe_w1'] = nrm((DEPTH, N_EXPERTS, D, EXPERT_HIDDEN), D ** -0.5)
    inp['moe_w3'] = nrm((DEPTH, N_EXPERTS, D, EXPERT_HIDDEN), D ** -0.5)
    inp['moe_w2'] = nrm((DEPTH, N_EXPERTS, EXPERT_HIDDEN, D), EXPERT_HIDDEN ** -0.5)
    inp['final_norm_g'] = gain((D,))
    return inp


def reference(x, c, ctx, c_ctx, w_mod, b_mod, norm_mix_g, norm_ffn_g, mix_w_in, mix_w_out,
              q_norm_g, k_norm_g, gmlp_norm_g, gmlp_w_spatial, gmlp_b_spatial,
              s5_w_in, s5_a_re, s5_a_im, s5_log_dt, s5_b_re, s5_b_im, s5_c_re, s5_c_im, s5_d,
              s5_w_glu, s5_b_glu, s5_w_out,
              moe_w_group, moe_b_group, moe_w_router, moe_b_router, moe_w1, moe_w3, moe_w2,
              final_norm_g):
    B, S, D = x.shape
    C = ctx.shape[1]
    rope = axial_rope_tables(S)
    for layer in range(DEPTH):
        last = layer == DEPTH - 1
        i = layer // 2
        m_lat = [m[:, None, :] for m in adaln(c, w_mod[layer], b_mod[layer])]
        m_ctx = adaln(c_ctx, w_mod[layer], b_mod[layer])
        h_lat = modulate(x, norm_mix_g[layer], m_lat[0], m_lat[1])
        h_ctx = modulate(ctx, norm_mix_g[layer], m_ctx[0], m_ctx[1])
        if layer % 2 == 0:
            y_lat, y_ctx = attn_gmlp_mixer(h_lat, h_ctx, rope, mix_w_in[i], mix_w_out[i],
                                           q_norm_g[i], k_norm_g[i], gmlp_norm_g[i],
                                           gmlp_w_spatial[i], gmlp_b_spatial[i], not last)
        else:
            y_lat, y_ctx = s5_mixer(h_lat, h_ctx, s5_w_in[i], s5_a_re[i], s5_a_im[i], s5_log_dt[i],
                                    s5_b_re[i], s5_b_im[i], s5_c_re[i], s5_c_im[i], s5_d[i],
                                    s5_w_glu[i], s5_b_glu[i], s5_w_out[i], not last)
        x = x + m_lat[2] * y_lat
        f_lat = modulate(x, norm_ffn_g[layer], m_lat[3], m_lat[4]).reshape(B * S, D)
        moe_args = (moe_w_group[layer], moe_b_group[layer], moe_w_router[layer], moe_b_router[layer],
                    moe_w1[layer], moe_w3[layer], moe_w2[layer])
        if last:
            x = x + m_lat[5] * hier_moe(f_lat, *moe_args).reshape(B, S, D)
        else:
            ctx = ctx + m_ctx[2] * y_ctx
            f_ctx = modulate(ctx, norm_ffn_g[layer], m_ctx[3], m_ctx[4]).reshape(B * C, D)
            f = hier_moe(jnp.concatenate([f_lat, f_ctx], axis=0), *moe_args)
            x = x + m_lat[5] * f[:B * S].reshape(B, S, D)
            ctx = ctx + m_ctx[5] * f[B * S:].reshape(B, C, D)
    return rms_norm(x, final_norm_g)
```

```python
import functools

import numpy as np
import jax
import jax.numpy as jnp
from jax import lax
from jax.experimental import pallas as pl
from jax.experimental.pallas import tpu as pltpu

F32 = jnp.float32
BF16 = jnp.bfloat16
I32 = jnp.int32

EPS = 1e-6
GRID_W = 64
HEAD_DIM = 64
N_Q_HEADS = 8
N_KV_HEADS = 2
ATTN_WIDTH = N_Q_HEADS * HEAD_DIM
KV_WIDTH = N_KV_HEADS * HEAD_DIM
AXIS_DIM = HEAD_DIM // 2
ROPE_THETA = 10000.0
GMLP_GROUPS = 8
GMLP_GROUP_DIM = 64
GMLP_WIDTH = GMLP_GROUPS * GMLP_GROUP_DIM
GMLP_CHUNK = 128
MIX_IN_WIDTH = ATTN_WIDTH + 2 * KV_WIDTH + 2 * GMLP_WIDTH
S5_GROUPS = 32
S5_GROUP_DIM = 16
S5_STATE = 64
S5_WIDTH = S5_GROUPS * S5_GROUP_DIM
N_EXPERT_GROUPS = 4
EXPERTS_PER_GROUP = 8
N_EXPERTS = N_EXPERT_GROUPS * EXPERTS_PER_GROUP
EXPERT_HIDDEN = 512
MOE_BLOCK = 256

LANES = 128
TOKEN_TILE = 256
S5_CHUNK = 64
S5_TILES = 4
VMEM_LIMIT = 48 * 1024 * 1024
NEG_BIG = -1e30

_Q_HEAD_ORDER = (0, 4, 1, 5, 2, 6, 3, 7)
_Q_COL_PERM = np.concatenate([np.arange(HEAD_DIM) + HEAD_DIM * h for h in _Q_HEAD_ORDER])


def _params(*sem):
    return pltpu.CompilerParams(dimension_semantics=sem, vmem_limit_bytes=VMEM_LIMIT)


def _rms(x):
    return x * lax.rsqrt(jnp.mean(x * x, axis=-1, keepdims=True) + EPS)


def _modulate(x, g, shift, scale):
    return _rms(x) * g * (1.0 + scale) + shift


def _full(shape):
    return pl.BlockSpec(shape, lambda *_: (0,) * len(shape))


def _store_row_tiles(ref, val):
    for s in range(ref.shape[1]):
        ref[:, s, :] = val[:, LANES * s:LANES * (s + 1)]


def _load_row_tiles(ref):
    return jnp.concatenate([ref[:, s, :] for s in range(ref.shape[1])], axis=1)


def _adaln_kernel(cond_ref, w_ref, b_ref, o_ref):
    c = cond_ref[...]
    s = c * jax.nn.sigmoid(c)
    o_ref[0] = jnp.dot(s, w_ref[0], preferred_element_type=F32,
                       precision=lax.Precision.HIGHEST) + b_ref[0]


def _adaln(cond, w_mod, b_mod):
    depth, d, n = w_mod.shape
    r = cond.shape[0]
    tn = 1536
    return pl.pallas_call(
        _adaln_kernel,
        out_shape=jax.ShapeDtypeStruct((depth, r, n), F32),
        grid=(depth, n // tn),
        in_specs=[pl.BlockSpec((r, d), lambda l, j: (0, 0)),
                  pl.BlockSpec((1, d, tn), lambda l, j: (l, 0, j)),
                  pl.BlockSpec((1, 1, tn), lambda l, j: (l, 0, j))],
        out_specs=pl.BlockSpec((1, r, tn), lambda l, j: (l, 0, j)),
        compiler_params=_params("parallel", "parallel"),
        name="adaln",
    )(cond, w_mod, b_mod.reshape(depth, 1, n))


def _l0_proj_kernel(x_ref, mod_ref, g_ref, win_ref, qg_ref, kg_ref, lng_ref, cos_ref, sa_ref, sb_ref,
                    gq_ref, gk_ref, q_ref, k_ref, v_ref, gu_ref, gv_ref):
    m = mod_ref[0, 0]
    h = _modulate(x_ref[0], g_ref[...], m[0:1], m[1:2]).astype(BF16)
    z = jnp.dot(h, win_ref[...], preferred_element_type=F32)
    cos, sa, sb = cos_ref[...], sa_ref[...], sb_ref[...]

    def head_norm_rope(t, gsum_ref, gain, reps):
        w = t.shape[1]
        ss = jnp.dot((t * t).astype(BF16), gsum_ref[...], preferred_element_type=F32)
        tn = t * lax.rsqrt(ss * (1.0 / HEAD_DIM) + EPS) * gain
        c, a, b = (jnp.concatenate([tab] * reps, axis=1) if reps > 1 else tab for tab in (cos, sa, sb))
        return tn * c + pltpu.roll(tn, w - AXIS_DIM // 2, 1) * a + pltpu.roll(tn, AXIS_DIM // 2, 1) * b

    q = head_norm_rope(z[:, :ATTN_WIDTH], gq_ref, qg_ref[...], ATTN_WIDTH // LANES)
    q_ref[0] = (q * (HEAD_DIM ** -0.5)).astype(BF16)
    k = head_norm_rope(z[:, ATTN_WIDTH:ATTN_WIDTH + KV_WIDTH], gk_ref, kg_ref[...], 1)
    k_ref[0] = k.astype(BF16)
    v_ref[0] = z[:, ATTN_WIDTH + KV_WIDTH:ATTN_WIDTH + 2 * KV_WIDTH].astype(BF16)
    off = ATTN_WIDTH + 2 * KV_WIDTH
    gu_ref[0] = jax.nn.gelu(z[:, off:off + GMLP_WIDTH], approximate=True).astype(BF16)
    gv = jax.nn.gelu(z[:, off + GMLP_WIDTH:], approximate=True)
    mu = jnp.mean(gv, axis=-1, keepdims=True)
    gc = gv - mu
    gn = gc * lax.rsqrt(jnp.mean(gc * gc, axis=-1, keepdims=True) + EPS) * lng_ref[...]
    gv_ref[0] = gn.astype(BF16)


def _l0_proj(xc, mod, g, w_in, qg, kg, lng, cos, sa, sb, gq, gk, nct):
    b, l, d = xc.shape
    t = TOKEN_TILE
    tok = lambda w: pl.BlockSpec((1, t, w), lambda bi, ti: (bi, ti, 0))
    tab = pl.BlockSpec((t, LANES), lambda bi, ti: (ti, 0))
    outs = [jax.ShapeDtypeStruct((b, l, w), BF16) for w in (ATTN_WIDTH, KV_WIDTH, KV_WIDTH, GMLP_WIDTH, GMLP_WIDTH)]
    return pl.pallas_call(
        _l0_proj_kernel,
        out_shape=outs,
        grid=(b, l // t),
        in_specs=[tok(d),
                  pl.BlockSpec((1, 1, 6, d), lambda bi, ti: (bi, jnp.where(ti >= nct, 1, 0), 0, 0)),
                  _full((1, d)), _full((d, MIX_IN_WIDTH)), _full((1, ATTN_WIDTH)), _full((1, KV_WIDTH)),
                  _full((1, GMLP_WIDTH)), tab, tab, tab,
                  _full((ATTN_WIDTH, ATTN_WIDTH)), _full((KV_WIDTH, KV_WIDTH))],
        out_specs=[tok(w) for w in (ATTN_WIDTH, KV_WIDTH, KV_WIDTH, GMLP_WIDTH, GMLP_WIDTH)],
        compiler_params=_params("parallel", "parallel"),
        name="l0_proj",
    )(xc, mod, g, w_in, qg, kg, lng, cos, sa, sb, gq, gk)


def _attn_kernel(q_ref, k_ref, v_ref, o_ref, *, n_ctx, nct):
    ti = pl.program_id(1)
    tq = q_ref.shape[1]
    lane = lax.broadcasted_iota(I32, (tq, LANES), 1)
    low = lane < HEAD_DIM

    def attend(nk):
        kk = k_ref[0, :nk, :]
        vv = v_ref[0, :nk, :]
        for m in range(ATTN_WIDTH // LANES):
            qp = q_ref[0, :, LANES * m:LANES * (m + 1)]
            halves = []
            for sel in (low, jnp.logical_not(low)):
                qm = jnp.where(sel, qp, jnp.zeros_like(qp))
                s = lax.dot_general(qm, kk, (((1,), (1,)), ((), ())), preferred_element_type=F32)
                p = jnp.exp(s - jnp.max(s, axis=-1, keepdims=True))
                den = jnp.sum(p, axis=-1, keepdims=True)
                pv = jnp.dot(p.astype(BF16), vv, preferred_element_type=F32)
                halves.append(pv / den)
            o_ref[0, :, LANES * m:LANES * (m + 1)] = jnp.where(low, halves[0], halves[1]).astype(BF16)

    @pl.when(ti < nct)
    def _():
        attend(n_ctx)

    @pl.when(ti >= nct)
    def _():
        attend(k_ref.shape[1])


def _attention(q, k, v, n_ctx):
    b, l, _ = q.shape
    t = TOKEN_TILE
    kern = functools.partial(_attn_kernel, n_ctx=n_ctx, nct=n_ctx // t)
    return pl.pallas_call(
        kern,
        out_shape=jax.ShapeDtypeStruct((b, l, ATTN_WIDTH), BF16),
        grid=(b, l // t),
        in_specs=[pl.BlockSpec((1, t, ATTN_WIDTH), lambda bi, ti: (bi, ti, 0)),
                  pl.BlockSpec((1, l, KV_WIDTH), lambda bi, ti: (bi, 0, 0)),
                  pl.BlockSpec((1, l, KV_WIDTH), lambda bi, ti: (bi, 0, 0))],
        out_specs=pl.BlockSpec((1, t, ATTN_WIDTH), lambda bi, ti: (bi, ti, 0)),
        compiler_params=_params("parallel", "parallel"),
        name="attention",
    )(q, k, v)


def _route_tail(x, y, m, gf, wr_ref, br_ref, base_ref, x_ref, f_ref, ri_ref, rw_ref, cnt_ref, first):
    x1 = x + m[2:3] * y
    x_ref[0] = x1
    f = _modulate(x1, gf, m[3:4], m[4:5])
    _store_row_tiles(f_ref, f)
    t = f.shape[0]
    logits = jnp.dot(f, wr_ref[...], preferred_element_type=F32,
                     precision=lax.Precision.HIGHEST) + br_ref[...]
    lane = lax.broadcasted_iota(I32, (t, LANES), 1).astype(F32)
    is_grp = lane < N_EXPERT_GROUPS
    gl = jnp.where(is_grp, logits, NEG_BIG)
    gmax = jnp.max(gl, axis=-1, keepdims=True)
    g_idx = jnp.min(jnp.where(gl == gmax, lane, float(LANES)), axis=-1, keepdims=True)
    g_w = 1.0 / jnp.sum(jnp.where(is_grp, jnp.exp(gl - gmax), 0.0), axis=-1, keepdims=True)
    e_lo = N_EXPERT_GROUPS + EXPERTS_PER_GROUP * g_idx
    in_grp = jnp.logical_and(lane >= e_lo, lane < e_lo + EXPERTS_PER_GROUP)
    sel = jnp.where(in_grp, logits, NEG_BIG)
    v1 = jnp.max(sel, axis=-1, keepdims=True)
    i1 = jnp.min(jnp.where(sel == v1, lane, float(LANES)), axis=-1, keepdims=True)
    sel2 = jnp.where(lane == i1, NEG_BIG, sel)
    v2 = jnp.max(sel2, axis=-1, keepdims=True)
    i2 = jnp.min(jnp.where(sel2 == v2, lane, float(LANES)), axis=-1, keepdims=True)
    e = jnp.exp(v2 - v1)
    w1 = g_w / (1.0 + e)
    w2 = g_w * e / (1.0 + e)
    eid1 = i1 - N_EXPERT_GROUPS
    eid2 = i2 - N_EXPERT_GROUPS

    @pl.when(first)
    def _():
        base_ref[...] = jnp.zeros_like(base_ref)

    hit1 = lane == eid1
    hit2 = lane == eid2
    cnt = (hit1.astype(F32) + hit2.astype(F32))
    row = lax.broadcasted_iota(I32, (t, t), 0)
    col = lax.broadcasted_iota(I32, (t, t), 1)
    tril = jnp.where(col < row, 1.0, 0.0).astype(BF16)
    before = jnp.dot(tril, cnt.astype(BF16), preferred_element_type=F32) + base_ref[0:1, :]
    rank1 = jnp.sum(jnp.where(hit1, before, 0.0), axis=-1, keepdims=True)
    rank2 = jnp.sum(jnp.where(hit2, before, 0.0), axis=-1, keepdims=True)
    total = base_ref[0:1, :] + jnp.sum(cnt, axis=0, keepdims=True)
    base_ref[...] = jnp.broadcast_to(total, base_ref.shape)
    cnt_ref[...] = jnp.broadcast_to(total, cnt_ref.shape)

    l8 = lax.broadcasted_iota(I32, (t, 8), 1)
    ri = jnp.where(l8 == 0, eid1, jnp.where(l8 == 1, eid2, jnp.where(l8 == 2, rank1,
                   jnp.where(l8 == 3, rank2, 0.0))))
    ri_ref[...] = ri.astype(I32)
    rw_ref[...] = jnp.where(l8 == 0, w1, jnp.where(l8 == 1, w2, 0.0))


def _tail_out_shapes(b, n_rows, d):
    t = TOKEN_TILE
    shapes = [jax.ShapeDtypeStruct((b, n_rows, d), F32), jax.ShapeDtypeStruct((b * n_rows, d // LANES, LANES), F32),
              jax.ShapeDtypeStruct((b * n_rows, 8), I32), jax.ShapeDtypeStruct((b * n_rows, 8), F32),
              jax.ShapeDtypeStruct((8, LANES), F32)]
    nt = n_rows // t
    flat = lambda w: pl.BlockSpec((t, w), lambda bi, ti: (bi * nt + ti, 0))
    specs = [pl.BlockSpec((1, t, d), lambda bi, ti: (bi, ti, 0)),
             pl.BlockSpec((t, d // LANES, LANES), lambda bi, ti: (bi * nt + ti, 0, 0)), flat(8), flat(8),
             pl.BlockSpec((8, LANES), lambda bi, ti: (0, 0))]
    return shapes, specs


def _l0_post_kernel(x_ref, o_ref, gu_ref, gv_ref, wsp_ref, bsp_ref, wout_ref, mod_ref, gf_ref, wr_ref, br_ref,
                    x1_ref, f_ref, ri_ref, rw_ref, cnt_ref, base_ref):
    tq = x_ref.shape[1]
    lane = lax.broadcasted_iota(I32, (GMLP_CHUNK, LANES), 1)
    low = lane < GMLP_GROUP_DIM
    gated = []
    for c in range(tq // GMLP_CHUNK):
        rows = slice(c * GMLP_CHUNK, (c + 1) * GMLP_CHUNK)
        mixed = []
        for m in range(GMLP_WIDTH // LANES):
            pair = gv_ref[0, rows, LANES * m:LANES * (m + 1)]
            a = jnp.dot(wsp_ref[2 * m], pair, preferred_element_type=F32)
            b = jnp.dot(wsp_ref[2 * m + 1], pair, preferred_element_type=F32)
            mixed.append(jnp.where(low, a, b))
        mixed = jnp.concatenate(mixed, axis=1) + bsp_ref[...]
        gated.append((gu_ref[0, rows, :].astype(F32) * mixed).astype(BF16))
    gated = jnp.concatenate(gated, axis=0)
    y = (jnp.dot(o_ref[0], wout_ref[:ATTN_WIDTH, :], preferred_element_type=F32)
         + jnp.dot(gated, wout_ref[ATTN_WIDTH:, :], preferred_element_type=F32))
    first = jnp.logical_and(pl.program_id(0) == 0, pl.program_id(1) == 0)
    _route_tail(x_ref[0], y, mod_ref[0, 0], gf_ref[...], wr_ref, br_ref, base_ref,
                x1_ref, f_ref, ri_ref, rw_ref, cnt_ref, first)


def _l0_post(xc, o, gu, gv, wsp, bsp, wout, mod, gf, wr, br, nct):
    b, l, d = xc.shape
    t = TOKEN_TILE
    tok = lambda w: pl.BlockSpec((1, t, w), lambda bi, ti: (bi, ti, 0))
    shapes, specs = _tail_out_shapes(b, l, d)
    return pl.pallas_call(
        _l0_post_kernel,
        out_shape=shapes,
        grid=(b, l // t),
        in_specs=[tok(d), tok(ATTN_WIDTH), tok(GMLP_WIDTH), tok(GMLP_WIDTH),
                  _full((GMLP_GROUPS, GMLP_CHUNK, GMLP_CHUNK)), _full((GMLP_CHUNK, GMLP_WIDTH)),
                  _full((ATTN_WIDTH + GMLP_WIDTH, d)),
                  pl.BlockSpec((1, 1, 6, d), lambda bi, ti: (bi, jnp.where(ti >= nct, 1, 0), 0, 0)),
                  _full((1, d)), _full((d, LANES)), _full((1, LANES))],
        out_specs=specs,
        scratch_shapes=[pltpu.VMEM((8, LANES), F32)],
        compiler_params=_params("arbitrary", "arbitrary"),
        name="l0_post",
    )(xc, o, gu, gv, wsp, bsp, wout, mod, gf, wr, br)


def _row_copy(src_ref, src_row, dst_ref, dst_row, sem):
    return pltpu.make_async_copy(src_ref.at[src_row], dst_ref.at[dst_row], sem)


def _dispatch_kernel(zstart_ref, nu_ref, f_ref, slots_ref, xb_ref, idx_ref, zero_ref, sem_idx, sem_rows):
    i = pl.program_id(0)
    t = f_ref.shape[0]
    idx_cp = pltpu.make_async_copy(slots_ref.at[i], idx_ref, sem_idx)
    idx_cp.start()

    @pl.when(i == 0)
    def _():
        zero_ref[...] = jnp.zeros_like(zero_ref)
        n_blocks = xb_ref.shape[0] // MOE_BLOCK

        def zero_block(start):
            return pltpu.make_async_copy(zero_ref, xb_ref.at[pl.ds(start, MOE_BLOCK)], sem_rows)

        def tail_start(j, carry):
            zero_block(j * MOE_BLOCK).start()
            return carry

        def tail_wait(j, carry):
            zero_block(0).wait()
            return carry

        for e in range(N_EXPERTS):
            zero_block(zstart_ref[e]).start()
        lax.fori_loop(nu_ref[0], n_blocks, tail_start, 0)
        for e in range(N_EXPERTS):
            zero_block(0).wait()
        lax.fori_loop(nu_ref[0], n_blocks, tail_wait, 0)

    idx_cp.wait()

    def issue(r, carry):
        _row_copy(f_ref, r, xb_ref, idx_ref[2 * r], sem_rows).start()
        _row_copy(f_ref, r, xb_ref, idx_ref[2 * r + 1], sem_rows).start()
        return carry

    lax.fori_loop(0, t, issue, 0)

    def drain(r, carry):
        _row_copy(f_ref, 0, xb_ref, 0, sem_rows).wait()
        _row_copy(f_ref, 0, xb_ref, 0, sem_rows).wait()
        return carry

    lax.fori_loop(0, t, drain, 0)


def _dispatch(f, slots, zstart, n_used, n_slots):
    n_tok, sub, lanes = f.shape
    t = TOKEN_TILE
    return pl.pallas_call(
        _dispatch_kernel,
        out_shape=jax.ShapeDtypeStruct((n_slots, sub, lanes), F32),
        grid_spec=pltpu.PrefetchScalarGridSpec(
            num_scalar_prefetch=2,
            grid=(n_tok // t,),
            in_specs=[pl.BlockSpec((t, sub, lanes), lambda i, z, nu: (i, 0, 0)),
                      pl.BlockSpec(memory_space=pltpu.VMEM)],
            out_specs=pl.BlockSpec(memory_space=pl.ANY),
            scratch_shapes=[pltpu.SMEM((2 * t,), I32), pltpu.VMEM((MOE_BLOCK, sub, lanes), F32),
                            pltpu.SemaphoreType.DMA, pltpu.SemaphoreType.DMA]),
        compiler_params=_params("arbitrary"),
        name="moe_dispatch",
    )(zstart, n_used, f, slots)


def _experts_kernel(be_ref, nu_ref, xb_ref, w1_ref, w3_ref, w2_ref, y_ref):
    @pl.when(pl.program_id(0) < nu_ref[0])
    def _():
        x = _load_row_tiles(xb_ref).astype(BF16)
        h1 = jnp.dot(x, w1_ref[...], preferred_element_type=F32)
        h3 = jnp.dot(x, w3_ref[...], preferred_element_type=F32)
        a = (h1 * jax.nn.sigmoid(h1) * h3).astype(BF16)
        _store_row_tiles(y_ref, jnp.dot(a, w2_ref[...], preferred_element_type=F32))

    @pl.when(pl.program_id(0) >= nu_ref[0])
    def _():
        y_ref[...] = jnp.zeros_like(y_ref)


def _experts(xb, w1, w3, w2, block_e, n_used):
    n_slots, sub, lanes = xb.shape
    d = sub * lanes
    hid = w1.shape[2]
    blk = lambda i, be, nu: (jnp.minimum(i, nu[0] - 1), 0, 0)
    return pl.pallas_call(
        _experts_kernel,
        out_shape=jax.ShapeDtypeStruct((n_slots, sub, lanes), F32),
        grid_spec=pltpu.PrefetchScalarGridSpec(
            num_scalar_prefetch=2,
            grid=(n_slots // MOE_BLOCK,),
            in_specs=[pl.BlockSpec((MOE_BLOCK, sub, lanes), blk),
                      pl.BlockSpec((None, d, hid), lambda i, be, nu: (be[i], 0, 0)),
                      pl.BlockSpec((None, d, hid), lambda i, be, nu: (be[i], 0, 0)),
                      pl.BlockSpec((None, hid, d), lambda i, be, nu: (be[i], 0, 0))],
            out_specs=pl.BlockSpec((MOE_BLOCK, sub, lanes), lambda i, be, nu: (i, 0, 0))),
        compiler_params=_params("arbitrary"),
        name="moe_experts",
    )(block_e, n_used, xb, w1, w3, w2)


def _moe_plan(route_i, counts, n_tok):
    eid, rank = route_i[:, 0:2], route_i[:, 2:4]
    cnt = counts[0, :N_EXPERTS].astype(I32)
    padded = (cnt + MOE_BLOCK - 1) // MOE_BLOCK * MOE_BLOCK
    pend = jnp.cumsum(padded)
    pstart = pend - padded
    slots = pstart[eid] + rank
    n_assign = 2 * n_tok
    n_slots = (n_assign + MOE_BLOCK - 1) // MOE_BLOCK * MOE_BLOCK + N_EXPERTS * MOE_BLOCK
    n_blocks = n_slots // MOE_BLOCK
    bstart = jnp.arange(n_blocks, dtype=I32) * MOE_BLOCK
    block_e = jnp.minimum(jnp.sum(bstart[:, None] >= pend[None, :], axis=1), N_EXPERTS - 1).astype(I32)
    n_used = pend[-1] // MOE_BLOCK
    last_e = block_e[jnp.maximum(n_used - 1, 0)]
    block_e = jnp.where(jnp.arange(n_blocks) < n_used, block_e, last_e).astype(I32)
    zstart = jnp.maximum(pend - MOE_BLOCK, 0).astype(I32)
    return slots.reshape(n_tok // TOKEN_TILE, 2 * TOKEN_TILE).astype(I32), block_e, n_used.reshape(1).astype(I32), zstart, n_slots


def _moe(f, route_i, counts, w1, w3, w2):
    n_tok = f.shape[0]
    slots, block_e, n_used, zstart, n_slots = _moe_plan(route_i, counts, n_tok)
    xb = _dispatch(f, slots, zstart, n_used, n_slots)
    y = _experts(xb, w1, w3, w2, block_e, n_used)
    return y, slots


def _gather_rows(y_ref, slots_ref, idx_ref, ybuf_ref, sem_idx, sem_rows, i):
    t = ybuf_ref.shape[1]
    idx_cp = pltpu.make_async_copy(slots_ref.at[i], idx_ref, sem_idx)
    idx_cp.start()
    idx_cp.wait()

    def issue(r, carry):
        _row_copy(y_ref, idx_ref[2 * r], ybuf_ref.at[0], r, sem_rows).start()
        _row_copy(y_ref, idx_ref[2 * r + 1], ybuf_ref.at[1], r, sem_rows).start()
        return carry

    lax.fori_loop(0, t, issue, 0)

    def drain(r, carry):
        _row_copy(y_ref, 0, ybuf_ref.at[0], 0, sem_rows).wait()
        _row_copy(y_ref, 0, ybuf_ref.at[1], 0, sem_rows).wait()
        return carry

    lax.fori_loop(0, t, drain, 0)


def _weighted_rows(rw_ref, ybuf_ref):
    rw = rw_ref[...]
    w1, w2 = rw[:, 0:1], rw[:, 1:2]
    return jnp.concatenate([w1 * ybuf_ref[0, :, s, :] + w2 * ybuf_ref[1, :, s, :]
                            for s in range(ybuf_ref.shape[2])], axis=1)


def _l0_combine_kernel(x_ref, rw_ref, slots_ref, y_ref, mod0_ref, mod1_ref, g_ref, win_ref,
                       x2_ref, u_ref, idx_ref, ybuf_ref, sem_idx, sem_rows):
    i = pl.program_id(0)
    _gather_rows(y_ref, slots_ref, idx_ref, ybuf_ref, sem_idx, sem_rows, i)
    moe = _weighted_rows(rw_ref, ybuf_ref)
    x2 = x_ref[...] + mod0_ref[0, 0][5:6] * moe
    x2_ref[...] = x2
    m1 = mod1_ref[0, 0]
    h = _modulate(x2, g_ref[...], m1[0:1], m1[1:2]).astype(BF16)
    u_ref[...] = jnp.dot(h, win_ref[...], preferred_element_type=F32)


def _l0_combine(x1, rw, slots, y, mod0, mod1, g, win, b, nct):
    n_tok, d = x1.shape
    t = TOKEN_TILE
    nt = n_tok // b // t
    l = n_tok // b
    modspec = pl.BlockSpec((1, 1, 6, d), lambda i: (i // nt, jnp.where(i % nt >= nct, 1, 0), 0, 0))
    return pl.pallas_call(
        _l0_combine_kernel,
        out_shape=[jax.ShapeDtypeStruct((n_tok, d), F32), jax.ShapeDtypeStruct((l, b * S5_WIDTH), F32)],
        grid=(n_tok // t,),
        in_specs=[pl.BlockSpec((t, d), lambda i: (i, 0)), pl.BlockSpec((t, 8), lambda i: (i, 0)),
                  pl.BlockSpec(memory_space=pltpu.VMEM), pl.BlockSpec(memory_space=pl.ANY),
                  modspec, modspec, _full((1, d)), _full((d, S5_WIDTH))],
        out_specs=[pl.BlockSpec((t, d), lambda i: (i, 0)),
                   pl.BlockSpec((t, S5_WIDTH), lambda i: (i % nt, i // nt))],
        scratch_shapes=[pltpu.SMEM((2 * t,), I32), pltpu.VMEM((2, t, d // LANES, LANES), F32),
                        pltpu.SemaphoreType.DMA, pltpu.SemaphoreType.DMA],
        compiler_params=_params("arbitrary"),
        name="l0_combine",
    )(x1, rw, slots, y, mod0, mod1, g, win)


def _s5_zoh_kernel(are_ref, aim_ref, ldt_ref, bre_ref, bim_ref, lre_ref, lim_ref, bbre_ref, bbim_ref):
    ar, ai = are_ref[...], aim_ref[...]
    dt = jnp.exp(ldt_ref[...])
    mag = jnp.exp(ar * dt)
    lr = mag * jnp.cos(ai * dt)
    li = mag * jnp.sin(ai * dt)
    lre_ref[...] = lr
    lim_ref[...] = li
    nr, ni = lr - 1.0, li
    den = ar * ar + ai * ai
    cr = (nr * ar + ni * ai) / den
    ci = (ni * ar - nr * ai) / den
    br, bi = bre_ref[...], bim_ref[...]
    bbre_ref[...] = cr * br - ci * bi
    bbim_ref[...] = cr * bi + ci * br


def _s5_zoh(a_re, a_im, log_dt, b_re, b_im):
    n_dir, g, n, p = b_re.shape
    rows = n_dir * g * p
    expand = lambda a: jnp.broadcast_to(a[:, :, None, :], (n_dir, g, p, n)).reshape(rows, n)
    ldt = jnp.broadcast_to(log_dt[:, :, None, None], (n_dir, g, p, n)).reshape(rows, n)
    bt = lambda a: jnp.transpose(a, (0, 1, 3, 2)).reshape(rows, n)
    outs = pl.pallas_call(
        _s5_zoh_kernel,
        out_shape=[jax.ShapeDtypeStruct((rows, n), F32)] * 4,
        name="s5_zoh",
    )(expand(a_re), expand(a_im), ldt, bt(b_re), bt(b_im))
    lre, lim, bbre, bbim = (o.reshape(n_dir, g, p, n) for o in outs)
    return lre[:, :, 0, :], lim[:, :, 0, :], bbre, bbim


def _s5_matrices(lre, lim, bbre, bbim, c_re, c_im):
    n_dir = lre.shape[0]
    nj = S5_GROUPS // (2 * S5_TILES)
    shp = (n_dir, nj, S5_TILES, 2, S5_GROUP_DIM, S5_STATE)
    eye_t = jnp.eye(S5_TILES, dtype=F32)
    eye_g = jnp.eye(2, dtype=F32)

    def place(a):
        return jnp.einsum("djtgpn,tu,gh->djtgpuhn", a.reshape(shp), eye_t, eye_g)

    kdim = S5_TILES * 2 * S5_GROUP_DIM
    sdim = S5_TILES * 2 * 2 * S5_STATE
    bm = jnp.stack([place(bbre), place(bbim)], axis=6)
    bm = bm.reshape(n_dir, nj, kdim, sdim).astype(BF16)
    cm = jnp.stack([place(c_re), place(-c_im)], axis=6)
    cm = jnp.transpose(cm, (0, 1, 5, 6, 7, 8, 2, 3, 4)).reshape(n_dir, nj, sdim, kdim).astype(BF16)
    lam = jnp.stack([lre.reshape(n_dir, nj, S5_TILES * LANES), lim.reshape(n_dir, nj, S5_TILES * LANES)], axis=2)
    return bm, cm, lam


def _s5_scan_kernel(uf_ref, ub_ref, bm_ref, cm_ref, lam_ref, yf_ref, yb_ref, v_ref, h_ref, *, tc, nb):
    @pl.when(pl.program_id(0) == 0)
    def _():
        h_ref[...] = jnp.zeros_like(h_ref)

    nj = bm_ref.shape[1]
    kdim = bm_ref.shape[2]
    for d, (u_ref, y_ref) in enumerate(((uf_ref, yf_ref), (ub_ref, yb_ref))):
        for jq in range(nj):
            uk = u_ref[:, kdim * jq:kdim * (jq + 1)].astype(BF16)
            v_ref[...] = jnp.dot(uk, bm_ref[d, jq], preferred_element_type=F32)
            lam = lam_ref[d, jq]
            lr = [jnp.broadcast_to(lam[0:1, LANES * t:LANES * (t + 1)], (nb, LANES)) for t in range(S5_TILES)]
            li = [jnp.broadcast_to(lam[1:2, LANES * t:LANES * (t + 1)], (nb, LANES)) for t in range(S5_TILES)]
            h0 = h_ref[d, jq]
            init = tuple(h0[:, LANES * c:LANES * (c + 1)] for c in range(2 * S5_TILES))

            def step(s, carry, d=d, lr=lr, li=li):
                tt = s if d == 0 else tc - 1 - s
                r0 = pl.multiple_of(tt * nb, nb)
                rows = v_ref[pl.ds(r0, nb), :]
                new = []
                for t in range(S5_TILES):
                    vr = rows[:, 2 * LANES * t:2 * LANES * t + LANES]
                    vi = rows[:, 2 * LANES * t + LANES:2 * LANES * (t + 1)]
                    hr, hi = carry[2 * t], carry[2 * t + 1]
                    new.append(lr[t] * hr - li[t] * hi + vr)
                    new.append(lr[t] * hi + li[t] * hr + vi)
                v_ref[pl.ds(r0, nb), :] = jnp.concatenate(new, axis=1)
                return tuple(new)

            fin = lax.fori_loop(0, tc, step, init)
            h_ref[d, jq] = jnp.concatenate(fin, axis=1)
            y_ref[:, kdim * jq:kdim * (jq + 1)] = jnp.dot(v_ref[...].astype(BF16), cm_ref[d, jq],
                                                         preferred_element_type=F32)


def _s5_scan(u_tm, bm, cm, lam, nb, n_ctx):
    rows_total, w = u_tm.shape
    tc = S5_CHUNK
    r = tc * nb
    n_chunks = rows_total // r
    ncc = n_ctx // tc
    bwd = lambda i: (jnp.where(i < ncc, ncc - 1 - i, n_chunks - 1 - (i - ncc)), 0)
    nj = bm.shape[1]
    sdim = bm.shape[3]
    kern = functools.partial(_s5_scan_kernel, tc=tc, nb=nb)
    return pl.pallas_call(
        kern,
        out_shape=[jax.ShapeDtypeStruct((rows_total, w), F32)] * 2,
        grid=(n_chunks,),
        in_specs=[pl.BlockSpec((r, w), lambda i: (i, 0)), pl.BlockSpec((r, w), bwd),
                  _full(bm.shape), _full(cm.shape), _full(lam.shape)],
        out_specs=[pl.BlockSpec((r, w), lambda i: (i, 0)), pl.BlockSpec((r, w), bwd)],
        scratch_shapes=[pltpu.VMEM((r, sdim), F32), pltpu.VMEM((2, nj, nb, sdim), F32)],
        compiler_params=_params("arbitrary"),
        name="s5_scan",
    )(u_tm, u_tm, bm, cm, lam)


def _l1_post_kernel(x_ref, u_ref, yf_ref, yb_ref, dsk_ref, wglu_ref, bglu_ref, wout_ref, mod_ref, gf_ref,
                    wr_ref, br_ref, x3_ref, f_ref, ri_ref, rw_ref, cnt_ref, base_ref):
    y = dsk_ref[...] * u_ref[...] + yf_ref[...] + yb_ref[...]
    g = jax.nn.gelu(y, approximate=True)
    z = jnp.dot(g.astype(BF16), wglu_ref[...], preferred_element_type=F32) + bglu_ref[...]
    o = jnp.dot((g * jax.nn.sigmoid(z)).astype(BF16), wout_ref[...], preferred_element_type=F32)
    first = jnp.logical_and(pl.program_id(0) == 0, pl.program_id(1) == 0)
    _route_tail(x_ref[0], o, mod_ref[0, 0], gf_ref[...], wr_ref, br_ref, base_ref,
                x3_ref, f_ref, ri_ref, rw_ref, cnt_ref, first)


def _l1_post(x2, u_tm, yf, yb, dsk, wglu, bglu, wout, mod, gf, wr, br, n_ctx):
    b, l, d = x2.shape
    t = TOKEN_TILE
    nct = n_ctx // t
    s = l - n_ctx
    tm = pl.BlockSpec((t, S5_WIDTH), lambda bi, ti: (ti + nct, bi))
    shapes, specs = _tail_out_shapes(b, s, d)
    return pl.pallas_call(
        _l1_post_kernel,
        out_shape=shapes,
        grid=(b, s // t),
        in_specs=[pl.BlockSpec((1, t, d), lambda bi, ti: (bi, ti + nct, 0)), tm, tm, tm,
                  _full((1, S5_WIDTH)), _full((S5_WIDTH, S5_WIDTH)), _full((1, S5_WIDTH)), _full((S5_WIDTH, d)),
                  pl.BlockSpec((1, 1, 6, d), lambda bi, ti: (bi, 1, 0, 0)),
                  _full((1, d)), _full((d, LANES)), _full((1, LANES))],
        out_specs=specs,
        scratch_shapes=[pltpu.VMEM((8, LANES), F32)],
        compiler_params=_params("arbitrary", "arbitrary"),
        name="l1_post",
    )(x2, u_tm, yf, yb, dsk, wglu, bglu, wout, mod, gf, wr, br)


def _final_kernel(x_ref, rw_ref, slots_ref, y_ref, mod_ref, g_ref, o_ref, idx_ref, ybuf_ref, sem_idx, sem_rows):
    i = pl.program_id(0)
    _gather_rows(y_ref, slots_ref, idx_ref, ybuf_ref, sem_idx, sem_rows, i)
    moe = _weighted_rows(rw_ref, ybuf_ref)
    x4 = x_ref[...] + mod_ref[0, 0][5:6] * moe
    o_ref[...] = _rms(x4) * g_ref[...]


def _final(x3, rw, slots, y, mod, g, b):
    n_tok, d = x3.shape
    t = TOKEN_TILE
    nt = n_tok // b // t
    return pl.pallas_call(
        _final_kernel,
        out_shape=jax.ShapeDtypeStruct((n_tok, d), F32),
        grid=(n_tok // t,),
        in_specs=[pl.BlockSpec((t, d), lambda i: (i, 0)), pl.BlockSpec((t, 8), lambda i: (i, 0)),
                  pl.BlockSpec(memory_space=pltpu.VMEM), pl.BlockSpec(memory_space=pl.ANY),
                  pl.BlockSpec((1, 1, 6, d), lambda i: (i // nt, 1, 0, 0)), _full((1, d))],
        out_specs=pl.BlockSpec((t, d), lambda i: (i, 0)),
        scratch_shapes=[pltpu.SMEM((2 * t,), I32), pltpu.VMEM((2, t, d // LANES, LANES), F32),
                        pltpu.SemaphoreType.DMA, pltpu.SemaphoreType.DMA],
        compiler_params=_params("arbitrary"),
        name="final_combine",
    )(x3, rw, slots, y, mod, g)


def _rope_tables(n_ctx, n_lat):
    pos = jnp.arange(n_lat, dtype=I32)
    row = (pos // GRID_W).astype(F32)
    col = (pos % GRID_W).astype(F32)
    inv_freq = jnp.power(ROPE_THETA, -jnp.arange(0, AXIS_DIM, 2, dtype=F32) / AXIS_DIM)
    lane = np.arange(LANES)
    dd = lane % HEAD_DIM
    use_col = dd >= AXIS_DIM
    first = (dd % AXIS_DIM) < AXIS_DIM // 2
    freq = inv_freq[(dd % AXIS_DIM) % (AXIS_DIM // 2)]
    ang = jnp.where(use_col[None, :], col[:, None], row[:, None]) * freq[None, :]
    cos, sin = jnp.cos(ang), jnp.sin(ang)
    sa = jnp.where(first[None, :], -sin, 0.0)
    sb = jnp.where(first[None, :], 0.0, sin)
    pad = lambda a, v: jnp.concatenate([jnp.full((n_ctx, LANES), v, F32), a], axis=0)
    return pad(cos, 1.0), pad(sa, 0.0), pad(sb, 0.0)


def _group_ones(width, group):
    idx = np.arange(width) // group
    return jnp.asarray(idx[:, None] == idx[None, :], dtype=BF16)


def _router_weights(w_grp, b_grp, w_rt, b_rt):
    d = w_grp.shape[0]
    used = N_EXPERT_GROUPS + N_EXPERTS
    wr = jnp.concatenate([w_grp, w_rt, jnp.zeros((d, LANES - used), F32)], axis=1)
    br = jnp.concatenate([b_grp, b_rt, jnp.zeros((LANES - used,), F32)]).reshape(1, LANES)
    return wr, br


def kernel(x, c, ctx, c_ctx, w_mod, b_mod, norm_mix_g, norm_ffn_g, mix_w_in, mix_w_out, q_norm_g, k_norm_g, gmlp_norm_g, gmlp_w_spatial, gmlp_b_spatial, s5_w_in, s5_a_re, s5_a_im, s5_log_dt, s5_b_re, s5_b_im, s5_c_re, s5_c_im, s5_d, s5_w_glu, s5_b_glu, s5_w_out, moe_w_group, moe_b_group, moe_w_router, moe_b_router, moe_w1, moe_w3, moe_w2, final_norm_g):
    b, s, d = x.shape
    n_ctx = ctx.shape[1]
    l = n_ctx + s
    assert w_mod.shape[0] == 2, "two layers: attention/gMLP then S5"
    assert n_ctx % TOKEN_TILE == 0 and s % TOKEN_TILE == 0 and b % 8 == 0
    nct = n_ctx // TOKEN_TILE

    r = (b + 1 + 7) // 8 * 8
    cond = jnp.concatenate([c, c_ctx[None, :], jnp.zeros((r - b - 1, d), F32)], axis=0)
    mods = _adaln(cond, w_mod, b_mod)
    lat = mods[:, :b].reshape(2, b, 1, 6, d)
    cxt = jnp.broadcast_to(mods[:, b].reshape(2, 1, 1, 6, d), (2, b, 1, 6, d))
    mod = jnp.concatenate([cxt, lat], axis=2)

    xc = jnp.concatenate([ctx, x], axis=1)

    w_in = jnp.concatenate([mix_w_in[0][:, _Q_COL_PERM], mix_w_in[0][:, ATTN_WIDTH:]], axis=1).astype(BF16)
    w_out = jnp.concatenate([mix_w_out[0][_Q_COL_PERM], mix_w_out[0][ATTN_WIDTH:]], axis=0).astype(BF16)
    cos, sa, sb = _rope_tables(n_ctx, s)
    qg = jnp.tile(q_norm_g[0], N_Q_HEADS).reshape(1, ATTN_WIDTH)
    kg = jnp.tile(k_norm_g[0], N_KV_HEADS).reshape(1, KV_WIDTH)
    q, k, v, gu, gv = _l0_proj(xc, mod[0], norm_mix_g[0].reshape(1, d), w_in, qg, kg,
                               gmlp_norm_g[0].reshape(1, GMLP_WIDTH), cos, sa, sb,
                               _group_ones(ATTN_WIDTH, HEAD_DIM), _group_ones(KV_WIDTH, HEAD_DIM), nct)
    o = _attention(q, k, v, n_ctx)
    bsp = jnp.repeat(gmlp_b_spatial[0].T, GMLP_GROUP_DIM, axis=1)
    wr0, br0 = _router_weights(moe_w_group[0], moe_b_group[0], moe_w_router[0], moe_b_router[0])
    x1, f0, ri0, rw0, cnt0 = _l0_post(xc, o, gu, gv, gmlp_w_spatial[0].astype(BF16), bsp, w_out, mod[0],
                                      norm_ffn_g[0].reshape(1, d), wr0, br0, nct)
    y0, slots0 = _moe(f0, ri0, cnt0, moe_w1[0].astype(BF16), moe_w3[0].astype(BF16), moe_w2[0].astype(BF16))
    x2, u_tm = _l0_combine(x1.reshape(b * l, d), rw0, slots0, y0, mod[0], mod[1], norm_mix_g[1].reshape(1, d),
                           s5_w_in[0].astype(BF16), b, nct)

    lre, lim, bbre, bbim = _s5_zoh(s5_a_re[0], s5_a_im[0], s5_log_dt[0], s5_b_re[0], s5_b_im[0])
    bm, cm, lam = _s5_matrices(lre, lim, bbre, bbim, s5_c_re[0], s5_c_im[0])
    yf, yb = _s5_scan(u_tm.reshape(l * b, S5_WIDTH), bm, cm, lam, b, n_ctx)
    wr1, br1 = _router_weights(moe_w_group[1], moe_b_group[1], moe_w_router[1], moe_b_router[1])
    tmv = lambda a: a.reshape(l, b * S5_WIDTH)
    x3, f1, ri1, rw1, cnt1 = _l1_post(x2.reshape(b, l, d), u_tm, tmv(yf), tmv(yb), s5_d[0].reshape(1, S5_WIDTH),
                                      s5_w_glu[0].astype(BF16), s5_b_glu[0].reshape(1, S5_WIDTH),
                                      s5_w_out[0].astype(BF16), mod[1], norm_ffn_g[1].reshape(1, d), wr1, br1, n_ctx)
    y1, slots1 = _moe(f1, ri1, cnt1, moe_w1[1].astype(BF16), moe_w3[1].astype(BF16), moe_w2[1].astype(BF16))
    out = _final(x3.reshape(b * s, d), rw1, slots1, y1, mod[1], final_norm_g.reshape(1, d), b)
    return out.reshape(b, s, d)
```

```python
import functools

import numpy as np
import jax
import jax.numpy as jnp
from jax import lax
from jax.experimental import pallas as pl
from jax.experimental.pallas import tpu as pltpu

F32 = jnp.float32
BF16 = jnp.bfloat16
I32 = jnp.int32

EPS = 1e-6
GRID_W = 64
HEAD_DIM = 64
N_Q_HEADS = 8
N_KV_HEADS = 2
ATTN_WIDTH = N_Q_HEADS * HEAD_DIM
KV_WIDTH = N_KV_HEADS * HEAD_DIM
AXIS_DIM = HEAD_DIM // 2
ROPE_THETA = 10000.0
GMLP_GROUPS = 8
GMLP_GROUP_DIM = 64
GMLP_WIDTH = GMLP_GROUPS * GMLP_GROUP_DIM
GMLP_CHUNK = 128
MIX_IN_WIDTH = ATTN_WIDTH + 2 * KV_WIDTH + 2 * GMLP_WIDTH
S5_GROUPS = 32
S5_GROUP_DIM = 16
S5_STATE = 64
S5_WIDTH = S5_GROUPS * S5_GROUP_DIM
N_EXPERT_GROUPS = 4
EXPERTS_PER_GROUP = 8
N_EXPERTS = N_EXPERT_GROUPS * EXPERTS_PER_GROUP
EXPERT_HIDDEN = 512
MOE_BLOCK = 256

D_MODEL = 1024
LANES = 128
ROW_SUB = D_MODEL // LANES
TOKEN_TILE = 256
ROW_DMA_UNROLL = 8
S5_CHUNK = 64
S5_TILES = 4
VMEM_LIMIT = 48 * 1024 * 1024
NEG_BIG = -1e30

_Q_HEAD_ORDER = (0, 4, 1, 5, 2, 6, 3, 7)
_Q_COL_PERM = np.concatenate([np.arange(HEAD_DIM) + HEAD_DIM * h for h in _Q_HEAD_ORDER])


def _params(*sem):
    return pltpu.CompilerParams(dimension_semantics=sem, vmem_limit_bytes=VMEM_LIMIT)


def _rms(x):
    return x * lax.rsqrt(jnp.mean(x * x, axis=-1, keepdims=True) + EPS)


def _modulate(x, g, shift, scale):
    return _rms(x) * g * (1.0 + scale) + shift


def _full(shape):
    return pl.BlockSpec(shape, lambda *_: (0,) * len(shape))


def _store_row_tiles(ref, val):
    n = val.shape[0]
    for s in range(ROW_SUB):
        ref[pl.ds(s, n, stride=ROW_SUB), :] = val[:, LANES * s:LANES * (s + 1)]


def _load_row_tiles(ref):
    n = ref.shape[0] // ROW_SUB
    return jnp.concatenate([ref[pl.ds(s, n, stride=ROW_SUB), :] for s in range(ROW_SUB)], axis=1)


def _adaln_kernel(cond_ref, w_ref, b_ref, o_ref):
    c = cond_ref[...]
    s = c * jax.nn.sigmoid(c)
    o_ref[0] = jnp.dot(s, w_ref[0], preferred_element_type=F32,
                       precision=lax.Precision.HIGHEST) + b_ref[0]


def _adaln(cond, w_mod, b_mod):
    depth, d, n = w_mod.shape
    r = cond.shape[0]
    tn = 1536
    return pl.pallas_call(
        _adaln_kernel,
        out_shape=jax.ShapeDtypeStruct((depth, r, n), F32),
        grid=(depth, n // tn),
        in_specs=[pl.BlockSpec((r, d), lambda l, j: (0, 0)),
                  pl.BlockSpec((1, d, tn), lambda l, j: (l, 0, j)),
                  pl.BlockSpec((1, 1, tn), lambda l, j: (l, 0, j))],
        out_specs=pl.BlockSpec((1, r, tn), lambda l, j: (l, 0, j)),
        compiler_params=_params("parallel", "parallel"),
        name="adaln",
    )(cond, w_mod, b_mod.reshape(depth, 1, n))


def _seq_tile(ctx_ref, x_ref, nct):
    return jnp.where(pl.program_id(1) < nct, ctx_ref[0], x_ref[0])


def _seq_specs(t, d, nct):
    return [pl.BlockSpec((1, t, d), lambda bi, ti: (bi, jnp.minimum(ti, nct - 1), 0)),
            pl.BlockSpec((1, t, d), lambda bi, ti: (bi, jnp.maximum(ti - nct, 0), 0))]


def _l0_proj_kernel(ctx_ref, x_ref, mod_ref, g_ref, win_ref, qg_ref, kg_ref, lng_ref, cos_ref, sa_ref, sb_ref,
                    gq_ref, gk_ref, q_ref, k_ref, v_ref, gu_ref, gv_ref, *, nct):
    m = mod_ref[0, 0]
    h = _modulate(_seq_tile(ctx_ref, x_ref, nct), g_ref[...], m[0:1], m[1:2]).astype(BF16)
    z = jnp.dot(h, win_ref[...], preferred_element_type=F32)
    cos, sa, sb = cos_ref[...], sa_ref[...], sb_ref[...]

    def head_norm_rope(t, gsum_ref, gain, reps):
        w = t.shape[1]
        ss = jnp.dot((t * t).astype(BF16), gsum_ref[...], preferred_element_type=F32)
        tn = t * lax.rsqrt(ss * (1.0 / HEAD_DIM) + EPS) * gain
        c, a, b = (jnp.concatenate([tab] * reps, axis=1) if reps > 1 else tab for tab in (cos, sa, sb))
        return tn * c + pltpu.roll(tn, w - AXIS_DIM // 2, 1) * a + pltpu.roll(tn, AXIS_DIM // 2, 1) * b

    q = head_norm_rope(z[:, :ATTN_WIDTH], gq_ref, qg_ref[...], ATTN_WIDTH // LANES)
    q_ref[0] = (q * (HEAD_DIM ** -0.5)).astype(BF16)
    k = head_norm_rope(z[:, ATTN_WIDTH:ATTN_WIDTH + KV_WIDTH], gk_ref, kg_ref[...], 1)
    k_ref[0] = k.astype(BF16)
    v_ref[0] = z[:, ATTN_WIDTH + KV_WIDTH:ATTN_WIDTH + 2 * KV_WIDTH].astype(BF16)
    off = ATTN_WIDTH + 2 * KV_WIDTH
    gu_ref[0] = jax.nn.gelu(z[:, off:off + GMLP_WIDTH], approximate=True).astype(BF16)
    gv = jax.nn.gelu(z[:, off + GMLP_WIDTH:], approximate=True)
    mu = jnp.mean(gv, axis=-1, keepdims=True)
    gc = gv - mu
    gn = gc * lax.rsqrt(jnp.mean(gc * gc, axis=-1, keepdims=True) + EPS) * lng_ref[...]
    gv_ref[0] = gn.astype(BF16)


def _l0_proj(ctx, x, mod, g, w_in, qg, kg, lng, cos, sa, sb, gq, gk, nct):
    b, s, d = x.shape
    l = s + ctx.shape[1]
    t = TOKEN_TILE
    tok = lambda w: pl.BlockSpec((1, t, w), lambda bi, ti: (bi, ti, 0))
    tab = pl.BlockSpec((t, LANES), lambda bi, ti: (ti, 0))
    outs = [jax.ShapeDtypeStruct((b, l, w), BF16) for w in (ATTN_WIDTH, KV_WIDTH, KV_WIDTH, GMLP_WIDTH, GMLP_WIDTH)]
    return pl.pallas_call(
        functools.partial(_l0_proj_kernel, nct=nct),
        out_shape=outs,
        grid=(b, l // t),
        in_specs=_seq_specs(t, d, nct) + [
                  pl.BlockSpec((1, 1, 6, d), lambda bi, ti: (bi, jnp.where(ti >= nct, 1, 0), 0, 0)),
                  _full((1, d)), _full((d, MIX_IN_WIDTH)), _full((1, ATTN_WIDTH)), _full((1, KV_WIDTH)),
                  _full((1, GMLP_WIDTH)), tab, tab, tab,
                  _full((ATTN_WIDTH, ATTN_WIDTH)), _full((KV_WIDTH, KV_WIDTH))],
        out_specs=[tok(w) for w in (ATTN_WIDTH, KV_WIDTH, KV_WIDTH, GMLP_WIDTH, GMLP_WIDTH)],
        compiler_params=_params("parallel", "parallel"),
        name="l0_proj",
    )(ctx, x, mod, g, w_in, qg, kg, lng, cos, sa, sb, gq, gk)


def _attn_kernel(q_ref, k_ref, v_ref, o_ref, *, n_ctx, nct):
    ti = pl.program_id(1)
    tq = q_ref.shape[1]
    lane = lax.broadcasted_iota(I32, (tq, LANES), 1)
    low = lane < HEAD_DIM

    def attend(nk):
        kk = k_ref[0, :nk, :]
        vv = v_ref[0, :nk, :]
        for m in range(ATTN_WIDTH // LANES):
            qp = q_ref[0, :, LANES * m:LANES * (m + 1)]
            halves = []
            for sel in (low, jnp.logical_not(low)):
                qm = jnp.where(sel, qp, jnp.zeros_like(qp))
                s = lax.dot_general(qm, kk, (((1,), (1,)), ((), ())), preferred_element_type=F32)
                p = jnp.exp(s - jnp.max(s, axis=-1, keepdims=True))
                den = jnp.sum(p, axis=-1, keepdims=True)
                pv = jnp.dot(p.astype(BF16), vv, preferred_element_type=F32)
                halves.append(pv / den)
            o_ref[0, :, LANES * m:LANES * (m + 1)] = jnp.where(low, halves[0], halves[1]).astype(BF16)

    @pl.when(ti < nct)
    def _():
        attend(n_ctx)

    @pl.when(ti >= nct)
    def _():
        attend(k_ref.shape[1])


def _attention(q, k, v, n_ctx):
    b, l, _ = q.shape
    t = TOKEN_TILE
    kern = functools.partial(_attn_kernel, n_ctx=n_ctx, nct=n_ctx // t)
    return pl.pallas_call(
        kern,
        out_shape=jax.ShapeDtypeStruct((b, l, ATTN_WIDTH), BF16),
        grid=(b, l // t),
        in_specs=[pl.BlockSpec((1, t, ATTN_WIDTH), lambda bi, ti: (bi, ti, 0)),
                  pl.BlockSpec((1, l, KV_WIDTH), lambda bi, ti: (bi, 0, 0)),
                  pl.BlockSpec((1, l, KV_WIDTH), lambda bi, ti: (bi, 0, 0))],
        out_specs=pl.BlockSpec((1, t, ATTN_WIDTH), lambda bi, ti: (bi, ti, 0)),
        compiler_params=_params("parallel", "parallel"),
        name="attention",
    )(q, k, v)


def _route_tail(x, y, m, gf, wr_ref, br_ref, base_ref, x_ref, f_ref, ri_ref, rw_ref, cnt_ref, first):
    x1 = x + m[2:3] * y
    x_ref[0] = x1
    f = _modulate(x1, gf, m[3:4], m[4:5])
    _store_row_tiles(f_ref, f)
    t = f.shape[0]
    logits = jnp.dot(f, wr_ref[...], preferred_element_type=F32,
                     precision=lax.Precision.HIGHEST) + br_ref[...]
    lane = lax.broadcasted_iota(I32, (t, LANES), 1).astype(F32)
    is_grp = lane < N_EXPERT_GROUPS
    gl = jnp.where(is_grp, logits, NEG_BIG)
    gmax = jnp.max(gl, axis=-1, keepdims=True)
    g_idx = jnp.min(jnp.where(gl == gmax, lane, float(LANES)), axis=-1, keepdims=True)
    g_w = 1.0 / jnp.sum(jnp.where(is_grp, jnp.exp(gl - gmax), 0.0), axis=-1, keepdims=True)
    e_lo = N_EXPERT_GROUPS + EXPERTS_PER_GROUP * g_idx
    in_grp = jnp.logical_and(lane >= e_lo, lane < e_lo + EXPERTS_PER_GROUP)
    sel = jnp.where(in_grp, logits, NEG_BIG)
    v1 = jnp.max(sel, axis=-1, keepdims=True)
    i1 = jnp.min(jnp.where(sel == v1, lane, float(LANES)), axis=-1, keepdims=True)
    sel2 = jnp.where(lane == i1, NEG_BIG, sel)
    v2 = jnp.max(sel2, axis=-1, keepdims=True)
    i2 = jnp.min(jnp.where(sel2 == v2, lane, float(LANES)), axis=-1, keepdims=True)
    e = jnp.exp(v2 - v1)
    w1 = g_w / (1.0 + e)
    w2 = g_w * e / (1.0 + e)
    eid1 = i1 - N_EXPERT_GROUPS
    eid2 = i2 - N_EXPERT_GROUPS

    @pl.when(first)
    def _():
        base_ref[...] = jnp.zeros_like(base_ref)

    hit1 = lane == eid1
    hit2 = lane == eid2
    cnt = (hit1.astype(F32) + hit2.astype(F32))
    row = lax.broadcasted_iota(I32, (t, t), 0)
    col = lax.broadcasted_iota(I32, (t, t), 1)
    tril = jnp.where(col < row, 1.0, 0.0).astype(BF16)
    before = jnp.dot(tril, cnt.astype(BF16), preferred_element_type=F32) + base_ref[0:1, :]
    rank1 = jnp.sum(jnp.where(hit1, before, 0.0), axis=-1, keepdims=True)
    rank2 = jnp.sum(jnp.where(hit2, before, 0.0), axis=-1, keepdims=True)
    total = base_ref[0:1, :] + jnp.sum(cnt, axis=0, keepdims=True)
    base_ref[...] = jnp.broadcast_to(total, base_ref.shape)
    cnt_ref[...] = jnp.broadcast_to(total, cnt_ref.shape)

    l8 = lax.broadcasted_iota(I32, (t, 8), 1)
    ri = jnp.where(l8 == 0, eid1, jnp.where(l8 == 1, eid2, jnp.where(l8 == 2, rank1,
                   jnp.where(l8 == 3, rank2, 0.0))))
    ri_ref[...] = ri.astype(I32)
    rw_ref[...] = jnp.where(l8 == 0, w1, jnp.where(l8 == 1, w2, 0.0))


def _tail_out_shapes(b, n_rows, d):
    t = TOKEN_TILE
    shapes = [jax.ShapeDtypeStruct((b, n_rows, d), F32), jax.ShapeDtypeStruct((b * n_rows * ROW_SUB, LANES), F32),
              jax.ShapeDtypeStruct((b * n_rows, 8), I32), jax.ShapeDtypeStruct((b * n_rows, 8), F32),
              jax.ShapeDtypeStruct((8, LANES), F32)]
    nt = n_rows // t
    flat = lambda w: pl.BlockSpec((t, w), lambda bi, ti: (bi * nt + ti, 0))
    specs = [pl.BlockSpec((1, t, d), lambda bi, ti: (bi, ti, 0)),
             pl.BlockSpec((t * ROW_SUB, LANES), lambda bi, ti: (bi * nt + ti, 0)), flat(8), flat(8),
             pl.BlockSpec((8, LANES), lambda bi, ti: (0, 0))]
    return shapes, specs


def _l0_post_kernel(ctx_ref, x_ref, o_ref, gu_ref, gv_ref, wsp_ref, bsp_ref, wout_ref, mod_ref, gf_ref, wr_ref,
                    br_ref, x1_ref, f_ref, ri_ref, rw_ref, cnt_ref, base_ref, *, nct):
    tq = x_ref.shape[1]
    lane = lax.broadcasted_iota(I32, (GMLP_CHUNK, LANES), 1)
    low = lane < GMLP_GROUP_DIM
    gated = []
    for c in range(tq // GMLP_CHUNK):
        rows = slice(c * GMLP_CHUNK, (c + 1) * GMLP_CHUNK)
        mixed = []
        for m in range(GMLP_WIDTH // LANES):
            pair = gv_ref[0, rows, LANES * m:LANES * (m + 1)]
            a = jnp.dot(wsp_ref[2 * m], pair, preferred_element_type=F32)
            b = jnp.dot(wsp_ref[2 * m + 1], pair, preferred_element_type=F32)
            mixed.append(jnp.where(low, a, b))
        mixed = jnp.concatenate(mixed, axis=1) + bsp_ref[...]
        gated.append((gu_ref[0, rows, :].astype(F32) * mixed).astype(BF16))
    gated = jnp.concatenate(gated, axis=0)
    y = (jnp.dot(o_ref[0], wout_ref[:ATTN_WIDTH, :], preferred_element_type=F32)
         + jnp.dot(gated, wout_ref[ATTN_WIDTH:, :], preferred_element_type=F32))
    first = jnp.logical_and(pl.program_id(0) == 0, pl.program_id(1) == 0)
    _route_tail(_seq_tile(ctx_ref, x_ref, nct), y, mod_ref[0, 0], gf_ref[...], wr_ref, br_ref, base_ref,
                x1_ref, f_ref, ri_ref, rw_ref, cnt_ref, first)


def _l0_post(ctx, x, o, gu, gv, wsp, bsp, wout, mod, gf, wr, br, nct):
    b, s, d = x.shape
    l = s + ctx.shape[1]
    t = TOKEN_TILE
    tok = lambda w: pl.BlockSpec((1, t, w), lambda bi, ti: (bi, ti, 0))
    shapes, specs = _tail_out_shapes(b, l, d)
    return pl.pallas_call(
        functools.partial(_l0_post_kernel, nct=nct),
        out_shape=shapes,
        grid=(b, l // t),
        in_specs=_seq_specs(t, d, nct) + [tok(ATTN_WIDTH), tok(GMLP_WIDTH), tok(GMLP_WIDTH),
                  _full((GMLP_GROUPS, GMLP_CHUNK, GMLP_CHUNK)), _full((GMLP_CHUNK, GMLP_WIDTH)),
                  _full((ATTN_WIDTH + GMLP_WIDTH, d)),
                  pl.BlockSpec((1, 1, 6, d), lambda bi, ti: (bi, jnp.where(ti >= nct, 1, 0), 0, 0)),
                  _full((1, d)), _full((d, LANES)), _full((1, LANES))],
        out_specs=specs,
        scratch_shapes=[pltpu.VMEM((8, LANES), F32)],
        compiler_params=_params("arbitrary", "arbitrary"),
        name="l0_post",
    )(ctx, x, o, gu, gv, wsp, bsp, wout, mod, gf, wr, br)


def _row_copy(src_ref, src_row, dst_ref, dst_row, sem):
    rows = lambda ref, r: ref.at[pl.ds(pl.multiple_of(r * ROW_SUB, ROW_SUB), ROW_SUB)]
    return pltpu.make_async_copy(rows(src_ref, src_row), rows(dst_ref, dst_row), sem)


def _dispatch_kernel(zstart_ref, nu_ref, f_ref, slots_ref, xb_ref, idx_ref, zero_ref, sem_idx, sem_rows):
    i = pl.program_id(0)
    t = f_ref.shape[0] // ROW_SUB
    idx_cp = pltpu.make_async_copy(slots_ref.at[i], idx_ref, sem_idx)
    idx_cp.start()

    @pl.when(i == 0)
    def _():
        zero_ref[...] = jnp.zeros_like(zero_ref)
        blk = MOE_BLOCK * ROW_SUB
        n_blocks = xb_ref.shape[0] // blk

        def zero_block(start):
            return pltpu.make_async_copy(zero_ref, xb_ref.at[pl.ds(pl.multiple_of(start * ROW_SUB, blk), blk)],
                                         sem_rows)

        def tail_start(j, carry):
            zero_block(j * MOE_BLOCK).start()
            return carry

        def tail_wait(j, carry):
            zero_block(0).wait()
            return carry

        for e in range(N_EXPERTS):
            zero_block(zstart_ref[e]).start()
        lax.fori_loop(nu_ref[0], n_blocks, tail_start, 0)
        for e in range(N_EXPERTS):
            zero_block(0).wait()
        lax.fori_loop(nu_ref[0], n_blocks, tail_wait, 0)

    idx_cp.wait()

    def issue(r, carry):
        _row_copy(f_ref, r, xb_ref, idx_ref[2 * r], sem_rows).start(priority=0)
        _row_copy(f_ref, r, xb_ref, idx_ref[2 * r + 1], sem_rows).start(priority=1)
        return carry

    lax.fori_loop(0, t, issue, 0, unroll=ROW_DMA_UNROLL)
    for _ in range(2):
        pltpu.make_async_copy(f_ref, xb_ref.at[pl.ds(0, t * ROW_SUB)], sem_rows).wait()


def _dispatch(f, slots, zstart, n_used, n_slots):
    t = TOKEN_TILE
    n_tok = f.shape[0] // ROW_SUB
    return pl.pallas_call(
        _dispatch_kernel,
        out_shape=jax.ShapeDtypeStruct((n_slots * ROW_SUB, LANES), F32),
        grid_spec=pltpu.PrefetchScalarGridSpec(
            num_scalar_prefetch=2,
            grid=(n_tok // t,),
            in_specs=[pl.BlockSpec((t * ROW_SUB, LANES), lambda i, z, nu: (i, 0)),
                      pl.BlockSpec(memory_space=pltpu.VMEM)],
            out_specs=pl.BlockSpec(memory_space=pl.ANY),
            scratch_shapes=[pltpu.SMEM((2 * t,), I32), pltpu.VMEM((MOE_BLOCK * ROW_SUB, LANES), F32),
                            pltpu.SemaphoreType.DMA, pltpu.SemaphoreType.DMA]),
        compiler_params=_params("arbitrary"),
        name="moe_dispatch",
    )(zstart, n_used, f, slots)


def _experts_kernel(be_ref, nu_ref, xb_ref, w1_ref, w3_ref, w2_ref, y_ref, w1b_ref, w3b_ref, w2b_ref):
    i = pl.program_id(0)

    @pl.when(jnp.logical_or(i == 0, be_ref[i] != be_ref[jnp.maximum(i - 1, 0)]))
    def _():
        w1b_ref[...] = w1_ref[...].astype(BF16)
        w3b_ref[...] = w3_ref[...].astype(BF16)
        w2b_ref[...] = w2_ref[...].astype(BF16)

    @pl.when(i < nu_ref[0])
    def _():
        x = _load_row_tiles(xb_ref).astype(BF16)
        h1 = jnp.dot(x, w1b_ref[...], preferred_element_type=F32)
        h3 = jnp.dot(x, w3b_ref[...], preferred_element_type=F32)
        a = (h1 * jax.nn.sigmoid(h1) * h3).astype(BF16)
        _store_row_tiles(y_ref, jnp.dot(a, w2b_ref[...], preferred_element_type=F32))

    @pl.when(i >= nu_ref[0])
    def _():
        y_ref[...] = jnp.zeros_like(y_ref)


def _experts(xb, w1, w3, w2, layer, block_e, n_used):
    n_slots = xb.shape[0] // ROW_SUB
    _, _, d, hid = w1.shape
    rb = MOE_BLOCK * ROW_SUB
    wspec = lambda s: pl.BlockSpec((None, None) + s, lambda i, be, nu: (layer, be[i], 0, 0))
    return pl.pallas_call(
        _experts_kernel,
        out_shape=jax.ShapeDtypeStruct((n_slots * ROW_SUB, LANES), F32),
        grid_spec=pltpu.PrefetchScalarGridSpec(
            num_scalar_prefetch=2,
            grid=(n_slots // MOE_BLOCK,),
            in_specs=[pl.BlockSpec((rb, LANES), lambda i, be, nu: (jnp.minimum(i, nu[0] - 1), 0)),
                      wspec((d, hid)), wspec((d, hid)), wspec((hid, d))],
            out_specs=pl.BlockSpec((rb, LANES), lambda i, be, nu: (i, 0)),
            scratch_shapes=[pltpu.VMEM((d, hid), BF16), pltpu.VMEM((d, hid), BF16), pltpu.VMEM((hid, d), BF16)]),
        compiler_params=_params("arbitrary"),
        name="moe_experts",
    )(block_e, n_used, xb, w1, w3, w2)


def _moe_plan(route_i, counts, n_tok):
    eid, rank = route_i[:, 0:2], route_i[:, 2:4]
    cnt = counts[0, :N_EXPERTS].astype(I32)
    padded = (cnt + MOE_BLOCK - 1) // MOE_BLOCK * MOE_BLOCK
    pend = jnp.cumsum(padded)
    pstart = pend - padded
    slots = pstart[eid] + rank
    n_assign = 2 * n_tok
    n_slots = (n_assign + MOE_BLOCK - 1) // MOE_BLOCK * MOE_BLOCK + N_EXPERTS * MOE_BLOCK
    n_blocks = n_slots // MOE_BLOCK
    bstart = jnp.arange(n_blocks, dtype=I32) * MOE_BLOCK
    block_e = jnp.minimum(jnp.sum(bstart[:, None] >= pend[None, :], axis=1), N_EXPERTS - 1).astype(I32)
    n_used = pend[-1] // MOE_BLOCK
    last_e = block_e[jnp.maximum(n_used - 1, 0)]
    block_e = jnp.where(jnp.arange(n_blocks) < n_used, block_e, last_e).astype(I32)
    zstart = jnp.maximum(pend - MOE_BLOCK, 0).astype(I32)
    return slots.reshape(n_tok // TOKEN_TILE, 2 * TOKEN_TILE).astype(I32), block_e, n_used.reshape(1).astype(I32), zstart, n_slots


def _moe(f, route_i, counts, w1, w3, w2, layer):
    n_tok = f.shape[0] // ROW_SUB
    slots, block_e, n_used, zstart, n_slots = _moe_plan(route_i, counts, n_tok)
    xb = _dispatch(f, slots, zstart, n_used, n_slots)
    y = _experts(xb, w1, w3, w2, layer, block_e, n_used)
    return y, slots


def _gather_rows(y_ref, slots_ref, idx_ref, ybuf_ref, sem_idx, sem_rows, i):
    t = ybuf_ref.shape[1] // ROW_SUB
    idx_cp = pltpu.make_async_copy(slots_ref.at[i], idx_ref, sem_idx)
    idx_cp.start()
    idx_cp.wait()

    def issue(r, carry):
        _row_copy(y_ref, idx_ref[2 * r], ybuf_ref.at[0], r, sem_rows).start(priority=0)
        _row_copy(y_ref, idx_ref[2 * r + 1], ybuf_ref.at[1], r, sem_rows).start(priority=1)
        return carry

    lax.fori_loop(0, t, issue, 0, unroll=ROW_DMA_UNROLL)
    for k in range(2):
        pltpu.make_async_copy(y_ref.at[pl.ds(0, t * ROW_SUB)], ybuf_ref.at[k], sem_rows).wait()


def _weighted_rows(rw_ref, ybuf_ref):
    rw = rw_ref[...]
    w1, w2 = rw[:, 0:1], rw[:, 1:2]
    t = ybuf_ref.shape[1] // ROW_SUB
    part = lambda k, s: ybuf_ref[k, pl.ds(s, t, stride=ROW_SUB), :]
    return jnp.concatenate([w1 * part(0, s) + w2 * part(1, s) for s in range(ROW_SUB)], axis=1)


def _l0_combine_kernel(x_ref, rw_ref, slots_ref, y_ref, mod0_ref, mod1_ref, g_ref, win_ref,
                       x2_ref, u_ref, idx_ref, ybuf_ref, sem_idx, sem_rows):
    i = pl.program_id(0)
    _gather_rows(y_ref, slots_ref, idx_ref, ybuf_ref, sem_idx, sem_rows, i)
    moe = _weighted_rows(rw_ref, ybuf_ref)
    x2 = x_ref[...] + mod0_ref[0, 0][5:6] * moe
    x2_ref[...] = x2
    m1 = mod1_ref[0, 0]
    h = _modulate(x2, g_ref[...], m1[0:1], m1[1:2]).astype(BF16)
    u_ref[...] = jnp.dot(h, win_ref[...], preferred_element_type=F32)


def _l0_combine(x1, rw, slots, y, mod0, mod1, g, win, b, nct):
    n_tok, d = x1.shape
    t = TOKEN_TILE
    nt = n_tok // b // t
    l = n_tok // b
    modspec = pl.BlockSpec((1, 1, 6, d), lambda i: (i // nt, jnp.where(i % nt >= nct, 1, 0), 0, 0))
    return pl.pallas_call(
        _l0_combine_kernel,
        out_shape=[jax.ShapeDtypeStruct((n_tok, d), F32), jax.ShapeDtypeStruct((l, b * S5_WIDTH), F32)],
        grid=(n_tok // t,),
        in_specs=[pl.BlockSpec((t, d), lambda i: (i, 0)), pl.BlockSpec((t, 8), lambda i: (i, 0)),
                  pl.BlockSpec(memory_space=pltpu.VMEM), pl.BlockSpec(memory_space=pl.ANY),
                  modspec, modspec, _full((1, d)), _full((d, S5_WIDTH))],
        out_specs=[pl.BlockSpec((t, d), lambda i: (i, 0)),
                   pl.BlockSpec((t, S5_WIDTH), lambda i: (i % nt, i // nt))],
        scratch_shapes=[pltpu.SMEM((2 * t,), I32), pltpu.VMEM((2, t * ROW_SUB, LANES), F32),
                        pltpu.SemaphoreType.DMA, pltpu.SemaphoreType.DMA],
        compiler_params=_params("arbitrary"),
        name="l0_combine",
    )(x1, rw, slots, y, mod0, mod1, g, win)


def _s5_zoh_kernel(are_ref, aim_ref, ldt_ref, bre_ref, bim_ref, lre_ref, lim_ref, bbre_ref, bbim_ref):
    ar, ai = are_ref[...], aim_ref[...]
    dt = jnp.exp(ldt_ref[...])
    mag = jnp.exp(ar * dt)
    lr = mag * jnp.cos(ai * dt)
    li = mag * jnp.sin(ai * dt)
    lre_ref[...] = lr
    lim_ref[...] = li
    nr, ni = lr - 1.0, li
    den = ar * ar + ai * ai
    cr = (nr * ar + ni * ai) / den
    ci = (ni * ar - nr * ai) / den
    br, bi = bre_ref[...], bim_ref[...]
    bbre_ref[...] = cr * br - ci * bi
    bbim_ref[...] = cr * bi + ci * br


def _s5_zoh(a_re, a_im, log_dt, b_re, b_im):
    n_dir, g, n, p = b_re.shape
    rows = n_dir * g * p
    expand = lambda a: jnp.broadcast_to(a[:, :, None, :], (n_dir, g, p, n)).reshape(rows, n)
    ldt = jnp.broadcast_to(log_dt[:, :, None, None], (n_dir, g, p, n)).reshape(rows, n)
    bt = lambda a: jnp.transpose(a, (0, 1, 3, 2)).reshape(rows, n)
    outs = pl.pallas_call(
        _s5_zoh_kernel,
        out_shape=[jax.ShapeDtypeStruct((rows, n), F32)] * 4,
        name="s5_zoh",
    )(expand(a_re), expand(a_im), ldt, bt(b_re), bt(b_im))
    lre, lim, bbre, bbim = (o.reshape(n_dir, g, p, n) for o in outs)
    return lre[:, :, 0, :], lim[:, :, 0, :], bbre, bbim


def _s5_matrices(lre, lim, bbre, bbim, c_re, c_im):
    n_dir = lre.shape[0]
    nj = S5_GROUPS // (2 * S5_TILES)
    shp = (n_dir, nj, S5_TILES, 2, S5_GROUP_DIM, S5_STATE)
    eye_t = jnp.eye(S5_TILES, dtype=F32)
    eye_g = jnp.eye(2, dtype=F32)

    def place(a):
        return jnp.einsum("djtgpn,tu,gh->djtgpuhn", a.reshape(shp), eye_t, eye_g)

    kdim = S5_TILES * 2 * S5_GROUP_DIM
    sdim = S5_TILES * 2 * 2 * S5_STATE
    bm = jnp.stack([place(bbre), place(bbim)], axis=6)
    bm = bm.reshape(n_dir, nj, kdim, sdim).astype(BF16)
    cm = jnp.stack([place(c_re), place(-c_im)], axis=6)
    cm = jnp.transpose(cm, (0, 1, 5, 6, 7, 8, 2, 3, 4)).reshape(n_dir, nj, sdim, kdim).astype(BF16)
    lam = jnp.stack([lre.reshape(n_dir, nj, S5_TILES * LANES), lim.reshape(n_dir, nj, S5_TILES * LANES)], axis=2)
    return bm, cm, lam


def _s5_scan_kernel(uf_ref, ub_ref, bm_ref, cm_ref, lam_ref, yf_ref, yb_ref, v_ref, h_ref, *, tc, nb):
    @pl.when(pl.program_id(0) == 0)
    def _():
        h_ref[...] = jnp.zeros_like(h_ref)

    nj = bm_ref.shape[1]
    kdim = bm_ref.shape[2]
    for d, (u_ref, y_ref) in enumerate(((uf_ref, yf_ref), (ub_ref, yb_ref))):
        for jq in range(nj):
            uk = u_ref[:, kdim * jq:kdim * (jq + 1)].astype(BF16)
            v_ref[...] = jnp.dot(uk, bm_ref[d, jq], preferred_element_type=F32)
            lam = lam_ref[d, jq]
            lr = [jnp.broadcast_to(lam[0:1, LANES * t:LANES * (t + 1)], (nb, LANES)) for t in range(S5_TILES)]
            li = [jnp.broadcast_to(lam[1:2, LANES * t:LANES * (t + 1)], (nb, LANES)) for t in range(S5_TILES)]
            h0 = h_ref[d, jq]
            init = tuple(h0[:, LANES * c:LANES * (c + 1)] for c in range(2 * S5_TILES))

            def step(s, carry, d=d, lr=lr, li=li):
                tt = s if d == 0 else tc - 1 - s
                r0 = pl.multiple_of(tt * nb, nb)
                rows = v_ref[pl.ds(r0, nb), :]
                new = []
                for t in range(S5_TILES):
                    vr = rows[:, 2 * LANES * t:2 * LANES * t + LANES]
                    vi = rows[:, 2 * LANES * t + LANES:2 * LANES * (t + 1)]
                    hr, hi = carry[2 * t], carry[2 * t + 1]
                    new.append(lr[t] * hr - li[t] * hi + vr)
                    new.append(lr[t] * hi + li[t] * hr + vi)
                v_ref[pl.ds(r0, nb), :] = jnp.concatenate(new, axis=1)
                return tuple(new)

            fin = lax.fori_loop(0, tc, step, init)
            h_ref[d, jq] = jnp.concatenate(fin, axis=1)
            y_ref[:, kdim * jq:kdim * (jq + 1)] = jnp.dot(v_ref[...].astype(BF16), cm_ref[d, jq],
                                                         preferred_element_type=F32)


def _s5_scan(u_tm, bm, cm, lam, nb, n_ctx):
    rows_total, w = u_tm.shape
    tc = S5_CHUNK
    r = tc * nb
    n_chunks = rows_total // r
    ncc = n_ctx // tc
    bwd = lambda i: (jnp.where(i < ncc, ncc - 1 - i, n_chunks - 1 - (i - ncc)), 0)
    nj = bm.shape[1]
    sdim = bm.shape[3]
    kern = functools.partial(_s5_scan_kernel, tc=tc, nb=nb)
    return pl.pallas_call(
        kern,
        out_shape=[jax.ShapeDtypeStruct((rows_total, w), F32)] * 2,
        grid=(n_chunks,),
        in_specs=[pl.BlockSpec((r, w), lambda i: (i, 0)), pl.BlockSpec((r, w), bwd),
                  _full(bm.shape), _full(cm.shape), _full(lam.shape)],
        out_specs=[pl.BlockSpec((r, w), lambda i: (i, 0)), pl.BlockSpec((r, w), bwd)],
        scratch_shapes=[pltpu.VMEM((r, sdim), F32), pltpu.VMEM((2, nj, nb, sdim), F32)],
        compiler_params=_params("arbitrary"),
        name="s5_scan",
    )(u_tm, u_tm, bm, cm, lam)


def _l1_post_kernel(x_ref, u_ref, yf_ref, yb_ref, dsk_ref, wglu_ref, bglu_ref, wout_ref, mod_ref, gf_ref,
                    wr_ref, br_ref, x3_ref, f_ref, ri_ref, rw_ref, cnt_ref, base_ref):
    y = dsk_ref[...] * u_ref[...] + yf_ref[...] + yb_ref[...]
    g = jax.nn.gelu(y, approximate=True)
    z = jnp.dot(g.astype(BF16), wglu_ref[...], preferred_element_type=F32) + bglu_ref[...]
    o = jnp.dot((g * jax.nn.sigmoid(z)).astype(BF16), wout_ref[...], preferred_element_type=F32)
    first = jnp.logical_and(pl.program_id(0) == 0, pl.program_id(1) == 0)
    _route_tail(x_ref[0], o, mod_ref[0, 0], gf_ref[...], wr_ref, br_ref, base_ref,
                x3_ref, f_ref, ri_ref, rw_ref, cnt_ref, first)


def _l1_post(x2, u_tm, yf, yb, dsk, wglu, bglu, wout, mod, gf, wr, br, n_ctx):
    b, l, d = x2.shape
    t = TOKEN_TILE
    nct = n_ctx // t
    s = l - n_ctx
    tm = pl.BlockSpec((t, S5_WIDTH), lambda bi, ti: (ti + nct, bi))
    shapes, specs = _tail_out_shapes(b, s, d)
    return pl.pallas_call(
        _l1_post_kernel,
        out_shape=shapes,
        grid=(b, s // t),
        in_specs=[pl.BlockSpec((1, t, d), lambda bi, ti: (bi, ti + nct, 0)), tm, tm, tm,
                  _full((1, S5_WIDTH)), _full((S5_WIDTH, S5_WIDTH)), _full((1, S5_WIDTH)), _full((S5_WIDTH, d)),
                  pl.BlockSpec((1, 1, 6, d), lambda bi, ti: (bi, 1, 0, 0)),
                  _full((1, d)), _full((d, LANES)), _full((1, LANES))],
        out_specs=specs,
        scratch_shapes=[pltpu.VMEM((8, LANES), F32)],
        compiler_params=_params("arbitrary", "arbitrary"),
        name="l1_post",
    )(x2, u_tm, yf, yb, dsk, wglu, bglu, wout, mod, gf, wr, br)


def _final_kernel(x_ref, rw_ref, slots_ref, y_ref, mod_ref, g_ref, o_ref, idx_ref, ybuf_ref, sem_idx, sem_rows):
    i = pl.program_id(0)
    _gather_rows(y_ref, slots_ref, idx_ref, ybuf_ref, sem_idx, sem_rows, i)
    moe = _weighted_rows(rw_ref, ybuf_ref)
    x4 = x_ref[...] + mod_ref[0, 0][5:6] * moe
    o_ref[...] = _rms(x4) * g_ref[...]


def _final(x3, rw, slots, y, mod, g, b):
    n_tok, d = x3.shape
    t = TOKEN_TILE
    nt = n_tok // b // t
    return pl.pallas_call(
        _final_kernel,
        out_shape=jax.ShapeDtypeStruct((n_tok, d), F32),
        grid=(n_tok // t,),
        in_specs=[pl.BlockSpec((t, d), lambda i: (i, 0)), pl.BlockSpec((t, 8), lambda i: (i, 0)),
                  pl.BlockSpec(memory_space=pltpu.VMEM), pl.BlockSpec(memory_space=pl.ANY),
                  pl.BlockSpec((1, 1, 6, d), lambda i: (i // nt, 1, 0, 0)), _full((1, d))],
        out_specs=pl.BlockSpec((t, d), lambda i: (i, 0)),
        scratch_shapes=[pltpu.SMEM((2 * t,), I32), pltpu.VMEM((2, t * ROW_SUB, LANES), F32),
                        pltpu.SemaphoreType.DMA, pltpu.SemaphoreType.DMA],
        compiler_params=_params("arbitrary"),
        name="final_combine",
    )(x3, rw, slots, y, mod, g)


def _rope_tables(n_ctx, n_lat):
    pos = jnp.arange(n_lat, dtype=I32)
    row = (pos // GRID_W).astype(F32)
    col = (pos % GRID_W).astype(F32)
    inv_freq = jnp.power(ROPE_THETA, -jnp.arange(0, AXIS_DIM, 2, dtype=F32) / AXIS_DIM)
    lane = np.arange(LANES)
    dd = lane % HEAD_DIM
    use_col = dd >= AXIS_DIM
    first = (dd % AXIS_DIM) < AXIS_DIM // 2
    freq = inv_freq[(dd % AXIS_DIM) % (AXIS_DIM // 2)]
    ang = jnp.where(use_col[None, :], col[:, None], row[:, None]) * freq[None, :]
    cos, sin = jnp.cos(ang), jnp.sin(ang)
    sa = jnp.where(first[None, :], -sin, 0.0)
    sb = jnp.where(first[None, :], 0.0, sin)
    pad = lambda a, v: jnp.concatenate([jnp.full((n_ctx, LANES), v, F32), a], axis=0)
    return pad(cos, 1.0), pad(sa, 0.0), pad(sb, 0.0)


def _group_ones(width, group):
    idx = np.arange(width) // group
    return jnp.asarray(idx[:, None] == idx[None, :], dtype=BF16)


def _router_weights(w_grp, b_grp, w_rt, b_rt):
    d = w_grp.shape[0]
    used = N_EXPERT_GROUPS + N_EXPERTS
    wr = jnp.concatenate([w_grp, w_rt, jnp.zeros((d, LANES - used), F32)], axis=1)
    br = jnp.concatenate([b_grp, b_rt, jnp.zeros((LANES - used,), F32)]).reshape(1, LANES)
    return wr, br


def kernel(x, c, ctx, c_ctx, w_mod, b_mod, norm_mix_g, norm_ffn_g, mix_w_in, mix_w_out, q_norm_g, k_norm_g, gmlp_norm_g, gmlp_w_spatial, gmlp_b_spatial, s5_w_in, s5_a_re, s5_a_im, s5_log_dt, s5_b_re, s5_b_im, s5_c_re, s5_c_im, s5_d, s5_w_glu, s5_b_glu, s5_w_out, moe_w_group, moe_b_group, moe_w_router, moe_b_router, moe_w1, moe_w3, moe_w2, final_norm_g):
    b, s, d = x.shape
    n_ctx = ctx.shape[1]
    l = n_ctx + s
    assert w_mod.shape[0] == 2, "two layers: attention/gMLP then S5"
    assert n_ctx % TOKEN_TILE == 0 and s % TOKEN_TILE == 0 and b % 8 == 0 and d == D_MODEL
    nct = n_ctx // TOKEN_TILE

    r = (b + 1 + 7) // 8 * 8
    cond = jnp.concatenate([c, c_ctx[None, :], jnp.zeros((r - b - 1, d), F32)], axis=0)
    mods = _adaln(cond, w_mod, b_mod)
    lat = mods[:, :b].reshape(2, b, 1, 6, d)
    cxt = jnp.broadcast_to(mods[:, b].reshape(2, 1, 1, 6, d), (2, b, 1, 6, d))
    mod = jnp.concatenate([cxt, lat], axis=2)

    w_in = jnp.concatenate([mix_w_in[0][:, _Q_COL_PERM], mix_w_in[0][:, ATTN_WIDTH:]], axis=1).astype(BF16)
    w_out = jnp.concatenate([mix_w_out[0][_Q_COL_PERM], mix_w_out[0][ATTN_WIDTH:]], axis=0).astype(BF16)
    cos, sa, sb = _rope_tables(n_ctx, s)
    qg = jnp.tile(q_norm_g[0], N_Q_HEADS).reshape(1, ATTN_WIDTH)
    kg = jnp.tile(k_norm_g[0], N_KV_HEADS).reshape(1, KV_WIDTH)
    q, k, v, gu, gv = _l0_proj(ctx, x, mod[0], norm_mix_g[0].reshape(1, d), w_in, qg, kg,
                               gmlp_norm_g[0].reshape(1, GMLP_WIDTH), cos, sa, sb,
                               _group_ones(ATTN_WIDTH, HEAD_DIM), _group_ones(KV_WIDTH, HEAD_DIM), nct)
    o = _attention(q, k, v, n_ctx)
    bsp = jnp.repeat(gmlp_b_spatial[0].T, GMLP_GROUP_DIM, axis=1)
    wr0, br0 = _router_weights(moe_w_group[0], moe_b_group[0], moe_w_router[0], moe_b_router[0])
    x1, f0, ri0, rw0, cnt0 = _l0_post(ctx, x, o, gu, gv, gmlp_w_spatial[0].astype(BF16), bsp, w_out, mod[0],
                                      norm_ffn_g[0].reshape(1, d), wr0, br0, nct)
    y0, slots0 = _moe(f0, ri0, cnt0, moe_w1, moe_w3, moe_w2, 0)
    x2, u_tm = _l0_combine(x1.reshape(b * l, d), rw0, slots0, y0, mod[0], mod[1], norm_mix_g[1].reshape(1, d),
                           s5_w_in[0].astype(BF16), b, nct)

    lre, lim, bbre, bbim = _s5_zoh(s5_a_re[0], s5_a_im[0], s5_log_dt[0], s5_b_re[0], s5_b_im[0])
    bm, cm, lam = _s5_matrices(lre, lim, bbre, bbim, s5_c_re[0], s5_c_im[0])
    yf, yb = _s5_scan(u_tm.reshape(l * b, S5_WIDTH), bm, cm, lam, b, n_ctx)
    wr1, br1 = _router_weights(moe_w_group[1], moe_b_group[1], moe_w_router[1], moe_b_router[1])
    tmv = lambda a: a.reshape(l, b * S5_WIDTH)
    x3, f1, ri1, rw1, cnt1 = _l1_post(x2.reshape(b, l, d), u_tm, tmv(yf), tmv(yb), s5_d[0].reshape(1, S5_WIDTH),
                                      s5_w_glu[0].astype(BF16), s5_b_glu[0].reshape(1, S5_WIDTH),
                                      s5_w_out[0].astype(BF16), mod[1], norm_ffn_g[1].reshape(1, d), wr1, br1, n_ctx)
    y1, slots1 = _moe(f1, ri1, cnt1, moe_w1, moe_w3, moe_w2, 1)
    out = _final(x3.reshape(b * s, d), rw1, slots1, y1, mod[1], final_norm_g.reshape(1, d), b)
    return out.reshape(b, s, d)
```

```python
import functools

import numpy as np
import jax
import jax.numpy as jnp
from jax import lax
from jax.experimental import pallas as pl
from jax.experimental.pallas import tpu as pltpu

F32 = jnp.float32
BF16 = jnp.bfloat16
I32 = jnp.int32

EPS = 1e-6
GRID_W = 64
HEAD_DIM = 64
N_Q_HEADS = 8
N_KV_HEADS = 2
ATTN_WIDTH = N_Q_HEADS * HEAD_DIM
KV_WIDTH = N_KV_HEADS * HEAD_DIM
AXIS_DIM = HEAD_DIM // 2
ROPE_THETA = 10000.0
GMLP_GROUPS = 8
GMLP_GROUP_DIM = 64
GMLP_WIDTH = GMLP_GROUPS * GMLP_GROUP_DIM
GMLP_CHUNK = 128
MIX_IN_WIDTH = ATTN_WIDTH + 2 * KV_WIDTH + 2 * GMLP_WIDTH
S5_GROUPS = 32
S5_GROUP_DIM = 16
S5_STATE = 64
S5_WIDTH = S5_GROUPS * S5_GROUP_DIM
N_EXPERT_GROUPS = 4
EXPERTS_PER_GROUP = 8
N_EXPERTS = N_EXPERT_GROUPS * EXPERTS_PER_GROUP
EXPERT_HIDDEN = 512
MOE_BLOCK = 256

D_MODEL = 1024
LANES = 128
ROW_SUB = D_MODEL // LANES
TOKEN_TILE = 256
ROW_DMA_UNROLL = 8
S5_CHUNK = 64
S5_TILES = 4
VMEM_LIMIT = 48 * 1024 * 1024
NEG_BIG = -1e30

_Q_HEAD_ORDER = (0, 4, 1, 5, 2, 6, 3, 7)
_Q_COL_PERM = np.concatenate([np.arange(HEAD_DIM) + HEAD_DIM * h for h in _Q_HEAD_ORDER])


def _params(*sem):
    return pltpu.CompilerParams(dimension_semantics=sem, vmem_limit_bytes=VMEM_LIMIT)


def _rms(x):
    return x * lax.rsqrt(jnp.mean(x * x, axis=-1, keepdims=True) + EPS)


def _modulate(x, g, shift, scale):
    return _rms(x) * g * (1.0 + scale) + shift


def _full(shape):
    return pl.BlockSpec(shape, lambda *_: (0,) * len(shape))


def _split_bf16(a):
    hi = a.astype(BF16)
    return hi, (a - hi.astype(F32)).astype(BF16)


def _dot_split(a, w_hi, w_lo):
    a_hi, a_lo = _split_bf16(a)
    dot = functools.partial(jnp.dot, preferred_element_type=F32)
    return dot(a_hi, w_hi) + (dot(a_lo, w_hi) + dot(a_hi, w_lo))


def _store_row_tiles(ref, val):
    n = val.shape[0]
    for s in range(ROW_SUB):
        ref[pl.ds(s, n, stride=ROW_SUB), :] = val[:, LANES * s:LANES * (s + 1)]


def _load_row_tiles(ref):
    n = ref.shape[0] // ROW_SUB
    return jnp.concatenate([ref[pl.ds(s, n, stride=ROW_SUB), :] for s in range(ROW_SUB)], axis=1)


def _adaln_kernel(cond_ref, w_ref, b_ref, o_ref):
    c = cond_ref[...]
    s = c * jax.nn.sigmoid(c)
    o_ref[0] = _dot_split(s, *_split_bf16(w_ref[0])) + b_ref[0]


def _adaln(cond, w_mod, b_mod):
    depth, d, n = w_mod.shape
    r = cond.shape[0]
    tn = 1536
    return pl.pallas_call(
        _adaln_kernel,
        out_shape=jax.ShapeDtypeStruct((depth, r, n), F32),
        grid=(depth, n // tn),
        in_specs=[pl.BlockSpec((r, d), lambda l, j: (0, 0)),
                  pl.BlockSpec((1, d, tn), lambda l, j: (l, 0, j)),
                  pl.BlockSpec((1, 1, tn), lambda l, j: (l, 0, j))],
        out_specs=pl.BlockSpec((1, r, tn), lambda l, j: (l, 0, j)),
        compiler_params=_params("parallel", "parallel"),
        name="adaln",
    )(cond, w_mod, b_mod.reshape(depth, 1, n))


def _seq_tile(ctx_ref, x_ref, nct):
    return jnp.where(pl.program_id(1) < nct, ctx_ref[0], x_ref[0])


def _seq_specs(t, d, nct):
    return [pl.BlockSpec((1, t, d), lambda bi, ti: (bi, jnp.minimum(ti, nct - 1), 0)),
            pl.BlockSpec((1, t, d), lambda bi, ti: (bi, jnp.maximum(ti - nct, 0), 0))]


def _l0_proj_kernel(ctx_ref, x_ref, mod_ref, g_ref, win_ref, qg_ref, kg_ref, lng_ref, cos_ref, sa_ref, sb_ref,
                    gq_ref, gk_ref, q_ref, k_ref, v_ref, gu_ref, gv_ref, *, nct):
    m = mod_ref[0, 0]
    h = _modulate(_seq_tile(ctx_ref, x_ref, nct), g_ref[...], m[0:1], m[1:2]).astype(BF16)
    z = jnp.dot(h, win_ref[...], preferred_element_type=F32)
    cos, sa, sb = cos_ref[...], sa_ref[...], sb_ref[...]

    def head_norm_rope(t, gsum_ref, gain, reps):
        w = t.shape[1]
        ss = jnp.dot((t * t).astype(BF16), gsum_ref[...], preferred_element_type=F32)
        tn = t * lax.rsqrt(ss * (1.0 / HEAD_DIM) + EPS) * gain
        c, a, b = (jnp.concatenate([tab] * reps, axis=1) if reps > 1 else tab for tab in (cos, sa, sb))
        return tn * c + pltpu.roll(tn, w - AXIS_DIM // 2, 1) * a + pltpu.roll(tn, AXIS_DIM // 2, 1) * b

    q = head_norm_rope(z[:, :ATTN_WIDTH], gq_ref, qg_ref[...], ATTN_WIDTH // LANES)
    q_ref[0] = (q * (HEAD_DIM ** -0.5)).astype(BF16)
    k = head_norm_rope(z[:, ATTN_WIDTH:ATTN_WIDTH + KV_WIDTH], gk_ref, kg_ref[...], 1)
    k_ref[0] = k.astype(BF16)
    v_ref[0] = z[:, ATTN_WIDTH + KV_WIDTH:ATTN_WIDTH + 2 * KV_WIDTH].astype(BF16)
    off = ATTN_WIDTH + 2 * KV_WIDTH
    gu_ref[0] = jax.nn.gelu(z[:, off:off + GMLP_WIDTH], approximate=True).astype(BF16)
    gv = jax.nn.gelu(z[:, off + GMLP_WIDTH:], approximate=True)
    mu = jnp.mean(gv, axis=-1, keepdims=True)
    gc = gv - mu
    gn = gc * lax.rsqrt(jnp.mean(gc * gc, axis=-1, keepdims=True) + EPS) * lng_ref[...]
    gv_ref[0] = gn.astype(BF16)


def _l0_proj(ctx, x, mod, g, w_in, qg, kg, lng, cos, sa, sb, gq, gk, nct):
    b, s, d = x.shape
    l = s + ctx.shape[1]
    t = TOKEN_TILE
    tok = lambda w: pl.BlockSpec((1, t, w), lambda bi, ti: (bi, ti, 0))
    tab = pl.BlockSpec((t, LANES), lambda bi, ti: (ti, 0))
    outs = [jax.ShapeDtypeStruct((b, l, w), BF16) for w in (ATTN_WIDTH, KV_WIDTH, KV_WIDTH, GMLP_WIDTH, GMLP_WIDTH)]
    return pl.pallas_call(
        functools.partial(_l0_proj_kernel, nct=nct),
        out_shape=outs,
        grid=(b, l // t),
        in_specs=_seq_specs(t, d, nct) + [
                  pl.BlockSpec((1, 1, 6, d), lambda bi, ti: (bi, jnp.where(ti >= nct, 1, 0), 0, 0)),
                  _full((1, d)), _full((d, MIX_IN_WIDTH)), _full((1, ATTN_WIDTH)), _full((1, KV_WIDTH)),
                  _full((1, GMLP_WIDTH)), tab, tab, tab,
                  _full((ATTN_WIDTH, ATTN_WIDTH)), _full((KV_WIDTH, KV_WIDTH))],
        out_specs=[tok(w) for w in (ATTN_WIDTH, KV_WIDTH, KV_WIDTH, GMLP_WIDTH, GMLP_WIDTH)],
        compiler_params=_params("parallel", "parallel"),
        name="l0_proj",
    )(ctx, x, mod, g, w_in, qg, kg, lng, cos, sa, sb, gq, gk)


def _attn_kernel(q_ref, k_ref, v_ref, o_ref, *, n_ctx, nct):
    ti = pl.program_id(1)
    tq = q_ref.shape[1]
    lane = lax.broadcasted_iota(I32, (tq, LANES), 1)
    low = lane < HEAD_DIM

    def attend(nk):
        kk = k_ref[0, :nk, :]
        vv = v_ref[0, :nk, :]
        for m in range(ATTN_WIDTH // LANES):
            qp = q_ref[0, :, LANES * m:LANES * (m + 1)]
            halves = []
            for sel in (low, jnp.logical_not(low)):
                qm = jnp.where(sel, qp, jnp.zeros_like(qp))
                s = lax.dot_general(qm, kk, (((1,), (1,)), ((), ())), preferred_element_type=F32)
                p = jnp.exp(s - jnp.max(s, axis=-1, keepdims=True))
                den = jnp.sum(p, axis=-1, keepdims=True)
                pv = jnp.dot(p.astype(BF16), vv, preferred_element_type=F32)
                halves.append(pv / den)
            o_ref[0, :, LANES * m:LANES * (m + 1)] = jnp.where(low, halves[0], halves[1]).astype(BF16)

    @pl.when(ti < nct)
    def _():
        attend(n_ctx)

    @pl.when(ti >= nct)
    def _():
        attend(k_ref.shape[1])


def _attention(q, k, v, n_ctx):
    b, l, _ = q.shape
    t = TOKEN_TILE
    kern = functools.partial(_attn_kernel, n_ctx=n_ctx, nct=n_ctx // t)
    return pl.pallas_call(
        kern,
        out_shape=jax.ShapeDtypeStruct((b, l, ATTN_WIDTH), BF16),
        grid=(b, l // t),
        in_specs=[pl.BlockSpec((1, t, ATTN_WIDTH), lambda bi, ti: (bi, ti, 0)),
                  pl.BlockSpec((1, l, KV_WIDTH), lambda bi, ti: (bi, 0, 0)),
                  pl.BlockSpec((1, l, KV_WIDTH), lambda bi, ti: (bi, 0, 0))],
        out_specs=pl.BlockSpec((1, t, ATTN_WIDTH), lambda bi, ti: (bi, ti, 0)),
        compiler_params=_params("parallel", "parallel"),
        name="attention",
    )(q, k, v)


def _route_tail(x, y, m, gf, wr_ref, br_ref, base_ref, x_ref, f_ref, ri_ref, rw_ref, cnt_ref, first):
    x1 = x + m[2:3] * y
    x_ref[0] = x1
    f = _modulate(x1, gf, m[3:4], m[4:5])
    _store_row_tiles(f_ref, f)
    t = f.shape[0]
    logits = _dot_split(f, wr_ref[0], wr_ref[1]) + br_ref[...]
    lane = lax.broadcasted_iota(I32, (t, LANES), 1).astype(F32)
    is_grp = lane < N_EXPERT_GROUPS
    gl = jnp.where(is_grp, logits, NEG_BIG)
    gmax = jnp.max(gl, axis=-1, keepdims=True)
    g_idx = jnp.min(jnp.where(gl == gmax, lane, float(LANES)), axis=-1, keepdims=True)
    g_w = 1.0 / jnp.sum(jnp.where(is_grp, jnp.exp(gl - gmax), 0.0), axis=-1, keepdims=True)
    e_lo = N_EXPERT_GROUPS + EXPERTS_PER_GROUP * g_idx
    in_grp = jnp.logical_and(lane >= e_lo, lane < e_lo + EXPERTS_PER_GROUP)
    sel = jnp.where(in_grp, logits, NEG_BIG)
    v1 = jnp.max(sel, axis=-1, keepdims=True)
    i1 = jnp.min(jnp.where(sel == v1, lane, float(LANES)), axis=-1, keepdims=True)
    sel2 = jnp.where(lane == i1, NEG_BIG, sel)
    v2 = jnp.max(sel2, axis=-1, keepdims=True)
    i2 = jnp.min(jnp.where(sel2 == v2, lane, float(LANES)), axis=-1, keepdims=True)
    e = jnp.exp(v2 - v1)
    w1 = g_w / (1.0 + e)
    w2 = g_w * e / (1.0 + e)
    eid1 = i1 - N_EXPERT_GROUPS
    eid2 = i2 - N_EXPERT_GROUPS

    @pl.when(first)
    def _():
        base_ref[...] = jnp.zeros_like(base_ref)

    hit1 = lane == eid1
    hit2 = lane == eid2
    cnt = (hit1.astype(F32) + hit2.astype(F32))
    row = lax.broadcasted_iota(I32, (t, t), 0)
    col = lax.broadcasted_iota(I32, (t, t), 1)
    tril = jnp.where(col < row, 1.0, 0.0).astype(BF16)
    before = jnp.dot(tril, cnt.astype(BF16), preferred_element_type=F32) + base_ref[0:1, :]
    rank1 = jnp.sum(jnp.where(hit1, before, 0.0), axis=-1, keepdims=True)
    rank2 = jnp.sum(jnp.where(hit2, before, 0.0), axis=-1, keepdims=True)
    total = base_ref[0:1, :] + jnp.sum(cnt, axis=0, keepdims=True)
    base_ref[...] = jnp.broadcast_to(total, base_ref.shape)
    cnt_ref[...] = jnp.broadcast_to(total, cnt_ref.shape)

    ri = jnp.where(lane == 0, eid1, jnp.where(lane == 1, eid2, jnp.where(lane == 2, rank1,
                   jnp.where(lane == 3, rank2, 0.0))))
    ri_ref[...] = ri.T[0:8, :].astype(I32)
    l8 = lax.broadcasted_iota(I32, (t, 8), 1)
    rw_ref[...] = jnp.where(l8 == 0, w1, jnp.where(l8 == 1, w2, 0.0))


def _tail_out_shapes(b, n_rows, d):
    t = TOKEN_TILE
    shapes = [jax.ShapeDtypeStruct((b, n_rows, d), F32), jax.ShapeDtypeStruct((b * n_rows * ROW_SUB, LANES), F32),
              jax.ShapeDtypeStruct((8, b * n_rows), I32), jax.ShapeDtypeStruct((b * n_rows, 8), F32),
              jax.ShapeDtypeStruct((8, LANES), F32)]
    nt = n_rows // t
    flat = lambda w: pl.BlockSpec((t, w), lambda bi, ti: (bi * nt + ti, 0))
    specs = [pl.BlockSpec((1, t, d), lambda bi, ti: (bi, ti, 0)),
             pl.BlockSpec((t * ROW_SUB, LANES), lambda bi, ti: (bi * nt + ti, 0)),
             pl.BlockSpec((8, t), lambda bi, ti: (0, bi * nt + ti)), flat(8),
             pl.BlockSpec((8, LANES), lambda bi, ti: (0, 0))]
    return shapes, specs


def _l0_post_kernel(ctx_ref, x_ref, o_ref, gu_ref, gv_ref, wsp_ref, bsp_ref, wout_ref, mod_ref, gf_ref, wr_ref,
                    br_ref, x1_ref, f_ref, ri_ref, rw_ref, cnt_ref, base_ref, *, nct):
    tq = x_ref.shape[1]
    lane = lax.broadcasted_iota(I32, (GMLP_CHUNK, LANES), 1)
    low = lane < GMLP_GROUP_DIM
    gated = []
    for c in range(tq // GMLP_CHUNK):
        rows = slice(c * GMLP_CHUNK, (c + 1) * GMLP_CHUNK)
        mixed = []
        for m in range(GMLP_WIDTH // LANES):
            pair = gv_ref[0, rows, LANES * m:LANES * (m + 1)]
            a = jnp.dot(wsp_ref[2 * m], pair, preferred_element_type=F32)
            b = jnp.dot(wsp_ref[2 * m + 1], pair, preferred_element_type=F32)
            mixed.append(jnp.where(low, a, b))
        mixed = jnp.concatenate(mixed, axis=1) + bsp_ref[...]
        gated.append((gu_ref[0, rows, :].astype(F32) * mixed).astype(BF16))
    gated = jnp.concatenate(gated, axis=0)
    y = (jnp.dot(o_ref[0], wout_ref[:ATTN_WIDTH, :], preferred_element_type=F32)
         + jnp.dot(gated, wout_ref[ATTN_WIDTH:, :], preferred_element_type=F32))
    first = jnp.logical_and(pl.program_id(0) == 0, pl.program_id(1) == 0)
    _route_tail(_seq_tile(ctx_ref, x_ref, nct), y, mod_ref[0, 0], gf_ref[...], wr_ref, br_ref, base_ref,
                x1_ref, f_ref, ri_ref, rw_ref, cnt_ref, first)


def _l0_post(ctx, x, o, gu, gv, wsp, bsp, wout, mod, gf, wr, br, nct):
    b, s, d = x.shape
    l = s + ctx.shape[1]
    t = TOKEN_TILE
    tok = lambda w: pl.BlockSpec((1, t, w), lambda bi, ti: (bi, ti, 0))
    shapes, specs = _tail_out_shapes(b, l, d)
    return pl.pallas_call(
        functools.partial(_l0_post_kernel, nct=nct),
        out_shape=shapes,
        grid=(b, l // t),
        in_specs=_seq_specs(t, d, nct) + [tok(ATTN_WIDTH), tok(GMLP_WIDTH), tok(GMLP_WIDTH),
                  _full((GMLP_GROUPS, GMLP_CHUNK, GMLP_CHUNK)), _full((GMLP_CHUNK, GMLP_WIDTH)),
                  _full((ATTN_WIDTH + GMLP_WIDTH, d)),
                  pl.BlockSpec((1, 1, 6, d), lambda bi, ti: (bi, jnp.where(ti >= nct, 1, 0), 0, 0)),
                  _full((1, d)), _full((2, d, LANES)), _full((1, LANES))],
        out_specs=specs,
        scratch_shapes=[pltpu.VMEM((8, LANES), F32)],
        compiler_params=_params("arbitrary", "arbitrary"),
        name="l0_post",
    )(ctx, x, o, gu, gv, wsp, bsp, wout, mod, gf, wr, br)


def _row_copy(src_ref, src_row, dst_ref, dst_row, sem):
    rows = lambda ref, r: ref.at[pl.ds(pl.multiple_of(r * ROW_SUB, ROW_SUB), ROW_SUB)]
    return pltpu.make_async_copy(rows(src_ref, src_row), rows(dst_ref, dst_row), sem)


def _gather_pipeline(src_ref, table_ref, idx_ref, buf_ref, sem_idx, sem_rows, step, n_steps):
    n_str, rows = buf_ref.shape[1], buf_ref.shape[2] // ROW_SUB

    def idx_copy(s):
        return pltpu.make_async_copy(table_ref.at[s], idx_ref.at[s % 2], sem_idx.at[s % 2])

    def issue_rows(s):
        slot = s % 2

        def issue(r, carry):
            for k in range(n_str):
                _row_copy(src_ref, idx_ref[slot, k * rows + r], buf_ref.at[slot, k], r,
                          sem_rows.at[slot]).start(priority=k % 2)
            return carry

        lax.fori_loop(0, rows, issue, 0, unroll=ROW_DMA_UNROLL)

    @pl.when(step == 0)
    def _():
        idx_copy(0).start()
        idx_copy(0).wait()
        issue_rows(0)

        @pl.when(n_steps > 1)
        def _():
            idx_copy(1).start()

    @pl.when(step + 1 < n_steps)
    def _():
        idx_copy(step + 1).wait()
        issue_rows(step + 1)

        @pl.when(step + 2 < n_steps)
        def _():
            idx_copy(step + 2).start()

    slot = step % 2
    for k in range(n_str):
        pltpu.make_async_copy(src_ref.at[pl.ds(0, rows * ROW_SUB)], buf_ref.at[slot, k], sem_rows.at[slot]).wait()
    return slot


def _gather_scratch(n_str, rows):
    return [pltpu.SMEM((2, n_str * rows), I32), pltpu.VMEM((2, n_str, rows * ROW_SUB, LANES), F32),
            pltpu.SemaphoreType.DMA((2,)), pltpu.SemaphoreType.DMA((2,))]


def _experts_kernel(be_ref, nu_ref, f_ref, tok_ref, w1_ref, w3_ref, w2_ref, y_ref, w1b_ref, w3b_ref, w2b_ref,
                    idx_ref, xbuf_ref, sem_idx, sem_rows):
    i = pl.program_id(0)
    n_used = nu_ref[0]

    @pl.when(jnp.logical_or(i == 0, be_ref[i] != be_ref[jnp.maximum(i - 1, 0)]))
    def _():
        w1b_ref[...] = w1_ref[...].astype(BF16)
        w3b_ref[...] = w3_ref[...].astype(BF16)
        w2b_ref[...] = w2_ref[...].astype(BF16)

    @pl.when(i < n_used)
    def _():
        slot = _gather_pipeline(f_ref, tok_ref, idx_ref, xbuf_ref, sem_idx, sem_rows, i, n_used)
        x = _load_row_tiles(xbuf_ref.at[slot, 0]).astype(BF16)
        h1 = jnp.dot(x, w1b_ref[...], preferred_element_type=F32)
        h3 = jnp.dot(x, w3b_ref[...], preferred_element_type=F32)
        a = (h1 * jax.nn.sigmoid(h1) * h3).astype(BF16)
        _store_row_tiles(y_ref, jnp.dot(a, w2b_ref[...], preferred_element_type=F32))

    @pl.when(i >= n_used)
    def _():
        y_ref[...] = jnp.zeros_like(y_ref)


def _experts(f, slot_tok, w1, w3, w2, layer, block_e, n_used):
    n_blocks = slot_tok.shape[0]
    _, _, d, hid = w1.shape
    rb = MOE_BLOCK * ROW_SUB
    wspec = lambda s: pl.BlockSpec((None, None) + s, lambda i, be, nu: (layer, be[i], 0, 0))
    return pl.pallas_call(
        _experts_kernel,
        out_shape=jax.ShapeDtypeStruct((n_blocks * rb, LANES), F32),
        grid_spec=pltpu.PrefetchScalarGridSpec(
            num_scalar_prefetch=2,
            grid=(n_blocks,),
            in_specs=[pl.BlockSpec(memory_space=pl.ANY), pl.BlockSpec(memory_space=pltpu.VMEM),
                      wspec((d, hid)), wspec((d, hid)), wspec((hid, d))],
            out_specs=pl.BlockSpec((rb, LANES), lambda i, be, nu: (i, 0)),
            scratch_shapes=[pltpu.VMEM((d, hid), BF16), pltpu.VMEM((d, hid), BF16), pltpu.VMEM((hid, d), BF16)]
                           + _gather_scratch(1, MOE_BLOCK)),
        compiler_params=_params("arbitrary"),
        name="moe_experts",
    )(block_e, n_used, f, slot_tok, w1, w3, w2)


def _moe_plan(route_i, counts, n_tok):
    eid, rank = route_i[0:2], route_i[2:4]
    cnt = counts[0, :N_EXPERTS].astype(I32)
    padded = (cnt + MOE_BLOCK - 1) // MOE_BLOCK * MOE_BLOCK
    pend = jnp.cumsum(padded)
    pstart = pend - padded
    slots = pstart[eid] + rank
    n_assign = 2 * n_tok
    n_slots = (n_assign + MOE_BLOCK - 1) // MOE_BLOCK * MOE_BLOCK + N_EXPERTS * MOE_BLOCK
    n_blocks = n_slots // MOE_BLOCK
    tok = jnp.broadcast_to(jnp.arange(n_tok, dtype=I32), (2, n_tok))
    slot_tok = jnp.zeros((n_slots,), I32).at[slots.reshape(-1)].set(tok.reshape(-1))
    bstart = jnp.arange(n_blocks, dtype=I32) * MOE_BLOCK
    block_e = jnp.minimum(jnp.sum(bstart[:, None] >= pend[None, :], axis=1), N_EXPERTS - 1).astype(I32)
    n_used = pend[-1] // MOE_BLOCK
    last_e = block_e[jnp.maximum(n_used - 1, 0)]
    block_e = jnp.where(jnp.arange(n_blocks) < n_used, block_e, last_e).astype(I32)
    slots = slots.reshape(2, n_tok // TOKEN_TILE, TOKEN_TILE).transpose(1, 0, 2)
    slots = slots.reshape(n_tok // TOKEN_TILE, 2 * TOKEN_TILE).astype(I32)
    return slots, slot_tok.reshape(n_blocks, MOE_BLOCK), block_e, n_used.reshape(1).astype(I32)


def _moe(f, route_i, counts, w1, w3, w2, layer):
    n_tok = f.shape[0] // ROW_SUB
    slots, slot_tok, block_e, n_used = _moe_plan(route_i, counts, n_tok)
    y = _experts(f, slot_tok, w1, w3, w2, layer, block_e, n_used)
    return y, slots


def _combined_rows(y_ref, slots_ref, rw_ref, idx_ref, ybuf_ref, sem_idx, sem_rows):
    slot = _gather_pipeline(y_ref, slots_ref, idx_ref, ybuf_ref, sem_idx, sem_rows,
                            pl.program_id(0), pl.num_programs(0))
    rw = rw_ref[...]
    w1, w2 = rw[:, 0:1], rw[:, 1:2]
    t = ybuf_ref.shape[2] // ROW_SUB
    part = lambda k, s: ybuf_ref[slot, k, pl.ds(s, t, stride=ROW_SUB), :]
    return jnp.concatenate([w1 * part(0, s) + w2 * part(1, s) for s in range(ROW_SUB)], axis=1)


def _l0_combine_kernel(x_ref, rw_ref, slots_ref, y_ref, mod0_ref, mod1_ref, g_ref, win_ref,
                       x2_ref, u_ref, idx_ref, ybuf_ref, sem_idx, sem_rows):
    moe = _combined_rows(y_ref, slots_ref, rw_ref, idx_ref, ybuf_ref, sem_idx, sem_rows)
    x2 = x_ref[...] + mod0_ref[0, 0][5:6] * moe
    x2_ref[...] = x2
    m1 = mod1_ref[0, 0]
    h = _modulate(x2, g_ref[...], m1[0:1], m1[1:2]).astype(BF16)
    u_ref[...] = jnp.dot(h, win_ref[...], preferred_element_type=F32)


def _l0_combine(x1, rw, slots, y, mod0, mod1, g, win, b, nct):
    n_tok, d = x1.shape
    t = TOKEN_TILE
    nt = n_tok // b // t
    l = n_tok // b
    modspec = pl.BlockSpec((1, 1, 6, d), lambda i: (i // nt, jnp.where(i % nt >= nct, 1, 0), 0, 0))
    return pl.pallas_call(
        _l0_combine_kernel,
        out_shape=[jax.ShapeDtypeStruct((n_tok, d), F32), jax.ShapeDtypeStruct((l, b * S5_WIDTH), F32)],
        grid=(n_tok // t,),
        in_specs=[pl.BlockSpec((t, d), lambda i: (i, 0)), pl.BlockSpec((t, 8), lambda i: (i, 0)),
                  pl.BlockSpec(memory_space=pltpu.VMEM), pl.BlockSpec(memory_space=pl.ANY),
                  modspec, modspec, _full((1, d)), _full((d, S5_WIDTH))],
        out_specs=[pl.BlockSpec((t, d), lambda i: (i, 0)),
                   pl.BlockSpec((t, S5_WIDTH), lambda i: (i % nt, i // nt))],
        scratch_shapes=_gather_scratch(2, t),
        compiler_params=_params("arbitrary"),
        name="l0_combine",
    )(x1, rw, slots, y, mod0, mod1, g, win)


def _s5_zoh_kernel(are_ref, aim_ref, ldt_ref, bre_ref, bim_ref, lre_ref, lim_ref, bbre_ref, bbim_ref):
    ar, ai = are_ref[...], aim_ref[...]
    dt = jnp.exp(ldt_ref[...])
    mag = jnp.exp(ar * dt)
    lr = mag * jnp.cos(ai * dt)
    li = mag * jnp.sin(ai * dt)
    lre_ref[...] = lr
    lim_ref[...] = li
    nr, ni = lr - 1.0, li
    den = ar * ar + ai * ai
    cr = (nr * ar + ni * ai) / den
    ci = (ni * ar - nr * ai) / den
    br, bi = bre_ref[...], bim_ref[...]
    bbre_ref[...] = cr * br - ci * bi
    bbim_ref[...] = cr * bi + ci * br


def _s5_zoh(a_re, a_im, log_dt, b_re, b_im):
    n_dir, g, n, p = b_re.shape
    rows = n_dir * g * p
    expand = lambda a: jnp.broadcast_to(a[:, :, None, :], (n_dir, g, p, n)).reshape(rows, n)
    ldt = jnp.broadcast_to(log_dt[:, :, None, None], (n_dir, g, p, n)).reshape(rows, n)
    bt = lambda a: jnp.transpose(a, (0, 1, 3, 2)).reshape(rows, n)
    outs = pl.pallas_call(
        _s5_zoh_kernel,
        out_shape=[jax.ShapeDtypeStruct((rows, n), F32)] * 4,
        name="s5_zoh",
    )(expand(a_re), expand(a_im), ldt, bt(b_re), bt(b_im))
    lre, lim, bbre, bbim = (o.reshape(n_dir, g, p, n) for o in outs)
    return lre[:, :, 0, :], lim[:, :, 0, :], bbre, bbim


def _s5_matrices(lre, lim, bbre, bbim, c_re, c_im):
    n_dir = lre.shape[0]
    nj = S5_GROUPS // (2 * S5_TILES)
    shp = (n_dir, nj, S5_TILES, 2, S5_GROUP_DIM, S5_STATE)
    eye_t = jnp.eye(S5_TILES, dtype=F32)
    eye_g = jnp.eye(2, dtype=F32)

    def place(a):
        return jnp.einsum("djtgpn,tu,gh->djtgpuhn", a.reshape(shp), eye_t, eye_g)

    kdim = S5_TILES * 2 * S5_GROUP_DIM
    sdim = S5_TILES * 2 * 2 * S5_STATE
    bm = jnp.stack([place(bbre), place(bbim)], axis=6)
    bm = bm.reshape(n_dir, nj, kdim, sdim).astype(BF16)
    cm = jnp.stack([place(c_re), place(-c_im)], axis=6)
    cm = jnp.transpose(cm, (0, 1, 5, 6, 7, 8, 2, 3, 4)).reshape(n_dir, nj, sdim, kdim).astype(BF16)
    lam = jnp.stack([lre.reshape(n_dir, nj, S5_TILES * LANES), lim.reshape(n_dir, nj, S5_TILES * LANES)], axis=2)
    return bm, cm, lam


def _s5_scan_kernel(uf_ref, ub_ref, bm_ref, cm_ref, lam_ref, yf_ref, yb_ref, v_ref, h_ref, *, tc, nb):
    @pl.when(pl.program_id(0) == 0)
    def _():
        h_ref[...] = jnp.zeros_like(h_ref)

    nj = bm_ref.shape[1]
    kdim = bm_ref.shape[2]
    for d, (u_ref, y_ref) in enumerate(((uf_ref, yf_ref), (ub_ref, yb_ref))):
        for jq in range(nj):
            uk = u_ref[:, kdim * jq:kdim * (jq + 1)].astype(BF16)
            v_ref[...] = jnp.dot(uk, bm_ref[d, jq], preferred_element_type=F32)
            lam = lam_ref[d, jq]
            lr = [jnp.broadcast_to(lam[0:1, LANES * t:LANES * (t + 1)], (nb, LANES)) for t in range(S5_TILES)]
            li = [jnp.broadcast_to(lam[1:2, LANES * t:LANES * (t + 1)], (nb, LANES)) for t in range(S5_TILES)]
            h0 = h_ref[d, jq]
            init = tuple(h0[:, LANES * c:LANES * (c + 1)] for c in range(2 * S5_TILES))

            def step(s, carry, d=d, lr=lr, li=li):
                tt = s if d == 0 else tc - 1 - s
                r0 = pl.multiple_of(tt * nb, nb)
                rows = v_ref[pl.ds(r0, nb), :]
                new = []
                for t in range(S5_TILES):
                    vr = rows[:, 2 * LANES * t:2 * LANES * t + LANES]
                    vi = rows[:, 2 * LANES * t + LANES:2 * LANES * (t + 1)]
                    hr, hi = carry[2 * t], carry[2 * t + 1]
                    new.append(lr[t] * hr - li[t] * hi + vr)
                    new.append(lr[t] * hi + li[t] * hr + vi)
                v_ref[pl.ds(r0, nb), :] = jnp.concatenate(new, axis=1)
                return tuple(new)

            fin = lax.fori_loop(0, tc, step, init)
            h_ref[d, jq] = jnp.concatenate(fin, axis=1)
            y_ref[:, kdim * jq:kdim * (jq + 1)] = jnp.dot(v_ref[...].astype(BF16), cm_ref[d, jq],
                                                         preferred_element_type=F32)


def _s5_scan(u_tm, bm, cm, lam, nb, n_ctx):
    rows_total, w = u_tm.shape
    tc = S5_CHUNK
    r = tc * nb
    n_chunks = rows_total // r
    ncc = n_ctx // tc
    bwd = lambda i: (jnp.where(i < ncc, ncc - 1 - i, n_chunks - 1 - (i - ncc)), 0)
    nj = bm.shape[1]
    sdim = bm.shape[3]
    kern = functools.partial(_s5_scan_kernel, tc=tc, nb=nb)
    return pl.pallas_call(
        kern,
        out_shape=[jax.ShapeDtypeStruct((rows_total, w), F32)] * 2,
        grid=(n_chunks,),
        in_specs=[pl.BlockSpec((r, w), lambda i: (i, 0)), pl.BlockSpec((r, w), bwd),
                  _full(bm.shape), _full(cm.shape), _full(lam.shape)],
        out_specs=[pl.BlockSpec((r, w), lambda i: (i, 0)), pl.BlockSpec((r, w), bwd)],
        scratch_shapes=[pltpu.VMEM((r, sdim), F32), pltpu.VMEM((2, nj, nb, sdim), F32)],
        compiler_params=_params("arbitrary"),
        name="s5_scan",
    )(u_tm, u_tm, bm, cm, lam)


def _l1_post_kernel(x_ref, u_ref, yf_ref, yb_ref, dsk_ref, wglu_ref, bglu_ref, wout_ref, mod_ref, gf_ref,
                    wr_ref, br_ref, x3_ref, f_ref, ri_ref, rw_ref, cnt_ref, base_ref):
    y = dsk_ref[...] * u_ref[...] + yf_ref[...] + yb_ref[...]
    g = jax.nn.gelu(y, approximate=True)
    z = jnp.dot(g.astype(BF16), wglu_ref[...], preferred_element_type=F32) + bglu_ref[...]
    o = jnp.dot((g * jax.nn.sigmoid(z)).astype(BF16), wout_ref[...], preferred_element_type=F32)
    first = jnp.logical_and(pl.program_id(0) == 0, pl.program_id(1) == 0)
    _route_tail(x_ref[0], o, mod_ref[0, 0], gf_ref[...], wr_ref, br_ref, base_ref,
                x3_ref, f_ref, ri_ref, rw_ref, cnt_ref, first)


def _l1_post(x2, u_tm, yf, yb, dsk, wglu, bglu, wout, mod, gf, wr, br, n_ctx):
    b, l, d = x2.shape
    t = TOKEN_TILE
    nct = n_ctx // t
    s = l - n_ctx
    tm = pl.BlockSpec((t, S5_WIDTH), lambda bi, ti: (ti + nct, bi))
    shapes, specs = _tail_out_shapes(b, s, d)
    return pl.pallas_call(
        _l1_post_kernel,
        out_shape=shapes,
        grid=(b, s // t),
        in_specs=[pl.BlockSpec((1, t, d), lambda bi, ti: (bi, ti + nct, 0)), tm, tm, tm,
                  _full((1, S5_WIDTH)), _full((S5_WIDTH, S5_WIDTH)), _full((1, S5_WIDTH)), _full((S5_WIDTH, d)),
                  pl.BlockSpec((1, 1, 6, d), lambda bi, ti: (bi, 1, 0, 0)),
                  _full((1, d)), _full((2, d, LANES)), _full((1, LANES))],
        out_specs=specs,
        scratch_shapes=[pltpu.VMEM((8, LANES), F32)],
        compiler_params=_params("arbitrary", "arbitrary"),
        name="l1_post",
    )(x2, u_tm, yf, yb, dsk, wglu, bglu, wout, mod, gf, wr, br)


def _final_kernel(x_ref, rw_ref, slots_ref, y_ref, mod_ref, g_ref, o_ref, idx_ref, ybuf_ref, sem_idx, sem_rows):
    moe = _combined_rows(y_ref, slots_ref, rw_ref, idx_ref, ybuf_ref, sem_idx, sem_rows)
    x4 = x_ref[...] + mod_ref[0, 0][5:6] * moe
    o_ref[...] = _rms(x4) * g_ref[...]


def _final(x3, rw, slots, y, mod, g, b):
    n_tok, d = x3.shape
    t = TOKEN_TILE
    nt = n_tok // b // t
    return pl.pallas_call(
        _final_kernel,
        out_shape=jax.ShapeDtypeStruct((n_tok, d), F32),
        grid=(n_tok // t,),
        in_specs=[pl.BlockSpec((t, d), lambda i: (i, 0)), pl.BlockSpec((t, 8), lambda i: (i, 0)),
                  pl.BlockSpec(memory_space=pltpu.VMEM), pl.BlockSpec(memory_space=pl.ANY),
                  pl.BlockSpec((1, 1, 6, d), lambda i: (i // nt, 1, 0, 0)), _full((1, d))],
        out_specs=pl.BlockSpec((t, d), lambda i: (i, 0)),
        scratch_shapes=_gather_scratch(2, t),
        compiler_params=_params("arbitrary"),
        name="final_combine",
    )(x3, rw, slots, y, mod, g)


def _rope_tables(n_ctx, n_lat):
    pos = jnp.arange(n_lat, dtype=I32)
    row = (pos // GRID_W).astype(F32)
    col = (pos % GRID_W).astype(F32)
    inv_freq = jnp.power(ROPE_THETA, -jnp.arange(0, AXIS_DIM, 2, dtype=F32) / AXIS_DIM)
    lane = np.arange(LANES)
    dd = lane % HEAD_DIM
    use_col = dd >= AXIS_DIM
    first = (dd % AXIS_DIM) < AXIS_DIM // 2
    freq = inv_freq[(dd % AXIS_DIM) % (AXIS_DIM // 2)]
    ang = jnp.where(use_col[None, :], col[:, None], row[:, None]) * freq[None, :]
    cos, sin = jnp.cos(ang), jnp.sin(ang)
    sa = jnp.where(first[None, :], -sin, 0.0)
    sb = jnp.where(first[None, :], 0.0, sin)
    pad = lambda a, v: jnp.concatenate([jnp.full((n_ctx, LANES), v, F32), a], axis=0)
    return pad(cos, 1.0), pad(sa, 0.0), pad(sb, 0.0)


def _group_ones(width, group):
    idx = np.arange(width) // group
    return jnp.asarray(idx[:, None] == idx[None, :], dtype=BF16)


def _router_weights(w_grp, b_grp, w_rt, b_rt):
    d = w_grp.shape[0]
    used = N_EXPERT_GROUPS + N_EXPERTS
    wr = jnp.concatenate([w_grp, w_rt, jnp.zeros((d, LANES - used), F32)], axis=1)
    br = jnp.concatenate([b_grp, b_rt, jnp.zeros((LANES - used,), F32)]).reshape(1, LANES)
    return jnp.stack(_split_bf16(wr)), br


def kernel(x, c, ctx, c_ctx, w_mod, b_mod, norm_mix_g, norm_ffn_g, mix_w_in, mix_w_out, q_norm_g, k_norm_g, gmlp_norm_g, gmlp_w_spatial, gmlp_b_spatial, s5_w_in, s5_a_re, s5_a_im, s5_log_dt, s5_b_re, s5_b_im, s5_c_re, s5_c_im, s5_d, s5_w_glu, s5_b_glu, s5_w_out, moe_w_group, moe_b_group, moe_w_router, moe_b_router, moe_w1, moe_w3, moe_w2, final_norm_g):
    b, s, d = x.shape
    n_ctx = ctx.shape[1]
    l = n_ctx + s
    assert w_mod.shape[0] == 2, "two layers: attention/gMLP then S5"
    assert n_ctx % TOKEN_TILE == 0 and s % TOKEN_TILE == 0 and b % 8 == 0 and d == D_MODEL
    nct = n_ctx // TOKEN_TILE

    r = (b + 1 + 7) // 8 * 8
    cond = jnp.concatenate([c, c_ctx[None, :], jnp.zeros((r - b - 1, d), F32)], axis=0)
    mods = _adaln(cond, w_mod, b_mod)
    lat = mods[:, :b].reshape(2, b, 1, 6, d)
    cxt = jnp.broadcast_to(mods[:, b].reshape(2, 1, 1, 6, d), (2, b, 1, 6, d))
    mod = jnp.concatenate([cxt, lat], axis=2)

    w_in = jnp.concatenate([mix_w_in[0][:, _Q_COL_PERM], mix_w_in[0][:, ATTN_WIDTH:]], axis=1).astype(BF16)
    w_out = jnp.concatenate([mix_w_out[0][_Q_COL_PERM], mix_w_out[0][ATTN_WIDTH:]], axis=0).astype(BF16)
    cos, sa, sb = _rope_tables(n_ctx, s)
    qg = jnp.tile(q_norm_g[0], N_Q_HEADS).reshape(1, ATTN_WIDTH)
    kg = jnp.tile(k_norm_g[0], N_KV_HEADS).reshape(1, KV_WIDTH)
    q, k, v, gu, gv = _l0_proj(ctx, x, mod[0], norm_mix_g[0].reshape(1, d), w_in, qg, kg,
                               gmlp_norm_g[0].reshape(1, GMLP_WIDTH), cos, sa, sb,
                               _group_ones(ATTN_WIDTH, HEAD_DIM), _group_ones(KV_WIDTH, HEAD_DIM), nct)
    o = _attention(q, k, v, n_ctx)
    bsp = jnp.repeat(gmlp_b_spatial[0].T, GMLP_GROUP_DIM, axis=1)
    wr0, br0 = _router_weights(moe_w_group[0], moe_b_group[0], moe_w_router[0], moe_b_router[0])
    x1, f0, ri0, rw0, cnt0 = _l0_post(ctx, x, o, gu, gv, gmlp_w_spatial[0].astype(BF16), bsp, w_out, mod[0],
                                      norm_ffn_g[0].reshape(1, d), wr0, br0, nct)
    y0, slots0 = _moe(f0, ri0, cnt0, moe_w1, moe_w3, moe_w2, 0)
    x2, u_tm = _l0_combine(x1.reshape(b * l, d), rw0, slots0, y0, mod[0], mod[1], norm_mix_g[1].reshape(1, d),
                           s5_w_in[0].astype(BF16), b, nct)

    lre, lim, bbre, bbim = _s5_zoh(s5_a_re[0], s5_a_im[0], s5_log_dt[0], s5_b_re[0], s5_b_im[0])
    bm, cm, lam = _s5_matrices(lre, lim, bbre, bbim, s5_c_re[0], s5_c_im[0])
    yf, yb = _s5_scan(u_tm.reshape(l * b, S5_WIDTH), bm, cm, lam, b, n_ctx)
    wr1, br1 = _router_weights(moe_w_group[1], moe_b_group[1], moe_w_router[1], moe_b_router[1])
    tmv = lambda a: a.reshape(l, b * S5_WIDTH)
    x3, f1, ri1, rw1, cnt1 = _l1_post(x2.reshape(b, l, d), u_tm, tmv(yf), tmv(yb), s5_d[0].reshape(1, S5_WIDTH),
                                      s5_w_glu[0].astype(BF16), s5_b_glu[0].reshape(1, S5_WIDTH),
                                      s5_w_out[0].astype(BF16), mod[1], norm_ffn_g[1].reshape(1, d), wr1, br1, n_ctx)
    y1, slots1 = _moe(f1, ri1, cnt1, moe_w1, moe_w3, moe_w2, 1)
    out = _final(x3.reshape(b * s, d), rw1, slots1, y1, mod[1], final_norm_g.reshape(1, d), b)
    return out.reshape(b, s, d)
```

```python
import functools

import numpy as np
import jax
import jax.numpy as jnp
from jax import lax
from jax.experimental import pallas as pl
from jax.experimental.pallas import tpu as pltpu

F32 = jnp.float32
BF16 = jnp.bfloat16
I32 = jnp.int32

EPS = 1e-6
GRID_W = 64
HEAD_DIM = 64
N_Q_HEADS = 8
N_KV_HEADS = 2
ATTN_WIDTH = N_Q_HEADS * HEAD_DIM
KV_WIDTH = N_KV_HEADS * HEAD_DIM
AXIS_DIM = HEAD_DIM // 2
ROPE_THETA = 10000.0
GMLP_GROUPS = 8
GMLP_GROUP_DIM = 64
GMLP_WIDTH = GMLP_GROUPS * GMLP_GROUP_DIM
GMLP_CHUNK = 128
MIX_IN_WIDTH = ATTN_WIDTH + 2 * KV_WIDTH + 2 * GMLP_WIDTH
S5_GROUPS = 32
S5_GROUP_DIM = 16
S5_STATE = 64
S5_WIDTH = S5_GROUPS * S5_GROUP_DIM
N_EXPERT_GROUPS = 4
EXPERTS_PER_GROUP = 8
N_EXPERTS = N_EXPERT_GROUPS * EXPERTS_PER_GROUP
EXPERT_HIDDEN = 512
MOE_BLOCK = 256

D_MODEL = 1024
LANES = 128
ROW_SUB = D_MODEL // LANES
TOKEN_TILE = 256
ROW_DMA_UNROLL = 8
S5_CHUNK = 64
S5_TILES = 4
VMEM_LIMIT = 48 * 1024 * 1024
NEG_BIG = -1e30

_Q_HEAD_ORDER = (0, 4, 1, 5, 2, 6, 3, 7)
_Q_COL_PERM = np.concatenate([np.arange(HEAD_DIM) + HEAD_DIM * h for h in _Q_HEAD_ORDER])


def _params(*sem):
    return pltpu.CompilerParams(dimension_semantics=sem, vmem_limit_bytes=VMEM_LIMIT)


def _rms(x):
    return x * lax.rsqrt(jnp.mean(x * x, axis=-1, keepdims=True) + EPS)


def _modulate(x, g, shift, scale):
    return _rms(x) * g * (1.0 + scale) + shift


def _full(shape):
    return pl.BlockSpec(shape, lambda *_: (0,) * len(shape))


def _split_bf16(a):
    hi = a.astype(BF16)
    return hi, (a - hi.astype(F32)).astype(BF16)


def _dot_split(a, w_hi, w_lo):
    a_hi, a_lo = _split_bf16(a)
    dot = functools.partial(jnp.dot, preferred_element_type=F32)
    return dot(a_hi, w_hi) + (dot(a_lo, w_hi) + dot(a_hi, w_lo))


def _store_row_tiles(ref, val):
    n = val.shape[0]
    for s in range(ROW_SUB):
        ref[pl.ds(s, n, stride=ROW_SUB), :] = val[:, LANES * s:LANES * (s + 1)]


def _load_row_tiles(ref):
    n = ref.shape[0] // ROW_SUB
    return jnp.concatenate([ref[pl.ds(s, n, stride=ROW_SUB), :] for s in range(ROW_SUB)], axis=1)


def _adaln_kernel(cond_ref, w_ref, b_ref, o_ref):
    c = cond_ref[...]
    s = c * jax.nn.sigmoid(c)
    o_ref[0] = _dot_split(s, *_split_bf16(w_ref[0])) + b_ref[0]


def _adaln(cond, w_mod, b_mod):
    depth, d, n = w_mod.shape
    r = cond.shape[0]
    tn = 1536
    return pl.pallas_call(
        _adaln_kernel,
        out_shape=jax.ShapeDtypeStruct((depth, r, n), F32),
        grid=(depth, n // tn),
        in_specs=[pl.BlockSpec((r, d), lambda l, j: (0, 0)),
                  pl.BlockSpec((1, d, tn), lambda l, j: (l, 0, j)),
                  pl.BlockSpec((1, 1, tn), lambda l, j: (l, 0, j))],
        out_specs=pl.BlockSpec((1, r, tn), lambda l, j: (l, 0, j)),
        compiler_params=_params("parallel", "parallel"),
        name="adaln",
    )(cond, w_mod, b_mod.reshape(depth, 1, n))


def _seq_tile(ctx_ref, x_ref, nct):
    return jnp.where(pl.program_id(1) < nct, ctx_ref[0], x_ref[0])


def _seq_specs(t, d, nct):
    return [pl.BlockSpec((1, t, d), lambda bi, ti: (bi, jnp.minimum(ti, nct - 1), 0)),
            pl.BlockSpec((1, t, d), lambda bi, ti: (bi, jnp.maximum(ti - nct, 0), 0))]


def _l0_proj_kernel(ctx_ref, x_ref, mod_ref, g_ref, win_ref, qg_ref, kg_ref, lng_ref, cos_ref, sa_ref, sb_ref,
                    gq_ref, gk_ref, q_ref, k_ref, v_ref, gu_ref, gv_ref, *, nct):
    m = mod_ref[0, 0]
    h = _modulate(_seq_tile(ctx_ref, x_ref, nct), g_ref[...], m[0:1], m[1:2]).astype(BF16)
    z = jnp.dot(h, win_ref[...], preferred_element_type=F32)
    cos, sa, sb = cos_ref[...], sa_ref[...], sb_ref[...]

    def head_norm_rope(t, gsum_ref, gain, reps):
        w = t.shape[1]
        ss = jnp.dot((t * t).astype(BF16), gsum_ref[...], preferred_element_type=F32)
        tn = t * lax.rsqrt(ss * (1.0 / HEAD_DIM) + EPS) * gain
        c, a, b = (jnp.concatenate([tab] * reps, axis=1) if reps > 1 else tab for tab in (cos, sa, sb))
        return tn * c + pltpu.roll(tn, w - AXIS_DIM // 2, 1) * a + pltpu.roll(tn, AXIS_DIM // 2, 1) * b

    q = head_norm_rope(z[:, :ATTN_WIDTH], gq_ref, qg_ref[...], ATTN_WIDTH // LANES)
    q_ref[0] = (q * (HEAD_DIM ** -0.5)).astype(BF16)
    k = head_norm_rope(z[:, ATTN_WIDTH:ATTN_WIDTH + KV_WIDTH], gk_ref, kg_ref[...], 1)
    k_ref[0] = k.astype(BF16)
    v_ref[0] = z[:, ATTN_WIDTH + KV_WIDTH:ATTN_WIDTH + 2 * KV_WIDTH].astype(BF16)
    off = ATTN_WIDTH + 2 * KV_WIDTH
    gu_ref[0] = jax.nn.gelu(z[:, off:off + GMLP_WIDTH], approximate=True).astype(BF16)
    gv = jax.nn.gelu(z[:, off + GMLP_WIDTH:], approximate=True)
    mu = jnp.mean(gv, axis=-1, keepdims=True)
    gc = gv - mu
    gn = gc * lax.rsqrt(jnp.mean(gc * gc, axis=-1, keepdims=True) + EPS) * lng_ref[...]
    gv_ref[0] = gn.astype(BF16)


def _l0_proj(ctx, x, mod, g, w_in, qg, kg, lng, cos, sa, sb, gq, gk, nct):
    b, s, d = x.shape
    l = s + ctx.shape[1]
    t = TOKEN_TILE
    tok = lambda w: pl.BlockSpec((1, t, w), lambda bi, ti: (bi, ti, 0))
    tab = pl.BlockSpec((t, LANES), lambda bi, ti: (ti, 0))
    outs = [jax.ShapeDtypeStruct((b, l, w), BF16) for w in (ATTN_WIDTH, KV_WIDTH, KV_WIDTH, GMLP_WIDTH, GMLP_WIDTH)]
    return pl.pallas_call(
        functools.partial(_l0_proj_kernel, nct=nct),
        out_shape=outs,
        grid=(b, l // t),
        in_specs=_seq_specs(t, d, nct) + [
                  pl.BlockSpec((1, 1, 6, d), lambda bi, ti: (bi, jnp.where(ti >= nct, 1, 0), 0, 0)),
                  _full((1, d)), _full((d, MIX_IN_WIDTH)), _full((1, ATTN_WIDTH)), _full((1, KV_WIDTH)),
                  _full((1, GMLP_WIDTH)), tab, tab, tab,
                  _full((ATTN_WIDTH, ATTN_WIDTH)), _full((KV_WIDTH, KV_WIDTH))],
        out_specs=[tok(w) for w in (ATTN_WIDTH, KV_WIDTH, KV_WIDTH, GMLP_WIDTH, GMLP_WIDTH)],
        compiler_params=_params("parallel", "parallel"),
        name="l0_proj",
    )(ctx, x, mod, g, w_in, qg, kg, lng, cos, sa, sb, gq, gk)


def _attn_kernel(q_ref, k_ref, v_ref, o_ref, *, n_ctx, nct):
    ti = pl.program_id(1)
    tq = q_ref.shape[1]
    lane = lax.broadcasted_iota(I32, (tq, LANES), 1)
    low = lane < HEAD_DIM

    def attend(nk):
        kk = k_ref[0, :nk, :]
        vv = v_ref[0, :nk, :]
        for m in range(ATTN_WIDTH // LANES):
            qp = q_ref[0, :, LANES * m:LANES * (m + 1)]
            halves = []
            for sel in (low, jnp.logical_not(low)):
                qm = jnp.where(sel, qp, jnp.zeros_like(qp))
                s = lax.dot_general(qm, kk, (((1,), (1,)), ((), ())), preferred_element_type=F32)
                p = jnp.exp(s - jnp.max(s, axis=-1, keepdims=True))
                den = jnp.sum(p, axis=-1, keepdims=True)
                pv = jnp.dot(p.astype(BF16), vv, preferred_element_type=F32)
                halves.append(pv / den)
            o_ref[0, :, LANES * m:LANES * (m + 1)] = jnp.where(low, halves[0], halves[1]).astype(BF16)

    @pl.when(ti < nct)
    def _():
        attend(n_ctx)

    @pl.when(ti >= nct)
    def _():
        attend(k_ref.shape[1])


def _attention(q, k, v, n_ctx):
    b, l, _ = q.shape
    t = TOKEN_TILE
    kern = functools.partial(_attn_kernel, n_ctx=n_ctx, nct=n_ctx // t)
    return pl.pallas_call(
        kern,
        out_shape=jax.ShapeDtypeStruct((b, l, ATTN_WIDTH), BF16),
        grid=(b, l // t),
        in_specs=[pl.BlockSpec((1, t, ATTN_WIDTH), lambda bi, ti: (bi, ti, 0)),
                  pl.BlockSpec((1, l, KV_WIDTH), lambda bi, ti: (bi, 0, 0)),
                  pl.BlockSpec((1, l, KV_WIDTH), lambda bi, ti: (bi, 0, 0))],
        out_specs=pl.BlockSpec((1, t, ATTN_WIDTH), lambda bi, ti: (bi, ti, 0)),
        compiler_params=_params("parallel", "parallel"),
        name="attention",
    )(q, k, v)


def _route_tail(x, y, m, gf, wr_ref, br_ref, base_ref, x_ref, f_ref, ri_ref, rw_ref, cnt_ref, first):
    x1 = x + m[2:3] * y
    x_ref[0] = x1
    f = _modulate(x1, gf, m[3:4], m[4:5])
    _store_row_tiles(f_ref, f)
    t = f.shape[0]
    logits = _dot_split(f, wr_ref[0], wr_ref[1]) + br_ref[...]
    lane = lax.broadcasted_iota(I32, (t, LANES), 1).astype(F32)
    is_grp = lane < N_EXPERT_GROUPS
    gl = jnp.where(is_grp, logits, NEG_BIG)
    gmax = jnp.max(gl, axis=-1, keepdims=True)
    g_idx = jnp.min(jnp.where(gl == gmax, lane, float(LANES)), axis=-1, keepdims=True)
    g_w = 1.0 / jnp.sum(jnp.where(is_grp, jnp.exp(gl - gmax), 0.0), axis=-1, keepdims=True)
    e_lo = N_EXPERT_GROUPS + EXPERTS_PER_GROUP * g_idx
    in_grp = jnp.logical_and(lane >= e_lo, lane < e_lo + EXPERTS_PER_GROUP)
    sel = jnp.where(in_grp, logits, NEG_BIG)
    v1 = jnp.max(sel, axis=-1, keepdims=True)
    i1 = jnp.min(jnp.where(sel == v1, lane, float(LANES)), axis=-1, keepdims=True)
    sel2 = jnp.where(lane == i1, NEG_BIG, sel)
    v2 = jnp.max(sel2, axis=-1, keepdims=True)
    i2 = jnp.min(jnp.where(sel2 == v2, lane, float(LANES)), axis=-1, keepdims=True)
    e = jnp.exp(v2 - v1)
    w1 = g_w / (1.0 + e)
    w2 = g_w * e / (1.0 + e)
    eid1 = i1 - N_EXPERT_GROUPS
    eid2 = i2 - N_EXPERT_GROUPS

    @pl.when(first)
    def _():
        base_ref[...] = jnp.zeros_like(base_ref)

    hit1 = lane == eid1
    hit2 = lane == eid2
    cnt = (hit1.astype(F32) + hit2.astype(F32))
    row = lax.broadcasted_iota(I32, (t, t), 0)
    col = lax.broadcasted_iota(I32, (t, t), 1)
    tril = jnp.where(col < row, 1.0, 0.0).astype(BF16)
    before = jnp.dot(tril, cnt.astype(BF16), preferred_element_type=F32) + base_ref[0:1, :]
    rank1 = jnp.sum(jnp.where(hit1, before, 0.0), axis=-1, keepdims=True)
    rank2 = jnp.sum(jnp.where(hit2, before, 0.0), axis=-1, keepdims=True)
    total = base_ref[0:1, :] + jnp.sum(cnt, axis=0, keepdims=True)
    base_ref[...] = jnp.broadcast_to(total, base_ref.shape)
    cnt_ref[...] = jnp.broadcast_to(total, cnt_ref.shape)

    ri = jnp.where(lane == 0, eid1, jnp.where(lane == 1, eid2, jnp.where(lane == 2, rank1,
                   jnp.where(lane == 3, rank2, 0.0))))
    ri_ref[...] = ri.T[0:8, :].astype(I32)
    l8 = lax.broadcasted_iota(I32, (t, 8), 1)
    rw_ref[...] = jnp.where(l8 == 0, w1, jnp.where(l8 == 1, w2, 0.0))


def _tail_out_shapes(b, n_rows, d):
    t = TOKEN_TILE
    shapes = [jax.ShapeDtypeStruct((b, n_rows, d), F32), jax.ShapeDtypeStruct((b * n_rows * ROW_SUB, LANES), F32),
              jax.ShapeDtypeStruct((8, b * n_rows), I32), jax.ShapeDtypeStruct((b * n_rows, 8), F32),
              jax.ShapeDtypeStruct((8, LANES), F32)]
    nt = n_rows // t
    flat = lambda w: pl.BlockSpec((t, w), lambda bi, ti: (bi * nt + ti, 0))
    specs = [pl.BlockSpec((1, t, d), lambda bi, ti: (bi, ti, 0)),
             pl.BlockSpec((t * ROW_SUB, LANES), lambda bi, ti: (bi * nt + ti, 0)),
             pl.BlockSpec((8, t), lambda bi, ti: (0, bi * nt + ti)), flat(8),
             pl.BlockSpec((8, LANES), lambda bi, ti: (0, 0))]
    return shapes, specs


def _l0_post_kernel(ctx_ref, x_ref, o_ref, gu_ref, gv_ref, wsp_ref, bsp_ref, wout_ref, mod_ref, gf_ref, wr_ref,
                    br_ref, x1_ref, f_ref, ri_ref, rw_ref, cnt_ref, base_ref, *, nct):
    tq = x_ref.shape[1]
    lane = lax.broadcasted_iota(I32, (GMLP_CHUNK, LANES), 1)
    low = lane < GMLP_GROUP_DIM
    gated = []
    for c in range(tq // GMLP_CHUNK):
        rows = slice(c * GMLP_CHUNK, (c + 1) * GMLP_CHUNK)
        mixed = []
        for m in range(GMLP_WIDTH // LANES):
            pair = gv_ref[0, rows, LANES * m:LANES * (m + 1)]
            a = jnp.dot(wsp_ref[2 * m], pair, preferred_element_type=F32)
            b = jnp.dot(wsp_ref[2 * m + 1], pair, preferred_element_type=F32)
            mixed.append(jnp.where(low, a, b))
        mixed = jnp.concatenate(mixed, axis=1) + bsp_ref[...]
        gated.append((gu_ref[0, rows, :].astype(F32) * mixed).astype(BF16))
    gated = jnp.concatenate(gated, axis=0)
    y = (jnp.dot(o_ref[0], wout_ref[:ATTN_WIDTH, :], preferred_element_type=F32)
         + jnp.dot(gated, wout_ref[ATTN_WIDTH:, :], preferred_element_type=F32))
    first = jnp.logical_and(pl.program_id(0) == 0, pl.program_id(1) == 0)
    _route_tail(_seq_tile(ctx_ref, x_ref, nct), y, mod_ref[0, 0], gf_ref[...], wr_ref, br_ref, base_ref,
                x1_ref, f_ref, ri_ref, rw_ref, cnt_ref, first)


def _l0_post(ctx, x, o, gu, gv, wsp, bsp, wout, mod, gf, wr, br, nct):
    b, s, d = x.shape
    l = s + ctx.shape[1]
    t = TOKEN_TILE
    tok = lambda w: pl.BlockSpec((1, t, w), lambda bi, ti: (bi, ti, 0))
    shapes, specs = _tail_out_shapes(b, l, d)
    return pl.pallas_call(
        functools.partial(_l0_post_kernel, nct=nct),
        out_shape=shapes,
        grid=(b, l // t),
        in_specs=_seq_specs(t, d, nct) + [tok(ATTN_WIDTH), tok(GMLP_WIDTH), tok(GMLP_WIDTH),
                  _full((GMLP_GROUPS, GMLP_CHUNK, GMLP_CHUNK)), _full((GMLP_CHUNK, GMLP_WIDTH)),
                  _full((ATTN_WIDTH + GMLP_WIDTH, d)),
                  pl.BlockSpec((1, 1, 6, d), lambda bi, ti: (bi, jnp.where(ti >= nct, 1, 0), 0, 0)),
                  _full((1, d)), _full((2, d, LANES)), _full((1, LANES))],
        out_specs=specs,
        scratch_shapes=[pltpu.VMEM((8, LANES), F32)],
        compiler_params=_params("arbitrary", "arbitrary"),
        name="l0_post",
    )(ctx, x, o, gu, gv, wsp, bsp, wout, mod, gf, wr, br)


def _row_copy(src_ref, src_row, dst_ref, dst_row, sem):
    rows = lambda ref, r: ref.at[pl.ds(pl.multiple_of(r * ROW_SUB, ROW_SUB), ROW_SUB)]
    return pltpu.make_async_copy(rows(src_ref, src_row), rows(dst_ref, dst_row), sem)


def _issue_rows(src_ref, index_of, buf_ref, sem_rows, s):
    n_str, rows = buf_ref.shape[1], buf_ref.shape[2] // ROW_SUB
    slot = s % 2

    def issue(j, carry):
        for half in range(2):
            r = 2 * j + half
            for k in range(n_str):
                _row_copy(src_ref, index_of(k * rows + r), buf_ref.at[slot, k], r,
                          sem_rows.at[slot]).start(priority=(k + half) % 2)
        return carry

    lax.fori_loop(0, rows // 2, issue, 0, unroll=ROW_DMA_UNROLL // 2)


def _wait_rows(src_ref, buf_ref, sem_rows, s):
    n_str, rows = buf_ref.shape[1], buf_ref.shape[2] // ROW_SUB
    slot = s % 2
    for k in range(n_str):
        pltpu.make_async_copy(src_ref.at[pl.ds(0, rows * ROW_SUB)], buf_ref.at[slot, k], sem_rows.at[slot]).wait()
    return slot


def _gather_pipeline(src_ref, table_ref, idx_ref, buf_ref, sem_idx, sem_rows, step, n_steps):
    def idx_copy(s):
        return pltpu.make_async_copy(table_ref.at[s], idx_ref.at[s % 2], sem_idx.at[s % 2])

    def issue_rows(s):
        _issue_rows(src_ref, lambda pos: idx_ref[s % 2, pos], buf_ref, sem_rows, s)

    @pl.when(step == 0)
    def _():
        idx_copy(0).start()
        idx_copy(0).wait()
        issue_rows(0)

        @pl.when(n_steps > 1)
        def _():
            idx_copy(1).start()

    @pl.when(step + 1 < n_steps)
    def _():
        idx_copy(step + 1).wait()
        issue_rows(step + 1)

        @pl.when(step + 2 < n_steps)
        def _():
            idx_copy(step + 2).start()

    return _wait_rows(src_ref, buf_ref, sem_rows, step)


def _invert_slots(slots_ref, pad_ref, inv_ref, stage_ref, sem):
    n_tiles, width = slots_ref.shape
    t = width // 2
    assert n_tiles % 2 == 0

    def clear(j, carry):
        inv_ref[j] = 0
        return carry

    for e in range(N_EXPERTS):
        lax.fori_loop(pad_ref[e], pad_ref[N_EXPERTS + e], clear, 0)

    def stage(s, slot):
        return pltpu.make_async_copy(slots_ref.at[s], stage_ref.at[slot], sem.at[slot])

    stage(0, 0).start()

    def tile_pair(p, carry):
        for slot in range(2):
            s = 2 * p + slot
            stage(s, slot).wait()

            @pl.when(s + 1 < n_tiles)
            def _():
                stage(s + 1, 1 - slot).start()

            base = s * t

            def row(r, c):
                inv_ref[stage_ref[slot, r]] = base + r
                inv_ref[stage_ref[slot, t + r]] = base + r
                return c

            lax.fori_loop(0, t, row, 0, unroll=ROW_DMA_UNROLL)
        return carry

    lax.fori_loop(0, n_tiles // 2, tile_pair, 0)


def _gather_scratch(n_str, rows):
    return [pltpu.SMEM((2, n_str * rows), I32), pltpu.VMEM((2, n_str, rows * ROW_SUB, LANES), F32),
            pltpu.SemaphoreType.DMA((2,)), pltpu.SemaphoreType.DMA((2,))]


def _experts_kernel(be_ref, nu_ref, pad_ref, f_ref, slots_ref, w1_ref, w3_ref, w2_ref, y_ref, w1b_ref, w3b_ref,
                    w2b_ref, inv_ref, stage_ref, xbuf_ref, sem_idx, sem_rows):
    i = pl.program_id(0)
    n_used = nu_ref[0]

    def issue_rows(s):
        _issue_rows(f_ref, lambda pos: inv_ref[s * MOE_BLOCK + pos], xbuf_ref, sem_rows, s)

    @pl.when(i == 0)
    def _():
        _invert_slots(slots_ref, pad_ref, inv_ref, stage_ref, sem_idx)
        issue_rows(0)

    @pl.when(jnp.logical_or(i == 0, be_ref[i] != be_ref[jnp.maximum(i - 1, 0)]))
    def _():
        w1b_ref[...] = w1_ref[...].astype(BF16)
        w3b_ref[...] = w3_ref[...].astype(BF16)
        w2b_ref[...] = w2_ref[...].astype(BF16)

    @pl.when(i < n_used)
    def _():
        @pl.when(i + 1 < n_used)
        def _():
            issue_rows(i + 1)

        slot = _wait_rows(f_ref, xbuf_ref, sem_rows, i)
        x = _load_row_tiles(xbuf_ref.at[slot, 0]).astype(BF16)
        h1 = jnp.dot(x, w1b_ref[...], preferred_element_type=F32)
        h3 = jnp.dot(x, w3b_ref[...], preferred_element_type=F32)
        a = (h1 * jax.nn.sigmoid(h1) * h3).astype(BF16)
        _store_row_tiles(y_ref, jnp.dot(a, w2b_ref[...], preferred_element_type=F32))

    @pl.when(i >= n_used)
    def _():
        y_ref[...] = jnp.zeros_like(y_ref)


def _experts(f, slots, w1, w3, w2, layer, block_e, n_used, pad):
    n_blocks = block_e.shape[0]
    _, _, d, hid = w1.shape
    rb = MOE_BLOCK * ROW_SUB
    wspec = lambda s: pl.BlockSpec((None, None) + s, lambda i, be, nu, pad: (layer, be[i], 0, 0))
    return pl.pallas_call(
        _experts_kernel,
        out_shape=jax.ShapeDtypeStruct((n_blocks * rb, LANES), F32),
        grid_spec=pltpu.PrefetchScalarGridSpec(
            num_scalar_prefetch=3,
            grid=(n_blocks,),
            in_specs=[pl.BlockSpec(memory_space=pl.ANY), pl.BlockSpec(memory_space=pltpu.VMEM),
                      wspec((d, hid)), wspec((d, hid)), wspec((hid, d))],
            out_specs=pl.BlockSpec((rb, LANES), lambda i, be, nu, pad: (i, 0)),
            scratch_shapes=[pltpu.VMEM((d, hid), BF16), pltpu.VMEM((d, hid), BF16), pltpu.VMEM((hid, d), BF16),
                            pltpu.SMEM((n_blocks * MOE_BLOCK,), I32), pltpu.SMEM((2, slots.shape[1]), I32),
                            pltpu.VMEM((2, 1, rb, LANES), F32),
                            pltpu.SemaphoreType.DMA((2,)), pltpu.SemaphoreType.DMA((2,))]),
        compiler_params=_params("arbitrary"),
        name="moe_experts",
    )(block_e, n_used, pad, f, slots, w1, w3, w2)


def _moe_plan(route_i, counts, n_tok):
    eid, rank = route_i[0:2], route_i[2:4]
    cnt = counts[0, :N_EXPERTS].astype(I32)
    padded = (cnt + MOE_BLOCK - 1) // MOE_BLOCK * MOE_BLOCK
    pend = jnp.cumsum(padded)
    pstart = pend - padded
    experts = jnp.arange(N_EXPERTS, dtype=I32)
    slots = rank + jnp.sum(jnp.where(eid[:, :, None] == experts, pstart, 0), axis=-1)
    n_assign = 2 * n_tok
    n_slots = (n_assign + MOE_BLOCK - 1) // MOE_BLOCK * MOE_BLOCK + N_EXPERTS * MOE_BLOCK
    n_blocks = n_slots // MOE_BLOCK
    bstart = jnp.arange(n_blocks, dtype=I32) * MOE_BLOCK
    block_e = jnp.minimum(jnp.sum(bstart[:, None] >= pend[None, :], axis=1), N_EXPERTS - 1).astype(I32)
    n_used = pend[-1] // MOE_BLOCK
    last_e = block_e[jnp.maximum(n_used - 1, 0)]
    block_e = jnp.where(jnp.arange(n_blocks) < n_used, block_e, last_e).astype(I32)
    slots = slots.reshape(2, n_tok // TOKEN_TILE, TOKEN_TILE).transpose(1, 0, 2)
    slots = slots.reshape(n_tok // TOKEN_TILE, 2 * TOKEN_TILE).astype(I32)
    pad = jnp.concatenate([pstart + cnt, pend]).astype(I32)
    return slots, block_e, n_used.reshape(1).astype(I32), pad


def _moe(f, route_i, counts, w1, w3, w2, layer):
    n_tok = f.shape[0] // ROW_SUB
    slots, block_e, n_used, pad = _moe_plan(route_i, counts, n_tok)
    y = _experts(f, slots, w1, w3, w2, layer, block_e, n_used, pad)
    return y, slots


def _combined_rows(y_ref, slots_ref, rw_ref, idx_ref, ybuf_ref, sem_idx, sem_rows):
    slot = _gather_pipeline(y_ref, slots_ref, idx_ref, ybuf_ref, sem_idx, sem_rows,
                            pl.program_id(0), pl.num_programs(0))
    rw = rw_ref[...]
    w1, w2 = rw[:, 0:1], rw[:, 1:2]
    t = ybuf_ref.shape[2] // ROW_SUB
    part = lambda k, s: ybuf_ref[slot, k, pl.ds(s, t, stride=ROW_SUB), :]
    return jnp.concatenate([w1 * part(0, s) + w2 * part(1, s) for s in range(ROW_SUB)], axis=1)


def _l0_combine_kernel(x_ref, rw_ref, slots_ref, y_ref, mod0_ref, mod1_ref, g_ref, win_ref,
                       x2_ref, u_ref, idx_ref, ybuf_ref, sem_idx, sem_rows):
    moe = _combined_rows(y_ref, slots_ref, rw_ref, idx_ref, ybuf_ref, sem_idx, sem_rows)
    x2 = x_ref[...] + mod0_ref[0, 0][5:6] * moe
    x2_ref[...] = x2
    m1 = mod1_ref[0, 0]
    h = _modulate(x2, g_ref[...], m1[0:1], m1[1:2]).astype(BF16)
    u_ref[...] = jnp.dot(h, win_ref[...], preferred_element_type=F32)


def _l0_combine(x1, rw, slots, y, mod0, mod1, g, win, b, nct):
    n_tok, d = x1.shape
    t = TOKEN_TILE
    nt = n_tok // b // t
    l = n_tok // b
    modspec = pl.BlockSpec((1, 1, 6, d), lambda i: (i // nt, jnp.where(i % nt >= nct, 1, 0), 0, 0))
    return pl.pallas_call(
        _l0_combine_kernel,
        out_shape=[jax.ShapeDtypeStruct((n_tok, d), F32), jax.ShapeDtypeStruct((l, b * S5_WIDTH), F32)],
        grid=(n_tok // t,),
        in_specs=[pl.BlockSpec((t, d), lambda i: (i, 0)), pl.BlockSpec((t, 8), lambda i: (i, 0)),
                  pl.BlockSpec(memory_space=pltpu.VMEM), pl.BlockSpec(memory_space=pl.ANY),
                  modspec, modspec, _full((1, d)), _full((d, S5_WIDTH))],
        out_specs=[pl.BlockSpec((t, d), lambda i: (i, 0)),
                   pl.BlockSpec((t, S5_WIDTH), lambda i: (i % nt, i // nt))],
        scratch_shapes=_gather_scratch(2, t),
        compiler_params=_params("arbitrary"),
        name="l0_combine",
    )(x1, rw, slots, y, mod0, mod1, g, win)


def _s5_zoh_kernel(are_ref, aim_ref, ldt_ref, bre_ref, bim_ref, lre_ref, lim_ref, bbre_ref, bbim_ref):
    ar, ai = are_ref[...], aim_ref[...]
    dt = jnp.exp(ldt_ref[...])
    mag = jnp.exp(ar * dt)
    lr = mag * jnp.cos(ai * dt)
    li = mag * jnp.sin(ai * dt)
    lre_ref[...] = lr
    lim_ref[...] = li
    nr, ni = lr - 1.0, li
    den = ar * ar + ai * ai
    cr = (nr * ar + ni * ai) / den
    ci = (ni * ar - nr * ai) / den
    br, bi = bre_ref[...], bim_ref[...]
    bbre_ref[...] = cr * br - ci * bi
    bbim_ref[...] = cr * bi + ci * br


def _s5_zoh(a_re, a_im, log_dt, b_re, b_im):
    n_dir, g, n, p = b_re.shape
    rows = n_dir * g * p
    expand = lambda a: jnp.broadcast_to(a[:, :, None, :], (n_dir, g, p, n)).reshape(rows, n)
    ldt = jnp.broadcast_to(log_dt[:, :, None, None], (n_dir, g, p, n)).reshape(rows, n)
    bt = lambda a: jnp.transpose(a, (0, 1, 3, 2)).reshape(rows, n)
    outs = pl.pallas_call(
        _s5_zoh_kernel,
        out_shape=[jax.ShapeDtypeStruct((rows, n), F32)] * 4,
        name="s5_zoh",
    )(expand(a_re), expand(a_im), ldt, bt(b_re), bt(b_im))
    lre, lim, bbre, bbim = (o.reshape(n_dir, g, p, n) for o in outs)
    return lre[:, :, 0, :], lim[:, :, 0, :], bbre, bbim


def _s5_matrices(lre, lim, bbre, bbim, c_re, c_im):
    n_dir = lre.shape[0]
    nj = S5_GROUPS // (2 * S5_TILES)
    shp = (n_dir, nj, S5_TILES, 2, S5_GROUP_DIM, S5_STATE)
    eye_t = jnp.eye(S5_TILES, dtype=F32)
    eye_g = jnp.eye(2, dtype=F32)

    def place(a):
        return jnp.einsum("djtgpn,tu,gh->djtgpuhn", a.reshape(shp), eye_t, eye_g)

    kdim = S5_TILES * 2 * S5_GROUP_DIM
    sdim = S5_TILES * 2 * 2 * S5_STATE
    bm = jnp.stack([place(bbre), place(bbim)], axis=6)
    bm = bm.reshape(n_dir, nj, kdim, sdim).astype(BF16)
    cm = jnp.stack([place(c_re), place(-c_im)], axis=6)
    cm = jnp.transpose(cm, (0, 1, 5, 6, 7, 8, 2, 3, 4)).reshape(n_dir, nj, sdim, kdim).astype(BF16)
    lam = jnp.stack([lre.reshape(n_dir, nj, S5_TILES * LANES), lim.reshape(n_dir, nj, S5_TILES * LANES)], axis=2)
    return bm, cm, lam


def _s5_scan_kernel(uf_ref, ub_ref, bm_ref, cm_ref, lam_ref, yf_ref, yb_ref, v_ref, h_ref, *, tc, nb):
    @pl.when(pl.program_id(0) == 0)
    def _():
        h_ref[...] = jnp.zeros_like(h_ref)

    nj = bm_ref.shape[1]
    kdim = bm_ref.shape[2]
    for d, (u_ref, y_ref) in enumerate(((uf_ref, yf_ref), (ub_ref, yb_ref))):
        for jq in range(nj):
            uk = u_ref[:, kdim * jq:kdim * (jq + 1)].astype(BF16)
            v_ref[...] = jnp.dot(uk, bm_ref[d, jq], preferred_element_type=F32)
            lam = lam_ref[d, jq]
            lr = [jnp.broadcast_to(lam[0:1, LANES * t:LANES * (t + 1)], (nb, LANES)) for t in range(S5_TILES)]
            li = [jnp.broadcast_to(lam[1:2, LANES * t:LANES * (t + 1)], (nb, LANES)) for t in range(S5_TILES)]
            h0 = h_ref[d, jq]
            init = tuple(h0[:, LANES * c:LANES * (c + 1)] for c in range(2 * S5_TILES))

            def step(s, carry, d=d, lr=lr, li=li):
                tt = s if d == 0 else tc - 1 - s
                r0 = pl.multiple_of(tt * nb, nb)
                rows = v_ref[pl.ds(r0, nb), :]
                new = []
                for t in range(S5_TILES):
                    vr = rows[:, 2 * LANES * t:2 * LANES * t + LANES]
                    vi = rows[:, 2 * LANES * t + LANES:2 * LANES * (t + 1)]
                    hr, hi = carry[2 * t], carry[2 * t + 1]
                    new.append(lr[t] * hr - li[t] * hi + vr)
                    new.append(lr[t] * hi + li[t] * hr + vi)
                v_ref[pl.ds(r0, nb), :] = jnp.concatenate(new, axis=1)
                return tuple(new)

            fin = lax.fori_loop(0, tc, step, init)
            h_ref[d, jq] = jnp.concatenate(fin, axis=1)
            y_ref[:, kdim * jq:kdim * (jq + 1)] = jnp.dot(v_ref[...].astype(BF16), cm_ref[d, jq],
                                                         preferred_element_type=F32)


def _s5_scan(u_tm, bm, cm, lam, nb, n_ctx):
    rows_total, w = u_tm.shape
    tc = S5_CHUNK
    r = tc * nb
    n_chunks = rows_total // r
    ncc = n_ctx // tc
    bwd = lambda i: (jnp.where(i < ncc, ncc - 1 - i, n_chunks - 1 - (i - ncc)), 0)
    nj = bm.shape[1]
    sdim = bm.shape[3]
    kern = functools.partial(_s5_scan_kernel, tc=tc, nb=nb)
    return pl.pallas_call(
        kern,
        out_shape=[jax.ShapeDtypeStruct((rows_total, w), F32)] * 2,
        grid=(n_chunks,),
        in_specs=[pl.BlockSpec((r, w), lambda i: (i, 0)), pl.BlockSpec((r, w), bwd),
                  _full(bm.shape), _full(cm.shape), _full(lam.shape)],
        out_specs=[pl.BlockSpec((r, w), lambda i: (i, 0)), pl.BlockSpec((r, w), bwd)],
        scratch_shapes=[pltpu.VMEM((r, sdim), F32), pltpu.VMEM((2, nj, nb, sdim), F32)],
        compiler_params=_params("arbitrary"),
        name="s5_scan",
    )(u_tm, u_tm, bm, cm, lam)


def _l1_post_kernel(x_ref, u_ref, yf_ref, yb_ref, dsk_ref, wglu_ref, bglu_ref, wout_ref, mod_ref, gf_ref,
                    wr_ref, br_ref, x3_ref, f_ref, ri_ref, rw_ref, cnt_ref, base_ref):
    y = dsk_ref[...] * u_ref[...] + yf_ref[...] + yb_ref[...]
    g = jax.nn.gelu(y, approximate=True)
    z = jnp.dot(g.astype(BF16), wglu_ref[...], preferred_element_type=F32) + bglu_ref[...]
    o = jnp.dot((g * jax.nn.sigmoid(z)).astype(BF16), wout_ref[...], preferred_element_type=F32)
    first = jnp.logical_and(pl.program_id(0) == 0, pl.program_id(1) == 0)
    _route_tail(x_ref[0], o, mod_ref[0, 0], gf_ref[...], wr_ref, br_ref, base_ref,
                x3_ref, f_ref, ri_ref, rw_ref, cnt_ref, first)


def _l1_post(x2, u_tm, yf, yb, dsk, wglu, bglu, wout, mod, gf, wr, br, n_ctx):
    b, l, d = x2.shape
    t = TOKEN_TILE
    nct = n_ctx // t
    s = l - n_ctx
    tm = pl.BlockSpec((t, S5_WIDTH), lambda bi, ti: (ti + nct, bi))
    shapes, specs = _tail_out_shapes(b, s, d)
    return pl.pallas_call(
        _l1_post_kernel,
        out_shape=shapes,
        grid=(b, s // t),
        in_specs=[pl.BlockSpec((1, t, d), lambda bi, ti: (bi, ti + nct, 0)), tm, tm, tm,
                  _full((1, S5_WIDTH)), _full((S5_WIDTH, S5_WIDTH)), _full((1, S5_WIDTH)), _full((S5_WIDTH, d)),
                  pl.BlockSpec((1, 1, 6, d), lambda bi, ti: (bi, 1, 0, 0)),
                  _full((1, d)), _full((2, d, LANES)), _full((1, LANES))],
        out_specs=specs,
        scratch_shapes=[pltpu.VMEM((8, LANES), F32)],
        compiler_params=_params("arbitrary", "arbitrary"),
        name="l1_post",
    )(x2, u_tm, yf, yb, dsk, wglu, bglu, wout, mod, gf, wr, br)


def _final_kernel(x_ref, rw_ref, slots_ref, y_ref, mod_ref, g_ref, o_ref, idx_ref, ybuf_ref, sem_idx, sem_rows):
    moe = _combined_rows(y_ref, slots_ref, rw_ref, idx_ref, ybuf_ref, sem_idx, sem_rows)
    x4 = x_ref[...] + mod_ref[0, 0][5:6] * moe
    o_ref[...] = _rms(x4) * g_ref[...]


def _final(x3, rw, slots, y, mod, g, b):
    n_tok, d = x3.shape
    t = TOKEN_TILE
    nt = n_tok // b // t
    return pl.pallas_call(
        _final_kernel,
        out_shape=jax.ShapeDtypeStruct((n_tok, d), F32),
        grid=(n_tok // t,),
        in_specs=[pl.BlockSpec((t, d), lambda i: (i, 0)), pl.BlockSpec((t, 8), lambda i: (i, 0)),
                  pl.BlockSpec(memory_space=pltpu.VMEM), pl.BlockSpec(memory_space=pl.ANY),
                  pl.BlockSpec((1, 1, 6, d), lambda i: (i // nt, 1, 0, 0)), _full((1, d))],
        out_specs=pl.BlockSpec((t, d), lambda i: (i, 0)),
        scratch_shapes=_gather_scratch(2, t),
        compiler_params=_params("arbitrary"),
        name="final_combine",
    )(x3, rw, slots, y, mod, g)


def _rope_tables(n_ctx, n_lat):
    pos = jnp.arange(n_lat, dtype=I32)
    row = (pos // GRID_W).astype(F32)
    col = (pos % GRID_W).astype(F32)
    inv_freq = jnp.power(ROPE_THETA, -jnp.arange(0, AXIS_DIM, 2, dtype=F32) / AXIS_DIM)
    lane = np.arange(LANES)
    dd = lane % HEAD_DIM
    use_col = dd >= AXIS_DIM
    first = (dd % AXIS_DIM) < AXIS_DIM // 2
    freq = inv_freq[(dd % AXIS_DIM) % (AXIS_DIM // 2)]
    ang = jnp.where(use_col[None, :], col[:, None], row[:, None]) * freq[None, :]
    cos, sin = jnp.cos(ang), jnp.sin(ang)
    sa = jnp.where(first[None, :], -sin, 0.0)
    sb = jnp.where(first[None, :], 0.0, sin)
    pad = lambda a, v: jnp.concatenate([jnp.full((n_ctx, LANES), v, F32), a], axis=0)
    return pad(cos, 1.0), pad(sa, 0.0), pad(sb, 0.0)


def _group_ones(width, group):
    idx = np.arange(width) // group
    return jnp.asarray(idx[:, None] == idx[None, :], dtype=BF16)


def _router_weights(w_grp, b_grp, w_rt, b_rt):
    d = w_grp.shape[0]
    used = N_EXPERT_GROUPS + N_EXPERTS
    wr = jnp.concatenate([w_grp, w_rt, jnp.zeros((d, LANES - used), F32)], axis=1)
    br = jnp.concatenate([b_grp, b_rt, jnp.zeros((LANES - used,), F32)]).reshape(1, LANES)
    return jnp.stack(_split_bf16(wr)), br


def kernel(x, c, ctx, c_ctx, w_mod, b_mod, norm_mix_g, norm_ffn_g, mix_w_in, mix_w_out, q_norm_g, k_norm_g, gmlp_norm_g, gmlp_w_spatial, gmlp_b_spatial, s5_w_in, s5_a_re, s5_a_im, s5_log_dt, s5_b_re, s5_b_im, s5_c_re, s5_c_im, s5_d, s5_w_glu, s5_b_glu, s5_w_out, moe_w_group, moe_b_group, moe_w_router, moe_b_router, moe_w1, moe_w3, moe_w2, final_norm_g):
    b, s, d = x.shape
    n_ctx = ctx.shape[1]
    l = n_ctx + s
    assert w_mod.shape[0] == 2, "two layers: attention/gMLP then S5"
    assert n_ctx % TOKEN_TILE == 0 and s % TOKEN_TILE == 0 and b % 8 == 0 and d == D_MODEL
    nct = n_ctx // TOKEN_TILE

    r = (b + 1 + 7) // 8 * 8
    cond = jnp.concatenate([c, c_ctx[None, :], jnp.zeros((r - b - 1, d), F32)], axis=0)
    mods = _adaln(cond, w_mod, b_mod)
    lat = mods[:, :b].reshape(2, b, 1, 6, d)
    cxt = jnp.broadcast_to(mods[:, b].reshape(2, 1, 1, 6, d), (2, b, 1, 6, d))
    mod = jnp.concatenate([cxt, lat], axis=2)

    w_in = jnp.concatenate([mix_w_in[0][:, _Q_COL_PERM], mix_w_in[0][:, ATTN_WIDTH:]], axis=1).astype(BF16)
    w_out = jnp.concatenate([mix_w_out[0][_Q_COL_PERM], mix_w_out[0][ATTN_WIDTH:]], axis=0).astype(BF16)
    cos, sa, sb = _rope_tables(n_ctx, s)
    qg = jnp.tile(q_norm_g[0], N_Q_HEADS).reshape(1, ATTN_WIDTH)
    kg = jnp.tile(k_norm_g[0], N_KV_HEADS).reshape(1, KV_WIDTH)
    q, k, v, gu, gv = _l0_proj(ctx, x, mod[0], norm_mix_g[0].reshape(1, d), w_in, qg, kg,
                               gmlp_norm_g[0].reshape(1, GMLP_WIDTH), cos, sa, sb,
                               _group_ones(ATTN_WIDTH, HEAD_DIM), _group_ones(KV_WIDTH, HEAD_DIM), nct)
    o = _attention(q, k, v, n_ctx)
    bsp = jnp.repeat(gmlp_b_spatial[0].T, GMLP_GROUP_DIM, axis=1)
    wr0, br0 = _router_weights(moe_w_group[0], moe_b_group[0], moe_w_router[0], moe_b_router[0])
    x1, f0, ri0, rw0, cnt0 = _l0_post(ctx, x, o, gu, gv, gmlp_w_spatial[0].astype(BF16), bsp, w_out, mod[0],
                                      norm_ffn_g[0].reshape(1, d), wr0, br0, nct)
    y0, slots0 = _moe(f0, ri0, cnt0, moe_w1, moe_w3, moe_w2, 0)
    x2, u_tm = _l0_combine(x1.reshape(b * l, d), rw0, slots0, y0, mod[0], mod[1], norm_mix_g[1].reshape(1, d),
                           s5_w_in[0].astype(BF16), b, nct)

    lre, lim, bbre, bbim = _s5_zoh(s5_a_re[0], s5_a_im[0], s5_log_dt[0], s5_b_re[0], s5_b_im[0])
    bm, cm, lam = _s5_matrices(lre, lim, bbre, bbim, s5_c_re[0], s5_c_im[0])
    yf, yb = _s5_scan(u_tm.reshape(l * b, S5_WIDTH), bm, cm, lam, b, n_ctx)
    wr1, br1 = _router_weights(moe_w_group[1], moe_b_group[1], moe_w_router[1], moe_b_router[1])
    tmv = lambda a: a.reshape(l, b * S5_WIDTH)
    x3, f1, ri1, rw1, cnt1 = _l1_post(x2.reshape(b, l, d), u_tm, tmv(yf), tmv(yb), s5_d[0].reshape(1, S5_WIDTH),
                                      s5_w_glu[0].astype(BF16), s5_b_glu[0].reshape(1, S5_WIDTH),
                                      s5_w_out[0].astype(BF16), mod[1], norm_ffn_g[1].reshape(1, d), wr1, br1, n_ctx)
    y1, slots1 = _moe(f1, ri1, cnt1, moe_w1, moe_w3, moe_w2, 1)
    out = _final(x3.reshape(b * s, d), rw1, slots1, y1, mod[1], final_norm_g.reshape(1, d), b)
    return out.reshape(b, s, d)
```

```python
import functools

import numpy as np
import jax
import jax.numpy as jnp
from jax import lax
from jax.experimental import pallas as pl
from jax.experimental.pallas import tpu as pltpu

F32 = jnp.float32
BF16 = jnp.bfloat16
I32 = jnp.int32

EPS = 1e-6
GRID_W = 64
HEAD_DIM = 64
N_Q_HEADS = 8
N_KV_HEADS = 2
ATTN_WIDTH = N_Q_HEADS * HEAD_DIM
KV_WIDTH = N_KV_HEADS * HEAD_DIM
AXIS_DIM = HEAD_DIM // 2
ROPE_THETA = 10000.0
GMLP_GROUPS = 8
GMLP_GROUP_DIM = 64
GMLP_WIDTH = GMLP_GROUPS * GMLP_GROUP_DIM
GMLP_CHUNK = 128
MIX_IN_WIDTH = ATTN_WIDTH + 2 * KV_WIDTH + 2 * GMLP_WIDTH
S5_GROUPS = 32
S5_GROUP_DIM = 16
S5_STATE = 64
S5_WIDTH = S5_GROUPS * S5_GROUP_DIM
N_EXPERT_GROUPS = 4
EXPERTS_PER_GROUP = 8
N_EXPERTS = N_EXPERT_GROUPS * EXPERTS_PER_GROUP
EXPERT_HIDDEN = 512
MOE_BLOCK = 256

D_MODEL = 1024
LANES = 128
ROW_SUB = D_MODEL // LANES
TOKEN_TILE = 256
ROW_DMA_UNROLL = 8
S5_CHUNK = 64
S5_TILES = 4
VMEM_LIMIT = 48 * 1024 * 1024
NEG_BIG = -1e30

_Q_HEAD_ORDER = (0, 4, 1, 5, 2, 6, 3, 7)
_Q_COL_PERM = np.concatenate([np.arange(HEAD_DIM) + HEAD_DIM * h for h in _Q_HEAD_ORDER])


def _params(*sem):
    return pltpu.CompilerParams(dimension_semantics=sem, vmem_limit_bytes=VMEM_LIMIT)


def _rms(x):
    return x * lax.rsqrt(jnp.mean(x * x, axis=-1, keepdims=True) + EPS)


def _modulate(x, g, shift, scale):
    return _rms(x) * g * (1.0 + scale) + shift


def _full(shape):
    return pl.BlockSpec(shape, lambda *_: (0,) * len(shape))


def _split_bf16(a):
    hi = a.astype(BF16)
    return hi, (a - hi.astype(F32)).astype(BF16)


def _dot_split(a, w_hi, w_lo):
    a_hi, a_lo = _split_bf16(a)
    dot = functools.partial(jnp.dot, preferred_element_type=F32)
    return dot(a_hi, w_hi) + (dot(a_lo, w_hi) + dot(a_hi, w_lo))


def _store_row_tiles(ref, val):
    n = val.shape[0]
    for s in range(ROW_SUB):
        ref[pl.ds(s, n, stride=ROW_SUB), :] = val[:, LANES * s:LANES * (s + 1)]


def _load_row_tiles(ref):
    n = ref.shape[0] // ROW_SUB
    return jnp.concatenate([ref[pl.ds(s, n, stride=ROW_SUB), :] for s in range(ROW_SUB)], axis=1)


def _adaln_kernel(cond_ref, w_ref, b_ref, o_ref):
    c = cond_ref[...]
    s = c * jax.nn.sigmoid(c)
    o_ref[0] = _dot_split(s, *_split_bf16(w_ref[0])) + b_ref[0]


def _adaln(cond, w_mod, b_mod):
    depth, d, n = w_mod.shape
    r = cond.shape[0]
    tn = 1536
    return pl.pallas_call(
        _adaln_kernel,
        out_shape=jax.ShapeDtypeStruct((depth, r, n), F32),
        grid=(depth, n // tn),
        in_specs=[pl.BlockSpec((r, d), lambda l, j: (0, 0)),
                  pl.BlockSpec((1, d, tn), lambda l, j: (l, 0, j)),
                  pl.BlockSpec((1, 1, tn), lambda l, j: (l, 0, j))],
        out_specs=pl.BlockSpec((1, r, tn), lambda l, j: (l, 0, j)),
        compiler_params=_params("parallel", "parallel"),
        name="adaln",
    )(cond, w_mod, b_mod.reshape(depth, 1, n))


def _seq_tile(ctx_ref, x_ref, nct):
    return jnp.where(pl.program_id(1) < nct, ctx_ref[0], x_ref[0])


def _seq_specs(t, d, nct):
    return [pl.BlockSpec((1, t, d), lambda bi, ti: (bi, jnp.minimum(ti, nct - 1), 0)),
            pl.BlockSpec((1, t, d), lambda bi, ti: (bi, jnp.maximum(ti - nct, 0), 0))]


def _l0_proj_kernel(ctx_ref, x_ref, mod_ref, g_ref, win_ref, qg_ref, kg_ref, lng_ref, cos_ref, sa_ref, sb_ref,
                    gq_ref, gk_ref, q_ref, k_ref, v_ref, gu_ref, gv_ref, *, nct):
    m = mod_ref[0, 0]
    h = _modulate(_seq_tile(ctx_ref, x_ref, nct), g_ref[...], m[0:1], m[1:2]).astype(BF16)
    z = jnp.dot(h, win_ref[...], preferred_element_type=F32)
    cos, sa, sb = cos_ref[...], sa_ref[...], sb_ref[...]

    def head_norm_rope(t, gsum_ref, gain, reps):
        w = t.shape[1]
        ss = jnp.dot((t * t).astype(BF16), gsum_ref[...], preferred_element_type=F32)
        tn = t * lax.rsqrt(ss * (1.0 / HEAD_DIM) + EPS) * gain
        c, a, b = (jnp.concatenate([tab] * reps, axis=1) if reps > 1 else tab for tab in (cos, sa, sb))
        return tn * c + pltpu.roll(tn, w - AXIS_DIM // 2, 1) * a + pltpu.roll(tn, AXIS_DIM // 2, 1) * b

    q = head_norm_rope(z[:, :ATTN_WIDTH], gq_ref, qg_ref[...], ATTN_WIDTH // LANES)
    q_ref[0] = (q * (HEAD_DIM ** -0.5)).astype(BF16)
    k = head_norm_rope(z[:, ATTN_WIDTH:ATTN_WIDTH + KV_WIDTH], gk_ref, kg_ref[...], 1)
    k_ref[0] = k.astype(BF16)
    v_ref[0] = z[:, ATTN_WIDTH + KV_WIDTH:ATTN_WIDTH + 2 * KV_WIDTH].astype(BF16)
    off = ATTN_WIDTH + 2 * KV_WIDTH
    gu_ref[0] = jax.nn.gelu(z[:, off:off + GMLP_WIDTH], approximate=True).astype(BF16)
    gv = jax.nn.gelu(z[:, off + GMLP_WIDTH:], approximate=True)
    mu = jnp.mean(gv, axis=-1, keepdims=True)
    gc = gv - mu
    gn = gc * lax.rsqrt(jnp.mean(gc * gc, axis=-1, keepdims=True) + EPS) * lng_ref[...]
    gv_ref[0] = gn.astype(BF16)


def _l0_proj(ctx, x, mod, g, w_in, qg, kg, lng, cos, sa, sb, gq, gk, nct):
    b, s, d = x.shape
    l = s + ctx.shape[1]
    t = TOKEN_TILE
    tok = lambda w: pl.BlockSpec((1, t, w), lambda bi, ti: (bi, ti, 0))
    tab = pl.BlockSpec((t, LANES), lambda bi, ti: (ti, 0))
    outs = [jax.ShapeDtypeStruct((b, l, w), BF16) for w in (ATTN_WIDTH, KV_WIDTH, KV_WIDTH, GMLP_WIDTH, GMLP_WIDTH)]
    return pl.pallas_call(
        functools.partial(_l0_proj_kernel, nct=nct),
        out_shape=outs,
        grid=(b, l // t),
        in_specs=_seq_specs(t, d, nct) + [
                  pl.BlockSpec((1, 1, 6, d), lambda bi, ti: (bi, jnp.where(ti >= nct, 1, 0), 0, 0)),
                  _full((1, d)), _full((d, MIX_IN_WIDTH)), _full((1, ATTN_WIDTH)), _full((1, KV_WIDTH)),
                  _full((1, GMLP_WIDTH)), tab, tab, tab,
                  _full((ATTN_WIDTH, ATTN_WIDTH)), _full((KV_WIDTH, KV_WIDTH))],
        out_specs=[tok(w) for w in (ATTN_WIDTH, KV_WIDTH, KV_WIDTH, GMLP_WIDTH, GMLP_WIDTH)],
        compiler_params=_params("parallel", "parallel"),
        name="l0_proj",
    )(ctx, x, mod, g, w_in, qg, kg, lng, cos, sa, sb, gq, gk)


def _attn_kernel(q_ref, k_ref, v_ref, o_ref, *, n_ctx, nct):
    ti = pl.program_id(1)
    tq = q_ref.shape[1]
    lane = lax.broadcasted_iota(I32, (tq, LANES), 1)
    low = lane < HEAD_DIM

    def attend(nk):
        kk = k_ref[0, :nk, :]
        vv = v_ref[0, :nk, :]
        for m in range(ATTN_WIDTH // LANES):
            qp = q_ref[0, :, LANES * m:LANES * (m + 1)]
            halves = []
            for sel in (low, jnp.logical_not(low)):
                qm = jnp.where(sel, qp, jnp.zeros_like(qp))
                s = lax.dot_general(qm, kk, (((1,), (1,)), ((), ())), preferred_element_type=F32)
                p = jnp.exp(s - jnp.max(s, axis=-1, keepdims=True))
                den = jnp.sum(p, axis=-1, keepdims=True)
                pv = jnp.dot(p.astype(BF16), vv, preferred_element_type=F32)
                halves.append(pv / den)
            o_ref[0, :, LANES * m:LANES * (m + 1)] = jnp.where(low, halves[0], halves[1]).astype(BF16)

    @pl.when(ti < nct)
    def _():
        attend(n_ctx)

    @pl.when(ti >= nct)
    def _():
        attend(k_ref.shape[1])


def _attention(q, k, v, n_ctx):
    b, l, _ = q.shape
    t = TOKEN_TILE
    kern = functools.partial(_attn_kernel, n_ctx=n_ctx, nct=n_ctx // t)
    return pl.pallas_call(
        kern,
        out_shape=jax.ShapeDtypeStruct((b, l, ATTN_WIDTH), BF16),
        grid=(b, l // t),
        in_specs=[pl.BlockSpec((1, t, ATTN_WIDTH), lambda bi, ti: (bi, ti, 0)),
                  pl.BlockSpec((1, l, KV_WIDTH), lambda bi, ti: (bi, 0, 0)),
                  pl.BlockSpec((1, l, KV_WIDTH), lambda bi, ti: (bi, 0, 0))],
        out_specs=pl.BlockSpec((1, t, ATTN_WIDTH), lambda bi, ti: (bi, ti, 0)),
        compiler_params=_params("parallel", "parallel"),
        name="attention",
    )(q, k, v)


def _route_tail(x, y, m, gf, wr_ref, br_ref, base_ref, x_ref, f_ref, ri_ref, rw_ref, cnt_ref, first):
    x1 = x + m[2:3] * y
    x_ref[0] = x1
    f = _modulate(x1, gf, m[3:4], m[4:5])
    _store_row_tiles(f_ref, f)
    t = f.shape[0]
    logits = _dot_split(f, wr_ref[0], wr_ref[1]) + br_ref[...]
    lane = lax.broadcasted_iota(I32, (t, LANES), 1).astype(F32)
    is_grp = lane < N_EXPERT_GROUPS
    gl = jnp.where(is_grp, logits, NEG_BIG)
    gmax = jnp.max(gl, axis=-1, keepdims=True)
    g_idx = jnp.min(jnp.where(gl == gmax, lane, float(LANES)), axis=-1, keepdims=True)
    g_w = 1.0 / jnp.sum(jnp.where(is_grp, jnp.exp(gl - gmax), 0.0), axis=-1, keepdims=True)
    e_lo = N_EXPERT_GROUPS + EXPERTS_PER_GROUP * g_idx
    in_grp = jnp.logical_and(lane >= e_lo, lane < e_lo + EXPERTS_PER_GROUP)
    sel = jnp.where(in_grp, logits, NEG_BIG)
    v1 = jnp.max(sel, axis=-1, keepdims=True)
    i1 = jnp.min(jnp.where(sel == v1, lane, float(LANES)), axis=-1, keepdims=True)
    sel2 = jnp.where(lane == i1, NEG_BIG, sel)
    v2 = jnp.max(sel2, axis=-1, keepdims=True)
    i2 = jnp.min(jnp.where(sel2 == v2, lane, float(LANES)), axis=-1, keepdims=True)
    e = jnp.exp(v2 - v1)
    w1 = g_w / (1.0 + e)
    w2 = g_w * e / (1.0 + e)
    eid1 = i1 - N_EXPERT_GROUPS
    eid2 = i2 - N_EXPERT_GROUPS

    @pl.when(first)
    def _():
        base_ref[...] = jnp.zeros_like(base_ref)

    hit1 = lane == eid1
    hit2 = lane == eid2
    cnt = (hit1.astype(F32) + hit2.astype(F32))
    row = lax.broadcasted_iota(I32, (t, t), 0)
    col = lax.broadcasted_iota(I32, (t, t), 1)
    tril = jnp.where(col < row, 1.0, 0.0).astype(BF16)
    before = jnp.dot(tril, cnt.astype(BF16), preferred_element_type=F32) + base_ref[0:1, :]
    rank1 = jnp.sum(jnp.where(hit1, before, 0.0), axis=-1, keepdims=True)
    rank2 = jnp.sum(jnp.where(hit2, before, 0.0), axis=-1, keepdims=True)
    total = base_ref[0:1, :] + jnp.sum(cnt, axis=0, keepdims=True)
    base_ref[...] = jnp.broadcast_to(total, base_ref.shape)
    cnt_ref[...] = jnp.broadcast_to(total, cnt_ref.shape)

    ri = jnp.where(lane == 0, eid1, jnp.where(lane == 1, eid2, jnp.where(lane == 2, rank1,
                   jnp.where(lane == 3, rank2, 0.0))))
    ri_ref[...] = ri.T[0:8, :].astype(I32)
    l8 = lax.broadcasted_iota(I32, (t, 8), 1)
    rw_ref[...] = jnp.where(l8 == 0, w1, jnp.where(l8 == 1, w2, 0.0))


def _tail_out_shapes(b, n_rows, d):
    t = TOKEN_TILE
    shapes = [jax.ShapeDtypeStruct((b, n_rows, d), F32), jax.ShapeDtypeStruct((b * n_rows * ROW_SUB, LANES), F32),
              jax.ShapeDtypeStruct((8, b * n_rows), I32), jax.ShapeDtypeStruct((b * n_rows, 8), F32),
              jax.ShapeDtypeStruct((8, LANES), F32)]
    nt = n_rows // t
    flat = lambda w: pl.BlockSpec((t, w), lambda bi, ti: (bi * nt + ti, 0))
    specs = [pl.BlockSpec((1, t, d), lambda bi, ti: (bi, ti, 0)),
             pl.BlockSpec((t * ROW_SUB, LANES), lambda bi, ti: (bi * nt + ti, 0)),
             pl.BlockSpec((8, t), lambda bi, ti: (0, bi * nt + ti)), flat(8),
             pl.BlockSpec((8, LANES), lambda bi, ti: (0, 0))]
    return shapes, specs


def _l0_post_kernel(ctx_ref, x_ref, o_ref, gu_ref, gv_ref, wsp_ref, bsp_ref, wout_ref, mod_ref, gf_ref, wr_ref,
                    br_ref, x1_ref, f_ref, ri_ref, rw_ref, cnt_ref, base_ref, *, nct):
    tq = x_ref.shape[1]
    lane = lax.broadcasted_iota(I32, (GMLP_CHUNK, LANES), 1)
    low = lane < GMLP_GROUP_DIM
    gated = []
    for c in range(tq // GMLP_CHUNK):
        rows = slice(c * GMLP_CHUNK, (c + 1) * GMLP_CHUNK)
        mixed = []
        for m in range(GMLP_WIDTH // LANES):
            pair = gv_ref[0, rows, LANES * m:LANES * (m + 1)]
            a = jnp.dot(wsp_ref[2 * m], pair, preferred_element_type=F32)
            b = jnp.dot(wsp_ref[2 * m + 1], pair, preferred_element_type=F32)
            mixed.append(jnp.where(low, a, b))
        mixed = jnp.concatenate(mixed, axis=1) + bsp_ref[...]
        gated.append((gu_ref[0, rows, :].astype(F32) * mixed).astype(BF16))
    gated = jnp.concatenate(gated, axis=0)
    y = (jnp.dot(o_ref[0], wout_ref[:ATTN_WIDTH, :], preferred_element_type=F32)
         + jnp.dot(gated, wout_ref[ATTN_WIDTH:, :], preferred_element_type=F32))
    first = jnp.logical_and(pl.program_id(0) == 0, pl.program_id(1) == 0)
    _route_tail(_seq_tile(ctx_ref, x_ref, nct), y, mod_ref[0, 0], gf_ref[...], wr_ref, br_ref, base_ref,
                x1_ref, f_ref, ri_ref, rw_ref, cnt_ref, first)


def _l0_post(ctx, x, o, gu, gv, wsp, bsp, wout, mod, gf, wr, br, nct):
    b, s, d = x.shape
    l = s + ctx.shape[1]
    t = TOKEN_TILE
    tok = lambda w: pl.BlockSpec((1, t, w), lambda bi, ti: (bi, ti, 0))
    shapes, specs = _tail_out_shapes(b, l, d)
    return pl.pallas_call(
        functools.partial(_l0_post_kernel, nct=nct),
        out_shape=shapes,
        grid=(b, l // t),
        in_specs=_seq_specs(t, d, nct) + [tok(ATTN_WIDTH), tok(GMLP_WIDTH), tok(GMLP_WIDTH),
                  _full((GMLP_GROUPS, GMLP_CHUNK, GMLP_CHUNK)), _full((GMLP_CHUNK, GMLP_WIDTH)),
                  _full((ATTN_WIDTH + GMLP_WIDTH, d)),
                  pl.BlockSpec((1, 1, 6, d), lambda bi, ti: (bi, jnp.where(ti >= nct, 1, 0), 0, 0)),
                  _full((1, d)), _full((2, d, LANES)), _full((1, LANES))],
        out_specs=specs,
        scratch_shapes=[pltpu.VMEM((8, LANES), F32)],
        compiler_params=_params("arbitrary", "arbitrary"),
        name="l0_post",
    )(ctx, x, o, gu, gv, wsp, bsp, wout, mod, gf, wr, br)


def _row_copy(src_ref, src_row, dst_ref, dst_row, sem):
    rows = lambda ref, r: ref.at[pl.ds(pl.multiple_of(r * ROW_SUB, ROW_SUB), ROW_SUB)]
    return pltpu.make_async_copy(rows(src_ref, src_row), rows(dst_ref, dst_row), sem)


def _dispatch_kernel(pad_ref, nu_ref, f_ref, slots_ref, xb_ref, dest_ref, idx_ref, zero_ref, inv_ref,
                     sem_idx, sem_rows, *, n_tok):
    i = pl.program_id(0)
    t = f_ref.shape[0] // ROW_SUB
    blk = MOE_BLOCK * ROW_SUB
    idx_cp = pltpu.make_async_copy(slots_ref.at[i], idx_ref, sem_idx)
    idx_cp.start()

    @pl.when(i == 0)
    def _():
        zero_ref[...] = jnp.zeros_like(zero_ref)
        n_blocks = xb_ref.shape[0] // blk

        def zero_block(start):
            return pltpu.make_async_copy(zero_ref, xb_ref.at[pl.ds(pl.multiple_of(start * ROW_SUB, blk), blk)],
                                         sem_rows)

        def tail_start(j, carry):
            zero_block(j * MOE_BLOCK).start()
            return carry

        def tail_wait(j, carry):
            zero_block(0).wait()
            return carry

        def spare(j, carry):
            inv_ref[j] = 2 * n_tok + lax.rem(j, MOE_BLOCK)
            return carry

        for e in range(N_EXPERTS):
            zero_block(jnp.maximum(pad_ref[N_EXPERTS + e] - MOE_BLOCK, 0)).start()
        lax.fori_loop(nu_ref[0], n_blocks, tail_start, 0)
        for e in range(N_EXPERTS):
            lax.fori_loop(pad_ref[e], pad_ref[N_EXPERTS + e], spare, 0)
        lax.fori_loop(nu_ref[0] * MOE_BLOCK, inv_ref.shape[0], spare, 0)
        for e in range(N_EXPERTS):
            zero_block(0).wait()
        lax.fori_loop(nu_ref[0], n_blocks, tail_wait, 0)

    idx_cp.wait()
    base = i * t

    def issue(r, carry):
        s0, s1 = idx_ref[r], idx_ref[t + r]
        _row_copy(f_ref, r, xb_ref, s0, sem_rows).start(priority=0)
        _row_copy(f_ref, r, xb_ref, s1, sem_rows).start(priority=1)
        inv_ref[s0] = base + r
        inv_ref[s1] = n_tok + base + r
        return carry

    lax.fori_loop(0, t, issue, 0, unroll=ROW_DMA_UNROLL)
    for _ in range(2):
        pltpu.make_async_copy(f_ref, xb_ref.at[pl.ds(0, t * ROW_SUB)], sem_rows).wait()

    @pl.when(i == pl.num_programs(0) - 1)
    def _():
        out = pltpu.make_async_copy(inv_ref, dest_ref, sem_idx)
        out.start()
        out.wait()


def _dispatch(f, slots, pad, n_used, n_slots):
    t = TOKEN_TILE
    n_tok = f.shape[0] // ROW_SUB
    return pl.pallas_call(
        functools.partial(_dispatch_kernel, n_tok=n_tok),
        out_shape=[jax.ShapeDtypeStruct((n_slots * ROW_SUB, LANES), F32), jax.ShapeDtypeStruct((n_slots,), I32)],
        grid_spec=pltpu.PrefetchScalarGridSpec(
            num_scalar_prefetch=2,
            grid=(n_tok // t,),
            in_specs=[pl.BlockSpec((t * ROW_SUB, LANES), lambda i, p, nu: (i, 0)),
                      pl.BlockSpec(memory_space=pltpu.VMEM)],
            out_specs=[pl.BlockSpec(memory_space=pl.ANY), pl.BlockSpec(memory_space=pl.ANY)],
            scratch_shapes=[pltpu.SMEM((2 * t,), I32), pltpu.VMEM((MOE_BLOCK * ROW_SUB, LANES), F32),
                            pltpu.SMEM((n_slots,), I32), pltpu.SemaphoreType.DMA, pltpu.SemaphoreType.DMA]),
        compiler_params=_params("arbitrary"),
        name="moe_dispatch",
    )(pad, n_used, f, slots)


def _experts_kernel(be_ref, nu_ref, xb_ref, dest_ref, w1_ref, w3_ref, w2_ref, yt_ref, w1b_ref, w3b_ref, w2b_ref,
                    idx_ref, ybuf_ref, zero_ref, sem_idx, sem_rows):
    i = pl.program_id(0)
    n_used = nu_ref[0]
    rb = MOE_BLOCK * ROW_SUB

    def idx_copy(s):
        return pltpu.make_async_copy(dest_ref.at[s], idx_ref.at[s % 2], sem_idx.at[s % 2])

    def rows_done(s):
        pltpu.make_async_copy(ybuf_ref.at[s % 2], yt_ref.at[pl.ds(0, rb)], sem_rows.at[s % 2]).wait()

    @pl.when(i == 0)
    def _():
        idx_copy(0).start()
        zero_ref[...] = jnp.zeros_like(zero_ref)
        spare = pltpu.make_async_copy(zero_ref, yt_ref.at[pl.ds(yt_ref.shape[0] - rb, rb)], sem_rows.at[1])
        spare.start()
        spare.wait()

    @pl.when(jnp.logical_or(i == 0, be_ref[i] != be_ref[jnp.maximum(i - 1, 0)]))
    def _():
        w1b_ref[...] = w1_ref[...].astype(BF16)
        w3b_ref[...] = w3_ref[...].astype(BF16)
        w2b_ref[...] = w2_ref[...].astype(BF16)

    @pl.when(i < n_used)
    def _():
        @pl.when(i + 1 < n_used)
        def _():
            idx_copy(i + 1).start()

        x = _load_row_tiles(xb_ref).astype(BF16)
        h1 = jnp.dot(x, w1b_ref[...], preferred_element_type=F32)
        h3 = jnp.dot(x, w3b_ref[...], preferred_element_type=F32)
        a = (h1 * jax.nn.sigmoid(h1) * h3).astype(BF16)
        y = jnp.dot(a, w2b_ref[...], preferred_element_type=F32)

        @pl.when(i >= 1)
        def _():
            rows_done(i - 1)

        slot = i % 2
        _store_row_tiles(ybuf_ref.at[slot], y)
        idx_copy(i).wait()

        def issue(j, carry):
            for half in range(2):
                r = 2 * j + half
                _row_copy(ybuf_ref.at[slot], r, yt_ref, idx_ref[slot, r], sem_rows.at[slot]).start(priority=half)
            return carry

        lax.fori_loop(0, MOE_BLOCK // 2, issue, 0, unroll=ROW_DMA_UNROLL // 2)

        @pl.when(i == n_used - 1)
        def _():
            rows_done(i)


def _experts(xb, dest, w1, w3, w2, layer, block_e, n_used, n_tok):
    n_blocks = block_e.shape[0]
    _, _, d, hid = w1.shape
    rb = MOE_BLOCK * ROW_SUB
    wspec = lambda s: pl.BlockSpec((None, None) + s, lambda i, be, nu: (layer, be[i], 0, 0))
    return pl.pallas_call(
        _experts_kernel,
        out_shape=jax.ShapeDtypeStruct(((2 * n_tok + MOE_BLOCK) * ROW_SUB, LANES), F32),
        grid_spec=pltpu.PrefetchScalarGridSpec(
            num_scalar_prefetch=2,
            grid=(n_blocks,),
            in_specs=[pl.BlockSpec((rb, LANES), lambda i, be, nu: (jnp.minimum(i, nu[0] - 1), 0)),
                      pl.BlockSpec(memory_space=pltpu.VMEM),
                      wspec((d, hid)), wspec((d, hid)), wspec((hid, d))],
            out_specs=pl.BlockSpec(memory_space=pl.ANY),
            scratch_shapes=[pltpu.VMEM((d, hid), BF16), pltpu.VMEM((d, hid), BF16), pltpu.VMEM((hid, d), BF16),
                            pltpu.SMEM((2, MOE_BLOCK), I32), pltpu.VMEM((2, rb, LANES), F32),
                            pltpu.VMEM((rb, LANES), F32),
                            pltpu.SemaphoreType.DMA((2,)), pltpu.SemaphoreType.DMA((2,))]),
        compiler_params=_params("arbitrary"),
        name="moe_experts",
    )(block_e, n_used, xb, dest.reshape(n_blocks, MOE_BLOCK), w1, w3, w2)


def _moe_plan(route_i, counts, n_tok):
    eid, rank = route_i[0:2], route_i[2:4]
    cnt = counts[0, :N_EXPERTS].astype(I32)
    padded = (cnt + MOE_BLOCK - 1) // MOE_BLOCK * MOE_BLOCK
    pend = jnp.cumsum(padded)
    pstart = pend - padded
    experts = jnp.arange(N_EXPERTS, dtype=I32)
    slots = rank + jnp.sum(jnp.where(eid[:, :, None] == experts, pstart, 0), axis=-1)
    n_assign = 2 * n_tok
    n_slots = (n_assign + MOE_BLOCK - 1) // MOE_BLOCK * MOE_BLOCK + N_EXPERTS * MOE_BLOCK
    n_blocks = n_slots // MOE_BLOCK
    bstart = jnp.arange(n_blocks, dtype=I32) * MOE_BLOCK
    block_e = jnp.minimum(jnp.sum(bstart[:, None] >= pend[None, :], axis=1), N_EXPERTS - 1).astype(I32)
    n_used = pend[-1] // MOE_BLOCK
    last_e = block_e[jnp.maximum(n_used - 1, 0)]
    block_e = jnp.where(jnp.arange(n_blocks) < n_used, block_e, last_e).astype(I32)
    slots = slots.reshape(2, n_tok // TOKEN_TILE, TOKEN_TILE).transpose(1, 0, 2)
    slots = slots.reshape(n_tok // TOKEN_TILE, 2 * TOKEN_TILE).astype(I32)
    pad = jnp.concatenate([pstart + cnt, pend]).astype(I32)
    return slots, block_e, n_used.reshape(1).astype(I32), pad, n_slots


def _moe(f, route_i, counts, w1, w3, w2, layer):
    n_tok = f.shape[0] // ROW_SUB
    slots, block_e, n_used, pad, n_slots = _moe_plan(route_i, counts, n_tok)
    xb, dest = _dispatch(f, slots, pad, n_used, n_slots)
    return _experts(xb, dest, w1, w3, w2, layer, block_e, n_used, n_tok)


def _combined_rows(y0_ref, y1_ref, rw_ref):
    rw = rw_ref[...]
    w1, w2 = rw[:, 0:1], rw[:, 1:2]
    t = y0_ref.shape[0] // ROW_SUB
    part = lambda ref, s: ref[pl.ds(s, t, stride=ROW_SUB), :]
    return jnp.concatenate([w1 * part(y0_ref, s) + w2 * part(y1_ref, s) for s in range(ROW_SUB)], axis=1)


def _expert_row_specs(n_tok):
    t = TOKEN_TILE
    return [pl.BlockSpec((t * ROW_SUB, LANES), lambda i: (i, 0)),
            pl.BlockSpec((t * ROW_SUB, LANES), lambda i: (n_tok // t + i, 0))]


def _l0_combine_kernel(x_ref, rw_ref, y0_ref, y1_ref, mod0_ref, mod1_ref, g_ref, win_ref, x2_ref, u_ref):
    moe = _combined_rows(y0_ref, y1_ref, rw_ref)
    x2 = x_ref[...] + mod0_ref[0, 0][5:6] * moe
    x2_ref[...] = x2
    m1 = mod1_ref[0, 0]
    h = _modulate(x2, g_ref[...], m1[0:1], m1[1:2]).astype(BF16)
    u_ref[...] = jnp.dot(h, win_ref[...], preferred_element_type=F32)


def _l0_combine(x1, rw, y, mod0, mod1, g, win, b, nct):
    n_tok, d = x1.shape
    t = TOKEN_TILE
    nt = n_tok // b // t
    l = n_tok // b
    modspec = pl.BlockSpec((1, 1, 6, d), lambda i: (i // nt, jnp.where(i % nt >= nct, 1, 0), 0, 0))
    return pl.pallas_call(
        _l0_combine_kernel,
        out_shape=[jax.ShapeDtypeStruct((n_tok, d), F32), jax.ShapeDtypeStruct((l, b * S5_WIDTH), F32)],
        grid=(n_tok // t,),
        in_specs=[pl.BlockSpec((t, d), lambda i: (i, 0)), pl.BlockSpec((t, 8), lambda i: (i, 0))]
                 + _expert_row_specs(n_tok) + [modspec, modspec, _full((1, d)), _full((d, S5_WIDTH))],
        out_specs=[pl.BlockSpec((t, d), lambda i: (i, 0)),
                   pl.BlockSpec((t, S5_WIDTH), lambda i: (i % nt, i // nt))],
        compiler_params=_params("parallel"),
        name="l0_combine",
    )(x1, rw, y, y, mod0, mod1, g, win)


def _s5_zoh_kernel(are_ref, aim_ref, ldt_ref, bre_ref, bim_ref, lre_ref, lim_ref, bbre_ref, bbim_ref):
    ar, ai = are_ref[...], aim_ref[...]
    dt = jnp.exp(ldt_ref[...])
    mag = jnp.exp(ar * dt)
    lr = mag * jnp.cos(ai * dt)
    li = mag * jnp.sin(ai * dt)
    lre_ref[...] = lr
    lim_ref[...] = li
    nr, ni = lr - 1.0, li
    den = ar * ar + ai * ai
    cr = (nr * ar + ni * ai) / den
    ci = (ni * ar - nr * ai) / den
    br, bi = bre_ref[...], bim_ref[...]
    bbre_ref[...] = cr * br - ci * bi
    bbim_ref[...] = cr * bi + ci * br


def _s5_zoh(a_re, a_im, log_dt, b_re, b_im):
    n_dir, g, n, p = b_re.shape
    rows = n_dir * g * p
    expand = lambda a: jnp.broadcast_to(a[:, :, None, :], (n_dir, g, p, n)).reshape(rows, n)
    ldt = jnp.broadcast_to(log_dt[:, :, None, None], (n_dir, g, p, n)).reshape(rows, n)
    bt = lambda a: jnp.transpose(a, (0, 1, 3, 2)).reshape(rows, n)
    outs = pl.pallas_call(
        _s5_zoh_kernel,
        out_shape=[jax.ShapeDtypeStruct((rows, n), F32)] * 4,
        name="s5_zoh",
    )(expand(a_re), expand(a_im), ldt, bt(b_re), bt(b_im))
    lre, lim, bbre, bbim = (o.reshape(n_dir, g, p, n) for o in outs)
    return lre[:, :, 0, :], lim[:, :, 0, :], bbre, bbim


def _s5_matrices(lre, lim, bbre, bbim, c_re, c_im):
    n_dir = lre.shape[0]
    nj = S5_GROUPS // (2 * S5_TILES)
    shp = (n_dir, nj, S5_TILES, 2, S5_GROUP_DIM, S5_STATE)
    eye_t = jnp.eye(S5_TILES, dtype=F32)
    eye_g = jnp.eye(2, dtype=F32)

    def place(a):
        return jnp.einsum("djtgpn,tu,gh->djtgpuhn", a.reshape(shp), eye_t, eye_g)

    kdim = S5_TILES * 2 * S5_GROUP_DIM
    sdim = S5_TILES * 2 * 2 * S5_STATE
    bm = jnp.stack([place(bbre), place(bbim)], axis=6)
    bm = bm.reshape(n_dir, nj, kdim, sdim).astype(BF16)
    cm = jnp.stack([place(c_re), place(-c_im)], axis=6)
    cm = jnp.transpose(cm, (0, 1, 5, 6, 7, 8, 2, 3, 4)).reshape(n_dir, nj, sdim, kdim).astype(BF16)
    lam = jnp.stack([lre.reshape(n_dir, nj, S5_TILES * LANES), lim.reshape(n_dir, nj, S5_TILES * LANES)], axis=2)
    return bm, cm, lam


def _s5_scan_kernel(uf_ref, ub_ref, bm_ref, cm_ref, lam_ref, yf_ref, yb_ref, v_ref, h_ref, *, tc, nb):
    @pl.when(pl.program_id(0) == 0)
    def _():
        h_ref[...] = jnp.zeros_like(h_ref)

    nj = bm_ref.shape[1]
    kdim = bm_ref.shape[2]
    for d, (u_ref, y_ref) in enumerate(((uf_ref, yf_ref), (ub_ref, yb_ref))):
        for jq in range(nj):
            uk = u_ref[:, kdim * jq:kdim * (jq + 1)].astype(BF16)
            v_ref[...] = jnp.dot(uk, bm_ref[d, jq], preferred_element_type=F32)
            lam = lam_ref[d, jq]
            lr = [jnp.broadcast_to(lam[0:1, LANES * t:LANES * (t + 1)], (nb, LANES)) for t in range(S5_TILES)]
            li = [jnp.broadcast_to(lam[1:2, LANES * t:LANES * (t + 1)], (nb, LANES)) for t in range(S5_TILES)]
            h0 = h_ref[d, jq]
            init = tuple(h0[:, LANES * c:LANES * (c + 1)] for c in range(2 * S5_TILES))

            def step(s, carry, d=d, lr=lr, li=li):
                tt = s if d == 0 else tc - 1 - s
                r0 = pl.multiple_of(tt * nb, nb)
                rows = v_ref[pl.ds(r0, nb), :]
                new = []
                for t in range(S5_TILES):
                    vr = rows[:, 2 * LANES * t:2 * LANES * t + LANES]
                    vi = rows[:, 2 * LANES * t + LANES:2 * LANES * (t + 1)]
                    hr, hi = carry[2 * t], carry[2 * t + 1]
                    new.append(lr[t] * hr - li[t] * hi + vr)
                    new.append(lr[t] * hi + li[t] * hr + vi)
                v_ref[pl.ds(r0, nb), :] = jnp.concatenate(new, axis=1)
                return tuple(new)

            fin = lax.fori_loop(0, tc, step, init)
            h_ref[d, jq] = jnp.concatenate(fin, axis=1)
            y_ref[:, kdim * jq:kdim * (jq + 1)] = jnp.dot(v_ref[...].astype(BF16), cm_ref[d, jq],
                                                         preferred_element_type=F32)


def _s5_scan(u_tm, bm, cm, lam, nb, n_ctx):
    rows_total, w = u_tm.shape
    tc = S5_CHUNK
    r = tc * nb
    n_chunks = rows_total // r
    ncc = n_ctx // tc
    bwd = lambda i: (jnp.where(i < ncc, ncc - 1 - i, n_chunks - 1 - (i - ncc)), 0)
    nj = bm.shape[1]
    sdim = bm.shape[3]
    kern = functools.partial(_s5_scan_kernel, tc=tc, nb=nb)
    return pl.pallas_call(
        kern,
        out_shape=[jax.ShapeDtypeStruct((rows_total, w), F32)] * 2,
        grid=(n_chunks,),
        in_specs=[pl.BlockSpec((r, w), lambda i: (i, 0)), pl.BlockSpec((r, w), bwd),
                  _full(bm.shape), _full(cm.shape), _full(lam.shape)],
        out_specs=[pl.BlockSpec((r, w), lambda i: (i, 0)), pl.BlockSpec((r, w), bwd)],
        scratch_shapes=[pltpu.VMEM((r, sdim), F32), pltpu.VMEM((2, nj, nb, sdim), F32)],
        compiler_params=_params("arbitrary"),
        name="s5_scan",
    )(u_tm, u_tm, bm, cm, lam)


def _l1_post_kernel(x_ref, u_ref, yf_ref, yb_ref, dsk_ref, wglu_ref, bglu_ref, wout_ref, mod_ref, gf_ref,
                    wr_ref, br_ref, x3_ref, f_ref, ri_ref, rw_ref, cnt_ref, base_ref):
    y = dsk_ref[...] * u_ref[...] + yf_ref[...] + yb_ref[...]
    g = jax.nn.gelu(y, approximate=True)
    z = jnp.dot(g.astype(BF16), wglu_ref[...], preferred_element_type=F32) + bglu_ref[...]
    o = jnp.dot((g * jax.nn.sigmoid(z)).astype(BF16), wout_ref[...], preferred_element_type=F32)
    first = jnp.logical_and(pl.program_id(0) == 0, pl.program_id(1) == 0)
    _route_tail(x_ref[0], o, mod_ref[0, 0], gf_ref[...], wr_ref, br_ref, base_ref,
                x3_ref, f_ref, ri_ref, rw_ref, cnt_ref, first)


def _l1_post(x2, u_tm, yf, yb, dsk, wglu, bglu, wout, mod, gf, wr, br, n_ctx):
    b, l, d = x2.shape
    t = TOKEN_TILE
    nct = n_ctx // t
    s = l - n_ctx
    tm = pl.BlockSpec((t, S5_WIDTH), lambda bi, ti: (ti + nct, bi))
    shapes, specs = _tail_out_shapes(b, s, d)
    return pl.pallas_call(
        _l1_post_kernel,
        out_shape=shapes,
        grid=(b, s // t),
        in_specs=[pl.BlockSpec((1, t, d), lambda bi, ti: (bi, ti + nct, 0)), tm, tm, tm,
                  _full((1, S5_WIDTH)), _full((S5_WIDTH, S5_WIDTH)), _full((1, S5_WIDTH)), _full((S5_WIDTH, d)),
                  pl.BlockSpec((1, 1, 6, d), lambda bi, ti: (bi, 1, 0, 0)),
                  _full((1, d)), _full((2, d, LANES)), _full((1, LANES))],
        out_specs=specs,
        scratch_shapes=[pltpu.VMEM((8, LANES), F32)],
        compiler_params=_params("arbitrary", "arbitrary"),
        name="l1_post",
    )(x2, u_tm, yf, yb, dsk, wglu, bglu, wout, mod, gf, wr, br)


def _final_kernel(x_ref, rw_ref, y0_ref, y1_ref, mod_ref, g_ref, o_ref):
    moe = _combined_rows(y0_ref, y1_ref, rw_ref)
    x4 = x_ref[...] + mod_ref[0, 0][5:6] * moe
    o_ref[...] = _rms(x4) * g_ref[...]


def _final(x3, rw, y, mod, g, b):
    n_tok, d = x3.shape
    t = TOKEN_TILE
    nt = n_tok // b // t
    return pl.pallas_call(
        _final_kernel,
        out_shape=jax.ShapeDtypeStruct((n_tok, d), F32),
        grid=(n_tok // t,),
        in_specs=[pl.BlockSpec((t, d), lambda i: (i, 0)), pl.BlockSpec((t, 8), lambda i: (i, 0))]
                 + _expert_row_specs(n_tok) + [pl.BlockSpec((1, 1, 6, d), lambda i: (i // nt, 1, 0, 0)), _full((1, d))],
        out_specs=pl.BlockSpec((t, d), lambda i: (i, 0)),
        compiler_params=_params("parallel"),
        name="final_combine",
    )(x3, rw, y, y, mod, g)


def _rope_tables(n_ctx, n_lat):
    pos = jnp.arange(n_lat, dtype=I32)
    row = (pos // GRID_W).astype(F32)
    col = (pos % GRID_W).astype(F32)
    inv_freq = jnp.power(ROPE_THETA, -jnp.arange(0, AXIS_DIM, 2, dtype=F32) / AXIS_DIM)
    lane = np.arange(LANES)
    dd = lane % HEAD_DIM
    use_col = dd >= AXIS_DIM
    first = (dd % AXIS_DIM) < AXIS_DIM // 2
    freq = inv_freq[(dd % AXIS_DIM) % (AXIS_DIM // 2)]
    ang = jnp.where(use_col[None, :], col[:, None], row[:, None]) * freq[None, :]
    cos, sin = jnp.cos(ang), jnp.sin(ang)
    sa = jnp.where(first[None, :], -sin, 0.0)
    sb = jnp.where(first[None, :], 0.0, sin)
    pad = lambda a, v: jnp.concatenate([jnp.full((n_ctx, LANES), v, F32), a], axis=0)
    return pad(cos, 1.0), pad(sa, 0.0), pad(sb, 0.0)


def _group_ones(width, group):
    idx = np.arange(width) // group
    return jnp.asarray(idx[:, None] == idx[None, :], dtype=BF16)


def _router_weights(w_grp, b_grp, w_rt, b_rt):
    d = w_grp.shape[0]
    used = N_EXPERT_GROUPS + N_EXPERTS
    wr = jnp.concatenate([w_grp, w_rt, jnp.zeros((d, LANES - used), F32)], axis=1)
    br = jnp.concatenate([b_grp, b_rt, jnp.zeros((LANES - used,), F32)]).reshape(1, LANES)
    return jnp.stack(_split_bf16(wr)), br


def kernel(x, c, ctx, c_ctx, w_mod, b_mod, norm_mix_g, norm_ffn_g, mix_w_in, mix_w_out, q_norm_g, k_norm_g, gmlp_norm_g, gmlp_w_spatial, gmlp_b_spatial, s5_w_in, s5_a_re, s5_a_im, s5_log_dt, s5_b_re, s5_b_im, s5_c_re, s5_c_im, s5_d, s5_w_glu, s5_b_glu, s5_w_out, moe_w_group, moe_b_group, moe_w_router, moe_b_router, moe_w1, moe_w3, moe_w2, final_norm_g):
    b, s, d = x.shape
    n_ctx = ctx.shape[1]
    l = n_ctx + s
    assert w_mod.shape[0] == 2, "two layers: attention/gMLP then S5"
    assert n_ctx % TOKEN_TILE == 0 and s % TOKEN_TILE == 0 and b % 8 == 0 and d == D_MODEL
    nct = n_ctx // TOKEN_TILE

    r = (b + 1 + 7) // 8 * 8
    cond = jnp.concatenate([c, c_ctx[None, :], jnp.zeros((r - b - 1, d), F32)], axis=0)
    mods = _adaln(cond, w_mod, b_mod)
    lat = mods[:, :b].reshape(2, b, 1, 6, d)
    cxt = jnp.broadcast_to(mods[:, b].reshape(2, 1, 1, 6, d), (2, b, 1, 6, d))
    mod = jnp.concatenate([cxt, lat], axis=2)

    w_in = jnp.concatenate([mix_w_in[0][:, _Q_COL_PERM], mix_w_in[0][:, ATTN_WIDTH:]], axis=1).astype(BF16)
    w_out = jnp.concatenate([mix_w_out[0][_Q_COL_PERM], mix_w_out[0][ATTN_WIDTH:]], axis=0).astype(BF16)
    cos, sa, sb = _rope_tables(n_ctx, s)
    qg = jnp.tile(q_norm_g[0], N_Q_HEADS).reshape(1, ATTN_WIDTH)
    kg = jnp.tile(k_norm_g[0], N_KV_HEADS).reshape(1, KV_WIDTH)
    q, k, v, gu, gv = _l0_proj(ctx, x, mod[0], norm_mix_g[0].reshape(1, d), w_in, qg, kg,
                               gmlp_norm_g[0].reshape(1, GMLP_WIDTH), cos, sa, sb,
                               _group_ones(ATTN_WIDTH, HEAD_DIM), _group_ones(KV_WIDTH, HEAD_DIM), nct)
    o = _attention(q, k, v, n_ctx)
    bsp = jnp.repeat(gmlp_b_spatial[0].T, GMLP_GROUP_DIM, axis=1)
    wr0, br0 = _router_weights(moe_w_group[0], moe_b_group[0], moe_w_router[0], moe_b_router[0])
    x1, f0, ri0, rw0, cnt0 = _l0_post(ctx, x, o, gu, gv, gmlp_w_spatial[0].astype(BF16), bsp, w_out, mod[0],
                                      norm_ffn_g[0].reshape(1, d), wr0, br0, nct)
    y0 = _moe(f0, ri0, cnt0, moe_w1, moe_w3, moe_w2, 0)
    x2, u_tm = _l0_combine(x1.reshape(b * l, d), rw0, y0, mod[0], mod[1], norm_mix_g[1].reshape(1, d),
                           s5_w_in[0].astype(BF16), b, nct)

    lre, lim, bbre, bbim = _s5_zoh(s5_a_re[0], s5_a_im[0], s5_log_dt[0], s5_b_re[0], s5_b_im[0])
    bm, cm, lam = _s5_matrices(lre, lim, bbre, bbim, s5_c_re[0], s5_c_im[0])
    yf, yb = _s5_scan(u_tm.reshape(l * b, S5_WIDTH), bm, cm, lam, b, n_ctx)
    wr1, br1 = _router_weights(moe_w_group[1], moe_b_group[1], moe_w_router[1], moe_b_router[1])
    tmv = lambda a: a.reshape(l, b * S5_WIDTH)
    x3, f1, ri1, rw1, cnt1 = _l1_post(x2.reshape(b, l, d), u_tm, tmv(yf), tmv(yb), s5_d[0].reshape(1, S5_WIDTH),
                                      s5_w_glu[0].astype(BF16), s5_b_glu[0].reshape(1, S5_WIDTH),
                                      s5_w_out[0].astype(BF16), mod[1], norm_ffn_g[1].reshape(1, d), wr1, br1, n_ctx)
    y1 = _moe(f1, ri1, cnt1, moe_w1, moe_w3, moe_w2, 1)
    out = _final(x3.reshape(b * s, d), rw1, y1, mod[1], final_norm_g.reshape(1, d), b)
    return out.reshape(b, s, d)
```

```python
import functools

import numpy as np
import jax
import jax.numpy as jnp
from jax import lax
from jax.experimental import pallas as pl
from jax.experimental.pallas import tpu as pltpu

F32 = jnp.float32
BF16 = jnp.bfloat16
I32 = jnp.int32

EPS = 1e-6
GRID_W = 64
HEAD_DIM = 64
N_Q_HEADS = 8
N_KV_HEADS = 2
ATTN_WIDTH = N_Q_HEADS * HEAD_DIM
KV_WIDTH = N_KV_HEADS * HEAD_DIM
AXIS_DIM = HEAD_DIM // 2
ROPE_THETA = 10000.0
GMLP_GROUPS = 8
GMLP_GROUP_DIM = 64
GMLP_WIDTH = GMLP_GROUPS * GMLP_GROUP_DIM
GMLP_CHUNK = 128
MIX_IN_WIDTH = ATTN_WIDTH + 2 * KV_WIDTH + 2 * GMLP_WIDTH
S5_GROUPS = 32
S5_GROUP_DIM = 16
S5_STATE = 64
S5_WIDTH = S5_GROUPS * S5_GROUP_DIM
N_EXPERT_GROUPS = 4
EXPERTS_PER_GROUP = 8
N_EXPERTS = N_EXPERT_GROUPS * EXPERTS_PER_GROUP
EXPERT_HIDDEN = 512
MOE_BLOCK = 256

D_MODEL = 1024
LANES = 128
ROW_SUB = D_MODEL // LANES
TOKEN_TILE = 256
ROW_DMA_UNROLL = 8
S5_CHUNK = 64
S5_TILES = 4
VMEM_LIMIT = 48 * 1024 * 1024
NEG_BIG = -1e30

_Q_HEAD_ORDER = (0, 4, 1, 5, 2, 6, 3, 7)
_Q_COL_PERM = np.concatenate([np.arange(HEAD_DIM) + HEAD_DIM * h for h in _Q_HEAD_ORDER])


def _params(*sem):
    return pltpu.CompilerParams(dimension_semantics=sem, vmem_limit_bytes=VMEM_LIMIT)


def _rms(x):
    return x * lax.rsqrt(jnp.mean(x * x, axis=-1, keepdims=True) + EPS)


def _modulate(x, g, shift, scale):
    return _rms(x) * g * (1.0 + scale) + shift


def _full(shape):
    return pl.BlockSpec(shape, lambda *_: (0,) * len(shape))


def _split_bf16(a):
    hi = a.astype(BF16)
    return hi, (a - hi.astype(F32)).astype(BF16)


def _dot_split(a, w_hi, w_lo):
    a_hi, a_lo = _split_bf16(a)
    dot = functools.partial(jnp.dot, preferred_element_type=F32)
    return dot(a_hi, w_hi) + (dot(a_lo, w_hi) + dot(a_hi, w_lo))


def _store_row_tiles(ref, val):
    n = val.shape[0]
    for s in range(ROW_SUB):
        ref[pl.ds(s, n, stride=ROW_SUB), :] = val[:, LANES * s:LANES * (s + 1)]


def _load_row_tiles(ref):
    n = ref.shape[0] // ROW_SUB
    return jnp.concatenate([ref[pl.ds(s, n, stride=ROW_SUB), :] for s in range(ROW_SUB)], axis=1)


def _adaln_kernel(cond_ref, w_ref, b_ref, o_ref):
    c = cond_ref[...]
    s = c * jax.nn.sigmoid(c)
    o_ref[0] = _dot_split(s, *_split_bf16(w_ref[0])) + b_ref[0]


def _adaln(cond, w_mod, b_mod):
    depth, d, n = w_mod.shape
    r = cond.shape[0]
    tn = 1536
    return pl.pallas_call(
        _adaln_kernel,
        out_shape=jax.ShapeDtypeStruct((depth, r, n), F32),
        grid=(depth, n // tn),
        in_specs=[pl.BlockSpec((r, d), lambda l, j: (0, 0)),
                  pl.BlockSpec((1, d, tn), lambda l, j: (l, 0, j)),
                  pl.BlockSpec((1, 1, tn), lambda l, j: (l, 0, j))],
        out_specs=pl.BlockSpec((1, r, tn), lambda l, j: (l, 0, j)),
        compiler_params=_params("parallel", "parallel"),
        name="adaln",
    )(cond, w_mod, b_mod.reshape(depth, 1, n))


def _seq_tile(ctx_ref, x_ref, nct):
    return jnp.where(pl.program_id(1) < nct, ctx_ref[0], x_ref[0])


def _seq_specs(t, d, nct):
    return [pl.BlockSpec((1, t, d), lambda bi, ti: (bi, jnp.minimum(ti, nct - 1), 0)),
            pl.BlockSpec((1, t, d), lambda bi, ti: (bi, jnp.maximum(ti - nct, 0), 0))]


def _l0_proj_kernel(ctx_ref, x_ref, mod_ref, g_ref, win_ref, qg_ref, kg_ref, lng_ref, cos_ref, sa_ref, sb_ref,
                    gq_ref, gk_ref, q_ref, k_ref, v_ref, gu_ref, gv_ref, *, nct):
    m = mod_ref[0, 0]
    h = _modulate(_seq_tile(ctx_ref, x_ref, nct), g_ref[...], m[0:1], m[1:2]).astype(BF16)
    z = jnp.dot(h, win_ref[...], preferred_element_type=F32)
    cos, sa, sb = cos_ref[...], sa_ref[...], sb_ref[...]

    def head_norm_rope(t, gsum_ref, gain, reps):
        w = t.shape[1]
        ss = jnp.dot((t * t).astype(BF16), gsum_ref[...], preferred_element_type=F32)
        tn = t * lax.rsqrt(ss * (1.0 / HEAD_DIM) + EPS) * gain
        c, a, b = (jnp.concatenate([tab] * reps, axis=1) if reps > 1 else tab for tab in (cos, sa, sb))
        return tn * c + pltpu.roll(tn, w - AXIS_DIM // 2, 1) * a + pltpu.roll(tn, AXIS_DIM // 2, 1) * b

    q = head_norm_rope(z[:, :ATTN_WIDTH], gq_ref, qg_ref[...], ATTN_WIDTH // LANES)
    q_ref[0] = (q * (HEAD_DIM ** -0.5)).astype(BF16)
    k = head_norm_rope(z[:, ATTN_WIDTH:ATTN_WIDTH + KV_WIDTH], gk_ref, kg_ref[...], 1)
    k_ref[0] = k.astype(BF16)
    v_ref[0] = z[:, ATTN_WIDTH + KV_WIDTH:ATTN_WIDTH + 2 * KV_WIDTH].astype(BF16)
    off = ATTN_WIDTH + 2 * KV_WIDTH
    gu_ref[0] = jax.nn.gelu(z[:, off:off + GMLP_WIDTH], approximate=True).astype(BF16)
    gv = jax.nn.gelu(z[:, off + GMLP_WIDTH:], approximate=True)
    mu = jnp.mean(gv, axis=-1, keepdims=True)
    gc = gv - mu
    gn = gc * lax.rsqrt(jnp.mean(gc * gc, axis=-1, keepdims=True) + EPS) * lng_ref[...]
    gv_ref[0] = gn.astype(BF16)


def _l0_proj(ctx, x, mod, g, w_in, qg, kg, lng, cos, sa, sb, gq, gk, nct):
    b, s, d = x.shape
    l = s + ctx.shape[1]
    t = TOKEN_TILE
    tok = lambda w: pl.BlockSpec((1, t, w), lambda bi, ti: (bi, ti, 0))
    tab = pl.BlockSpec((t, LANES), lambda bi, ti: (ti, 0))
    outs = [jax.ShapeDtypeStruct((b, l, w), BF16) for w in (ATTN_WIDTH, KV_WIDTH, KV_WIDTH, GMLP_WIDTH, GMLP_WIDTH)]
    return pl.pallas_call(
        functools.partial(_l0_proj_kernel, nct=nct),
        out_shape=outs,
        grid=(b, l // t),
        in_specs=_seq_specs(t, d, nct) + [
                  pl.BlockSpec((1, 1, 6, d), lambda bi, ti: (bi, jnp.where(ti >= nct, 1, 0), 0, 0)),
                  _full((1, d)), _full((d, MIX_IN_WIDTH)), _full((1, ATTN_WIDTH)), _full((1, KV_WIDTH)),
                  _full((1, GMLP_WIDTH)), tab, tab, tab,
                  _full((ATTN_WIDTH, ATTN_WIDTH)), _full((KV_WIDTH, KV_WIDTH))],
        out_specs=[tok(w) for w in (ATTN_WIDTH, KV_WIDTH, KV_WIDTH, GMLP_WIDTH, GMLP_WIDTH)],
        compiler_params=_params("parallel", "parallel"),
        name="l0_proj",
    )(ctx, x, mod, g, w_in, qg, kg, lng, cos, sa, sb, gq, gk)


def _attn_kernel(q_ref, k_ref, v_ref, o_ref, *, n_ctx, nct):
    ti = pl.program_id(1)
    tq = q_ref.shape[1]
    lane = lax.broadcasted_iota(I32, (tq, LANES), 1)
    low = lane < HEAD_DIM

    def attend(nk):
        kk = k_ref[0, :nk, :]
        vv = v_ref[0, :nk, :]
        for m in range(ATTN_WIDTH // LANES):
            qp = q_ref[0, :, LANES * m:LANES * (m + 1)]
            halves = []
            for sel in (low, jnp.logical_not(low)):
                qm = jnp.where(sel, qp, jnp.zeros_like(qp))
                s = lax.dot_general(qm, kk, (((1,), (1,)), ((), ())), preferred_element_type=F32)
                p = jnp.exp(s - jnp.max(s, axis=-1, keepdims=True))
                den = jnp.sum(p, axis=-1, keepdims=True)
                pv = jnp.dot(p.astype(BF16), vv, preferred_element_type=F32)
                halves.append(pv / den)
            o_ref[0, :, LANES * m:LANES * (m + 1)] = jnp.where(low, halves[0], halves[1]).astype(BF16)

    @pl.when(ti < nct)
    def _():
        attend(n_ctx)

    @pl.when(ti >= nct)
    def _():
        attend(k_ref.shape[1])


def _attention(q, k, v, n_ctx):
    b, l, _ = q.shape
    t = TOKEN_TILE
    kern = functools.partial(_attn_kernel, n_ctx=n_ctx, nct=n_ctx // t)
    return pl.pallas_call(
        kern,
        out_shape=jax.ShapeDtypeStruct((b, l, ATTN_WIDTH), BF16),
        grid=(b, l // t),
        in_specs=[pl.BlockSpec((1, t, ATTN_WIDTH), lambda bi, ti: (bi, ti, 0)),
                  pl.BlockSpec((1, l, KV_WIDTH), lambda bi, ti: (bi, 0, 0)),
                  pl.BlockSpec((1, l, KV_WIDTH), lambda bi, ti: (bi, 0, 0))],
        out_specs=pl.BlockSpec((1, t, ATTN_WIDTH), lambda bi, ti: (bi, ti, 0)),
        compiler_params=_params("parallel", "parallel"),
        name="attention",
    )(q, k, v)


def _route_tail(x, y, m, gf, wr_ref, br_ref, base_ref, x_ref, f_ref, ri_ref, rw_ref, cnt_ref, first):
    x1 = x + m[2:3] * y
    x_ref[0] = x1
    f = _modulate(x1, gf, m[3:4], m[4:5])
    _store_row_tiles(f_ref, f)
    t = f.shape[0]
    logits = _dot_split(f, wr_ref[0], wr_ref[1]) + br_ref[...]
    lane = lax.broadcasted_iota(I32, (t, LANES), 1).astype(F32)
    is_grp = lane < N_EXPERT_GROUPS
    gl = jnp.where(is_grp, logits, NEG_BIG)
    gmax = jnp.max(gl, axis=-1, keepdims=True)
    g_idx = jnp.min(jnp.where(gl == gmax, lane, float(LANES)), axis=-1, keepdims=True)
    g_w = 1.0 / jnp.sum(jnp.where(is_grp, jnp.exp(gl - gmax), 0.0), axis=-1, keepdims=True)
    e_lo = N_EXPERT_GROUPS + EXPERTS_PER_GROUP * g_idx
    in_grp = jnp.logical_and(lane >= e_lo, lane < e_lo + EXPERTS_PER_GROUP)
    sel = jnp.where(in_grp, logits, NEG_BIG)
    v1 = jnp.max(sel, axis=-1, keepdims=True)
    i1 = jnp.min(jnp.where(sel == v1, lane, float(LANES)), axis=-1, keepdims=True)
    sel2 = jnp.where(lane == i1, NEG_BIG, sel)
    v2 = jnp.max(sel2, axis=-1, keepdims=True)
    i2 = jnp.min(jnp.where(sel2 == v2, lane, float(LANES)), axis=-1, keepdims=True)
    e = jnp.exp(v2 - v1)
    w1 = g_w / (1.0 + e)
    w2 = g_w * e / (1.0 + e)
    eid1 = i1 - N_EXPERT_GROUPS
    eid2 = i2 - N_EXPERT_GROUPS

    @pl.when(first)
    def _():
        base_ref[...] = jnp.zeros_like(base_ref)

    hit1 = lane == eid1
    hit2 = lane == eid2
    cnt = (hit1.astype(F32) + hit2.astype(F32))
    row = lax.broadcasted_iota(I32, (t, t), 0)
    col = lax.broadcasted_iota(I32, (t, t), 1)
    tril = jnp.where(col < row, 1.0, 0.0).astype(BF16)
    before = jnp.dot(tril, cnt.astype(BF16), preferred_element_type=F32) + base_ref[0:1, :]
    rank1 = jnp.sum(jnp.where(hit1, before, 0.0), axis=-1, keepdims=True)
    rank2 = jnp.sum(jnp.where(hit2, before, 0.0), axis=-1, keepdims=True)
    total = base_ref[0:1, :] + jnp.sum(cnt, axis=0, keepdims=True)
    base_ref[...] = jnp.broadcast_to(total, base_ref.shape)
    cnt_ref[...] = jnp.broadcast_to(total, cnt_ref.shape)

    ri = jnp.where(lane == 0, eid1, jnp.where(lane == 1, eid2, jnp.where(lane == 2, rank1,
                   jnp.where(lane == 3, rank2, 0.0))))
    ri_ref[...] = ri.T[0:8, :].astype(I32)
    l8 = lax.broadcasted_iota(I32, (t, 8), 1)
    rw_ref[...] = jnp.where(l8 == 0, w1, jnp.where(l8 == 1, w2, 0.0))


def _tail_out_shapes(b, n_rows, d):
    t = TOKEN_TILE
    shapes = [jax.ShapeDtypeStruct((b, n_rows, d), F32), jax.ShapeDtypeStruct((b * n_rows * ROW_SUB, LANES), F32),
              jax.ShapeDtypeStruct((8, b * n_rows), I32), jax.ShapeDtypeStruct((b * n_rows, 8), F32),
              jax.ShapeDtypeStruct((8, LANES), F32)]
    nt = n_rows // t
    flat = lambda w: pl.BlockSpec((t, w), lambda bi, ti: (bi * nt + ti, 0))
    specs = [pl.BlockSpec((1, t, d), lambda bi, ti: (bi, ti, 0)),
             pl.BlockSpec((t * ROW_SUB, LANES), lambda bi, ti: (bi * nt + ti, 0)),
             pl.BlockSpec((8, t), lambda bi, ti: (0, bi * nt + ti)), flat(8),
             pl.BlockSpec((8, LANES), lambda bi, ti: (0, 0))]
    return shapes, specs


def _l0_post_kernel(ctx_ref, x_ref, o_ref, gu_ref, gv_ref, wsp_ref, bsp_ref, wout_ref, mod_ref, gf_ref, wr_ref,
                    br_ref, x1_ref, f_ref, ri_ref, rw_ref, cnt_ref, base_ref, *, nct):
    tq = x_ref.shape[1]
    lane = lax.broadcasted_iota(I32, (GMLP_CHUNK, LANES), 1)
    low = lane < GMLP_GROUP_DIM
    gated = []
    for c in range(tq // GMLP_CHUNK):
        rows = slice(c * GMLP_CHUNK, (c + 1) * GMLP_CHUNK)
        mixed = []
        for m in range(GMLP_WIDTH // LANES):
            pair = gv_ref[0, rows, LANES * m:LANES * (m + 1)]
            a = jnp.dot(wsp_ref[2 * m], pair, preferred_element_type=F32)
            b = jnp.dot(wsp_ref[2 * m + 1], pair, preferred_element_type=F32)
            mixed.append(jnp.where(low, a, b))
        mixed = jnp.concatenate(mixed, axis=1) + bsp_ref[...]
        gated.append((gu_ref[0, rows, :].astype(F32) * mixed).astype(BF16))
    gated = jnp.concatenate(gated, axis=0)
    y = (jnp.dot(o_ref[0], wout_ref[:ATTN_WIDTH, :], preferred_element_type=F32)
         + jnp.dot(gated, wout_ref[ATTN_WIDTH:, :], preferred_element_type=F32))
    first = jnp.logical_and(pl.program_id(0) == 0, pl.program_id(1) == 0)
    _route_tail(_seq_tile(ctx_ref, x_ref, nct), y, mod_ref[0, 0], gf_ref[...], wr_ref, br_ref, base_ref,
                x1_ref, f_ref, ri_ref, rw_ref, cnt_ref, first)


def _l0_post(ctx, x, o, gu, gv, wsp, bsp, wout, mod, gf, wr, br, nct):
    b, s, d = x.shape
    l = s + ctx.shape[1]
    t = TOKEN_TILE
    tok = lambda w: pl.BlockSpec((1, t, w), lambda bi, ti: (bi, ti, 0))
    shapes, specs = _tail_out_shapes(b, l, d)
    return pl.pallas_call(
        functools.partial(_l0_post_kernel, nct=nct),
        out_shape=shapes,
        grid=(b, l // t),
        in_specs=_seq_specs(t, d, nct) + [tok(ATTN_WIDTH), tok(GMLP_WIDTH), tok(GMLP_WIDTH),
                  _full((GMLP_GROUPS, GMLP_CHUNK, GMLP_CHUNK)), _full((GMLP_CHUNK, GMLP_WIDTH)),
                  _full((ATTN_WIDTH + GMLP_WIDTH, d)),
                  pl.BlockSpec((1, 1, 6, d), lambda bi, ti: (bi, jnp.where(ti >= nct, 1, 0), 0, 0)),
                  _full((1, d)), _full((2, d, LANES)), _full((1, LANES))],
        out_specs=specs,
        scratch_shapes=[pltpu.VMEM((8, LANES), F32)],
        compiler_params=_params("arbitrary", "arbitrary"),
        name="l0_post",
    )(ctx, x, o, gu, gv, wsp, bsp, wout, mod, gf, wr, br)


def _row_copy(src_ref, src_row, dst_ref, dst_row, sem):
    rows = lambda ref, r: ref.at[pl.ds(pl.multiple_of(r * ROW_SUB, ROW_SUB), ROW_SUB)]
    return pltpu.make_async_copy(rows(src_ref, src_row), rows(dst_ref, dst_row), sem)


def _issue_rows(src_ref, index_of, buf_ref, sem_rows, s):
    n_str, rows = buf_ref.shape[1], buf_ref.shape[2] // ROW_SUB
    slot = s % 2

    def issue(j, carry):
        for half in range(2):
            r = 2 * j + half
            for k in range(n_str):
                _row_copy(src_ref, index_of(k * rows + r), buf_ref.at[slot, k], r,
                          sem_rows.at[slot]).start(priority=(k + half) % 2)
        return carry

    lax.fori_loop(0, rows // 2, issue, 0, unroll=ROW_DMA_UNROLL // 2)


def _wait_rows(src_ref, buf_ref, sem_rows, s):
    n_str, rows = buf_ref.shape[1], buf_ref.shape[2] // ROW_SUB
    slot = s % 2
    for k in range(n_str):
        pltpu.make_async_copy(src_ref.at[pl.ds(0, rows * ROW_SUB)], buf_ref.at[slot, k], sem_rows.at[slot]).wait()
    return slot


def _gather_pipeline(src_ref, table_ref, idx_ref, buf_ref, sem_idx, sem_rows, step, n_steps):
    def idx_copy(s):
        return pltpu.make_async_copy(table_ref.at[s], idx_ref.at[s % 2], sem_idx.at[s % 2])

    def issue_rows(s):
        _issue_rows(src_ref, lambda pos: idx_ref[s % 2, pos], buf_ref, sem_rows, s)

    @pl.when(step == 0)
    def _():
        idx_copy(0).start()
        idx_copy(0).wait()
        issue_rows(0)

        @pl.when(n_steps > 1)
        def _():
            idx_copy(1).start()

    @pl.when(step + 1 < n_steps)
    def _():
        idx_copy(step + 1).wait()
        issue_rows(step + 1)

        @pl.when(step + 2 < n_steps)
        def _():
            idx_copy(step + 2).start()

    return _wait_rows(src_ref, buf_ref, sem_rows, step)


def _gather_scratch(n_str, rows):
    return [pltpu.SMEM((2, n_str * rows), I32), pltpu.VMEM((2, n_str, rows * ROW_SUB, LANES), F32),
            pltpu.SemaphoreType.DMA((2,)), pltpu.SemaphoreType.DMA((2,))]


def _dispatch_kernel(pad_ref, nu_ref, f_ref, slots_ref, xb_ref, idx_ref, zero_ref, sem_idx, sem_rows):
    i = pl.program_id(0)
    t = f_ref.shape[0] // ROW_SUB
    blk = MOE_BLOCK * ROW_SUB
    idx_cp = pltpu.make_async_copy(slots_ref.at[i], idx_ref, sem_idx)
    idx_cp.start()

    @pl.when(i == 0)
    def _():
        zero_ref[...] = jnp.zeros_like(zero_ref)
        n_blocks = xb_ref.shape[0] // blk

        def zero_block(start):
            return pltpu.make_async_copy(zero_ref, xb_ref.at[pl.ds(pl.multiple_of(start * ROW_SUB, blk), blk)],
                                         sem_rows)

        def tail_start(j, carry):
            zero_block(j * MOE_BLOCK).start()
            return carry

        def tail_wait(j, carry):
            zero_block(0).wait()
            return carry

        for e in range(N_EXPERTS):
            zero_block(jnp.maximum(pad_ref[N_EXPERTS + e] - MOE_BLOCK, 0)).start()
        lax.fori_loop(nu_ref[0], n_blocks, tail_start, 0)
        for e in range(N_EXPERTS):
            zero_block(0).wait()
        lax.fori_loop(nu_ref[0], n_blocks, tail_wait, 0)

    idx_cp.wait()

    def issue(r, carry):
        _row_copy(f_ref, r, xb_ref, idx_ref[r], sem_rows).start(priority=0)
        _row_copy(f_ref, r, xb_ref, idx_ref[t + r], sem_rows).start(priority=1)
        return carry

    lax.fori_loop(0, t, issue, 0, unroll=ROW_DMA_UNROLL)
    for _ in range(2):
        pltpu.make_async_copy(f_ref, xb_ref.at[pl.ds(0, t * ROW_SUB)], sem_rows).wait()


def _dispatch(f, slots, pad, n_used, n_slots):
    t = TOKEN_TILE
    n_tok = f.shape[0] // ROW_SUB
    return pl.pallas_call(
        _dispatch_kernel,
        out_shape=jax.ShapeDtypeStruct((n_slots * ROW_SUB, LANES), F32),
        grid_spec=pltpu.PrefetchScalarGridSpec(
            num_scalar_prefetch=2,
            grid=(n_tok // t,),
            in_specs=[pl.BlockSpec((t * ROW_SUB, LANES), lambda i, p, nu: (i, 0)),
                      pl.BlockSpec(memory_space=pltpu.VMEM)],
            out_specs=pl.BlockSpec(memory_space=pl.ANY),
            scratch_shapes=[pltpu.SMEM((2 * t,), I32), pltpu.VMEM((MOE_BLOCK * ROW_SUB, LANES), F32),
                            pltpu.SemaphoreType.DMA, pltpu.SemaphoreType.DMA]),
        compiler_params=_params("arbitrary"),
        name="moe_dispatch",
    )(pad, n_used, f, slots)


def _experts_kernel(be_ref, nu_ref, xb_ref, w1_ref, w3_ref, w2_ref, y_ref, w1b_ref, w3b_ref, w2b_ref):
    i = pl.program_id(0)

    @pl.when(jnp.logical_or(i == 0, be_ref[i] != be_ref[jnp.maximum(i - 1, 0)]))
    def _():
        w1b_ref[...] = w1_ref[...].astype(BF16)
        w3b_ref[...] = w3_ref[...].astype(BF16)
        w2b_ref[...] = w2_ref[...].astype(BF16)

    @pl.when(i < nu_ref[0])
    def _():
        x = _load_row_tiles(xb_ref).astype(BF16)
        h1 = jnp.dot(x, w1b_ref[...], preferred_element_type=F32)
        h3 = jnp.dot(x, w3b_ref[...], preferred_element_type=F32)
        a = (h1 * jax.nn.sigmoid(h1) * h3).astype(BF16)
        _store_row_tiles(y_ref, jnp.dot(a, w2b_ref[...], preferred_element_type=F32))

    @pl.when(i >= nu_ref[0])
    def _():
        y_ref[...] = jnp.zeros_like(y_ref)


def _experts(xb, w1, w3, w2, layer, block_e, n_used):
    n_blocks = block_e.shape[0]
    _, _, d, hid = w1.shape
    rb = MOE_BLOCK * ROW_SUB
    wspec = lambda s: pl.BlockSpec((None, None) + s, lambda i, be, nu: (layer, be[i], 0, 0))
    return pl.pallas_call(
        _experts_kernel,
        out_shape=jax.ShapeDtypeStruct((n_blocks * rb, LANES), F32),
        grid_spec=pltpu.PrefetchScalarGridSpec(
            num_scalar_prefetch=2,
            grid=(n_blocks,),
            in_specs=[pl.BlockSpec((rb, LANES), lambda i, be, nu: (jnp.minimum(i, nu[0] - 1), 0)),
                      wspec((d, hid)), wspec((d, hid)), wspec((hid, d))],
            out_specs=pl.BlockSpec((rb, LANES), lambda i, be, nu: (i, 0)),
            scratch_shapes=[pltpu.VMEM((d, hid), BF16), pltpu.VMEM((d, hid), BF16), pltpu.VMEM((hid, d), BF16)]),
        compiler_params=_params("arbitrary"),
        name="moe_experts",
    )(block_e, n_used, xb, w1, w3, w2)


def _moe_plan(route_i, counts, n_tok):
    eid, rank = route_i[0:2], route_i[2:4]
    cnt = counts[0, :N_EXPERTS].astype(I32)
    padded = (cnt + MOE_BLOCK - 1) // MOE_BLOCK * MOE_BLOCK
    pend = jnp.cumsum(padded)
    pstart = pend - padded
    experts = jnp.arange(N_EXPERTS, dtype=I32)
    slots = rank + jnp.sum(jnp.where(eid[:, :, None] == experts, pstart, 0), axis=-1)
    n_assign = 2 * n_tok
    n_slots = (n_assign + MOE_BLOCK - 1) // MOE_BLOCK * MOE_BLOCK + N_EXPERTS * MOE_BLOCK
    n_blocks = n_slots // MOE_BLOCK
    bstart = jnp.arange(n_blocks, dtype=I32) * MOE_BLOCK
    block_e = jnp.minimum(jnp.sum(bstart[:, None] >= pend[None, :], axis=1), N_EXPERTS - 1).astype(I32)
    n_used = pend[-1] // MOE_BLOCK
    last_e = block_e[jnp.maximum(n_used - 1, 0)]
    block_e = jnp.where(jnp.arange(n_blocks) < n_used, block_e, last_e).astype(I32)
    slots = slots.reshape(2, n_tok // TOKEN_TILE, TOKEN_TILE).transpose(1, 0, 2)
    slots = slots.reshape(n_tok // TOKEN_TILE, 2 * TOKEN_TILE).astype(I32)
    pad = jnp.concatenate([pstart + cnt, pend]).astype(I32)
    return slots, block_e, n_used.reshape(1).astype(I32), pad, n_slots


def _moe(f, route_i, counts, w1, w3, w2, layer):
    n_tok = f.shape[0] // ROW_SUB
    slots, block_e, n_used, pad, n_slots = _moe_plan(route_i, counts, n_tok)
    xb = _dispatch(f, slots, pad, n_used, n_slots)
    y = _experts(xb, w1, w3, w2, layer, block_e, n_used)
    return y, slots


def _combined_rows(y_ref, slots_ref, rw_ref, idx_ref, ybuf_ref, sem_idx, sem_rows):
    slot = _gather_pipeline(y_ref, slots_ref, idx_ref, ybuf_ref, sem_idx, sem_rows,
                            pl.program_id(0), pl.num_programs(0))
    rw = rw_ref[...]
    w1, w2 = rw[:, 0:1], rw[:, 1:2]
    t = ybuf_ref.shape[2] // ROW_SUB
    part = lambda k, s: ybuf_ref[slot, k, pl.ds(s, t, stride=ROW_SUB), :]
    return jnp.concatenate([w1 * part(0, s) + w2 * part(1, s) for s in range(ROW_SUB)], axis=1)


def _l0_combine_kernel(x_ref, rw_ref, slots_ref, y_ref, mod0_ref, mod1_ref, g_ref, win_ref,
                       x2_ref, u_ref, idx_ref, ybuf_ref, sem_idx, sem_rows):
    moe = _combined_rows(y_ref, slots_ref, rw_ref, idx_ref, ybuf_ref, sem_idx, sem_rows)
    x2 = x_ref[...] + mod0_ref[0, 0][5:6] * moe
    x2_ref[...] = x2
    m1 = mod1_ref[0, 0]
    h = _modulate(x2, g_ref[...], m1[0:1], m1[1:2]).astype(BF16)
    u_ref[...] = jnp.dot(h, win_ref[...], preferred_element_type=F32)


def _l0_combine(x1, rw, slots, y, mod0, mod1, g, win, b, nct):
    n_tok, d = x1.shape
    t = TOKEN_TILE
    nt = n_tok // b // t
    l = n_tok // b
    modspec = pl.BlockSpec((1, 1, 6, d), lambda i: (i // nt, jnp.where(i % nt >= nct, 1, 0), 0, 0))
    return pl.pallas_call(
        _l0_combine_kernel,
        out_shape=[jax.ShapeDtypeStruct((n_tok, d), F32), jax.ShapeDtypeStruct((l, b * S5_WIDTH), F32)],
        grid=(n_tok // t,),
        in_specs=[pl.BlockSpec((t, d), lambda i: (i, 0)), pl.BlockSpec((t, 8), lambda i: (i, 0)),
                  pl.BlockSpec(memory_space=pltpu.VMEM), pl.BlockSpec(memory_space=pl.ANY),
                  modspec, modspec, _full((1, d)), _full((d, S5_WIDTH))],
        out_specs=[pl.BlockSpec((t, d), lambda i: (i, 0)),
                   pl.BlockSpec((t, S5_WIDTH), lambda i: (i % nt, i // nt))],
        scratch_shapes=_gather_scratch(2, t),
        compiler_params=_params("arbitrary"),
        name="l0_combine",
    )(x1, rw, slots, y, mod0, mod1, g, win)


def _s5_zoh_kernel(are_ref, aim_ref, ldt_ref, bre_ref, bim_ref, lre_ref, lim_ref, bbre_ref, bbim_ref):
    ar, ai = are_ref[...], aim_ref[...]
    dt = jnp.exp(ldt_ref[...])
    mag = jnp.exp(ar * dt)
    lr = mag * jnp.cos(ai * dt)
    li = mag * jnp.sin(ai * dt)
    lre_ref[...] = lr
    lim_ref[...] = li
    nr, ni = lr - 1.0, li
    den = ar * ar + ai * ai
    cr = (nr * ar + ni * ai) / den
    ci = (ni * ar - nr * ai) / den
    br, bi = bre_ref[...], bim_ref[...]
    bbre_ref[...] = cr * br - ci * bi
    bbim_ref[...] = cr * bi + ci * br


def _s5_zoh(a_re, a_im, log_dt, b_re, b_im):
    n_dir, g, n, p = b_re.shape
    rows = n_dir * g * p
    expand = lambda a: jnp.broadcast_to(a[:, :, None, :], (n_dir, g, p, n)).reshape(rows, n)
    ldt = jnp.broadcast_to(log_dt[:, :, None, None], (n_dir, g, p, n)).reshape(rows, n)
    bt = lambda a: jnp.transpose(a, (0, 1, 3, 2)).reshape(rows, n)
    outs = pl.pallas_call(
        _s5_zoh_kernel,
        out_shape=[jax.ShapeDtypeStruct((rows, n), F32)] * 4,
        name="s5_zoh",
    )(expand(a_re), expand(a_im), ldt, bt(b_re), bt(b_im))
    lre, lim, bbre, bbim = (o.reshape(n_dir, g, p, n) for o in outs)
    return lre[:, :, 0, :], lim[:, :, 0, :], bbre, bbim


def _s5_matrices(lre, lim, bbre, bbim, c_re, c_im):
    n_dir = lre.shape[0]
    nj = S5_GROUPS // (2 * S5_TILES)
    shp = (n_dir, nj, S5_TILES, 2, S5_GROUP_DIM, S5_STATE)
    eye_t = jnp.eye(S5_TILES, dtype=F32)
    eye_g = jnp.eye(2, dtype=F32)

    def place(a):
        return jnp.einsum("djtgpn,tu,gh->djtgpuhn", a.reshape(shp), eye_t, eye_g)

    kdim = S5_TILES * 2 * S5_GROUP_DIM
    sdim = S5_TILES * 2 * 2 * S5_STATE
    bm = jnp.stack([place(bbre), place(bbim)], axis=6)
    bm = bm.reshape(n_dir, nj, kdim, sdim).astype(BF16)
    cm = jnp.stack([place(c_re), place(-c_im)], axis=6)
    cm = jnp.transpose(cm, (0, 1, 5, 6, 7, 8, 2, 3, 4)).reshape(n_dir, nj, sdim, kdim).astype(BF16)
    lam = jnp.stack([lre.reshape(n_dir, nj, S5_TILES * LANES), lim.reshape(n_dir, nj, S5_TILES * LANES)], axis=2)
    return bm, cm, lam


def _s5_scan_kernel(uf_ref, ub_ref, bm_ref, cm_ref, lam_ref, yf_ref, yb_ref, v_ref, h_ref, *, tc, nb):
    @pl.when(pl.program_id(0) == 0)
    def _():
        h_ref[...] = jnp.zeros_like(h_ref)

    nj = bm_ref.shape[1]
    kdim = bm_ref.shape[2]
    for d, (u_ref, y_ref) in enumerate(((uf_ref, yf_ref), (ub_ref, yb_ref))):
        for jq in range(nj):
            uk = u_ref[:, kdim * jq:kdim * (jq + 1)].astype(BF16)
            v_ref[...] = jnp.dot(uk, bm_ref[d, jq], preferred_element_type=F32)
            lam = lam_ref[d, jq]
            lr = [jnp.broadcast_to(lam[0:1, LANES * t:LANES * (t + 1)], (nb, LANES)) for t in range(S5_TILES)]
            li = [jnp.broadcast_to(lam[1:2, LANES * t:LANES * (t + 1)], (nb, LANES)) for t in range(S5_TILES)]
            h0 = h_ref[d, jq]
            init = tuple(h0[:, LANES * c:LANES * (c + 1)] for c in range(2 * S5_TILES))

            def step(s, carry, d=d, lr=lr, li=li):
                tt = s if d == 0 else tc - 1 - s
                r0 = pl.multiple_of(tt * nb, nb)
                rows = v_ref[pl.ds(r0, nb), :]
                new = []
                for t in range(S5_TILES):
                    vr = rows[:, 2 * LANES * t:2 * LANES * t + LANES]
                    vi = rows[:, 2 * LANES * t + LANES:2 * LANES * (t + 1)]
                    hr, hi = carry[2 * t], carry[2 * t + 1]
                    new.append(lr[t] * hr - li[t] * hi + vr)
                    new.append(lr[t] * hi + li[t] * hr + vi)
                v_ref[pl.ds(r0, nb), :] = jnp.concatenate(new, axis=1)
                return tuple(new)

            fin = lax.fori_loop(0, tc, step, init)
            h_ref[d, jq] = jnp.concatenate(fin, axis=1)
            y_ref[:, kdim * jq:kdim * (jq + 1)] = jnp.dot(v_ref[...].astype(BF16), cm_ref[d, jq],
                                                         preferred_element_type=F32)


def _s5_scan(u_tm, bm, cm, lam, nb, n_ctx):
    rows_total, w = u_tm.shape
    tc = S5_CHUNK
    r = tc * nb
    n_chunks = rows_total // r
    ncc = n_ctx // tc
    bwd = lambda i: (jnp.where(i < ncc, ncc - 1 - i, n_chunks - 1 - (i - ncc)), 0)
    nj = bm.shape[1]
    sdim = bm.shape[3]
    kern = functools.partial(_s5_scan_kernel, tc=tc, nb=nb)
    return pl.pallas_call(
        kern,
        out_shape=[jax.ShapeDtypeStruct((rows_total, w), F32)] * 2,
        grid=(n_chunks,),
        in_specs=[pl.BlockSpec((r, w), lambda i: (i, 0)), pl.BlockSpec((r, w), bwd),
                  _full(bm.shape), _full(cm.shape), _full(lam.shape)],
        out_specs=[pl.BlockSpec((r, w), lambda i: (i, 0)), pl.BlockSpec((r, w), bwd)],
        scratch_shapes=[pltpu.VMEM((r, sdim), F32), pltpu.VMEM((2, nj, nb, sdim), F32)],
        compiler_params=_params("arbitrary"),
        name="s5_scan",
    )(u_tm, u_tm, bm, cm, lam)


def _l1_post_kernel(x_ref, u_ref, yf_ref, yb_ref, dsk_ref, wglu_ref, bglu_ref, wout_ref, mod_ref, gf_ref,
                    wr_ref, br_ref, x3_ref, f_ref, ri_ref, rw_ref, cnt_ref, base_ref):
    y = dsk_ref[...] * u_ref[...] + yf_ref[...] + yb_ref[...]
    g = jax.nn.gelu(y, approximate=True)
    z = jnp.dot(g.astype(BF16), wglu_ref[...], preferred_element_type=F32) + bglu_ref[...]
    o = jnp.dot((g * jax.nn.sigmoid(z)).astype(BF16), wout_ref[...], preferred_element_type=F32)
    first = jnp.logical_and(pl.program_id(0) == 0, pl.program_id(1) == 0)
    _route_tail(x_ref[0], o, mod_ref[0, 0], gf_ref[...], wr_ref, br_ref, base_ref,
                x3_ref, f_ref, ri_ref, rw_ref, cnt_ref, first)


def _l1_post(x2, u_tm, yf, yb, dsk, wglu, bglu, wout, mod, gf, wr, br, n_ctx):
    b, l, d = x2.shape
    t = TOKEN_TILE
    nct = n_ctx // t
    s = l - n_ctx
    tm = pl.BlockSpec((t, S5_WIDTH), lambda bi, ti: (ti + nct, bi))
    shapes, specs = _tail_out_shapes(b, s, d)
    return pl.pallas_call(
        _l1_post_kernel,
        out_shape=shapes,
        grid=(b, s // t),
        in_specs=[pl.BlockSpec((1, t, d), lambda bi, ti: (bi, ti + nct, 0)), tm, tm, tm,
                  _full((1, S5_WIDTH)), _full((S5_WIDTH, S5_WIDTH)), _full((1, S5_WIDTH)), _full((S5_WIDTH, d)),
                  pl.BlockSpec((1, 1, 6, d), lambda bi, ti: (bi, 1, 0, 0)),
                  _full((1, d)), _full((2, d, LANES)), _full((1, LANES))],
        out_specs=specs,
        scratch_shapes=[pltpu.VMEM((8, LANES), F32)],
        compiler_params=_params("arbitrary", "arbitrary"),
        name="l1_post",
    )(x2, u_tm, yf, yb, dsk, wglu, bglu, wout, mod, gf, wr, br)


def _final_kernel(x_ref, rw_ref, slots_ref, y_ref, mod_ref, g_ref, o_ref, idx_ref, ybuf_ref, sem_idx, sem_rows):
    moe = _combined_rows(y_ref, slots_ref, rw_ref, idx_ref, ybuf_ref, sem_idx, sem_rows)
    x4 = x_ref[...] + mod_ref[0, 0][5:6] * moe
    o_ref[...] = _rms(x4) * g_ref[...]


def _final(x3, rw, slots, y, mod, g, b):
    n_tok, d = x3.shape
    t = TOKEN_TILE
    nt = n_tok // b // t
    return pl.pallas_call(
        _final_kernel,
        out_shape=jax.ShapeDtypeStruct((n_tok, d), F32),
        grid=(n_tok // t,),
        in_specs=[pl.BlockSpec((t, d), lambda i: (i, 0)), pl.BlockSpec((t, 8), lambda i: (i, 0)),
                  pl.BlockSpec(memory_space=pltpu.VMEM), pl.BlockSpec(memory_space=pl.ANY),
                  pl.BlockSpec((1, 1, 6, d), lambda i: (i // nt, 1, 0, 0)), _full((1, d))],
        out_specs=pl.BlockSpec((t, d), lambda i: (i, 0)),
        scratch_shapes=_gather_scratch(2, t),
        compiler_params=_params("arbitrary"),
        name="final_combine",
    )(x3, rw, slots, y, mod, g)


def _rope_tables(n_ctx, n_lat):
    pos = jnp.arange(n_lat, dtype=I32)
    row = (pos // GRID_W).astype(F32)
    col = (pos % GRID_W).astype(F32)
    inv_freq = jnp.power(ROPE_THETA, -jnp.arange(0, AXIS_DIM, 2, dtype=F32) / AXIS_DIM)
    lane = np.arange(LANES)
    dd = lane % HEAD_DIM
    use_col = dd >= AXIS_DIM
    first = (dd % AXIS_DIM) < AXIS_DIM // 2
    freq = inv_freq[(dd % AXIS_DIM) % (AXIS_DIM // 2)]
    ang = jnp.where(use_col[None, :], col[:, None], row[:, None]) * freq[None, :]
    cos, sin = jnp.cos(ang), jnp.sin(ang)
    sa = jnp.where(first[None, :], -sin, 0.0)
    sb = jnp.where(first[None, :], 0.0, sin)
    pad = lambda a, v: jnp.concatenate([jnp.full((n_ctx, LANES), v, F32), a], axis=0)
    return pad(cos, 1.0), pad(sa, 0.0), pad(sb, 0.0)


def _group_ones(width, group):
    idx = np.arange(width) // group
    return jnp.asarray(idx[:, None] == idx[None, :], dtype=BF16)


def _router_weights(w_grp, b_grp, w_rt, b_rt):
    d = w_grp.shape[0]
    used = N_EXPERT_GROUPS + N_EXPERTS
    wr = jnp.concatenate([w_grp, w_rt, jnp.zeros((d, LANES - used), F32)], axis=1)
    br = jnp.concatenate([b_grp, b_rt, jnp.zeros((LANES - used,), F32)]).reshape(1, LANES)
    return jnp.stack(_split_bf16(wr)), br


def kernel(x, c, ctx, c_ctx, w_mod, b_mod, norm_mix_g, norm_ffn_g, mix_w_in, mix_w_out, q_norm_g, k_norm_g, gmlp_norm_g, gmlp_w_spatial, gmlp_b_spatial, s5_w_in, s5_a_re, s5_a_im, s5_log_dt, s5_b_re, s5_b_im, s5_c_re, s5_c_im, s5_d, s5_w_glu, s5_b_glu, s5_w_out, moe_w_group, moe_b_group, moe_w_router, moe_b_router, moe_w1, moe_w3, moe_w2, final_norm_g):
    b, s, d = x.shape
    n_ctx = ctx.shape[1]
    l = n_ctx + s
    assert w_mod.shape[0] == 2, "two layers: attention/gMLP then S5"
    assert n_ctx % TOKEN_TILE == 0 and s % TOKEN_TILE == 0 and b % 8 == 0 and d == D_MODEL
    nct = n_ctx // TOKEN_TILE

    r = (b + 1 + 7) // 8 * 8
    cond = jnp.concatenate([c, c_ctx[None, :], jnp.zeros((r - b - 1, d), F32)], axis=0)
    mods = _adaln(cond, w_mod, b_mod)
    lat = mods[:, :b].reshape(2, b, 1, 6, d)
    cxt = jnp.broadcast_to(mods[:, b].reshape(2, 1, 1, 6, d), (2, b, 1, 6, d))
    mod = jnp.concatenate([cxt, lat], axis=2)

    w_in = jnp.concatenate([mix_w_in[0][:, _Q_COL_PERM], mix_w_in[0][:, ATTN_WIDTH:]], axis=1).astype(BF16)
    w_out = jnp.concatenate([mix_w_out[0][_Q_COL_PERM], mix_w_out[0][ATTN_WIDTH:]], axis=0).astype(BF16)
    cos, sa, sb = _rope_tables(n_ctx, s)
    qg = jnp.tile(q_norm_g[0], N_Q_HEADS).reshape(1, ATTN_WIDTH)
    kg = jnp.tile(k_norm_g[0], N_KV_HEADS).reshape(1, KV_WIDTH)
    q, k, v, gu, gv = _l0_proj(ctx, x, mod[0], norm_mix_g[0].reshape(1, d), w_in, qg, kg,
                               gmlp_norm_g[0].reshape(1, GMLP_WIDTH), cos, sa, sb,
                               _group_ones(ATTN_WIDTH, HEAD_DIM), _group_ones(KV_WIDTH, HEAD_DIM), nct)
    o = _attention(q, k, v, n_ctx)
    bsp = jnp.repeat(gmlp_b_spatial[0].T, GMLP_GROUP_DIM, axis=1)
    wr0, br0 = _router_weights(moe_w_group[0], moe_b_group[0], moe_w_router[0], moe_b_router[0])
    x1, f0, ri0, rw0, cnt0 = _l0_post(ctx, x, o, gu, gv, gmlp_w_spatial[0].astype(BF16), bsp, w_out, mod[0],
                                      norm_ffn_g[0].reshape(1, d), wr0, br0, nct)
    y0, slots0 = _moe(f0, ri0, cnt0, moe_w1, moe_w3, moe_w2, 0)
    x2, u_tm = _l0_combine(x1.reshape(b * l, d), rw0, slots0, y0, mod[0], mod[1], norm_mix_g[1].reshape(1, d),
                           s5_w_in[0].astype(BF16), b, nct)

    lre, lim, bbre, bbim = _s5_zoh(s5_a_re[0], s5_a_im[0], s5_log_dt[0], s5_b_re[0], s5_b_im[0])
    bm, cm, lam = _s5_matrices(lre, lim, bbre, bbim, s5_c_re[0], s5_c_im[0])
    yf, yb = _s5_scan(u_tm.reshape(l * b, S5_WIDTH), bm, cm, lam, b, n_ctx)
    wr1, br1 = _router_weights(moe_w_group[1], moe_b_group[1], moe_w_router[1], moe_b_router[1])
    tmv = lambda a: a.reshape(l, b * S5_WIDTH)
    x3, f1, ri1, rw1, cnt1 = _l1_post(x2.reshape(b, l, d), u_tm, tmv(yf), tmv(yb), s5_d[0].reshape(1, S5_WIDTH),
                                      s5_w_glu[0].astype(BF16), s5_b_glu[0].reshape(1, S5_WIDTH),
                                      s5_w_out[0].astype(BF16), mod[1], norm_ffn_g[1].reshape(1, d), wr1, br1, n_ctx)
    y1, slots1 = _moe(f1, ri1, cnt1, moe_w1, moe_w3, moe_w2, 1)
    out = _final(x3.reshape(b * s, d), rw1, slots1, y1, mod[1], final_norm_g.reshape(1, d), b)
    return out.reshape(b, s, d)
```

```python
import functools

import numpy as np
import jax
import jax.numpy as jnp
from jax import lax
from jax.experimental import pallas as pl
from jax.experimental.pallas import tpu as pltpu

F32 = jnp.float32
BF16 = jnp.bfloat16
I32 = jnp.int32

EPS = 1e-6
GRID_W = 64
HEAD_DIM = 64
N_Q_HEADS = 8
N_KV_HEADS = 2
ATTN_WIDTH = N_Q_HEADS * HEAD_DIM
KV_WIDTH = N_KV_HEADS * HEAD_DIM
AXIS_DIM = HEAD_DIM // 2
ROPE_THETA = 10000.0
GMLP_GROUPS = 8
GMLP_GROUP_DIM = 64
GMLP_WIDTH = GMLP_GROUPS * GMLP_GROUP_DIM
GMLP_CHUNK = 128
MIX_IN_WIDTH = ATTN_WIDTH + 2 * KV_WIDTH + 2 * GMLP_WIDTH
S5_GROUPS = 32
S5_GROUP_DIM = 16
S5_STATE = 64
S5_WIDTH = S5_GROUPS * S5_GROUP_DIM
N_EXPERT_GROUPS = 4
EXPERTS_PER_GROUP = 8
N_EXPERTS = N_EXPERT_GROUPS * EXPERTS_PER_GROUP
EXPERT_HIDDEN = 512
MOE_BLOCK = 256

D_MODEL = 1024
LANES = 128
ROW_SUB = D_MODEL // LANES
TOKEN_TILE = 256
ROW_DMA_UNROLL = 8
S5_CHUNK = 64
S5_TILES = 4
VMEM_LIMIT = 48 * 1024 * 1024
NEG_BIG = -1e30

_Q_HEAD_ORDER = (0, 4, 1, 5, 2, 6, 3, 7)
_Q_COL_PERM = np.concatenate([np.arange(HEAD_DIM) + HEAD_DIM * h for h in _Q_HEAD_ORDER])


def _params(*sem):
    return pltpu.CompilerParams(dimension_semantics=sem, vmem_limit_bytes=VMEM_LIMIT)


def _rms(x):
    return x * lax.rsqrt(jnp.mean(x * x, axis=-1, keepdims=True) + EPS)


def _modulate(x, g, shift, scale):
    return _rms(x) * g * (1.0 + scale) + shift


def _full(shape):
    return pl.BlockSpec(shape, lambda *_: (0,) * len(shape))


def _split_bf16(a):
    hi = a.astype(BF16)
    return hi, (a - hi.astype(F32)).astype(BF16)


def _dot_split(a, w_hi, w_lo):
    a_hi, a_lo = _split_bf16(a)
    dot = functools.partial(jnp.dot, preferred_element_type=F32)
    return dot(a_hi, w_hi) + (dot(a_lo, w_hi) + dot(a_hi, w_lo))


def _store_row_tiles(ref, val):
    n = val.shape[0]
    for s in range(ROW_SUB):
        ref[pl.ds(s, n, stride=ROW_SUB), :] = val[:, LANES * s:LANES * (s + 1)]


def _load_row_tiles(ref):
    n = ref.shape[0] // ROW_SUB
    return jnp.concatenate([ref[pl.ds(s, n, stride=ROW_SUB), :] for s in range(ROW_SUB)], axis=1)


def _adaln_kernel(cond_ref, w_ref, b_ref, o_ref):
    c = cond_ref[...]
    s = c * jax.nn.sigmoid(c)
    o_ref[0] = _dot_split(s, *_split_bf16(w_ref[0])) + b_ref[0]


def _adaln(cond, w_mod, b_mod):
    depth, d, n = w_mod.shape
    r = cond.shape[0]
    tn = 1536
    return pl.pallas_call(
        _adaln_kernel,
        out_shape=jax.ShapeDtypeStruct((depth, r, n), F32),
        grid=(depth, n // tn),
        in_specs=[pl.BlockSpec((r, d), lambda l, j: (0, 0)),
                  pl.BlockSpec((1, d, tn), lambda l, j: (l, 0, j)),
                  pl.BlockSpec((1, 1, tn), lambda l, j: (l, 0, j))],
        out_specs=pl.BlockSpec((1, r, tn), lambda l, j: (l, 0, j)),
        compiler_params=_params("parallel", "parallel"),
        name="adaln",
    )(cond, w_mod, b_mod.reshape(depth, 1, n))


def _seq_tile(ctx_ref, x_ref, nct):
    return jnp.where(pl.program_id(1) < nct, ctx_ref[0], x_ref[0])


def _seq_specs(t, d, nct):
    return [pl.BlockSpec((1, t, d), lambda bi, ti: (bi, jnp.minimum(ti, nct - 1), 0)),
            pl.BlockSpec((1, t, d), lambda bi, ti: (bi, jnp.maximum(ti - nct, 0), 0))]


def _l0_proj_kernel(ctx_ref, x_ref, mod_ref, g_ref, win_ref, qg_ref, kg_ref, lng_ref, cos_ref, sa_ref, sb_ref,
                    gq_ref, gk_ref, q_ref, k_ref, v_ref, gu_ref, gv_ref, *, nct):
    m = mod_ref[0, 0]
    h = _modulate(_seq_tile(ctx_ref, x_ref, nct), g_ref[...], m[0:1], m[1:2]).astype(BF16)
    z = jnp.dot(h, win_ref[...], preferred_element_type=F32)
    cos, sa, sb = cos_ref[...], sa_ref[...], sb_ref[...]

    def head_norm_rope(t, gsum_ref, gain, reps):
        w = t.shape[1]
        ss = jnp.dot((t * t).astype(BF16), gsum_ref[...], preferred_element_type=F32)
        tn = t * lax.rsqrt(ss * (1.0 / HEAD_DIM) + EPS) * gain
        c, a, b = (jnp.concatenate([tab] * reps, axis=1) if reps > 1 else tab for tab in (cos, sa, sb))
        return tn * c + pltpu.roll(tn, w - AXIS_DIM // 2, 1) * a + pltpu.roll(tn, AXIS_DIM // 2, 1) * b

    q = head_norm_rope(z[:, :ATTN_WIDTH], gq_ref, qg_ref[...], ATTN_WIDTH // LANES)
    q_ref[0] = (q * (HEAD_DIM ** -0.5)).astype(BF16)
    k = head_norm_rope(z[:, ATTN_WIDTH:ATTN_WIDTH + KV_WIDTH], gk_ref, kg_ref[...], 1)
    k_ref[0] = k.astype(BF16)
    v_ref[0] = z[:, ATTN_WIDTH + KV_WIDTH:ATTN_WIDTH + 2 * KV_WIDTH].astype(BF16)
    off = ATTN_WIDTH + 2 * KV_WIDTH
    gu_ref[0] = jax.nn.gelu(z[:, off:off + GMLP_WIDTH], approximate=True).astype(BF16)
    gv = jax.nn.gelu(z[:, off + GMLP_WIDTH:], approximate=True)
    mu = jnp.mean(gv, axis=-1, keepdims=True)
    gc = gv - mu
    gn = gc * lax.rsqrt(jnp.mean(gc * gc, axis=-1, keepdims=True) + EPS) * lng_ref[...]
    gv_ref[0] = gn.astype(BF16)


def _l0_proj(ctx, x, mod, g, w_in, qg, kg, lng, cos, sa, sb, gq, gk, nct):
    b, s, d = x.shape
    l = s + ctx.shape[1]
    t = TOKEN_TILE
    tok = lambda w: pl.BlockSpec((1, t, w), lambda bi, ti: (bi, ti, 0))
    tab = pl.BlockSpec((t, LANES), lambda bi, ti: (ti, 0))
    outs = [jax.ShapeDtypeStruct((b, l, w), BF16) for w in (ATTN_WIDTH, KV_WIDTH, KV_WIDTH, GMLP_WIDTH, GMLP_WIDTH)]
    return pl.pallas_call(
        functools.partial(_l0_proj_kernel, nct=nct),
        out_shape=outs,
        grid=(b, l // t),
        in_specs=_seq_specs(t, d, nct) + [
                  pl.BlockSpec((1, 1, 6, d), lambda bi, ti: (bi, jnp.where(ti >= nct, 1, 0), 0, 0)),
                  _full((1, d)), _full((d, MIX_IN_WIDTH)), _full((1, ATTN_WIDTH)), _full((1, KV_WIDTH)),
                  _full((1, GMLP_WIDTH)), tab, tab, tab,
                  _full((ATTN_WIDTH, ATTN_WIDTH)), _full((KV_WIDTH, KV_WIDTH))],
        out_specs=[tok(w) for w in (ATTN_WIDTH, KV_WIDTH, KV_WIDTH, GMLP_WIDTH, GMLP_WIDTH)],
        compiler_params=_params("parallel", "parallel"),
        name="l0_proj",
    )(ctx, x, mod, g, w_in, qg, kg, lng, cos, sa, sb, gq, gk)


def _attn_kernel(q_ref, k_ref, v_ref, o_ref, *, n_ctx, nct):
    ti = pl.program_id(1)
    tq = q_ref.shape[1]
    lane = lax.broadcasted_iota(I32, (tq, LANES), 1)
    low = lane < HEAD_DIM

    def attend(nk):
        kk = k_ref[0, :nk, :]
        vv = v_ref[0, :nk, :]
        for m in range(ATTN_WIDTH // LANES):
            qp = q_ref[0, :, LANES * m:LANES * (m + 1)]
            halves = []
            for sel in (low, jnp.logical_not(low)):
                qm = jnp.where(sel, qp, jnp.zeros_like(qp))
                s = lax.dot_general(qm, kk, (((1,), (1,)), ((), ())), preferred_element_type=F32)
                p = jnp.exp(s - jnp.max(s, axis=-1, keepdims=True))
                den = jnp.sum(p, axis=-1, keepdims=True)
                pv = jnp.dot(p.astype(BF16), vv, preferred_element_type=F32)
                halves.append(pv / den)
            o_ref[0, :, LANES * m:LANES * (m + 1)] = jnp.where(low, halves[0], halves[1]).astype(BF16)

    @pl.when(ti < nct)
    def _():
        attend(n_ctx)

    @pl.when(ti >= nct)
    def _():
        attend(k_ref.shape[1])


def _attention(q, k, v, n_ctx):
    b, l, _ = q.shape
    t = TOKEN_TILE
    kern = functools.partial(_attn_kernel, n_ctx=n_ctx, nct=n_ctx // t)
    return pl.pallas_call(
        kern,
        out_shape=jax.ShapeDtypeStruct((b, l, ATTN_WIDTH), BF16),
        grid=(b, l // t),
        in_specs=[pl.BlockSpec((1, t, ATTN_WIDTH), lambda bi, ti: (bi, ti, 0)),
                  pl.BlockSpec((1, l, KV_WIDTH), lambda bi, ti: (bi, 0, 0)),
                  pl.BlockSpec((1, l, KV_WIDTH), lambda bi, ti: (bi, 0, 0))],
        out_specs=pl.BlockSpec((1, t, ATTN_WIDTH), lambda bi, ti: (bi, ti, 0)),
        compiler_params=_params("parallel", "parallel"),
        name="attention",
    )(q, k, v)


def _route_tail(x, y, m, gf, wr_ref, br_ref, base_ref, x_ref, f_ref, ri_ref, rw_ref, cnt_ref, first):
    x1 = x + m[2:3] * y
    x_ref[0] = x1
    f = _modulate(x1, gf, m[3:4], m[4:5])
    _store_row_tiles(f_ref, f)
    t = f.shape[0]
    logits = _dot_split(f, wr_ref[0], wr_ref[1]) + br_ref[...]
    lane = lax.broadcasted_iota(I32, (t, LANES), 1).astype(F32)
    is_grp = lane < N_EXPERT_GROUPS
    gl = jnp.where(is_grp, logits, NEG_BIG)
    gmax = jnp.max(gl, axis=-1, keepdims=True)
    g_idx = jnp.min(jnp.where(gl == gmax, lane, float(LANES)), axis=-1, keepdims=True)
    g_w = 1.0 / jnp.sum(jnp.where(is_grp, jnp.exp(gl - gmax), 0.0), axis=-1, keepdims=True)
    e_lo = N_EXPERT_GROUPS + EXPERTS_PER_GROUP * g_idx
    in_grp = jnp.logical_and(lane >= e_lo, lane < e_lo + EXPERTS_PER_GROUP)
    sel = jnp.where(in_grp, logits, NEG_BIG)
    v1 = jnp.max(sel, axis=-1, keepdims=True)
    i1 = jnp.min(jnp.where(sel == v1, lane, float(LANES)), axis=-1, keepdims=True)
    sel2 = jnp.where(lane == i1, NEG_BIG, sel)
    v2 = jnp.max(sel2, axis=-1, keepdims=True)
    i2 = jnp.min(jnp.where(sel2 == v2, lane, float(LANES)), axis=-1, keepdims=True)
    e = jnp.exp(v2 - v1)
    w1 = g_w / (1.0 + e)
    w2 = g_w * e / (1.0 + e)
    eid1 = i1 - N_EXPERT_GROUPS
    eid2 = i2 - N_EXPERT_GROUPS

    @pl.when(first)
    def _():
        base_ref[...] = jnp.zeros_like(base_ref)

    hit1 = lane == eid1
    hit2 = lane == eid2
    cnt = (hit1.astype(F32) + hit2.astype(F32))
    row = lax.broadcasted_iota(I32, (t, t), 0)
    col = lax.broadcasted_iota(I32, (t, t), 1)
    tril = jnp.where(col < row, 1.0, 0.0).astype(BF16)
    before = jnp.dot(tril, cnt.astype(BF16), preferred_element_type=F32) + base_ref[0:1, :]
    rank1 = jnp.sum(jnp.where(hit1, before, 0.0), axis=-1, keepdims=True)
    rank2 = jnp.sum(jnp.where(hit2, before, 0.0), axis=-1, keepdims=True)
    total = base_ref[0:1, :] + jnp.sum(cnt, axis=0, keepdims=True)
    base_ref[...] = jnp.broadcast_to(total, base_ref.shape)
    cnt_ref[...] = jnp.broadcast_to(total, cnt_ref.shape)

    ri = jnp.where(lane == 0, eid1, jnp.where(lane == 1, eid2, jnp.where(lane == 2, rank1,
                   jnp.where(lane == 3, rank2, 0.0))))
    ri_ref[...] = ri.T[0:8, :].astype(I32)
    l8 = lax.broadcasted_iota(I32, (t, 8), 1)
    rw_ref[...] = jnp.where(l8 == 0, w1, jnp.where(l8 == 1, w2, 0.0))


def _tail_out_shapes(b, n_rows, d):
    t = TOKEN_TILE
    shapes = [jax.ShapeDtypeStruct((b, n_rows, d), F32), jax.ShapeDtypeStruct((b * n_rows * ROW_SUB, LANES), F32),
              jax.ShapeDtypeStruct((8, b * n_rows), I32), jax.ShapeDtypeStruct((b * n_rows, 8), F32),
              jax.ShapeDtypeStruct((8, LANES), F32)]
    nt = n_rows // t
    flat = lambda w: pl.BlockSpec((t, w), lambda bi, ti: (bi * nt + ti, 0))
    specs = [pl.BlockSpec((1, t, d), lambda bi, ti: (bi, ti, 0)),
             pl.BlockSpec((t * ROW_SUB, LANES), lambda bi, ti: (bi * nt + ti, 0)),
             pl.BlockSpec((8, t), lambda bi, ti: (0, bi * nt + ti)), flat(8),
             pl.BlockSpec((8, LANES), lambda bi, ti: (0, 0))]
    return shapes, specs


def _l0_post_kernel(ctx_ref, x_ref, o_ref, gu_ref, gv_ref, wsp_ref, bsp_ref, wout_ref, mod_ref, gf_ref, wr_ref,
                    br_ref, x1_ref, f_ref, ri_ref, rw_ref, cnt_ref, base_ref, *, nct):
    tq = x_ref.shape[1]
    lane = lax.broadcasted_iota(I32, (GMLP_CHUNK, LANES), 1)
    low = lane < GMLP_GROUP_DIM
    gated = []
    for c in range(tq // GMLP_CHUNK):
        rows = slice(c * GMLP_CHUNK, (c + 1) * GMLP_CHUNK)
        mixed = []
        for m in range(GMLP_WIDTH // LANES):
            pair = gv_ref[0, rows, LANES * m:LANES * (m + 1)]
            a = jnp.dot(wsp_ref[2 * m], pair, preferred_element_type=F32)
            b = jnp.dot(wsp_ref[2 * m + 1], pair, preferred_element_type=F32)
            mixed.append(jnp.where(low, a, b))
        mixed = jnp.concatenate(mixed, axis=1) + bsp_ref[...]
        gated.append((gu_ref[0, rows, :].astype(F32) * mixed).astype(BF16))
    gated = jnp.concatenate(gated, axis=0)
    y = (jnp.dot(o_ref[0], wout_ref[:ATTN_WIDTH, :], preferred_element_type=F32)
         + jnp.dot(gated, wout_ref[ATTN_WIDTH:, :], preferred_element_type=F32))
    first = jnp.logical_and(pl.program_id(0) == 0, pl.program_id(1) == 0)
    _route_tail(_seq_tile(ctx_ref, x_ref, nct), y, mod_ref[0, 0], gf_ref[...], wr_ref, br_ref, base_ref,
                x1_ref, f_ref, ri_ref, rw_ref, cnt_ref, first)


def _l0_post(ctx, x, o, gu, gv, wsp, bsp, wout, mod, gf, wr, br, nct):
    b, s, d = x.shape
    l = s + ctx.shape[1]
    t = TOKEN_TILE
    tok = lambda w: pl.BlockSpec((1, t, w), lambda bi, ti: (bi, ti, 0))
    shapes, specs = _tail_out_shapes(b, l, d)
    return pl.pallas_call(
        functools.partial(_l0_post_kernel, nct=nct),
        out_shape=shapes,
        grid=(b, l // t),
        in_specs=_seq_specs(t, d, nct) + [tok(ATTN_WIDTH), tok(GMLP_WIDTH), tok(GMLP_WIDTH),
                  _full((GMLP_GROUPS, GMLP_CHUNK, GMLP_CHUNK)), _full((GMLP_CHUNK, GMLP_WIDTH)),
                  _full((ATTN_WIDTH + GMLP_WIDTH, d)),
                  pl.BlockSpec((1, 1, 6, d), lambda bi, ti: (bi, jnp.where(ti >= nct, 1, 0), 0, 0)),
                  _full((1, d)), _full((2, d, LANES)), _full((1, LANES))],
        out_specs=specs,
        scratch_shapes=[pltpu.VMEM((8, LANES), F32)],
        compiler_params=_params("arbitrary", "arbitrary"),
        name="l0_post",
    )(ctx, x, o, gu, gv, wsp, bsp, wout, mod, gf, wr, br)


def _row_copy(src_ref, src_row, dst_ref, dst_row, sem):
    rows = lambda ref, r: ref.at[pl.ds(pl.multiple_of(r * ROW_SUB, ROW_SUB), ROW_SUB)]
    return pltpu.make_async_copy(rows(src_ref, src_row), rows(dst_ref, dst_row), sem)


def _issue_rows(src_ref, index_of, buf_ref, sem_rows, s):
    n_str, rows = buf_ref.shape[1], buf_ref.shape[2] // ROW_SUB
    slot = s % 2

    def issue(j, carry):
        for half in range(2):
            r = 2 * j + half
            for k in range(n_str):
                _row_copy(src_ref, index_of(k * rows + r), buf_ref.at[slot, k], r,
                          sem_rows.at[slot]).start(priority=(k + half) % 2)
        return carry

    lax.fori_loop(0, rows // 2, issue, 0, unroll=ROW_DMA_UNROLL // 2)


def _wait_rows(src_ref, buf_ref, sem_rows, s):
    n_str, rows = buf_ref.shape[1], buf_ref.shape[2] // ROW_SUB
    slot = s % 2
    for k in range(n_str):
        pltpu.make_async_copy(src_ref.at[pl.ds(0, rows * ROW_SUB)], buf_ref.at[slot, k], sem_rows.at[slot]).wait()
    return slot


def _gather_pipeline(src_ref, table_ref, idx_ref, buf_ref, sem_idx, sem_rows, step, n_steps):
    def idx_copy(s):
        return pltpu.make_async_copy(table_ref.at[s], idx_ref.at[s % 2], sem_idx.at[s % 2])

    def issue_rows(s):
        _issue_rows(src_ref, lambda pos: idx_ref[s % 2, pos], buf_ref, sem_rows, s)

    @pl.when(step == 0)
    def _():
        idx_copy(0).start()
        idx_copy(0).wait()
        issue_rows(0)

        @pl.when(n_steps > 1)
        def _():
            idx_copy(1).start()

    @pl.when(step + 1 < n_steps)
    def _():
        idx_copy(step + 1).wait()
        issue_rows(step + 1)

        @pl.when(step + 2 < n_steps)
        def _():
            idx_copy(step + 2).start()

    return _wait_rows(src_ref, buf_ref, sem_rows, step)


def _gather_scratch(n_str, rows):
    return [pltpu.SMEM((2, n_str * rows), I32), pltpu.VMEM((2, n_str, rows * ROW_SUB, LANES), F32),
            pltpu.SemaphoreType.DMA((2,)), pltpu.SemaphoreType.DMA((2,))]


def _dispatch_kernel(pad_ref, nu_ref, f_ref, slots_ref, xb_ref, idx_ref, zero_ref, sem_idx, sem_rows):
    i = pl.program_id(0)
    t = f_ref.shape[0] // ROW_SUB
    blk = MOE_BLOCK * ROW_SUB
    idx_cp = pltpu.make_async_copy(slots_ref.at[i], idx_ref, sem_idx)
    idx_cp.start()

    @pl.when(i == 0)
    def _():
        zero_ref[...] = jnp.zeros_like(zero_ref)
        n_blocks = xb_ref.shape[0] // blk

        def zero_block(start):
            return pltpu.make_async_copy(zero_ref, xb_ref.at[pl.ds(pl.multiple_of(start * ROW_SUB, blk), blk)],
                                         sem_rows)

        def tail_start(j, carry):
            zero_block(j * MOE_BLOCK).start()
            return carry

        def tail_wait(j, carry):
            zero_block(0).wait()
            return carry

        for e in range(N_EXPERTS):
            zero_block(jnp.maximum(pad_ref[N_EXPERTS + e] - MOE_BLOCK, 0)).start()
        lax.fori_loop(nu_ref[0], n_blocks, tail_start, 0)
        for e in range(N_EXPERTS):
            zero_block(0).wait()
        lax.fori_loop(nu_ref[0], n_blocks, tail_wait, 0)

    idx_cp.wait()

    def issue(r, carry):
        _row_copy(f_ref, r, xb_ref, idx_ref[r], sem_rows).start(priority=0)
        _row_copy(f_ref, r, xb_ref, idx_ref[t + r], sem_rows).start(priority=1)
        return carry

    lax.fori_loop(0, t, issue, 0, unroll=ROW_DMA_UNROLL)
    for _ in range(2):
        pltpu.make_async_copy(f_ref, xb_ref.at[pl.ds(0, t * ROW_SUB)], sem_rows).wait()


def _dispatch(f, slots, pad, n_used, n_slots):
    t = TOKEN_TILE
    n_tok = f.shape[0] // ROW_SUB
    return pl.pallas_call(
        _dispatch_kernel,
        out_shape=jax.ShapeDtypeStruct((n_slots * ROW_SUB, LANES), F32),
        grid_spec=pltpu.PrefetchScalarGridSpec(
            num_scalar_prefetch=2,
            grid=(n_tok // t,),
            in_specs=[pl.BlockSpec((t * ROW_SUB, LANES), lambda i, p, nu: (i, 0)),
                      pl.BlockSpec(memory_space=pltpu.VMEM)],
            out_specs=pl.BlockSpec(memory_space=pl.ANY),
            scratch_shapes=[pltpu.SMEM((2 * t,), I32), pltpu.VMEM((MOE_BLOCK * ROW_SUB, LANES), F32),
                            pltpu.SemaphoreType.DMA, pltpu.SemaphoreType.DMA]),
        compiler_params=_params("arbitrary"),
        name="moe_dispatch",
    )(pad, n_used, f, slots)


def _experts_kernel(be_ref, nu_ref, xb_ref, w1_ref, w3_ref, w2_ref, y_ref, w1b_ref, w3b_ref, w2b_ref):
    i = pl.program_id(0)

    @pl.when(jnp.logical_or(i == 0, be_ref[i] != be_ref[jnp.maximum(i - 1, 0)]))
    def _():
        w1b_ref[...] = w1_ref[...].astype(BF16)
        w3b_ref[...] = w3_ref[...].astype(BF16)
        w2b_ref[...] = w2_ref[...].astype(BF16)

    @pl.when(i < nu_ref[0])
    def _():
        x = _load_row_tiles(xb_ref).astype(BF16)
        h1 = jnp.dot(x, w1b_ref[...], preferred_element_type=F32)
        h3 = jnp.dot(x, w3b_ref[...], preferred_element_type=F32)
        a = (h1 * jax.nn.sigmoid(h1) * h3).astype(BF16)
        _store_row_tiles(y_ref, jnp.dot(a, w2b_ref[...], preferred_element_type=F32))

    @pl.when(i >= nu_ref[0])
    def _():
        y_ref[...] = jnp.zeros_like(y_ref)


def _experts(xb, w1, w3, w2, layer, block_e, n_used):
    n_blocks = block_e.shape[0]
    _, _, d, hid = w1.shape
    rb = MOE_BLOCK * ROW_SUB
    wspec = lambda s: pl.BlockSpec((None, None) + s, lambda i, be, nu: (layer, be[i], 0, 0))
    return pl.pallas_call(
        _experts_kernel,
        out_shape=jax.ShapeDtypeStruct((n_blocks * rb, LANES), F32),
        grid_spec=pltpu.PrefetchScalarGridSpec(
            num_scalar_prefetch=2,
            grid=(n_blocks,),
            in_specs=[pl.BlockSpec((rb, LANES), lambda i, be, nu: (jnp.minimum(i, nu[0] - 1), 0)),
                      wspec((d, hid)), wspec((d, hid)), wspec((hid, d))],
            out_specs=pl.BlockSpec((rb, LANES), lambda i, be, nu: (i, 0)),
            scratch_shapes=[pltpu.VMEM((d, hid), BF16), pltpu.VMEM((d, hid), BF16), pltpu.VMEM((hid, d), BF16)]),
        compiler_params=_params("arbitrary"),
        name="moe_experts",
    )(block_e, n_used, xb, w1, w3, w2)


def _moe_plan(route_i, counts, n_tok):
    eid, rank = route_i[0:2], route_i[2:4]
    cnt = counts[0, :N_EXPERTS].astype(I32)
    padded = (cnt + MOE_BLOCK - 1) // MOE_BLOCK * MOE_BLOCK
    pend = jnp.cumsum(padded)
    pstart = pend - padded
    experts = jnp.arange(N_EXPERTS, dtype=I32)
    slots = rank + jnp.sum(jnp.where(eid[:, :, None] == experts, pstart, 0), axis=-1)
    n_assign = 2 * n_tok
    n_slots = (n_assign + MOE_BLOCK - 1) // MOE_BLOCK * MOE_BLOCK + N_EXPERTS * MOE_BLOCK
    n_blocks = n_slots // MOE_BLOCK
    bstart = jnp.arange(n_blocks, dtype=I32) * MOE_BLOCK
    block_e = jnp.minimum(jnp.sum(bstart[:, None] >= pend[None, :], axis=1), N_EXPERTS - 1).astype(I32)
    n_used = pend[-1] // MOE_BLOCK
    last_e = block_e[jnp.maximum(n_used - 1, 0)]
    block_e = jnp.where(jnp.arange(n_blocks) < n_used, block_e, last_e).astype(I32)
    slots = slots.reshape(2, n_tok // TOKEN_TILE, TOKEN_TILE).transpose(1, 0, 2)
    slots = slots.reshape(n_tok // TOKEN_TILE, 2 * TOKEN_TILE).astype(I32)
    pad = jnp.concatenate([pstart + cnt, pend]).astype(I32)
    return slots, block_e, n_used.reshape(1).astype(I32), pad, n_slots


def _moe(f, route_i, counts, w1, w3, w2, layer):
    n_tok = f.shape[0] // ROW_SUB
    slots, block_e, n_used, pad, n_slots = _moe_plan(route_i, counts, n_tok)
    xb = _dispatch(f, slots, pad, n_used, n_slots)
    y = _experts(xb, w1, w3, w2, layer, block_e, n_used)
    return y, slots


def _combined_rows(y_ref, slots_ref, rw_ref, idx_ref, ybuf_ref, sem_idx, sem_rows):
    slot = _gather_pipeline(y_ref, slots_ref, idx_ref, ybuf_ref, sem_idx, sem_rows,
                            pl.program_id(0), pl.num_programs(0))
    rw = rw_ref[...]
    w1, w2 = rw[:, 0:1], rw[:, 1:2]
    t = ybuf_ref.shape[2] // ROW_SUB
    part = lambda k, s: ybuf_ref[slot, k, pl.ds(s, t, stride=ROW_SUB), :]
    return jnp.concatenate([w1 * part(0, s) + w2 * part(1, s) for s in range(ROW_SUB)], axis=1)


def _l0_combine_kernel(x_ref, rw_ref, slots_ref, y_ref, mod0_ref, mod1_ref, g_ref, win_ref,
                       x2_ref, u_ref, idx_ref, ybuf_ref, sem_idx, sem_rows):
    moe = _combined_rows(y_ref, slots_ref, rw_ref, idx_ref, ybuf_ref, sem_idx, sem_rows)
    x2 = x_ref[...] + mod0_ref[0, 0][5:6] * moe
    x2_ref[...] = x2
    m1 = mod1_ref[0, 0]
    h = _modulate(x2, g_ref[...], m1[0:1], m1[1:2]).astype(BF16)
    u_ref[...] = jnp.dot(h, win_ref[...], preferred_element_type=F32)


def _l0_combine(x1, rw, slots, y, mod0, mod1, g, win, b, nct):
    n_tok, d = x1.shape
    t = TOKEN_TILE
    nt = n_tok // b // t
    l = n_tok // b
    modspec = pl.BlockSpec((1, 1, 6, d), lambda i: (i // nt, jnp.where(i % nt >= nct, 1, 0), 0, 0))
    return pl.pallas_call(
        _l0_combine_kernel,
        out_shape=[jax.ShapeDtypeStruct((n_tok, d), F32), jax.ShapeDtypeStruct((l, b * S5_WIDTH), F32)],
        grid=(n_tok // t,),
        in_specs=[pl.BlockSpec((t, d), lambda i: (i, 0)), pl.BlockSpec((t, 8), lambda i: (i, 0)),
                  pl.BlockSpec(memory_space=pltpu.VMEM), pl.BlockSpec(memory_space=pl.ANY),
                  modspec, modspec, _full((1, d)), _full((d, S5_WIDTH))],
        out_specs=[pl.BlockSpec((t, d), lambda i: (i, 0)),
                   pl.BlockSpec((t, S5_WIDTH), lambda i: (i % nt, i // nt))],
        scratch_shapes=_gather_scratch(2, t),
        compiler_params=_params("arbitrary"),
        name="l0_combine",
    )(x1, rw, slots, y, mod0, mod1, g, win)


def _s5_zoh_kernel(are_ref, aim_ref, ldt_ref, bre_ref, bim_ref, lre_ref, lim_ref, bbre_ref, bbim_ref):
    ar, ai = are_ref[...], aim_ref[...]
    dt = jnp.exp(ldt_ref[...])
    mag = jnp.exp(ar * dt)
    lr = mag * jnp.cos(ai * dt)
    li = mag * jnp.sin(ai * dt)
    lre_ref[...] = lr
    lim_ref[...] = li
    nr, ni = lr - 1.0, li
    den = ar * ar + ai * ai
    cr = (nr * ar + ni * ai) / den
    ci = (ni * ar - nr * ai) / den
    br, bi = bre_ref[...], bim_ref[...]
    bbre_ref[...] = cr * br - ci * bi
    bbim_ref[...] = cr * bi + ci * br


def _s5_zoh(a_re, a_im, log_dt, b_re, b_im):
    n_dir, g, n, p = b_re.shape
    rows = n_dir * g * p
    expand = lambda a: jnp.broadcast_to(a[:, :, None, :], (n_dir, g, p, n)).reshape(rows, n)
    ldt = jnp.broadcast_to(log_dt[:, :, None, None], (n_dir, g, p, n)).reshape(rows, n)
    bt = lambda a: jnp.transpose(a, (0, 1, 3, 2)).reshape(rows, n)
    outs = pl.pallas_call(
        _s5_zoh_kernel,
        out_shape=[jax.ShapeDtypeStruct((rows, n), F32)] * 4,
        name="s5_zoh",
    )(expand(a_re), expand(a_im), ldt, bt(b_re), bt(b_im))
    lre, lim, bbre, bbim = (o.reshape(n_dir, g, p, n) for o in outs)
    return lre[:, :, 0, :], lim[:, :, 0, :], bbre, bbim


def _s5_matrices(lre, lim, bbre, bbim, c_re, c_im):
    n_dir = lre.shape[0]
    nj = S5_GROUPS // (2 * S5_TILES)
    shp = (n_dir, nj, S5_TILES, 2, S5_GROUP_DIM, S5_STATE)
    eye_t = jnp.eye(S5_TILES, dtype=F32)
    eye_g = jnp.eye(2, dtype=F32)

    def place(a):
        return jnp.einsum("djtgpn,tu,gh->djtgpuhn", a.reshape(shp), eye_t, eye_g)

    kdim = S5_TILES * 2 * S5_GROUP_DIM
    sdim = S5_TILES * 2 * 2 * S5_STATE
    bm = jnp.stack([place(bbre), place(bbim)], axis=6)
    bm = bm.reshape(n_dir, nj, kdim, sdim).astype(BF16)
    cm = jnp.stack([place(c_re), place(-c_im)], axis=6)
    cm = jnp.transpose(cm, (0, 1, 5, 6, 7, 8, 2, 3, 4)).reshape(n_dir, nj, sdim, kdim).astype(BF16)
    lam = jnp.stack([lre.reshape(n_dir, nj, S5_TILES * LANES), lim.reshape(n_dir, nj, S5_TILES * LANES)], axis=2)
    return bm, cm, lam


def _s5_scan_kernel(uf_ref, ub_ref, bm_ref, cm_ref, lam_ref, yf_ref, yb_ref, v_ref, h_ref, *, tc, nb):
    @pl.when(pl.program_id(0) == 0)
    def _():
        h_ref[...] = jnp.zeros_like(h_ref)

    nj, kdim, sdim = bm_ref.shape[1], bm_ref.shape[2], bm_ref.shape[3]
    u_refs, y_refs = (uf_ref, ub_ref), (yf_ref, yb_ref)
    phases = [(d, jq) for d in range(2) for jq in range(nj)]
    n_sub = 2 * S5_TILES
    col = sdim // S5_TILES
    run = tc // n_sub
    state = {}

    def drive_piece(p, c):
        d, jq = phases[p]
        if c == 0:
            state["uk"] = u_refs[d][:, kdim * jq:kdim * (jq + 1)].astype(BF16)
        v_ref[p % 3, :, col * c:col * (c + 1)] = jnp.dot(state["uk"], bm_ref[d, jq, :, col * c:col * (c + 1)],
                                                         preferred_element_type=F32)

    def readout_piece(p, c):
        d, jq = phases[p]
        part = jnp.dot(v_ref[p % 3, :, col * c:col * (c + 1)].astype(BF16), cm_ref[d, jq, col * c:col * (c + 1), :],
                       preferred_element_type=F32)
        out = y_refs[d].at[:, kdim * jq:kdim * (jq + 1)]
        out[...] = part if c == 0 else out[...] + part

    def scan_run(p, c):
        d, jq = phases[p]
        if c == 0:
            lam = lam_ref[d, jq]
            state["lr"] = [jnp.broadcast_to(lam[0:1, LANES * t:LANES * (t + 1)], (nb, LANES)) for t in range(S5_TILES)]
            state["li"] = [jnp.broadcast_to(lam[1:2, LANES * t:LANES * (t + 1)], (nb, LANES)) for t in range(S5_TILES)]
            h0 = h_ref[d, jq]
            state["h"] = [h0[:, LANES * i:LANES * (i + 1)] for i in range(2 * S5_TILES)]
        lr, li, h = state["lr"], state["li"], state["h"]
        for s in range(run * c, run * (c + 1)):
            r0 = (s if d == 0 else tc - 1 - s) * nb
            rows = v_ref[p % 3, r0:r0 + nb, :]
            new = []
            for t in range(S5_TILES):
                vr = rows[:, 2 * LANES * t:2 * LANES * t + LANES]
                vi = rows[:, 2 * LANES * t + LANES:2 * LANES * (t + 1)]
                hr, hi = h[2 * t], h[2 * t + 1]
                new.append(lr[t] * hr - li[t] * hi + vr)
                new.append(lr[t] * hi + li[t] * hr + vi)
            v_ref[p % 3, r0:r0 + nb, :] = jnp.concatenate(new, axis=1)
            h = new
        state["h"] = h
        if c == n_sub - 1:
            h_ref[d, jq] = jnp.concatenate(h, axis=1)

    for slot in range(len(phases) + 2):
        for c in range(n_sub):
            if 0 <= slot - 1 < len(phases):
                scan_run(slot - 1, c)
            if c < S5_TILES and slot < len(phases):
                drive_piece(slot, c)
            if c >= S5_TILES and 0 <= slot - 2 < len(phases):
                readout_piece(slot - 2, c - S5_TILES)


def _s5_scan(u_tm, bm, cm, lam, nb, n_ctx):
    rows_total, w = u_tm.shape
    tc = S5_CHUNK
    r = tc * nb
    n_chunks = rows_total // r
    ncc = n_ctx // tc
    bwd = lambda i: (jnp.where(i < ncc, ncc - 1 - i, n_chunks - 1 - (i - ncc)), 0)
    nj = bm.shape[1]
    sdim = bm.shape[3]
    kern = functools.partial(_s5_scan_kernel, tc=tc, nb=nb)
    return pl.pallas_call(
        kern,
        out_shape=[jax.ShapeDtypeStruct((rows_total, w), F32)] * 2,
        grid=(n_chunks,),
        in_specs=[pl.BlockSpec((r, w), lambda i: (i, 0)), pl.BlockSpec((r, w), bwd),
                  _full(bm.shape), _full(cm.shape), _full(lam.shape)],
        out_specs=[pl.BlockSpec((r, w), lambda i: (i, 0)), pl.BlockSpec((r, w), bwd)],
        scratch_shapes=[pltpu.VMEM((3, r, sdim), F32), pltpu.VMEM((2, nj, nb, sdim), F32)],
        compiler_params=_params("arbitrary"),
        name="s5_scan",
    )(u_tm, u_tm, bm, cm, lam)


def _l1_post_kernel(x_ref, u_ref, yf_ref, yb_ref, dsk_ref, wglu_ref, bglu_ref, wout_ref, mod_ref, gf_ref,
                    wr_ref, br_ref, x3_ref, f_ref, ri_ref, rw_ref, cnt_ref, base_ref):
    y = dsk_ref[...] * u_ref[...] + yf_ref[...] + yb_ref[...]
    g = jax.nn.gelu(y, approximate=True)
    z = jnp.dot(g.astype(BF16), wglu_ref[...], preferred_element_type=F32) + bglu_ref[...]
    o = jnp.dot((g * jax.nn.sigmoid(z)).astype(BF16), wout_ref[...], preferred_element_type=F32)
    first = jnp.logical_and(pl.program_id(0) == 0, pl.program_id(1) == 0)
    _route_tail(x_ref[0], o, mod_ref[0, 0], gf_ref[...], wr_ref, br_ref, base_ref,
                x3_ref, f_ref, ri_ref, rw_ref, cnt_ref, first)


def _l1_post(x2, u_tm, yf, yb, dsk, wglu, bglu, wout, mod, gf, wr, br, n_ctx):
    b, l, d = x2.shape
    t = TOKEN_TILE
    nct = n_ctx // t
    s = l - n_ctx
    tm = pl.BlockSpec((t, S5_WIDTH), lambda bi, ti: (ti + nct, bi))
    shapes, specs = _tail_out_shapes(b, s, d)
    return pl.pallas_call(
        _l1_post_kernel,
        out_shape=shapes,
        grid=(b, s // t),
        in_specs=[pl.BlockSpec((1, t, d), lambda bi, ti: (bi, ti + nct, 0)), tm, tm, tm,
                  _full((1, S5_WIDTH)), _full((S5_WIDTH, S5_WIDTH)), _full((1, S5_WIDTH)), _full((S5_WIDTH, d)),
                  pl.BlockSpec((1, 1, 6, d), lambda bi, ti: (bi, 1, 0, 0)),
                  _full((1, d)), _full((2, d, LANES)), _full((1, LANES))],
        out_specs=specs,
        scratch_shapes=[pltpu.VMEM((8, LANES), F32)],
        compiler_params=_params("arbitrary", "arbitrary"),
        name="l1_post",
    )(x2, u_tm, yf, yb, dsk, wglu, bglu, wout, mod, gf, wr, br)


def _final_kernel(x_ref, rw_ref, slots_ref, y_ref, mod_ref, g_ref, o_ref, idx_ref, ybuf_ref, sem_idx, sem_rows):
    moe = _combined_rows(y_ref, slots_ref, rw_ref, idx_ref, ybuf_ref, sem_idx, sem_rows)
    x4 = x_ref[...] + mod_ref[0, 0][5:6] * moe
    o_ref[...] = _rms(x4) * g_ref[...]


def _final(x3, rw, slots, y, mod, g, b):
    n_tok, d = x3.shape
    t = TOKEN_TILE
    nt = n_tok // b // t
    return pl.pallas_call(
        _final_kernel,
        out_shape=jax.ShapeDtypeStruct((n_tok, d), F32),
        grid=(n_tok // t,),
        in_specs=[pl.BlockSpec((t, d), lambda i: (i, 0)), pl.BlockSpec((t, 8), lambda i: (i, 0)),
                  pl.BlockSpec(memory_space=pltpu.VMEM), pl.BlockSpec(memory_space=pl.ANY),
                  pl.BlockSpec((1, 1, 6, d), lambda i: (i // nt, 1, 0, 0)), _full((1, d))],
        out_specs=pl.BlockSpec((t, d), lambda i: (i, 0)),
        scratch_shapes=_gather_scratch(2, t),
        compiler_params=_params("arbitrary"),
        name="final_combine",
    )(x3, rw, slots, y, mod, g)


def _rope_tables(n_ctx, n_lat):
    pos = jnp.arange(n_lat, dtype=I32)
    row = (pos // GRID_W).astype(F32)
    col = (pos % GRID_W).astype(F32)
    inv_freq = jnp.power(ROPE_THETA, -jnp.arange(0, AXIS_DIM, 2, dtype=F32) / AXIS_DIM)
    lane = np.arange(LANES)
    dd = lane % HEAD_DIM
    use_col = dd >= AXIS_DIM
    first = (dd % AXIS_DIM) < AXIS_DIM // 2
    freq = inv_freq[(dd % AXIS_DIM) % (AXIS_DIM // 2)]
    ang = jnp.where(use_col[None, :], col[:, None], row[:, None]) * freq[None, :]
    cos, sin = jnp.cos(ang), jnp.sin(ang)
    sa = jnp.where(first[None, :], -sin, 0.0)
    sb = jnp.where(first[None, :], 0.0, sin)
    pad = lambda a, v: jnp.concatenate([jnp.full((n_ctx, LANES), v, F32), a], axis=0)
    return pad(cos, 1.0), pad(sa, 0.0), pad(sb, 0.0)


def _group_ones(width, group):
    idx = np.arange(width) // group
    return jnp.asarray(idx[:, None] == idx[None, :], dtype=BF16)


def _router_weights(w_grp, b_grp, w_rt, b_rt):
    d = w_grp.shape[0]
    used = N_EXPERT_GROUPS + N_EXPERTS
    wr = jnp.concatenate([w_grp, w_rt, jnp.zeros((d, LANES - used), F32)], axis=1)
    br = jnp.concatenate([b_grp, b_rt, jnp.zeros((LANES - used,), F32)]).reshape(1, LANES)
    return jnp.stack(_split_bf16(wr)), br


def kernel(x, c, ctx, c_ctx, w_mod, b_mod, norm_mix_g, norm_ffn_g, mix_w_in, mix_w_out, q_norm_g, k_norm_g, gmlp_norm_g, gmlp_w_spatial, gmlp_b_spatial, s5_w_in, s5_a_re, s5_a_im, s5_log_dt, s5_b_re, s5_b_im, s5_c_re, s5_c_im, s5_d, s5_w_glu, s5_b_glu, s5_w_out, moe_w_group, moe_b_group, moe_w_router, moe_b_router, moe_w1, moe_w3, moe_w2, final_norm_g):
    b, s, d = x.shape
    n_ctx = ctx.shape[1]
    l = n_ctx + s
    assert w_mod.shape[0] == 2, "two layers: attention/gMLP then S5"
    assert n_ctx % TOKEN_TILE == 0 and s % TOKEN_TILE == 0 and b % 8 == 0 and d == D_MODEL
    nct = n_ctx // TOKEN_TILE

    r = (b + 1 + 7) // 8 * 8
    cond = jnp.concatenate([c, c_ctx[None, :], jnp.zeros((r - b - 1, d), F32)], axis=0)
    mods = _adaln(cond, w_mod, b_mod)
    lat = mods[:, :b].reshape(2, b, 1, 6, d)
    cxt = jnp.broadcast_to(mods[:, b].reshape(2, 1, 1, 6, d), (2, b, 1, 6, d))
    mod = jnp.concatenate([cxt, lat], axis=2)

    w_in = jnp.concatenate([mix_w_in[0][:, _Q_COL_PERM], mix_w_in[0][:, ATTN_WIDTH:]], axis=1).astype(BF16)
    w_out = jnp.concatenate([mix_w_out[0][_Q_COL_PERM], mix_w_out[0][ATTN_WIDTH:]], axis=0).astype(BF16)
    cos, sa, sb = _rope_tables(n_ctx, s)
    qg = jnp.tile(q_norm_g[0], N_Q_HEADS).reshape(1, ATTN_WIDTH)
    kg = jnp.tile(k_norm_g[0], N_KV_HEADS).reshape(1, KV_WIDTH)
    q, k, v, gu, gv = _l0_proj(ctx, x, mod[0], norm_mix_g[0].reshape(1, d), w_in, qg, kg,
                               gmlp_norm_g[0].reshape(1, GMLP_WIDTH), cos, sa, sb,
                               _group_ones(ATTN_WIDTH, HEAD_DIM), _group_ones(KV_WIDTH, HEAD_DIM), nct)
    o = _attention(q, k, v, n_ctx)
    bsp = jnp.repeat(gmlp_b_spatial[0].T, GMLP_GROUP_DIM, axis=1)
    wr0, br0 = _router_weights(moe_w_group[0], moe_b_group[0], moe_w_router[0], moe_b_router[0])
    x1, f0, ri0, rw0, cnt0 = _l0_post(ctx, x, o, gu, gv, gmlp_w_spatial[0].astype(BF16), bsp, w_out, mod[0],
                                      norm_ffn_g[0].reshape(1, d), wr0, br0, nct)
    y0, slots0 = _moe(f0, ri0, cnt0, moe_w1, moe_w3, moe_w2, 0)
    x2, u_tm = _l0_combine(x1.reshape(b * l, d), rw0, slots0, y0, mod[0], mod[1], norm_mix_g[1].reshape(1, d),
                           s5_w_in[0].astype(BF16), b, nct)

    lre, lim, bbre, bbim = _s5_zoh(s5_a_re[0], s5_a_im[0], s5_log_dt[0], s5_b_re[0], s5_b_im[0])
    bm, cm, lam = _s5_matrices(lre, lim, bbre, bbim, s5_c_re[0], s5_c_im[0])
    yf, yb = _s5_scan(u_tm.reshape(l * b, S5_WIDTH), bm, cm, lam, b, n_ctx)
    wr1, br1 = _router_weights(moe_w_group[1], moe_b_group[1], moe_w_router[1], moe_b_router[1])
    tmv = lambda a: a.reshape(l, b * S5_WIDTH)
    x3, f1, ri1, rw1, cnt1 = _l1_post(x2.reshape(b, l, d), u_tm, tmv(yf), tmv(yb), s5_d[0].reshape(1, S5_WIDTH),
                                      s5_w_glu[0].astype(BF16), s5_b_glu[0].reshape(1, S5_WIDTH),
                                      s5_w_out[0].astype(BF16), mod[1], norm_ffn_g[1].reshape(1, d), wr1, br1, n_ctx)
    y1, slots1 = _moe(f1, ri1, cnt1, moe_w1, moe_w3, moe_w2, 1)
    out = _final(x3.reshape(b * s, d), rw1, slots1, y1, mod[1], final_norm_g.reshape(1, d), b)
    return out.reshape(b, s, d)
```

```python
import functools

import numpy as np
import jax
import jax.numpy as jnp
from jax import lax
from jax.experimental import pallas as pl
from jax.experimental.pallas import tpu as pltpu

F32 = jnp.float32
BF16 = jnp.bfloat16
I32 = jnp.int32

EPS = 1e-6
GRID_W = 64
HEAD_DIM = 64
N_Q_HEADS = 8
N_KV_HEADS = 2
ATTN_WIDTH = N_Q_HEADS * HEAD_DIM
KV_WIDTH = N_KV_HEADS * HEAD_DIM
AXIS_DIM = HEAD_DIM // 2
ROPE_THETA = 10000.0
GMLP_GROUPS = 8
GMLP_GROUP_DIM = 64
GMLP_WIDTH = GMLP_GROUPS * GMLP_GROUP_DIM
GMLP_CHUNK = 128
MIX_IN_WIDTH = ATTN_WIDTH + 2 * KV_WIDTH + 2 * GMLP_WIDTH
S5_GROUPS = 32
S5_GROUP_DIM = 16
S5_STATE = 64
S5_WIDTH = S5_GROUPS * S5_GROUP_DIM
N_EXPERT_GROUPS = 4
EXPERTS_PER_GROUP = 8
N_EXPERTS = N_EXPERT_GROUPS * EXPERTS_PER_GROUP
EXPERT_HIDDEN = 512
MOE_BLOCK = 256

D_MODEL = 1024
LANES = 128
ROW_SUB = D_MODEL // LANES
TOKEN_TILE = 256
ROW_DMA_UNROLL = 8
ATTN_KEY_BLOCK = 128
V_ONES_ROWS = 16
V_AUG_ROWS = N_KV_HEADS * (HEAD_DIM + V_ONES_ROWS)
S5_CHUNK = 64
S5_TILES = 4
VMEM_LIMIT = 48 * 1024 * 1024
NEG_BIG = -1e30

_Q_HEAD_ORDER = (0, 4, 1, 5, 2, 6, 3, 7)
_Q_COL_PERM = np.concatenate([np.arange(HEAD_DIM) + HEAD_DIM * h for h in _Q_HEAD_ORDER])


def _params(*sem):
    return pltpu.CompilerParams(dimension_semantics=sem, vmem_limit_bytes=VMEM_LIMIT)


def _rms(x):
    return x * lax.rsqrt(jnp.mean(x * x, axis=-1, keepdims=True) + EPS)


def _modulate(x, g, shift, scale):
    return _rms(x) * g * (1.0 + scale) + shift


def _full(shape):
    return pl.BlockSpec(shape, lambda *_: (0,) * len(shape))


def _split_bf16(a):
    hi = a.astype(BF16)
    return hi, (a - hi.astype(F32)).astype(BF16)


def _dot_split(a, w_hi, w_lo):
    a_hi, a_lo = _split_bf16(a)
    dot = functools.partial(jnp.dot, preferred_element_type=F32)
    return dot(a_hi, w_hi) + (dot(a_lo, w_hi) + dot(a_hi, w_lo))


def _store_row_tiles(ref, val):
    n = val.shape[0]
    for s in range(ROW_SUB):
        ref[pl.ds(s, n, stride=ROW_SUB), :] = val[:, LANES * s:LANES * (s + 1)]


def _load_row_tiles(ref):
    n = ref.shape[0] // ROW_SUB
    return jnp.concatenate([ref[pl.ds(s, n, stride=ROW_SUB), :] for s in range(ROW_SUB)], axis=1)


def _adaln_kernel(cond_ref, w_ref, b_ref, o_ref):
    c = cond_ref[...]
    s = c * jax.nn.sigmoid(c)
    o_ref[0] = _dot_split(s, *_split_bf16(w_ref[0])) + b_ref[0]


def _adaln(cond, w_mod, b_mod):
    depth, d, n = w_mod.shape
    r = cond.shape[0]
    tn = 1536
    return pl.pallas_call(
        _adaln_kernel,
        out_shape=jax.ShapeDtypeStruct((depth, r, n), F32),
        grid=(depth, n // tn),
        in_specs=[pl.BlockSpec((r, d), lambda l, j: (0, 0)),
                  pl.BlockSpec((1, d, tn), lambda l, j: (l, 0, j)),
                  pl.BlockSpec((1, 1, tn), lambda l, j: (l, 0, j))],
        out_specs=pl.BlockSpec((1, r, tn), lambda l, j: (l, 0, j)),
        compiler_params=_params("parallel", "parallel"),
        name="adaln",
    )(cond, w_mod, b_mod.reshape(depth, 1, n))


def _seq_tile(ctx_ref, x_ref, nct):
    return jnp.where(pl.program_id(1) < nct, ctx_ref[0], x_ref[0])


def _seq_specs(t, d, nct):
    return [pl.BlockSpec((1, t, d), lambda bi, ti: (bi, jnp.minimum(ti, nct - 1), 0)),
            pl.BlockSpec((1, t, d), lambda bi, ti: (bi, jnp.maximum(ti - nct, 0), 0))]


def _l0_proj_kernel(ctx_ref, x_ref, mod_ref, g_ref, win_ref, qg_ref, kg_ref, lng_ref, cos_ref, sa_ref, sb_ref,
                    gq_ref, gk_ref, q_ref, k_ref, v_ref, gu_ref, gv_ref, *, nct):
    m = mod_ref[0, 0]
    h = _modulate(_seq_tile(ctx_ref, x_ref, nct), g_ref[...], m[0:1], m[1:2]).astype(BF16)
    z = jnp.dot(h, win_ref[...], preferred_element_type=F32)
    cos, sa, sb = cos_ref[...], sa_ref[...], sb_ref[...]

    def head_norm_rope(t, gsum_ref, gain, reps):
        w = t.shape[1]
        ss = jnp.dot((t * t).astype(BF16), gsum_ref[...], preferred_element_type=F32)
        tn = t * lax.rsqrt(ss * (1.0 / HEAD_DIM) + EPS) * gain
        c, a, b = (jnp.concatenate([tab] * reps, axis=1) if reps > 1 else tab for tab in (cos, sa, sb))
        return tn * c + pltpu.roll(tn, w - AXIS_DIM // 2, 1) * a + pltpu.roll(tn, AXIS_DIM // 2, 1) * b

    q = head_norm_rope(z[:, :ATTN_WIDTH], gq_ref, qg_ref[...], ATTN_WIDTH // LANES)
    q_ref[0] = (q * (HEAD_DIM ** -0.5)).T.astype(BF16)
    k = head_norm_rope(z[:, ATTN_WIDTH:ATTN_WIDTH + KV_WIDTH], gk_ref, kg_ref[...], 1)
    k_ref[0] = k.astype(BF16)
    vt = z[:, ATTN_WIDTH + KV_WIDTH:ATTN_WIDTH + 2 * KV_WIDTH].T
    ones = jnp.ones((V_ONES_ROWS, vt.shape[1]), F32)
    v_ref[0] = jnp.concatenate([part for j in range(N_KV_HEADS)
                                for part in (vt[HEAD_DIM * j:HEAD_DIM * (j + 1)], ones)], axis=0).astype(BF16)
    off = ATTN_WIDTH + 2 * KV_WIDTH
    gu_ref[0] = jax.nn.gelu(z[:, off:off + GMLP_WIDTH], approximate=True).astype(BF16)
    gv = jax.nn.gelu(z[:, off + GMLP_WIDTH:], approximate=True)
    mu = jnp.mean(gv, axis=-1, keepdims=True)
    gc = gv - mu
    gn = gc * lax.rsqrt(jnp.mean(gc * gc, axis=-1, keepdims=True) + EPS) * lng_ref[...]
    gv_ref[0] = gn.astype(BF16)


def _l0_proj(ctx, x, mod, g, w_in, qg, kg, lng, cos, sa, sb, gq, gk, nct):
    b, s, d = x.shape
    l = s + ctx.shape[1]
    t = TOKEN_TILE
    tok = lambda w: pl.BlockSpec((1, t, w), lambda bi, ti: (bi, ti, 0))
    tab = pl.BlockSpec((t, LANES), lambda bi, ti: (ti, 0))
    tok_t = lambda w: pl.BlockSpec((1, w, t), lambda bi, ti: (bi, 0, ti))
    outs = ([jax.ShapeDtypeStruct((b, ATTN_WIDTH, l), BF16), jax.ShapeDtypeStruct((b, l, KV_WIDTH), BF16),
             jax.ShapeDtypeStruct((b, V_AUG_ROWS, l), BF16)]
            + [jax.ShapeDtypeStruct((b, l, GMLP_WIDTH), BF16)] * 2)
    return pl.pallas_call(
        functools.partial(_l0_proj_kernel, nct=nct),
        out_shape=outs,
        grid=(b, l // t),
        in_specs=_seq_specs(t, d, nct) + [
                  pl.BlockSpec((1, 1, 6, d), lambda bi, ti: (bi, jnp.where(ti >= nct, 1, 0), 0, 0)),
                  _full((1, d)), _full((d, MIX_IN_WIDTH)), _full((1, ATTN_WIDTH)), _full((1, KV_WIDTH)),
                  _full((1, GMLP_WIDTH)), tab, tab, tab,
                  _full((ATTN_WIDTH, ATTN_WIDTH)), _full((KV_WIDTH, KV_WIDTH))],
        out_specs=[tok_t(ATTN_WIDTH), tok(KV_WIDTH), tok_t(V_AUG_ROWS), tok(GMLP_WIDTH), tok(GMLP_WIDTH)],
        compiler_params=_params("parallel", "parallel"),
        name="l0_proj",
    )(ctx, x, mod, g, w_in, qg, kg, lng, cos, sa, sb, gq, gk)


def _attn_kernel(qt_ref, k_ref, vt_ref, o_ref, s_ref, p_ref, *, n_ctx, nct):
    ti = pl.program_id(1)
    tq = qt_ref.shape[2]
    row = lax.broadcasted_iota(I32, (LANES, tq), 0)
    low = row < HEAD_DIM
    kb = ATTN_KEY_BLOCK
    va = HEAD_DIM + V_ONES_ROWS

    def attend(nk):
        heads = [(m, j) for m in range(ATTN_WIDTH // LANES) for j in range(2)]
        blocks = [slice(kb * b, kb * (b + 1)) for b in range(nk // kb)]
        state = {}

        def start(h):
            m, j = heads[h]
            qp = qt_ref[0, LANES * m:LANES * (m + 1), :]
            state[h] = dict(qm=jnp.where(low if j == 0 else jnp.logical_not(low), qp, jnp.zeros_like(qp)), mx=None)

        def score_block(h, blk):
            s = jnp.dot(k_ref[0, blk, :], state[h]["qm"], preferred_element_type=F32)
            s_ref[h % 2, blk, :] = s
            bm = jnp.max(s, axis=0, keepdims=True)
            state[h]["mx"] = bm if state[h]["mx"] is None else jnp.maximum(state[h]["mx"], bm)

        def exp_block(h, blk):
            p_ref[h % 2, blk, :] = jnp.exp(s_ref[h % 2, blk, :] - state[h]["mx"]).astype(BF16)

        def finish(h):
            j = heads[h][1]
            pv = jnp.dot(vt_ref[0, va * j:va * (j + 1), :nk], p_ref[h % 2, :nk, :],
                         preferred_element_type=F32)
            return pv[:HEAD_DIM] / pv[HEAD_DIM:HEAD_DIM + 1]

        start(0)
        for blk in blocks:
            score_block(0, blk)
        halves = []
        for h in range(len(heads)):
            if h + 1 < len(heads):
                start(h + 1)
            for blk in blocks:
                if h + 1 < len(heads):
                    score_block(h + 1, blk)
                exp_block(h, blk)
            halves.append(finish(h))
            if heads[h][1] == 1:
                m = heads[h][0]
                o_ref[0, :, LANES * m:LANES * (m + 1)] = jnp.concatenate(halves, axis=0).T.astype(BF16)
                halves = []

    @pl.when(ti < nct)
    def _():
        attend(n_ctx)

    @pl.when(ti >= nct)
    def _():
        attend(k_ref.shape[1])


def _attention(qt, k, vt, n_ctx):
    b, l, _ = k.shape
    t = TOKEN_TILE
    kern = functools.partial(_attn_kernel, n_ctx=n_ctx, nct=n_ctx // t)
    return pl.pallas_call(
        kern,
        out_shape=jax.ShapeDtypeStruct((b, l, ATTN_WIDTH), BF16),
        grid=(b, l // t),
        in_specs=[pl.BlockSpec((1, ATTN_WIDTH, t), lambda bi, ti: (bi, 0, ti)),
                  pl.BlockSpec((1, l, KV_WIDTH), lambda bi, ti: (bi, 0, 0)),
                  pl.BlockSpec((1, V_AUG_ROWS, l), lambda bi, ti: (bi, 0, 0))],
        out_specs=pl.BlockSpec((1, t, ATTN_WIDTH), lambda bi, ti: (bi, ti, 0)),
        scratch_shapes=[pltpu.VMEM((2, l, t), F32), pltpu.VMEM((2, l, t), BF16)],
        compiler_params=_params("parallel", "parallel"),
        name="attention",
    )(qt, k, vt)


def _route_tail(x, y, m, gf, wr_ref, br_ref, base_ref, x_ref, f_ref, ri_ref, rw_ref, cnt_ref, first):
    x1 = x + m[2:3] * y
    x_ref[0] = x1
    f = _modulate(x1, gf, m[3:4], m[4:5])
    _store_row_tiles(f_ref, f)
    t = f.shape[0]
    logits = _dot_split(f, wr_ref[0], wr_ref[1]) + br_ref[...]
    lane = lax.broadcasted_iota(I32, (t, LANES), 1).astype(F32)
    is_grp = lane < N_EXPERT_GROUPS
    gl = jnp.where(is_grp, logits, NEG_BIG)
    gmax = jnp.max(gl, axis=-1, keepdims=True)
    g_idx = jnp.min(jnp.where(gl == gmax, lane, float(LANES)), axis=-1, keepdims=True)
    g_w = 1.0 / jnp.sum(jnp.where(is_grp, jnp.exp(gl - gmax), 0.0), axis=-1, keepdims=True)
    e_lo = N_EXPERT_GROUPS + EXPERTS_PER_GROUP * g_idx
    in_grp = jnp.logical_and(lane >= e_lo, lane < e_lo + EXPERTS_PER_GROUP)
    sel = jnp.where(in_grp, logits, NEG_BIG)
    v1 = jnp.max(sel, axis=-1, keepdims=True)
    i1 = jnp.min(jnp.where(sel == v1, lane, float(LANES)), axis=-1, keepdims=True)
    sel2 = jnp.where(lane == i1, NEG_BIG, sel)
    v2 = jnp.max(sel2, axis=-1, keepdims=True)
    i2 = jnp.min(jnp.where(sel2 == v2, lane, float(LANES)), axis=-1, keepdims=True)
    e = jnp.exp(v2 - v1)
    w1 = g_w / (1.0 + e)
    w2 = g_w * e / (1.0 + e)
    eid1 = i1 - N_EXPERT_GROUPS
    eid2 = i2 - N_EXPERT_GROUPS

    @pl.when(first)
    def _():
        base_ref[...] = jnp.zeros_like(base_ref)

    hit1 = lane == eid1
    hit2 = lane == eid2
    cnt = (hit1.astype(F32) + hit2.astype(F32))
    row = lax.broadcasted_iota(I32, (t, t), 0)
    col = lax.broadcasted_iota(I32, (t, t), 1)
    tril = jnp.where(col < row, 1.0, 0.0).astype(BF16)
    before = jnp.dot(tril, cnt.astype(BF16), preferred_element_type=F32) + base_ref[0:1, :]
    rank1 = jnp.sum(jnp.where(hit1, before, 0.0), axis=-1, keepdims=True)
    rank2 = jnp.sum(jnp.where(hit2, before, 0.0), axis=-1, keepdims=True)
    total = base_ref[0:1, :] + jnp.sum(cnt, axis=0, keepdims=True)
    base_ref[...] = jnp.broadcast_to(total, base_ref.shape)
    cnt_ref[...] = jnp.broadcast_to(total, cnt_ref.shape)

    ri = jnp.where(lane == 0, eid1, jnp.where(lane == 1, eid2, jnp.where(lane == 2, rank1,
                   jnp.where(lane == 3, rank2, 0.0))))
    ri_ref[...] = ri.T[0:8, :].astype(I32)
    l8 = lax.broadcasted_iota(I32, (t, 8), 1)
    rw_ref[...] = jnp.where(l8 == 0, w1, jnp.where(l8 == 1, w2, 0.0))


def _tail_out_shapes(b, n_rows, d):
    t = TOKEN_TILE
    shapes = [jax.ShapeDtypeStruct((b, n_rows, d), F32), jax.ShapeDtypeStruct((b * n_rows * ROW_SUB, LANES), F32),
              jax.ShapeDtypeStruct((8, b * n_rows), I32), jax.ShapeDtypeStruct((b * n_rows, 8), F32),
              jax.ShapeDtypeStruct((8, LANES), F32)]
    nt = n_rows // t
    flat = lambda w: pl.BlockSpec((t, w), lambda bi, ti: (bi * nt + ti, 0))
    specs = [pl.BlockSpec((1, t, d), lambda bi, ti: (bi, ti, 0)),
             pl.BlockSpec((t * ROW_SUB, LANES), lambda bi, ti: (bi * nt + ti, 0)),
             pl.BlockSpec((8, t), lambda bi, ti: (0, bi * nt + ti)), flat(8),
             pl.BlockSpec((8, LANES), lambda bi, ti: (0, 0))]
    return shapes, specs


def _l0_post_kernel(ctx_ref, x_ref, o_ref, gu_ref, gv_ref, wsp_ref, bsp_ref, wout_ref, mod_ref, gf_ref, wr_ref,
                    br_ref, x1_ref, f_ref, ri_ref, rw_ref, cnt_ref, base_ref, *, nct):
    tq = x_ref.shape[1]
    lane = lax.broadcasted_iota(I32, (GMLP_CHUNK, LANES), 1)
    low = lane < GMLP_GROUP_DIM
    gated = []
    for c in range(tq // GMLP_CHUNK):
        rows = slice(c * GMLP_CHUNK, (c + 1) * GMLP_CHUNK)
        mixed = []
        for m in range(GMLP_WIDTH // LANES):
            pair = gv_ref[0, rows, LANES * m:LANES * (m + 1)]
            a = jnp.dot(wsp_ref[2 * m], pair, preferred_element_type=F32)
            b = jnp.dot(wsp_ref[2 * m + 1], pair, preferred_element_type=F32)
            mixed.append(jnp.where(low, a, b))
        mixed = jnp.concatenate(mixed, axis=1) + bsp_ref[...]
        gated.append((gu_ref[0, rows, :].astype(F32) * mixed).astype(BF16))
    gated = jnp.concatenate(gated, axis=0)
    y = (jnp.dot(o_ref[0], wout_ref[:ATTN_WIDTH, :], preferred_element_type=F32)
         + jnp.dot(gated, wout_ref[ATTN_WIDTH:, :], preferred_element_type=F32))
    first = jnp.logical_and(pl.program_id(0) == 0, pl.program_id(1) == 0)
    _route_tail(_seq_tile(ctx_ref, x_ref, nct), y, mod_ref[0, 0], gf_ref[...], wr_ref, br_ref, base_ref,
                x1_ref, f_ref, ri_ref, rw_ref, cnt_ref, first)


def _l0_post(ctx, x, o, gu, gv, wsp, bsp, wout, mod, gf, wr, br, nct):
    b, s, d = x.shape
    l = s + ctx.shape[1]
    t = TOKEN_TILE
    tok = lambda w: pl.BlockSpec((1, t, w), lambda bi, ti: (bi, ti, 0))
    shapes, specs = _tail_out_shapes(b, l, d)
    return pl.pallas_call(
        functools.partial(_l0_post_kernel, nct=nct),
        out_shape=shapes,
        grid=(b, l // t),
        in_specs=_seq_specs(t, d, nct) + [tok(ATTN_WIDTH), tok(GMLP_WIDTH), tok(GMLP_WIDTH),
                  _full((GMLP_GROUPS, GMLP_CHUNK, GMLP_CHUNK)), _full((GMLP_CHUNK, GMLP_WIDTH)),
                  _full((ATTN_WIDTH + GMLP_WIDTH, d)),
                  pl.BlockSpec((1, 1, 6, d), lambda bi, ti: (bi, jnp.where(ti >= nct, 1, 0), 0, 0)),
                  _full((1, d)), _full((2, d, LANES)), _full((1, LANES))],
        out_specs=specs,
        scratch_shapes=[pltpu.VMEM((8, LANES), F32)],
        compiler_params=_params("arbitrary", "arbitrary"),
        name="l0_post",
    )(ctx, x, o, gu, gv, wsp, bsp, wout, mod, gf, wr, br)


def _row_copy(src_ref, src_row, dst_ref, dst_row, sem):
    rows = lambda ref, r: ref.at[pl.ds(pl.multiple_of(r * ROW_SUB, ROW_SUB), ROW_SUB)]
    return pltpu.make_async_copy(rows(src_ref, src_row), rows(dst_ref, dst_row), sem)


def _issue_rows(src_ref, index_of, buf_ref, sem_rows, s):
    n_str, rows = buf_ref.shape[1], buf_ref.shape[2] // ROW_SUB
    slot = s % 2

    def issue(j, carry):
        for half in range(2):
            r = 2 * j + half
            for k in range(n_str):
                _row_copy(src_ref, index_of(k * rows + r), buf_ref.at[slot, k], r,
                          sem_rows.at[slot]).start(priority=(k + half) % 2)
        return carry

    lax.fori_loop(0, rows // 2, issue, 0, unroll=ROW_DMA_UNROLL // 2)


def _wait_rows(src_ref, buf_ref, sem_rows, s):
    n_str, rows = buf_ref.shape[1], buf_ref.shape[2] // ROW_SUB
    slot = s % 2
    for k in range(n_str):
        pltpu.make_async_copy(src_ref.at[pl.ds(0, rows * ROW_SUB)], buf_ref.at[slot, k], sem_rows.at[slot]).wait()
    return slot


def _gather_pipeline(src_ref, table_ref, idx_ref, buf_ref, sem_idx, sem_rows, step, n_steps):
    def idx_copy(s):
        return pltpu.make_async_copy(table_ref.at[s], idx_ref.at[s % 2], sem_idx.at[s % 2])

    def issue_rows(s):
        _issue_rows(src_ref, lambda pos: idx_ref[s % 2, pos], buf_ref, sem_rows, s)

    @pl.when(step == 0)
    def _():
        idx_copy(0).start()
        idx_copy(0).wait()
        issue_rows(0)

        @pl.when(n_steps > 1)
        def _():
            idx_copy(1).start()

    @pl.when(step + 1 < n_steps)
    def _():
        idx_copy(step + 1).wait()
        issue_rows(step + 1)

        @pl.when(step + 2 < n_steps)
        def _():
            idx_copy(step + 2).start()

    return _wait_rows(src_ref, buf_ref, sem_rows, step)


def _gather_scratch(n_str, rows):
    return [pltpu.SMEM((2, n_str * rows), I32), pltpu.VMEM((2, n_str, rows * ROW_SUB, LANES), F32),
            pltpu.SemaphoreType.DMA((2,)), pltpu.SemaphoreType.DMA((2,))]


def _dispatch_kernel(pad_ref, nu_ref, f_ref, slots_ref, xb_ref, idx_ref, zero_ref, sem_idx, sem_rows):
    i = pl.program_id(0)
    t = f_ref.shape[0] // ROW_SUB
    blk = MOE_BLOCK * ROW_SUB
    idx_cp = pltpu.make_async_copy(slots_ref.at[i], idx_ref, sem_idx)
    idx_cp.start()

    @pl.when(i == 0)
    def _():
        zero_ref[...] = jnp.zeros_like(zero_ref)
        n_blocks = xb_ref.shape[0] // blk

        def zero_block(start):
            return pltpu.make_async_copy(zero_ref, xb_ref.at[pl.ds(pl.multiple_of(start * ROW_SUB, blk), blk)],
                                         sem_rows)

        def tail_start(j, carry):
            zero_block(j * MOE_BLOCK).start()
            return carry

        def tail_wait(j, carry):
            zero_block(0).wait()
            return carry

        for e in range(N_EXPERTS):
            zero_block(jnp.maximum(pad_ref[N_EXPERTS + e] - MOE_BLOCK, 0)).start()
        lax.fori_loop(nu_ref[0], n_blocks, tail_start, 0)
        for e in range(N_EXPERTS):
            zero_block(0).wait()
        lax.fori_loop(nu_ref[0], n_blocks, tail_wait, 0)

    idx_cp.wait()

    def issue(r, carry):
        _row_copy(f_ref, r, xb_ref, idx_ref[r], sem_rows).start(priority=0)
        _row_copy(f_ref, r, xb_ref, idx_ref[t + r], sem_rows).start(priority=1)
        return carry

    lax.fori_loop(0, t, issue, 0, unroll=ROW_DMA_UNROLL)
    for _ in range(2):
        pltpu.make_async_copy(f_ref, xb_ref.at[pl.ds(0, t * ROW_SUB)], sem_rows).wait()


def _dispatch(f, slots, pad, n_used, n_slots):
    t = TOKEN_TILE
    n_tok = f.shape[0] // ROW_SUB
    return pl.pallas_call(
        _dispatch_kernel,
        out_shape=jax.ShapeDtypeStruct((n_slots * ROW_SUB, LANES), F32),
        grid_spec=pltpu.PrefetchScalarGridSpec(
            num_scalar_prefetch=2,
            grid=(n_tok // t,),
            in_specs=[pl.BlockSpec((t * ROW_SUB, LANES), lambda i, p, nu: (i, 0)),
                      pl.BlockSpec(memory_space=pltpu.VMEM)],
            out_specs=pl.BlockSpec(memory_space=pl.ANY),
            scratch_shapes=[pltpu.SMEM((2 * t,), I32), pltpu.VMEM((MOE_BLOCK * ROW_SUB, LANES), F32),
                            pltpu.SemaphoreType.DMA, pltpu.SemaphoreType.DMA]),
        compiler_params=_params("arbitrary"),
        name="moe_dispatch",
    )(pad, n_used, f, slots)


def _experts_kernel(be_ref, nu_ref, xb_ref, w1_ref, w3_ref, w2_ref, y_ref, w1b_ref, w3b_ref, w2b_ref):
    i = pl.program_id(0)

    @pl.when(jnp.logical_or(i == 0, be_ref[i] != be_ref[jnp.maximum(i - 1, 0)]))
    def _():
        w1b_ref[...] = w1_ref[...].astype(BF16)
        w3b_ref[...] = w3_ref[...].astype(BF16)
        w2b_ref[...] = w2_ref[...].astype(BF16)

    @pl.when(i < nu_ref[0])
    def _():
        x = _load_row_tiles(xb_ref).astype(BF16)
        h1 = jnp.dot(x, w1b_ref[...], preferred_element_type=F32)
        h3 = jnp.dot(x, w3b_ref[...], preferred_element_type=F32)
        a = (h1 * jax.nn.sigmoid(h1) * h3).astype(BF16)
        _store_row_tiles(y_ref, jnp.dot(a, w2b_ref[...], preferred_element_type=F32))

    @pl.when(i >= nu_ref[0])
    def _():
        y_ref[...] = jnp.zeros_like(y_ref)


def _experts(xb, w1, w3, w2, layer, block_e, n_used):
    n_blocks = block_e.shape[0]
    _, _, d, hid = w1.shape
    rb = MOE_BLOCK * ROW_SUB
    wspec = lambda s: pl.BlockSpec((None, None) + s, lambda i, be, nu: (layer, be[i], 0, 0))
    return pl.pallas_call(
        _experts_kernel,
        out_shape=jax.ShapeDtypeStruct((n_blocks * rb, LANES), F32),
        grid_spec=pltpu.PrefetchScalarGridSpec(
            num_scalar_prefetch=2,
            grid=(n_blocks,),
            in_specs=[pl.BlockSpec((rb, LANES), lambda i, be, nu: (jnp.minimum(i, nu[0] - 1), 0)),
                      wspec((d, hid)), wspec((d, hid)), wspec((hid, d))],
            out_specs=pl.BlockSpec((rb, LANES), lambda i, be, nu: (i, 0)),
            scratch_shapes=[pltpu.VMEM((d, hid), BF16), pltpu.VMEM((d, hid), BF16), pltpu.VMEM((hid, d), BF16)]),
        compiler_params=_params("arbitrary"),
        name="moe_experts",
    )(block_e, n_used, xb, w1, w3, w2)


def _moe_plan(route_i, counts, n_tok):
    eid, rank = route_i[0:2], route_i[2:4]
    cnt = counts[0, :N_EXPERTS].astype(I32)
    padded = (cnt + MOE_BLOCK - 1) // MOE_BLOCK * MOE_BLOCK
    pend = jnp.cumsum(padded)
    pstart = pend - padded
    experts = jnp.arange(N_EXPERTS, dtype=I32)
    slots = rank + jnp.sum(jnp.where(eid[:, :, None] == experts, pstart, 0), axis=-1)
    n_assign = 2 * n_tok
    n_slots = (n_assign + MOE_BLOCK - 1) // MOE_BLOCK * MOE_BLOCK + N_EXPERTS * MOE_BLOCK
    n_blocks = n_slots // MOE_BLOCK
    bstart = jnp.arange(n_blocks, dtype=I32) * MOE_BLOCK
    block_e = jnp.minimum(jnp.sum(bstart[:, None] >= pend[None, :], axis=1), N_EXPERTS - 1).astype(I32)
    n_used = pend[-1] // MOE_BLOCK
    last_e = block_e[jnp.maximum(n_used - 1, 0)]
    block_e = jnp.where(jnp.arange(n_blocks) < n_used, block_e, last_e).astype(I32)
    slots = slots.reshape(2, n_tok // TOKEN_TILE, TOKEN_TILE).transpose(1, 0, 2)
    slots = slots.reshape(n_tok // TOKEN_TILE, 2 * TOKEN_TILE).astype(I32)
    pad = jnp.concatenate([pstart + cnt, pend]).astype(I32)
    return slots, block_e, n_used.reshape(1).astype(I32), pad, n_slots


def _moe(f, route_i, counts, w1, w3, w2, layer):
    n_tok = f.shape[0] // ROW_SUB
    slots, block_e, n_used, pad, n_slots = _moe_plan(route_i, counts, n_tok)
    xb = _dispatch(f, slots, pad, n_used, n_slots)
    y = _experts(xb, w1, w3, w2, layer, block_e, n_used)
    return y, slots


def _combined_rows(y_ref, slots_ref, rw_ref, idx_ref, ybuf_ref, sem_idx, sem_rows):
    slot = _gather_pipeline(y_ref, slots_ref, idx_ref, ybuf_ref, sem_idx, sem_rows,
                            pl.program_id(0), pl.num_programs(0))
    rw = rw_ref[...]
    w1, w2 = rw[:, 0:1], rw[:, 1:2]
    t = ybuf_ref.shape[2] // ROW_SUB
    part = lambda k, s: ybuf_ref[slot, k, pl.ds(s, t, stride=ROW_SUB), :]
    return jnp.concatenate([w1 * part(0, s) + w2 * part(1, s) for s in range(ROW_SUB)], axis=1)


def _l0_combine_kernel(x_ref, rw_ref, slots_ref, y_ref, mod0_ref, mod1_ref, g_ref, win_ref,
                       x2_ref, u_ref, idx_ref, ybuf_ref, sem_idx, sem_rows):
    moe = _combined_rows(y_ref, slots_ref, rw_ref, idx_ref, ybuf_ref, sem_idx, sem_rows)
    x2 = x_ref[...] + mod0_ref[0, 0][5:6] * moe
    x2_ref[...] = x2
    m1 = mod1_ref[0, 0]
    h = _modulate(x2, g_ref[...], m1[0:1], m1[1:2]).astype(BF16)
    u_ref[...] = jnp.dot(h, win_ref[...], preferred_element_type=F32)


def _l0_combine(x1, rw, slots, y, mod0, mod1, g, win, b, nct):
    n_tok, d = x1.shape
    t = TOKEN_TILE
    nt = n_tok // b // t
    l = n_tok // b
    modspec = pl.BlockSpec((1, 1, 6, d), lambda i: (i // nt, jnp.where(i % nt >= nct, 1, 0), 0, 0))
    return pl.pallas_call(
        _l0_combine_kernel,
        out_shape=[jax.ShapeDtypeStruct((n_tok, d), F32), jax.ShapeDtypeStruct((l, b * S5_WIDTH), F32)],
        grid=(n_tok // t,),
        in_specs=[pl.BlockSpec((t, d), lambda i: (i, 0)), pl.BlockSpec((t, 8), lambda i: (i, 0)),
                  pl.BlockSpec(memory_space=pltpu.VMEM), pl.BlockSpec(memory_space=pl.ANY),
                  modspec, modspec, _full((1, d)), _full((d, S5_WIDTH))],
        out_specs=[pl.BlockSpec((t, d), lambda i: (i, 0)),
                   pl.BlockSpec((t, S5_WIDTH), lambda i: (i % nt, i // nt))],
        scratch_shapes=_gather_scratch(2, t),
        compiler_params=_params("arbitrary"),
        name="l0_combine",
    )(x1, rw, slots, y, mod0, mod1, g, win)


def _s5_zoh_kernel(are_ref, aim_ref, ldt_ref, bre_ref, bim_ref, lre_ref, lim_ref, bbre_ref, bbim_ref):
    ar, ai = are_ref[...], aim_ref[...]
    dt = jnp.exp(ldt_ref[...])
    mag = jnp.exp(ar * dt)
    lr = mag * jnp.cos(ai * dt)
    li = mag * jnp.sin(ai * dt)
    lre_ref[...] = lr
    lim_ref[...] = li
    nr, ni = lr - 1.0, li
    den = ar * ar + ai * ai
    cr = (nr * ar + ni * ai) / den
    ci = (ni * ar - nr * ai) / den
    br, bi = bre_ref[...], bim_ref[...]
    bbre_ref[...] = cr * br - ci * bi
    bbim_ref[...] = cr * bi + ci * br


def _s5_zoh(a_re, a_im, log_dt, b_re, b_im):
    n_dir, g, n, p = b_re.shape
    rows = n_dir * g * p
    expand = lambda a: jnp.broadcast_to(a[:, :, None, :], (n_dir, g, p, n)).reshape(rows, n)
    ldt = jnp.broadcast_to(log_dt[:, :, None, None], (n_dir, g, p, n)).reshape(rows, n)
    bt = lambda a: jnp.transpose(a, (0, 1, 3, 2)).reshape(rows, n)
    outs = pl.pallas_call(
        _s5_zoh_kernel,
        out_shape=[jax.ShapeDtypeStruct((rows, n), F32)] * 4,
        name="s5_zoh",
    )(expand(a_re), expand(a_im), ldt, bt(b_re), bt(b_im))
    lre, lim, bbre, bbim = (o.reshape(n_dir, g, p, n) for o in outs)
    return lre[:, :, 0, :], lim[:, :, 0, :], bbre, bbim


def _s5_matrices(lre, lim, bbre, bbim, c_re, c_im):
    n_dir = lre.shape[0]
    nj = S5_GROUPS // (2 * S5_TILES)
    shp = (n_dir, nj, S5_TILES, 2, S5_GROUP_DIM, S5_STATE)
    eye_t = jnp.eye(S5_TILES, dtype=F32)
    eye_g = jnp.eye(2, dtype=F32)

    def place(a):
        return jnp.einsum("djtgpn,tu,gh->djtgpuhn", a.reshape(shp), eye_t, eye_g)

    kdim = S5_TILES * 2 * S5_GROUP_DIM
    sdim = S5_TILES * 2 * 2 * S5_STATE
    bm = jnp.stack([place(bbre), place(bbim)], axis=6)
    bm = bm.reshape(n_dir, nj, kdim, sdim).astype(BF16)
    cm = jnp.stack([place(c_re), place(-c_im)], axis=6)
    cm = jnp.transpose(cm, (0, 1, 5, 6, 7, 8, 2, 3, 4)).reshape(n_dir, nj, sdim, kdim).astype(BF16)
    lam = jnp.stack([lre.reshape(n_dir, nj, S5_TILES * LANES), lim.reshape(n_dir, nj, S5_TILES * LANES)], axis=2)
    return bm, cm, lam


def _s5_scan_kernel(uf_ref, ub_ref, bm_ref, cm_ref, lam_ref, yf_ref, yb_ref, v_ref, h_ref, *, tc, nb):
    @pl.when(pl.program_id(0) == 0)
    def _():
        h_ref[...] = jnp.zeros_like(h_ref)

    nj, kdim, sdim = bm_ref.shape[1], bm_ref.shape[2], bm_ref.shape[3]
    u_refs, y_refs = (uf_ref, ub_ref), (yf_ref, yb_ref)
    phases = [(d, jq) for d in range(2) for jq in range(nj)]
    n_sub = 2 * S5_TILES
    col = sdim // S5_TILES
    run = tc // n_sub
    state = {}

    def drive_piece(p, c):
        d, jq = phases[p]
        if c == 0:
            state["uk"] = u_refs[d][:, kdim * jq:kdim * (jq + 1)].astype(BF16)
        v_ref[p % 3, :, col * c:col * (c + 1)] = jnp.dot(state["uk"], bm_ref[d, jq, :, col * c:col * (c + 1)],
                                                         preferred_element_type=F32)

    def readout_piece(p, c):
        d, jq = phases[p]
        part = jnp.dot(v_ref[p % 3, :, col * c:col * (c + 1)].astype(BF16), cm_ref[d, jq, col * c:col * (c + 1), :],
                       preferred_element_type=F32)
        out = y_refs[d].at[:, kdim * jq:kdim * (jq + 1)]
        out[...] = part if c == 0 else out[...] + part

    def scan_run(p, c):
        d, jq = phases[p]
        if c == 0:
            lam = lam_ref[d, jq]
            state["lr"] = [jnp.broadcast_to(lam[0:1, LANES * t:LANES * (t + 1)], (nb, LANES)) for t in range(S5_TILES)]
            state["li"] = [jnp.broadcast_to(lam[1:2, LANES * t:LANES * (t + 1)], (nb, LANES)) for t in range(S5_TILES)]
            h0 = h_ref[d, jq]
            state["h"] = [h0[:, LANES * i:LANES * (i + 1)] for i in range(2 * S5_TILES)]
        lr, li, h = state["lr"], state["li"], state["h"]
        for s in range(run * c, run * (c + 1)):
            r0 = (s if d == 0 else tc - 1 - s) * nb
            rows = v_ref[p % 3, r0:r0 + nb, :]
            new = []
            for t in range(S5_TILES):
                vr = rows[:, 2 * LANES * t:2 * LANES * t + LANES]
                vi = rows[:, 2 * LANES * t + LANES:2 * LANES * (t + 1)]
                hr, hi = h[2 * t], h[2 * t + 1]
                new.append(lr[t] * hr - li[t] * hi + vr)
                new.append(lr[t] * hi + li[t] * hr + vi)
            v_ref[p % 3, r0:r0 + nb, :] = jnp.concatenate(new, axis=1)
            h = new
        state["h"] = h
        if c == n_sub - 1:
            h_ref[d, jq] = jnp.concatenate(h, axis=1)

    for slot in range(len(phases) + 2):
        for c in range(n_sub):
            if 0 <= slot - 1 < len(phases):
                scan_run(slot - 1, c)
            if c < S5_TILES and slot < len(phases):
                drive_piece(slot, c)
            if c >= S5_TILES and 0 <= slot - 2 < len(phases):
                readout_piece(slot - 2, c - S5_TILES)


def _s5_scan(u_tm, bm, cm, lam, nb, n_ctx):
    rows_total, w = u_tm.shape
    tc = S5_CHUNK
    r = tc * nb
    n_chunks = rows_total // r
    ncc = n_ctx // tc
    bwd = lambda i: (jnp.where(i < ncc, ncc - 1 - i, n_chunks - 1 - (i - ncc)), 0)
    nj = bm.shape[1]
    sdim = bm.shape[3]
    kern = functools.partial(_s5_scan_kernel, tc=tc, nb=nb)
    return pl.pallas_call(
        kern,
        out_shape=[jax.ShapeDtypeStruct((rows_total, w), F32)] * 2,
        grid=(n_chunks,),
        in_specs=[pl.BlockSpec((r, w), lambda i: (i, 0)), pl.BlockSpec((r, w), bwd),
                  _full(bm.shape), _full(cm.shape), _full(lam.shape)],
        out_specs=[pl.BlockSpec((r, w), lambda i: (i, 0)), pl.BlockSpec((r, w), bwd)],
        scratch_shapes=[pltpu.VMEM((3, r, sdim), F32), pltpu.VMEM((2, nj, nb, sdim), F32)],
        compiler_params=_params("arbitrary"),
        name="s5_scan",
    )(u_tm, u_tm, bm, cm, lam)


def _l1_post_kernel(x_ref, u_ref, yf_ref, yb_ref, dsk_ref, wglu_ref, bglu_ref, wout_ref, mod_ref, gf_ref,
                    wr_ref, br_ref, x3_ref, f_ref, ri_ref, rw_ref, cnt_ref, base_ref):
    y = dsk_ref[...] * u_ref[...] + yf_ref[...] + yb_ref[...]
    g = jax.nn.gelu(y, approximate=True)
    z = jnp.dot(g.astype(BF16), wglu_ref[...], preferred_element_type=F32) + bglu_ref[...]
    o = jnp.dot((g * jax.nn.sigmoid(z)).astype(BF16), wout_ref[...], preferred_element_type=F32)
    first = jnp.logical_and(pl.program_id(0) == 0, pl.program_id(1) == 0)
    _route_tail(x_ref[0], o, mod_ref[0, 0], gf_ref[...], wr_ref, br_ref, base_ref,
                x3_ref, f_ref, ri_ref, rw_ref, cnt_ref, first)


def _l1_post(x2, u_tm, yf, yb, dsk, wglu, bglu, wout, mod, gf, wr, br, n_ctx):
    b, l, d = x2.shape
    t = TOKEN_TILE
    nct = n_ctx // t
    s = l - n_ctx
    tm = pl.BlockSpec((t, S5_WIDTH), lambda bi, ti: (ti + nct, bi))
    shapes, specs = _tail_out_shapes(b, s, d)
    return pl.pallas_call(
        _l1_post_kernel,
        out_shape=shapes,
        grid=(b, s // t),
        in_specs=[pl.BlockSpec((1, t, d), lambda bi, ti: (bi, ti + nct, 0)), tm, tm, tm,
                  _full((1, S5_WIDTH)), _full((S5_WIDTH, S5_WIDTH)), _full((1, S5_WIDTH)), _full((S5_WIDTH, d)),
                  pl.BlockSpec((1, 1, 6, d), lambda bi, ti: (bi, 1, 0, 0)),
                  _full((1, d)), _full((2, d, LANES)), _full((1, LANES))],
        out_specs=specs,
        scratch_shapes=[pltpu.VMEM((8, LANES), F32)],
        compiler_params=_params("arbitrary", "arbitrary"),
        name="l1_post",
    )(x2, u_tm, yf, yb, dsk, wglu, bglu, wout, mod, gf, wr, br)


def _final_kernel(x_ref, rw_ref, slots_ref, y_ref, mod_ref, g_ref, o_ref, idx_ref, ybuf_ref, sem_idx, sem_rows):
    moe = _combined_rows(y_ref, slots_ref, rw_ref, idx_ref, ybuf_ref, sem_idx, sem_rows)
    x4 = x_ref[...] + mod_ref[0, 0][5:6] * moe
    o_ref[...] = _rms(x4) * g_ref[...]


def _final(x3, rw, slots, y, mod, g, b):
    n_tok, d = x3.shape
    t = TOKEN_TILE
    nt = n_tok // b // t
    return pl.pallas_call(
        _final_kernel,
        out_shape=jax.ShapeDtypeStruct((n_tok, d), F32),
        grid=(n_tok // t,),
        in_specs=[pl.BlockSpec((t, d), lambda i: (i, 0)), pl.BlockSpec((t, 8), lambda i: (i, 0)),
                  pl.BlockSpec(memory_space=pltpu.VMEM), pl.BlockSpec(memory_space=pl.ANY),
                  pl.BlockSpec((1, 1, 6, d), lambda i: (i // nt, 1, 0, 0)), _full((1, d))],
        out_specs=pl.BlockSpec((t, d), lambda i: (i, 0)),
        scratch_shapes=_gather_scratch(2, t),
        compiler_params=_params("arbitrary"),
        name="final_combine",
    )(x3, rw, slots, y, mod, g)


def _rope_tables(n_ctx, n_lat):
    pos = jnp.arange(n_lat, dtype=I32)
    row = (pos // GRID_W).astype(F32)
    col = (pos % GRID_W).astype(F32)
    inv_freq = jnp.power(ROPE_THETA, -jnp.arange(0, AXIS_DIM, 2, dtype=F32) / AXIS_DIM)
    lane = np.arange(LANES)
    dd = lane % HEAD_DIM
    use_col = dd >= AXIS_DIM
    first = (dd % AXIS_DIM) < AXIS_DIM // 2
    freq = inv_freq[(dd % AXIS_DIM) % (AXIS_DIM // 2)]
    ang = jnp.where(use_col[None, :], col[:, None], row[:, None]) * freq[None, :]
    cos, sin = jnp.cos(ang), jnp.sin(ang)
    sa = jnp.where(first[None, :], -sin, 0.0)
    sb = jnp.where(first[None, :], 0.0, sin)
    pad = lambda a, v: jnp.concatenate([jnp.full((n_ctx, LANES), v, F32), a], axis=0)
    return pad(cos, 1.0), pad(sa, 0.0), pad(sb, 0.0)


def _group_ones(width, group):
    idx = np.arange(width) // group
    return jnp.asarray(idx[:, None] == idx[None, :], dtype=BF16)


def _router_weights(w_grp, b_grp, w_rt, b_rt):
    d = w_grp.shape[0]
    used = N_EXPERT_GROUPS + N_EXPERTS
    wr = jnp.concatenate([w_grp, w_rt, jnp.zeros((d, LANES - used), F32)], axis=1)
    br = jnp.concatenate([b_grp, b_rt, jnp.zeros((LANES - used,), F32)]).reshape(1, LANES)
    return jnp.stack(_split_bf16(wr)), br


def kernel(x, c, ctx, c_ctx, w_mod, b_mod, norm_mix_g, norm_ffn_g, mix_w_in, mix_w_out, q_norm_g, k_norm_g, gmlp_norm_g, gmlp_w_spatial, gmlp_b_spatial, s5_w_in, s5_a_re, s5_a_im, s5_log_dt, s5_b_re, s5_b_im, s5_c_re, s5_c_im, s5_d, s5_w_glu, s5_b_glu, s5_w_out, moe_w_group, moe_b_group, moe_w_router, moe_b_router, moe_w1, moe_w3, moe_w2, final_norm_g):
    b, s, d = x.shape
    n_ctx = ctx.shape[1]
    l = n_ctx + s
    assert w_mod.shape[0] == 2, "two layers: attention/gMLP then S5"
    assert n_ctx % TOKEN_TILE == 0 and s % TOKEN_TILE == 0 and b % 8 == 0 and d == D_MODEL
    nct = n_ctx // TOKEN_TILE

    r = (b + 1 + 7) // 8 * 8
    cond = jnp.concatenate([c, c_ctx[None, :], jnp.zeros((r - b - 1, d), F32)], axis=0)
    mods = _adaln(cond, w_mod, b_mod)
    lat = mods[:, :b].reshape(2, b, 1, 6, d)
    cxt = jnp.broadcast_to(mods[:, b].reshape(2, 1, 1, 6, d), (2, b, 1, 6, d))
    mod = jnp.concatenate([cxt, lat], axis=2)

    w_in = jnp.concatenate([mix_w_in[0][:, _Q_COL_PERM], mix_w_in[0][:, ATTN_WIDTH:]], axis=1).astype(BF16)
    w_out = jnp.concatenate([mix_w_out[0][_Q_COL_PERM], mix_w_out[0][ATTN_WIDTH:]], axis=0).astype(BF16)
    cos, sa, sb = _rope_tables(n_ctx, s)
    qg = jnp.tile(q_norm_g[0], N_Q_HEADS).reshape(1, ATTN_WIDTH)
    kg = jnp.tile(k_norm_g[0], N_KV_HEADS).reshape(1, KV_WIDTH)
    q, k, v, gu, gv = _l0_proj(ctx, x, mod[0], norm_mix_g[0].reshape(1, d), w_in, qg, kg,
                               gmlp_norm_g[0].reshape(1, GMLP_WIDTH), cos, sa, sb,
                               _group_ones(ATTN_WIDTH, HEAD_DIM), _group_ones(KV_WIDTH, HEAD_DIM), nct)
    o = _attention(q, k, v, n_ctx)
    bsp = jnp.repeat(gmlp_b_spatial[0].T, GMLP_GROUP_DIM, axis=1)
    wr0, br0 = _router_weights(moe_w_group[0], moe_b_group[0], moe_w_router[0], moe_b_router[0])
    x1, f0, ri0, rw0, cnt0 = _l0_post(ctx, x, o, gu, gv, gmlp_w_spatial[0].astype(BF16), bsp, w_out, mod[0],
                                      norm_ffn_g[0].reshape(1, d), wr0, br0, nct)
    y0, slots0 = _moe(f0, ri0, cnt0, moe_w1, moe_w3, moe_w2, 0)
    x2, u_tm = _l0_combine(x1.reshape(b * l, d), rw0, slots0, y0, mod[0], mod[1], norm_mix_g[1].reshape(1, d),
                           s5_w_in[0].astype(BF16), b, nct)

    lre, lim, bbre, bbim = _s5_zoh(s5_a_re[0], s5_a_im[0], s5_log_dt[0], s5_b_re[0], s5_b_im[0])
    bm, cm, lam = _s5_matrices(lre, lim, bbre, bbim, s5_c_re[0], s5_c_im[0])
    yf, yb = _s5_scan(u_tm.reshape(l * b, S5_WIDTH), bm, cm, lam, b, n_ctx)
    wr1, br1 = _router_weights(moe_w_group[1], moe_b_group[1], moe_w_router[1], moe_b_router[1])
    tmv = lambda a: a.reshape(l, b * S5_WIDTH)
    x3, f1, ri1, rw1, cnt1 = _l1_post(x2.reshape(b, l, d), u_tm, tmv(yf), tmv(yb), s5_d[0].reshape(1, S5_WIDTH),
                                      s5_w_glu[0].astype(BF16), s5_b_glu[0].reshape(1, S5_WIDTH),
                                      s5_w_out[0].astype(BF16), mod[1], norm_ffn_g[1].reshape(1, d), wr1, br1, n_ctx)
    y1, slots1 = _moe(f1, ri1, cnt1, moe_w1, moe_w3, moe_w2, 1)
    out = _final(x3.reshape(b * s, d), rw1, slots1, y1, mod[1], final_norm_g.reshape(1, d), b)
    return out.reshape(b, s, d)
```

```python
import functools

import numpy as np
import jax
import jax.numpy as jnp
from jax import lax
from jax.experimental import pallas as pl
from jax.experimental.pallas import tpu as pltpu

F32 = jnp.float32
BF16 = jnp.bfloat16
I32 = jnp.int32

EPS = 1e-6
GRID_W = 64
HEAD_DIM = 64
N_Q_HEADS = 8
N_KV_HEADS = 2
ATTN_WIDTH = N_Q_HEADS * HEAD_DIM
KV_WIDTH = N_KV_HEADS * HEAD_DIM
AXIS_DIM = HEAD_DIM // 2
ROPE_THETA = 10000.0
GMLP_GROUPS = 8
GMLP_GROUP_DIM = 64
GMLP_WIDTH = GMLP_GROUPS * GMLP_GROUP_DIM
GMLP_CHUNK = 128
MIX_IN_WIDTH = ATTN_WIDTH + 2 * KV_WIDTH + 2 * GMLP_WIDTH
S5_GROUPS = 32
S5_GROUP_DIM = 16
S5_STATE = 64
S5_WIDTH = S5_GROUPS * S5_GROUP_DIM
N_EXPERT_GROUPS = 4
EXPERTS_PER_GROUP = 8
N_EXPERTS = N_EXPERT_GROUPS * EXPERTS_PER_GROUP
EXPERT_HIDDEN = 512
MOE_BLOCK = 512

D_MODEL = 1024
LANES = 128
ROW_SUB = D_MODEL // LANES
TOKEN_TILE = 256
ROW_DMA_UNROLL = 8
ATTN_KEY_BLOCK = 128
V_ONES_ROWS = 16
V_AUG_ROWS = N_KV_HEADS * (HEAD_DIM + V_ONES_ROWS)
S5_CHUNK = 64
S5_TILES = 4
VMEM_LIMIT = 48 * 1024 * 1024
NEG_BIG = -1e30

_Q_HEAD_ORDER = (0, 4, 1, 5, 2, 6, 3, 7)
_Q_COL_PERM = np.concatenate([np.arange(HEAD_DIM) + HEAD_DIM * h for h in _Q_HEAD_ORDER])


def _params(*sem):
    return pltpu.CompilerParams(dimension_semantics=sem, vmem_limit_bytes=VMEM_LIMIT)


def _rms(x):
    return x * lax.rsqrt(jnp.mean(x * x, axis=-1, keepdims=True) + EPS)


def _modulate(x, g, shift, scale):
    return _rms(x) * g * (1.0 + scale) + shift


def _full(shape):
    return pl.BlockSpec(shape, lambda *_: (0,) * len(shape))


def _split_bf16(a):
    hi = a.astype(BF16)
    return hi, (a - hi.astype(F32)).astype(BF16)


def _dot_split(a, w_hi, w_lo):
    a_hi, a_lo = _split_bf16(a)
    dot = functools.partial(jnp.dot, preferred_element_type=F32)
    return dot(a_hi, w_hi) + (dot(a_lo, w_hi) + dot(a_hi, w_lo))


def _store_row_tiles(ref, val):
    n = val.shape[0]
    for s in range(ROW_SUB):
        ref[pl.ds(s, n, stride=ROW_SUB), :] = val[:, LANES * s:LANES * (s + 1)]


def _load_row_tiles(ref):
    n = ref.shape[0] // ROW_SUB
    return jnp.concatenate([ref[pl.ds(s, n, stride=ROW_SUB), :] for s in range(ROW_SUB)], axis=1)


def _adaln_kernel(cond_ref, w_ref, b_ref, o_ref):
    c = cond_ref[...]
    s = c * jax.nn.sigmoid(c)
    o_ref[0] = _dot_split(s, *_split_bf16(w_ref[0])) + b_ref[0]


def _adaln(cond, w_mod, b_mod):
    depth, d, n = w_mod.shape
    r = cond.shape[0]
    tn = 1536
    return pl.pallas_call(
        _adaln_kernel,
        out_shape=jax.ShapeDtypeStruct((depth, r, n), F32),
        grid=(depth, n // tn),
        in_specs=[pl.BlockSpec((r, d), lambda l, j: (0, 0)),
                  pl.BlockSpec((1, d, tn), lambda l, j: (l, 0, j)),
                  pl.BlockSpec((1, 1, tn), lambda l, j: (l, 0, j))],
        out_specs=pl.BlockSpec((1, r, tn), lambda l, j: (l, 0, j)),
        compiler_params=_params("parallel", "parallel"),
        name="adaln",
    )(cond, w_mod, b_mod.reshape(depth, 1, n))


def _seq_tile(ctx_ref, x_ref, nct):
    return jnp.where(pl.program_id(1) < nct, ctx_ref[0], x_ref[0])


def _seq_specs(t, d, nct):
    return [pl.BlockSpec((1, t, d), lambda bi, ti: (bi, jnp.minimum(ti, nct - 1), 0)),
            pl.BlockSpec((1, t, d), lambda bi, ti: (bi, jnp.maximum(ti - nct, 0), 0))]


def _l0_proj_kernel(ctx_ref, x_ref, mod_ref, g_ref, win_ref, qg_ref, kg_ref, lng_ref, cos_ref, sa_ref, sb_ref,
                    gq_ref, gk_ref, q_ref, k_ref, v_ref, gu_ref, gv_ref, *, nct):
    m = mod_ref[0, 0]
    h = _modulate(_seq_tile(ctx_ref, x_ref, nct), g_ref[...], m[0:1], m[1:2]).astype(BF16)
    z = jnp.dot(h, win_ref[...], preferred_element_type=F32)
    cos, sa, sb = cos_ref[...], sa_ref[...], sb_ref[...]

    def head_norm_rope(t, gsum_ref, gain, reps):
        w = t.shape[1]
        ss = jnp.dot((t * t).astype(BF16), gsum_ref[...], preferred_element_type=F32)
        tn = t * lax.rsqrt(ss * (1.0 / HEAD_DIM) + EPS) * gain
        c, a, b = (jnp.concatenate([tab] * reps, axis=1) if reps > 1 else tab for tab in (cos, sa, sb))
        return tn * c + pltpu.roll(tn, w - AXIS_DIM // 2, 1) * a + pltpu.roll(tn, AXIS_DIM // 2, 1) * b

    q = head_norm_rope(z[:, :ATTN_WIDTH], gq_ref, qg_ref[...], ATTN_WIDTH // LANES)
    q_ref[0] = (q * (HEAD_DIM ** -0.5)).T.astype(BF16)
    k = head_norm_rope(z[:, ATTN_WIDTH:ATTN_WIDTH + KV_WIDTH], gk_ref, kg_ref[...], 1)
    k_ref[0] = k.astype(BF16)
    vt = z[:, ATTN_WIDTH + KV_WIDTH:ATTN_WIDTH + 2 * KV_WIDTH].T
    ones = jnp.ones((V_ONES_ROWS, vt.shape[1]), F32)
    v_ref[0] = jnp.concatenate([part for j in range(N_KV_HEADS)
                                for part in (vt[HEAD_DIM * j:HEAD_DIM * (j + 1)], ones)], axis=0).astype(BF16)
    off = ATTN_WIDTH + 2 * KV_WIDTH
    gu_ref[0] = jax.nn.gelu(z[:, off:off + GMLP_WIDTH], approximate=True).astype(BF16)
    gv = jax.nn.gelu(z[:, off + GMLP_WIDTH:], approximate=True)
    mu = jnp.mean(gv, axis=-1, keepdims=True)
    gc = gv - mu
    gn = gc * lax.rsqrt(jnp.mean(gc * gc, axis=-1, keepdims=True) + EPS) * lng_ref[...]
    gv_ref[0] = gn.astype(BF16)


def _l0_proj(ctx, x, mod, g, w_in, qg, kg, lng, cos, sa, sb, gq, gk, nct):
    b, s, d = x.shape
    l = s + ctx.shape[1]
    t = TOKEN_TILE
    tok = lambda w: pl.BlockSpec((1, t, w), lambda bi, ti: (bi, ti, 0))
    tab = pl.BlockSpec((t, LANES), lambda bi, ti: (ti, 0))
    tok_t = lambda w: pl.BlockSpec((1, w, t), lambda bi, ti: (bi, 0, ti))
    outs = ([jax.ShapeDtypeStruct((b, ATTN_WIDTH, l), BF16), jax.ShapeDtypeStruct((b, l, KV_WIDTH), BF16),
             jax.ShapeDtypeStruct((b, V_AUG_ROWS, l), BF16)]
            + [jax.ShapeDtypeStruct((b, l, GMLP_WIDTH), BF16)] * 2)
    return pl.pallas_call(
        functools.partial(_l0_proj_kernel, nct=nct),
        out_shape=outs,
        grid=(b, l // t),
        in_specs=_seq_specs(t, d, nct) + [
                  pl.BlockSpec((1, 1, 6, d), lambda bi, ti: (bi, jnp.where(ti >= nct, 1, 0), 0, 0)),
                  _full((1, d)), _full((d, MIX_IN_WIDTH)), _full((1, ATTN_WIDTH)), _full((1, KV_WIDTH)),
                  _full((1, GMLP_WIDTH)), tab, tab, tab,
                  _full((ATTN_WIDTH, ATTN_WIDTH)), _full((KV_WIDTH, KV_WIDTH))],
        out_specs=[tok_t(ATTN_WIDTH), tok(KV_WIDTH), tok_t(V_AUG_ROWS), tok(GMLP_WIDTH), tok(GMLP_WIDTH)],
        compiler_params=_params("parallel", "parallel"),
        name="l0_proj",
    )(ctx, x, mod, g, w_in, qg, kg, lng, cos, sa, sb, gq, gk)


def _attn_kernel(qt_ref, k_ref, vt_ref, o_ref, s_ref, p_ref, *, n_ctx, nct):
    ti = pl.program_id(1)
    tq = qt_ref.shape[2]
    row = lax.broadcasted_iota(I32, (LANES, tq), 0)
    low = row < HEAD_DIM
    kb = ATTN_KEY_BLOCK
    va = HEAD_DIM + V_ONES_ROWS

    def attend(nk):
        heads = [(m, j) for m in range(ATTN_WIDTH // LANES) for j in range(2)]
        blocks = [slice(kb * b, kb * (b + 1)) for b in range(nk // kb)]
        state = {}

        def start(h):
            m, j = heads[h]
            qp = qt_ref[0, LANES * m:LANES * (m + 1), :]
            state[h] = dict(qm=jnp.where(low if j == 0 else jnp.logical_not(low), qp, jnp.zeros_like(qp)), mx=None)

        def score_block(h, blk):
            s = jnp.dot(k_ref[0, blk, :], state[h]["qm"], preferred_element_type=F32)
            s_ref[h % 2, blk, :] = s
            bm = jnp.max(s, axis=0, keepdims=True)
            state[h]["mx"] = bm if state[h]["mx"] is None else jnp.maximum(state[h]["mx"], bm)

        def exp_block(h, blk):
            p_ref[h % 2, blk, :] = jnp.exp(s_ref[h % 2, blk, :] - state[h]["mx"]).astype(BF16)

        def finish(h):
            j = heads[h][1]
            pv = jnp.dot(vt_ref[0, va * j:va * (j + 1), :nk], p_ref[h % 2, :nk, :],
                         preferred_element_type=F32)
            return pv[:HEAD_DIM] / pv[HEAD_DIM:HEAD_DIM + 1]

        start(0)
        for blk in blocks:
            score_block(0, blk)
        halves = []
        for h in range(len(heads)):
            if h + 1 < len(heads):
                start(h + 1)
            for blk in blocks:
                if h + 1 < len(heads):
                    score_block(h + 1, blk)
                exp_block(h, blk)
            halves.append(finish(h))
            if heads[h][1] == 1:
                m = heads[h][0]
                o_ref[0, :, LANES * m:LANES * (m + 1)] = jnp.concatenate(halves, axis=0).T.astype(BF16)
                halves = []

    @pl.when(ti < nct)
    def _():
        attend(n_ctx)

    @pl.when(ti >= nct)
    def _():
        attend(k_ref.shape[1])


def _attention(qt, k, vt, n_ctx):
    b, l, _ = k.shape
    t = TOKEN_TILE
    kern = functools.partial(_attn_kernel, n_ctx=n_ctx, nct=n_ctx // t)
    return pl.pallas_call(
        kern,
        out_shape=jax.ShapeDtypeStruct((b, l, ATTN_WIDTH), BF16),
        grid=(b, l // t),
        in_specs=[pl.BlockSpec((1, ATTN_WIDTH, t), lambda bi, ti: (bi, 0, ti)),
                  pl.BlockSpec((1, l, KV_WIDTH), lambda bi, ti: (bi, 0, 0)),
                  pl.BlockSpec((1, V_AUG_ROWS, l), lambda bi, ti: (bi, 0, 0))],
        out_specs=pl.BlockSpec((1, t, ATTN_WIDTH), lambda bi, ti: (bi, ti, 0)),
        scratch_shapes=[pltpu.VMEM((2, l, t), F32), pltpu.VMEM((2, l, t), BF16)],
        compiler_params=_params("parallel", "parallel"),
        name="attention",
    )(qt, k, vt)


def _route_tail(x, y, m, gf, wr_ref, br_ref, base_ref, x_ref, f_ref, ri_ref, rw_ref, cnt_ref, first):
    x1 = x + m[2:3] * y
    x_ref[0] = x1
    f = _modulate(x1, gf, m[3:4], m[4:5])
    _store_row_tiles(f_ref, f)
    t = f.shape[0]
    logits = _dot_split(f, wr_ref[0], wr_ref[1]) + br_ref[...]
    lane = lax.broadcasted_iota(I32, (t, LANES), 1).astype(F32)
    is_grp = lane < N_EXPERT_GROUPS
    gl = jnp.where(is_grp, logits, NEG_BIG)
    gmax = jnp.max(gl, axis=-1, keepdims=True)
    g_idx = jnp.min(jnp.where(gl == gmax, lane, float(LANES)), axis=-1, keepdims=True)
    g_w = 1.0 / jnp.sum(jnp.where(is_grp, jnp.exp(gl - gmax), 0.0), axis=-1, keepdims=True)
    e_lo = N_EXPERT_GROUPS + EXPERTS_PER_GROUP * g_idx
    in_grp = jnp.logical_and(lane >= e_lo, lane < e_lo + EXPERTS_PER_GROUP)
    sel = jnp.where(in_grp, logits, NEG_BIG)
    v1 = jnp.max(sel, axis=-1, keepdims=True)
    i1 = jnp.min(jnp.where(sel == v1, lane, float(LANES)), axis=-1, keepdims=True)
    sel2 = jnp.where(lane == i1, NEG_BIG, sel)
    v2 = jnp.max(sel2, axis=-1, keepdims=True)
    i2 = jnp.min(jnp.where(sel2 == v2, lane, float(LANES)), axis=-1, keepdims=True)
    e = jnp.exp(v2 - v1)
    w1 = g_w / (1.0 + e)
    w2 = g_w * e / (1.0 + e)
    eid1 = i1 - N_EXPERT_GROUPS
    eid2 = i2 - N_EXPERT_GROUPS

    @pl.when(first)
    def _():
        base_ref[...] = jnp.zeros_like(base_ref)

    hit1 = lane == eid1
    hit2 = lane == eid2
    cnt = (hit1.astype(F32) + hit2.astype(F32))
    row = lax.broadcasted_iota(I32, (t, t), 0)
    col = lax.broadcasted_iota(I32, (t, t), 1)
    tril = jnp.where(col < row, 1.0, 0.0).astype(BF16)
    before = jnp.dot(tril, cnt.astype(BF16), preferred_element_type=F32) + base_ref[0:1, :]
    rank1 = jnp.sum(jnp.where(hit1, before, 0.0), axis=-1, keepdims=True)
    rank2 = jnp.sum(jnp.where(hit2, before, 0.0), axis=-1, keepdims=True)
    total = base_ref[0:1, :] + jnp.sum(cnt, axis=0, keepdims=True)
    base_ref[...] = jnp.broadcast_to(total, base_ref.shape)
    cnt_ref[...] = jnp.broadcast_to(total, cnt_ref.shape)

    ri = jnp.where(lane == 0, eid1, jnp.where(lane == 1, eid2, jnp.where(lane == 2, rank1,
                   jnp.where(lane == 3, rank2, 0.0))))
    ri_ref[...] = ri.T[0:8, :].astype(I32)
    l8 = lax.broadcasted_iota(I32, (t, 8), 1)
    rw_ref[...] = jnp.where(l8 == 0, w1, jnp.where(l8 == 1, w2, 0.0))


def _tail_out_shapes(b, n_rows, d):
    t = TOKEN_TILE
    shapes = [jax.ShapeDtypeStruct((b, n_rows, d), F32), jax.ShapeDtypeStruct((b * n_rows * ROW_SUB, LANES), F32),
              jax.ShapeDtypeStruct((8, b * n_rows), I32), jax.ShapeDtypeStruct((b * n_rows, 8), F32),
              jax.ShapeDtypeStruct((8, LANES), F32)]
    nt = n_rows // t
    flat = lambda w: pl.BlockSpec((t, w), lambda bi, ti: (bi * nt + ti, 0))
    specs = [pl.BlockSpec((1, t, d), lambda bi, ti: (bi, ti, 0)),
             pl.BlockSpec((t * ROW_SUB, LANES), lambda bi, ti: (bi * nt + ti, 0)),
             pl.BlockSpec((8, t), lambda bi, ti: (0, bi * nt + ti)), flat(8),
             pl.BlockSpec((8, LANES), lambda bi, ti: (0, 0))]
    return shapes, specs


def _l0_post_kernel(ctx_ref, x_ref, o_ref, gu_ref, gv_ref, wsp_ref, bsp_ref, wout_ref, mod_ref, gf_ref, wr_ref,
                    br_ref, x1_ref, f_ref, ri_ref, rw_ref, cnt_ref, base_ref, *, nct):
    tq = x_ref.shape[1]
    lane = lax.broadcasted_iota(I32, (GMLP_CHUNK, LANES), 1)
    low = lane < GMLP_GROUP_DIM
    gated = []
    for c in range(tq // GMLP_CHUNK):
        rows = slice(c * GMLP_CHUNK, (c + 1) * GMLP_CHUNK)
        mixed = []
        for m in range(GMLP_WIDTH // LANES):
            pair = gv_ref[0, rows, LANES * m:LANES * (m + 1)]
            a = jnp.dot(wsp_ref[2 * m], pair, preferred_element_type=F32)
            b = jnp.dot(wsp_ref[2 * m + 1], pair, preferred_element_type=F32)
            mixed.append(jnp.where(low, a, b))
        mixed = jnp.concatenate(mixed, axis=1) + bsp_ref[...]
        gated.append((gu_ref[0, rows, :].astype(F32) * mixed).astype(BF16))
    gated = jnp.concatenate(gated, axis=0)
    y = (jnp.dot(o_ref[0], wout_ref[:ATTN_WIDTH, :], preferred_element_type=F32)
         + jnp.dot(gated, wout_ref[ATTN_WIDTH:, :], preferred_element_type=F32))
    first = jnp.logical_and(pl.program_id(0) == 0, pl.program_id(1) == 0)
    _route_tail(_seq_tile(ctx_ref, x_ref, nct), y, mod_ref[0, 0], gf_ref[...], wr_ref, br_ref, base_ref,
                x1_ref, f_ref, ri_ref, rw_ref, cnt_ref, first)


def _l0_post(ctx, x, o, gu, gv, wsp, bsp, wout, mod, gf, wr, br, nct):
    b, s, d = x.shape
    l = s + ctx.shape[1]
    t = TOKEN_TILE
    tok = lambda w: pl.BlockSpec((1, t, w), lambda bi, ti: (bi, ti, 0))
    shapes, specs = _tail_out_shapes(b, l, d)
    return pl.pallas_call(
        functools.partial(_l0_post_kernel, nct=nct),
        out_shape=shapes,
        grid=(b, l // t),
        in_specs=_seq_specs(t, d, nct) + [tok(ATTN_WIDTH), tok(GMLP_WIDTH), tok(GMLP_WIDTH),
                  _full((GMLP_GROUPS, GMLP_CHUNK, GMLP_CHUNK)), _full((GMLP_CHUNK, GMLP_WIDTH)),
                  _full((ATTN_WIDTH + GMLP_WIDTH, d)),
                  pl.BlockSpec((1, 1, 6, d), lambda bi, ti: (bi, jnp.where(ti >= nct, 1, 0), 0, 0)),
                  _full((1, d)), _full((2, d, LANES)), _full((1, LANES))],
        out_specs=specs,
        scratch_shapes=[pltpu.VMEM((8, LANES), F32)],
        compiler_params=_params("arbitrary", "arbitrary"),
        name="l0_post",
    )(ctx, x, o, gu, gv, wsp, bsp, wout, mod, gf, wr, br)


def _row_copy(src_ref, src_row, dst_ref, dst_row, sem):
    rows = lambda ref, r: ref.at[pl.ds(pl.multiple_of(r * ROW_SUB, ROW_SUB), ROW_SUB)]
    return pltpu.make_async_copy(rows(src_ref, src_row), rows(dst_ref, dst_row), sem)


def _issue_rows(src_ref, index_of, buf_ref, sem_rows, s):
    n_str, rows = buf_ref.shape[1], buf_ref.shape[2] // ROW_SUB
    slot = s % 2

    def issue(j, carry):
        for half in range(2):
            r = 2 * j + half
            for k in range(n_str):
                _row_copy(src_ref, index_of(k * rows + r), buf_ref.at[slot, k], r,
                          sem_rows.at[slot]).start(priority=(k + half) % 2)
        return carry

    lax.fori_loop(0, rows // 2, issue, 0, unroll=ROW_DMA_UNROLL // 2)


def _wait_rows(src_ref, buf_ref, sem_rows, s):
    n_str, rows = buf_ref.shape[1], buf_ref.shape[2] // ROW_SUB
    slot = s % 2
    for k in range(n_str):
        pltpu.make_async_copy(src_ref.at[pl.ds(0, rows * ROW_SUB)], buf_ref.at[slot, k], sem_rows.at[slot]).wait()
    return slot


def _gather_pipeline(src_ref, table_ref, idx_ref, buf_ref, sem_idx, sem_rows, step, n_steps):
    def idx_copy(s):
        return pltpu.make_async_copy(table_ref.at[s], idx_ref.at[s % 2], sem_idx.at[s % 2])

    def issue_rows(s):
        _issue_rows(src_ref, lambda pos: idx_ref[s % 2, pos], buf_ref, sem_rows, s)

    @pl.when(step == 0)
    def _():
        idx_copy(0).start()
        idx_copy(0).wait()
        issue_rows(0)

        @pl.when(n_steps > 1)
        def _():
            idx_copy(1).start()

    @pl.when(step + 1 < n_steps)
    def _():
        idx_copy(step + 1).wait()
        issue_rows(step + 1)

        @pl.when(step + 2 < n_steps)
        def _():
            idx_copy(step + 2).start()

    return _wait_rows(src_ref, buf_ref, sem_rows, step)


def _gather_scratch(n_str, rows):
    return [pltpu.SMEM((2, n_str * rows), I32), pltpu.VMEM((2, n_str, rows * ROW_SUB, LANES), F32),
            pltpu.SemaphoreType.DMA((2,)), pltpu.SemaphoreType.DMA((2,))]


def _dispatch_kernel(pad_ref, nu_ref, f_ref, slots_ref, xb_ref, idx_ref, zero_ref, sem_idx, sem_rows):
    i = pl.program_id(0)
    t = f_ref.shape[0] // ROW_SUB
    blk = MOE_BLOCK * ROW_SUB
    idx_cp = pltpu.make_async_copy(slots_ref.at[i], idx_ref, sem_idx)
    idx_cp.start()

    @pl.when(i == 0)
    def _():
        zero_ref[...] = jnp.zeros_like(zero_ref)
        n_blocks = xb_ref.shape[0] // blk

        def zero_block(start):
            return pltpu.make_async_copy(zero_ref, xb_ref.at[pl.ds(pl.multiple_of(start * ROW_SUB, blk), blk)],
                                         sem_rows)

        def tail_start(j, carry):
            zero_block(j * MOE_BLOCK).start()
            return carry

        def tail_wait(j, carry):
            zero_block(0).wait()
            return carry

        for e in range(N_EXPERTS):
            zero_block(jnp.maximum(pad_ref[N_EXPERTS + e] - MOE_BLOCK, 0)).start()
        lax.fori_loop(nu_ref[0], n_blocks, tail_start, 0)
        for e in range(N_EXPERTS):
            zero_block(0).wait()
        lax.fori_loop(nu_ref[0], n_blocks, tail_wait, 0)

    idx_cp.wait()

    def issue(r, carry):
        _row_copy(f_ref, r, xb_ref, idx_ref[r], sem_rows).start(priority=0)
        _row_copy(f_ref, r, xb_ref, idx_ref[t + r], sem_rows).start(priority=1)
        return carry

    lax.fori_loop(0, t, issue, 0, unroll=ROW_DMA_UNROLL)
    for _ in range(2):
        pltpu.make_async_copy(f_ref, xb_ref.at[pl.ds(0, t * ROW_SUB)], sem_rows).wait()


def _dispatch(f, slots, pad, n_used, n_slots):
    t = TOKEN_TILE
    n_tok = f.shape[0] // ROW_SUB
    return pl.pallas_call(
        _dispatch_kernel,
        out_shape=jax.ShapeDtypeStruct((n_slots * ROW_SUB, LANES), F32),
        grid_spec=pltpu.PrefetchScalarGridSpec(
            num_scalar_prefetch=2,
            grid=(n_tok // t,),
            in_specs=[pl.BlockSpec((t * ROW_SUB, LANES), lambda i, p, nu: (i, 0)),
                      pl.BlockSpec(memory_space=pltpu.VMEM)],
            out_specs=pl.BlockSpec(memory_space=pl.ANY),
            scratch_shapes=[pltpu.SMEM((2 * t,), I32), pltpu.VMEM((MOE_BLOCK * ROW_SUB, LANES), F32),
                            pltpu.SemaphoreType.DMA, pltpu.SemaphoreType.DMA]),
        compiler_params=_params("arbitrary"),
        name="moe_dispatch",
    )(pad, n_used, f, slots)


def _experts_kernel(be_ref, nu_ref, xb_ref, w1_ref, w3_ref, w2_ref, y_ref, w1b_ref, w3b_ref, w2b_ref):
    i = pl.program_id(0)

    @pl.when(jnp.logical_or(i == 0, be_ref[i] != be_ref[jnp.maximum(i - 1, 0)]))
    def _():
        w1b_ref[...] = w1_ref[...].astype(BF16)
        w3b_ref[...] = w3_ref[...].astype(BF16)
        w2b_ref[...] = w2_ref[...].astype(BF16)

    @pl.when(i < nu_ref[0])
    def _():
        x = _load_row_tiles(xb_ref).astype(BF16)
        h1 = jnp.dot(x, w1b_ref[...], preferred_element_type=F32)
        h3 = jnp.dot(x, w3b_ref[...], preferred_element_type=F32)
        a = (h1 * jax.nn.sigmoid(h1) * h3).astype(BF16)
        _store_row_tiles(y_ref, jnp.dot(a, w2b_ref[...], preferred_element_type=F32))

    @pl.when(i >= nu_ref[0])
    def _():
        y_ref[...] = jnp.zeros_like(y_ref)


def _experts(xb, w1, w3, w2, layer, block_e, n_used):
    n_blocks = block_e.shape[0]
    _, _, d, hid = w1.shape
    rb = MOE_BLOCK * ROW_SUB
    wspec = lambda s: pl.BlockSpec((None, None) + s, lambda i, be, nu: (layer, be[i], 0, 0))
    return pl.pallas_call(
        _experts_kernel,
        out_shape=jax.ShapeDtypeStruct((n_blocks * rb, LANES), F32),
        grid_spec=pltpu.PrefetchScalarGridSpec(
            num_scalar_prefetch=2,
            grid=(n_blocks,),
            in_specs=[pl.BlockSpec((rb, LANES), lambda i, be, nu: (jnp.minimum(i, nu[0] - 1), 0)),
                      wspec((d, hid)), wspec((d, hid)), wspec((hid, d))],
            out_specs=pl.BlockSpec((rb, LANES), lambda i, be, nu: (i, 0)),
            scratch_shapes=[pltpu.VMEM((d, hid), BF16), pltpu.VMEM((d, hid), BF16), pltpu.VMEM((hid, d), BF16)]),
        compiler_params=_params("arbitrary"),
        name="moe_experts",
    )(block_e, n_used, xb, w1, w3, w2)


def _moe_plan(route_i, counts, n_tok):
    eid, rank = route_i[0:2], route_i[2:4]
    cnt = counts[0, :N_EXPERTS].astype(I32)
    padded = (cnt + MOE_BLOCK - 1) // MOE_BLOCK * MOE_BLOCK
    pend = jnp.cumsum(padded)
    pstart = pend - padded
    experts = jnp.arange(N_EXPERTS, dtype=I32)
    slots = rank + jnp.sum(jnp.where(eid[:, :, None] == experts, pstart, 0), axis=-1)
    n_assign = 2 * n_tok
    n_slots = (n_assign + MOE_BLOCK - 1) // MOE_BLOCK * MOE_BLOCK + N_EXPERTS * MOE_BLOCK
    n_blocks = n_slots // MOE_BLOCK
    bstart = jnp.arange(n_blocks, dtype=I32) * MOE_BLOCK
    block_e = jnp.minimum(jnp.sum(bstart[:, None] >= pend[None, :], axis=1), N_EXPERTS - 1).astype(I32)
    n_used = pend[-1] // MOE_BLOCK
    last_e = block_e[jnp.maximum(n_used - 1, 0)]
    block_e = jnp.where(jnp.arange(n_blocks) < n_used, block_e, last_e).astype(I32)
    slots = slots.reshape(2, n_tok // TOKEN_TILE, TOKEN_TILE).transpose(1, 0, 2)
    slots = slots.reshape(n_tok // TOKEN_TILE, 2 * TOKEN_TILE).astype(I32)
    pad = jnp.concatenate([pstart + cnt, pend]).astype(I32)
    return slots, block_e, n_used.reshape(1).astype(I32), pad, n_slots


def _moe(f, route_i, counts, w1, w3, w2, layer):
    n_tok = f.shape[0] // ROW_SUB
    slots, block_e, n_used, pad, n_slots = _moe_plan(route_i, counts, n_tok)
    xb = _dispatch(f, slots, pad, n_used, n_slots)
    y = _experts(xb, w1, w3, w2, layer, block_e, n_used)
    return y, slots


def _combined_rows(y_ref, slots_ref, rw_ref, idx_ref, ybuf_ref, sem_idx, sem_rows):
    slot = _gather_pipeline(y_ref, slots_ref, idx_ref, ybuf_ref, sem_idx, sem_rows,
                            pl.program_id(0), pl.num_programs(0))
    rw = rw_ref[...]
    w1, w2 = rw[:, 0:1], rw[:, 1:2]
    t = ybuf_ref.shape[2] // ROW_SUB
    part = lambda k, s: ybuf_ref[slot, k, pl.ds(s, t, stride=ROW_SUB), :]
    return jnp.concatenate([w1 * part(0, s) + w2 * part(1, s) for s in range(ROW_SUB)], axis=1)


def _l0_combine_kernel(x_ref, rw_ref, slots_ref, y_ref, mod0_ref, mod1_ref, g_ref, win_ref,
                       x2_ref, u_ref, idx_ref, ybuf_ref, sem_idx, sem_rows):
    moe = _combined_rows(y_ref, slots_ref, rw_ref, idx_ref, ybuf_ref, sem_idx, sem_rows)
    x2 = x_ref[...] + mod0_ref[0, 0][5:6] * moe
    x2_ref[...] = x2
    m1 = mod1_ref[0, 0]
    h = _modulate(x2, g_ref[...], m1[0:1], m1[1:2]).astype(BF16)
    u_ref[...] = jnp.dot(h, win_ref[...], preferred_element_type=F32)


def _l0_combine(x1, rw, slots, y, mod0, mod1, g, win, b, nct):
    n_tok, d = x1.shape
    t = TOKEN_TILE
    nt = n_tok // b // t
    l = n_tok // b
    modspec = pl.BlockSpec((1, 1, 6, d), lambda i: (i // nt, jnp.where(i % nt >= nct, 1, 0), 0, 0))
    return pl.pallas_call(
        _l0_combine_kernel,
        out_shape=[jax.ShapeDtypeStruct((n_tok, d), F32), jax.ShapeDtypeStruct((l, b * S5_WIDTH), F32)],
        grid=(n_tok // t,),
        in_specs=[pl.BlockSpec((t, d), lambda i: (i, 0)), pl.BlockSpec((t, 8), lambda i: (i, 0)),
                  pl.BlockSpec(memory_space=pltpu.VMEM), pl.BlockSpec(memory_space=pl.ANY),
                  modspec, modspec, _full((1, d)), _full((d, S5_WIDTH))],
        out_specs=[pl.BlockSpec((t, d), lambda i: (i, 0)),
                   pl.BlockSpec((t, S5_WIDTH), lambda i: (i % nt, i // nt))],
        scratch_shapes=_gather_scratch(2, t),
        compiler_params=_params("arbitrary"),
        name="l0_combine",
    )(x1, rw, slots, y, mod0, mod1, g, win)


def _s5_zoh_kernel(are_ref, aim_ref, ldt_ref, bre_ref, bim_ref, lre_ref, lim_ref, bbre_ref, bbim_ref):
    ar, ai = are_ref[...], aim_ref[...]
    dt = jnp.exp(ldt_ref[...])
    mag = jnp.exp(ar * dt)
    lr = mag * jnp.cos(ai * dt)
    li = mag * jnp.sin(ai * dt)
    lre_ref[...] = lr
    lim_ref[...] = li
    nr, ni = lr - 1.0, li
    den = ar * ar + ai * ai
    cr = (nr * ar + ni * ai) / den
    ci = (ni * ar - nr * ai) / den
    br, bi = bre_ref[...], bim_ref[...]
    bbre_ref[...] = cr * br - ci * bi
    bbim_ref[...] = cr * bi + ci * br


def _s5_zoh(a_re, a_im, log_dt, b_re, b_im):
    n_dir, g, n, p = b_re.shape
    rows = n_dir * g * p
    expand = lambda a: jnp.broadcast_to(a[:, :, None, :], (n_dir, g, p, n)).reshape(rows, n)
    ldt = jnp.broadcast_to(log_dt[:, :, None, None], (n_dir, g, p, n)).reshape(rows, n)
    bt = lambda a: jnp.transpose(a, (0, 1, 3, 2)).reshape(rows, n)
    outs = pl.pallas_call(
        _s5_zoh_kernel,
        out_shape=[jax.ShapeDtypeStruct((rows, n), F32)] * 4,
        name="s5_zoh",
    )(expand(a_re), expand(a_im), ldt, bt(b_re), bt(b_im))
    lre, lim, bbre, bbim = (o.reshape(n_dir, g, p, n) for o in outs)
    return lre[:, :, 0, :], lim[:, :, 0, :], bbre, bbim


def _s5_matrices(lre, lim, bbre, bbim, c_re, c_im):
    n_dir = lre.shape[0]
    nj = S5_GROUPS // (2 * S5_TILES)
    shp = (n_dir, nj, S5_TILES, 2, S5_GROUP_DIM, S5_STATE)
    eye_t = jnp.eye(S5_TILES, dtype=F32)
    eye_g = jnp.eye(2, dtype=F32)

    def place(a):
        return jnp.einsum("djtgpn,tu,gh->djtgpuhn", a.reshape(shp), eye_t, eye_g)

    kdim = S5_TILES * 2 * S5_GROUP_DIM
    sdim = S5_TILES * 2 * 2 * S5_STATE
    bm = jnp.stack([place(bbre), place(bbim)], axis=6)
    bm = bm.reshape(n_dir, nj, kdim, sdim).astype(BF16)
    cm = jnp.stack([place(c_re), place(-c_im)], axis=6)
    cm = jnp.transpose(cm, (0, 1, 5, 6, 7, 8, 2, 3, 4)).reshape(n_dir, nj, sdim, kdim).astype(BF16)
    lam = jnp.stack([lre.reshape(n_dir, nj, S5_TILES * LANES), lim.reshape(n_dir, nj, S5_TILES * LANES)], axis=2)
    return bm, cm, lam


def _s5_scan_kernel(uf_ref, ub_ref, bm_ref, cm_ref, lam_ref, yf_ref, yb_ref, v_ref, h_ref, *, tc, nb):
    @pl.when(pl.program_id(0) == 0)
    def _():
        h_ref[...] = jnp.zeros_like(h_ref)

    nj, kdim, sdim = bm_ref.shape[1], bm_ref.shape[2], bm_ref.shape[3]
    u_refs, y_refs = (uf_ref, ub_ref), (yf_ref, yb_ref)
    phases = [(d, jq) for d in range(2) for jq in range(nj)]
    n_sub = 2 * S5_TILES
    col = sdim // S5_TILES
    run = tc // n_sub
    state = {}

    def drive_piece(p, c):
        d, jq = phases[p]
        if c == 0:
            state["uk"] = u_refs[d][:, kdim * jq:kdim * (jq + 1)].astype(BF16)
        v_ref[p % 3, :, col * c:col * (c + 1)] = jnp.dot(state["uk"], bm_ref[d, jq, :, col * c:col * (c + 1)],
                                                         preferred_element_type=F32)

    def readout_piece(p, c):
        d, jq = phases[p]
        part = jnp.dot(v_ref[p % 3, :, col * c:col * (c + 1)].astype(BF16), cm_ref[d, jq, col * c:col * (c + 1), :],
                       preferred_element_type=F32)
        out = y_refs[d].at[:, kdim * jq:kdim * (jq + 1)]
        out[...] = part if c == 0 else out[...] + part

    def scan_run(p, c):
        d, jq = phases[p]
        if c == 0:
            lam = lam_ref[d, jq]
            state["lr"] = [jnp.broadcast_to(lam[0:1, LANES * t:LANES * (t + 1)], (nb, LANES)) for t in range(S5_TILES)]
            state["li"] = [jnp.broadcast_to(lam[1:2, LANES * t:LANES * (t + 1)], (nb, LANES)) for t in range(S5_TILES)]
            h0 = h_ref[d, jq]
            state["h"] = [h0[:, LANES * i:LANES * (i + 1)] for i in range(2 * S5_TILES)]
        lr, li, h = state["lr"], state["li"], state["h"]
        for s in range(run * c, run * (c + 1)):
            r0 = (s if d == 0 else tc - 1 - s) * nb
            rows = v_ref[p % 3, r0:r0 + nb, :]
            new = []
            for t in range(S5_TILES):
                vr = rows[:, 2 * LANES * t:2 * LANES * t + LANES]
                vi = rows[:, 2 * LANES * t + LANES:2 * LANES * (t + 1)]
                hr, hi = h[2 * t], h[2 * t + 1]
                new.append(lr[t] * hr - li[t] * hi + vr)
                new.append(lr[t] * hi + li[t] * hr + vi)
            v_ref[p % 3, r0:r0 + nb, :] = jnp.concatenate(new, axis=1)
            h = new
        state["h"] = h
        if c == n_sub - 1:
            h_ref[d, jq] = jnp.concatenate(h, axis=1)

    for slot in range(len(phases) + 2):
        for c in range(n_sub):
            if 0 <= slot - 1 < len(phases):
                scan_run(slot - 1, c)
            if c < S5_TILES and slot < len(phases):
                drive_piece(slot, c)
            if c >= S5_TILES and 0 <= slot - 2 < len(phases):
                readout_piece(slot - 2, c - S5_TILES)


def _s5_scan(u_tm, bm, cm, lam, nb, n_ctx):
    rows_total, w = u_tm.shape
    tc = S5_CHUNK
    r = tc * nb
    n_chunks = rows_total // r
    ncc = n_ctx // tc
    bwd = lambda i: (jnp.where(i < ncc, ncc - 1 - i, n_chunks - 1 - (i - ncc)), 0)
    nj = bm.shape[1]
    sdim = bm.shape[3]
    kern = functools.partial(_s5_scan_kernel, tc=tc, nb=nb)
    return pl.pallas_call(
        kern,
        out_shape=[jax.ShapeDtypeStruct((rows_total, w), F32)] * 2,
        grid=(n_chunks,),
        in_specs=[pl.BlockSpec((r, w), lambda i: (i, 0)), pl.BlockSpec((r, w), bwd),
                  _full(bm.shape), _full(cm.shape), _full(lam.shape)],
        out_specs=[pl.BlockSpec((r, w), lambda i: (i, 0)), pl.BlockSpec((r, w), bwd)],
        scratch_shapes=[pltpu.VMEM((3, r, sdim), F32), pltpu.VMEM((2, nj, nb, sdim), F32)],
        compiler_params=_params("arbitrary"),
        name="s5_scan",
    )(u_tm, u_tm, bm, cm, lam)


def _l1_post_kernel(x_ref, u_ref, yf_ref, yb_ref, dsk_ref, wglu_ref, bglu_ref, wout_ref, mod_ref, gf_ref,
                    wr_ref, br_ref, x3_ref, f_ref, ri_ref, rw_ref, cnt_ref, base_ref):
    y = dsk_ref[...] * u_ref[...] + yf_ref[...] + yb_ref[...]
    g = jax.nn.gelu(y, approximate=True)
    z = jnp.dot(g.astype(BF16), wglu_ref[...], preferred_element_type=F32) + bglu_ref[...]
    o = jnp.dot((g * jax.nn.sigmoid(z)).astype(BF16), wout_ref[...], preferred_element_type=F32)
    first = jnp.logical_and(pl.program_id(0) == 0, pl.program_id(1) == 0)
    _route_tail(x_ref[0], o, mod_ref[0, 0], gf_ref[...], wr_ref, br_ref, base_ref,
                x3_ref, f_ref, ri_ref, rw_ref, cnt_ref, first)


def _l1_post(x2, u_tm, yf, yb, dsk, wglu, bglu, wout, mod, gf, wr, br, n_ctx):
    b, l, d = x2.shape
    t = TOKEN_TILE
    nct = n_ctx // t
    s = l - n_ctx
    tm = pl.BlockSpec((t, S5_WIDTH), lambda bi, ti: (ti + nct, bi))
    shapes, specs = _tail_out_shapes(b, s, d)
    return pl.pallas_call(
        _l1_post_kernel,
        out_shape=shapes,
        grid=(b, s // t),
        in_specs=[pl.BlockSpec((1, t, d), lambda bi, ti: (bi, ti + nct, 0)), tm, tm, tm,
                  _full((1, S5_WIDTH)), _full((S5_WIDTH, S5_WIDTH)), _full((1, S5_WIDTH)), _full((S5_WIDTH, d)),
                  pl.BlockSpec((1, 1, 6, d), lambda bi, ti: (bi, 1, 0, 0)),
                  _full((1, d)), _full((2, d, LANES)), _full((1, LANES))],
        out_specs=specs,
        scratch_shapes=[pltpu.VMEM((8, LANES), F32)],
        compiler_params=_params("arbitrary", "arbitrary"),
        name="l1_post",
    )(x2, u_tm, yf, yb, dsk, wglu, bglu, wout, mod, gf, wr, br)


def _final_kernel(x_ref, rw_ref, slots_ref, y_ref, mod_ref, g_ref, o_ref, idx_ref, ybuf_ref, sem_idx, sem_rows):
    moe = _combined_rows(y_ref, slots_ref, rw_ref, idx_ref, ybuf_ref, sem_idx, sem_rows)
    x4 = x_ref[...] + mod_ref[0, 0][5:6] * moe
    o_ref[...] = _rms(x4) * g_ref[...]


def _final(x3, rw, slots, y, mod, g, b):
    n_tok, d = x3.shape
    t = TOKEN_TILE
    nt = n_tok // b // t
    return pl.pallas_call(
        _final_kernel,
        out_shape=jax.ShapeDtypeStruct((n_tok, d), F32),
        grid=(n_tok // t,),
        in_specs=[pl.BlockSpec((t, d), lambda i: (i, 0)), pl.BlockSpec((t, 8), lambda i: (i, 0)),
                  pl.BlockSpec(memory_space=pltpu.VMEM), pl.BlockSpec(memory_space=pl.ANY),
                  pl.BlockSpec((1, 1, 6, d), lambda i: (i // nt, 1, 0, 0)), _full((1, d))],
        out_specs=pl.BlockSpec((t, d), lambda i: (i, 0)),
        scratch_shapes=_gather_scratch(2, t),
        compiler_params=_params("arbitrary"),
        name="final_combine",
    )(x3, rw, slots, y, mod, g)


def _rope_tables(n_ctx, n_lat):
    pos = jnp.arange(n_lat, dtype=I32)
    row = (pos // GRID_W).astype(F32)
    col = (pos % GRID_W).astype(F32)
    inv_freq = jnp.power(ROPE_THETA, -jnp.arange(0, AXIS_DIM, 2, dtype=F32) / AXIS_DIM)
    lane = np.arange(LANES)
    dd = lane % HEAD_DIM
    use_col = dd >= AXIS_DIM
    first = (dd % AXIS_DIM) < AXIS_DIM // 2
    freq = inv_freq[(dd % AXIS_DIM) % (AXIS_DIM // 2)]
    ang = jnp.where(use_col[None, :], col[:, None], row[:, None]) * freq[None, :]
    cos, sin = jnp.cos(ang), jnp.sin(ang)
    sa = jnp.where(first[None, :], -sin, 0.0)
    sb = jnp.where(first[None, :], 0.0, sin)
    pad = lambda a, v: jnp.concatenate([jnp.full((n_ctx, LANES), v, F32), a], axis=0)
    return pad(cos, 1.0), pad(sa, 0.0), pad(sb, 0.0)


def _group_ones(width, group):
    idx = np.arange(width) // group
    return jnp.asarray(idx[:, None] == idx[None, :], dtype=BF16)


def _router_weights(w_grp, b_grp, w_rt, b_rt):
    d = w_grp.shape[0]
    used = N_EXPERT_GROUPS + N_EXPERTS
    wr = jnp.concatenate([w_grp, w_rt, jnp.zeros((d, LANES - used), F32)], axis=1)
    br = jnp.concatenate([b_grp, b_rt, jnp.zeros((LANES - used,), F32)]).reshape(1, LANES)
    return jnp.stack(_split_bf16(wr)), br


def kernel(x, c, ctx, c_ctx, w_mod, b_mod, norm_mix_g, norm_ffn_g, mix_w_in, mix_w_out, q_norm_g, k_norm_g, gmlp_norm_g, gmlp_w_spatial, gmlp_b_spatial, s5_w_in, s5_a_re, s5_a_im, s5_log_dt, s5_b_re, s5_b_im, s5_c_re, s5_c_im, s5_d, s5_w_glu, s5_b_glu, s5_w_out, moe_w_group, moe_b_group, moe_w_router, moe_b_router, moe_w1, moe_w3, moe_w2, final_norm_g):
    b, s, d = x.shape
    n_ctx = ctx.shape[1]
    l = n_ctx + s
    assert w_mod.shape[0] == 2, "two layers: attention/gMLP then S5"
    assert n_ctx % TOKEN_TILE == 0 and s % TOKEN_TILE == 0 and b % 8 == 0 and d == D_MODEL
    nct = n_ctx // TOKEN_TILE

    r = (b + 1 + 7) // 8 * 8
    cond = jnp.concatenate([c, c_ctx[None, :], jnp.zeros((r - b - 1, d), F32)], axis=0)
    mods = _adaln(cond, w_mod, b_mod)
    lat = mods[:, :b].reshape(2, b, 1, 6, d)
    cxt = jnp.broadcast_to(mods[:, b].reshape(2, 1, 1, 6, d), (2, b, 1, 6, d))
    mod = jnp.concatenate([cxt, lat], axis=2)

    w_in = jnp.concatenate([mix_w_in[0][:, _Q_COL_PERM], mix_w_in[0][:, ATTN_WIDTH:]], axis=1).astype(BF16)
    w_out = jnp.concatenate([mix_w_out[0][_Q_COL_PERM], mix_w_out[0][ATTN_WIDTH:]], axis=0).astype(BF16)
    cos, sa, sb = _rope_tables(n_ctx, s)
    qg = jnp.tile(q_norm_g[0], N_Q_HEADS).reshape(1, ATTN_WIDTH)
    kg = jnp.tile(k_norm_g[0], N_KV_HEADS).reshape(1, KV_WIDTH)
    q, k, v, gu, gv = _l0_proj(ctx, x, mod[0], norm_mix_g[0].reshape(1, d), w_in, qg, kg,
                               gmlp_norm_g[0].reshape(1, GMLP_WIDTH), cos, sa, sb,
                               _group_ones(ATTN_WIDTH, HEAD_DIM), _group_ones(KV_WIDTH, HEAD_DIM), nct)
    o = _attention(q, k, v, n_ctx)
    bsp = jnp.repeat(gmlp_b_spatial[0].T, GMLP_GROUP_DIM, axis=1)
    wr0, br0 = _router_weights(moe_w_group[0], moe_b_group[0], moe_w_router[0], moe_b_router[0])
    x1, f0, ri0, rw0, cnt0 = _l0_post(ctx, x, o, gu, gv, gmlp_w_spatial[0].astype(BF16), bsp, w_out, mod[0],
                                      norm_ffn_g[0].reshape(1, d), wr0, br0, nct)
    y0, slots0 = _moe(f0, ri0, cnt0, moe_w1, moe_w3, moe_w2, 0)
    x2, u_tm = _l0_combine(x1.reshape(b * l, d), rw0, slots0, y0, mod[0], mod[1], norm_mix_g[1].reshape(1, d),
                           s5_w_in[0].astype(BF16), b, nct)

    lre, lim, bbre, bbim = _s5_zoh(s5_a_re[0], s5_a_im[0], s5_log_dt[0], s5_b_re[0], s5_b_im[0])
    bm, cm, lam = _s5_matrices(lre, lim, bbre, bbim, s5_c_re[0], s5_c_im[0])
    yf, yb = _s5_scan(u_tm.reshape(l * b, S5_WIDTH), bm, cm, lam, b, n_ctx)
    wr1, br1 = _router_weights(moe_w_group[1], moe_b_group[1], moe_w_router[1], moe_b_router[1])
    tmv = lambda a: a.reshape(l, b * S5_WIDTH)
    x3, f1, ri1, rw1, cnt1 = _l1_post(x2.reshape(b, l, d), u_tm, tmv(yf), tmv(yb), s5_d[0].reshape(1, S5_WIDTH),
                                      s5_w_glu[0].astype(BF16), s5_b_glu[0].reshape(1, S5_WIDTH),
                                      s5_w_out[0].astype(BF16), mod[1], norm_ffn_g[1].reshape(1, d), wr1, br1, n_ctx)
    y1, slots1 = _moe(f1, ri1, cnt1, moe_w1, moe_w3, moe_w2, 1)
    out = _final(x3.reshape(b * s, d), rw1, slots1, y1, mod[1], final_norm_g.reshape(1, d), b)
    return out.reshape(b, s, d)
```

```python
import functools

import numpy as np
import jax
import jax.numpy as jnp
from jax import lax
from jax.experimental import pallas as pl
from jax.experimental.pallas import tpu as pltpu

F32 = jnp.float32
BF16 = jnp.bfloat16
I32 = jnp.int32

EPS = 1e-6
GRID_W = 64
HEAD_DIM = 64
N_Q_HEADS = 8
N_KV_HEADS = 2
ATTN_WIDTH = N_Q_HEADS * HEAD_DIM
KV_WIDTH = N_KV_HEADS * HEAD_DIM
AXIS_DIM = HEAD_DIM // 2
ROPE_THETA = 10000.0
GMLP_GROUPS = 8
GMLP_GROUP_DIM = 64
GMLP_WIDTH = GMLP_GROUPS * GMLP_GROUP_DIM
GMLP_CHUNK = 128
MIX_IN_WIDTH = ATTN_WIDTH + 2 * KV_WIDTH + 2 * GMLP_WIDTH
S5_GROUPS = 32
S5_GROUP_DIM = 16
S5_STATE = 64
S5_WIDTH = S5_GROUPS * S5_GROUP_DIM
N_EXPERT_GROUPS = 4
EXPERTS_PER_GROUP = 8
N_EXPERTS = N_EXPERT_GROUPS * EXPERTS_PER_GROUP
EXPERT_HIDDEN = 512
MOE_BLOCK = 512

D_MODEL = 1024
LANES = 128
ROW_SUB = D_MODEL // LANES
TOKEN_TILE = 256
ROW_DMA_UNROLL = 8
ATTN_KEY_BLOCK = 128
V_ONES_ROWS = 16
V_AUG_ROWS = N_KV_HEADS * (HEAD_DIM + V_ONES_ROWS)
S5_CHUNK = 64
S5_TILES = 4
VMEM_LIMIT = 48 * 1024 * 1024
NEG_BIG = -1e30

_Q_HEAD_ORDER = (0, 4, 1, 5, 2, 6, 3, 7)
_Q_COL_PERM = np.concatenate([np.arange(HEAD_DIM) + HEAD_DIM * h for h in _Q_HEAD_ORDER])


def _params(*sem):
    return pltpu.CompilerParams(dimension_semantics=sem, vmem_limit_bytes=VMEM_LIMIT)


def _rms(x):
    return x * lax.rsqrt(jnp.mean(x * x, axis=-1, keepdims=True) + EPS)


def _modulate(x, g, shift, scale):
    return _rms(x) * g * (1.0 + scale) + shift


def _full(shape):
    return pl.BlockSpec(shape, lambda *_: (0,) * len(shape))


def _split_bf16(a):
    hi = a.astype(BF16)
    return hi, (a - hi.astype(F32)).astype(BF16)


def _dot_split(a, w_hi, w_lo):
    a_hi, a_lo = _split_bf16(a)
    dot = functools.partial(jnp.dot, preferred_element_type=F32)
    return dot(a_hi, w_hi) + (dot(a_lo, w_hi) + dot(a_hi, w_lo))


def _store_row_tiles(ref, val):
    n = val.shape[0]
    for s in range(ROW_SUB):
        ref[pl.ds(s, n, stride=ROW_SUB), :] = val[:, LANES * s:LANES * (s + 1)]


def _load_row_tiles(ref):
    n = ref.shape[0] // ROW_SUB
    return jnp.concatenate([ref[pl.ds(s, n, stride=ROW_SUB), :] for s in range(ROW_SUB)], axis=1)


def _adaln_kernel(cond_ref, w_ref, b_ref, o_ref):
    c = cond_ref[...]
    s = c * jax.nn.sigmoid(c)
    o_ref[0] = _dot_split(s, *_split_bf16(w_ref[0])) + b_ref[0]


def _adaln(cond, w_mod, b_mod):
    depth, d, n = w_mod.shape
    r = cond.shape[0]
    tn = 1536
    return pl.pallas_call(
        _adaln_kernel,
        out_shape=jax.ShapeDtypeStruct((depth, r, n), F32),
        grid=(depth, n // tn),
        in_specs=[pl.BlockSpec((r, d), lambda l, j: (0, 0)),
                  pl.BlockSpec((1, d, tn), lambda l, j: (l, 0, j)),
                  pl.BlockSpec((1, 1, tn), lambda l, j: (l, 0, j))],
        out_specs=pl.BlockSpec((1, r, tn), lambda l, j: (l, 0, j)),
        compiler_params=_params("parallel", "parallel"),
        name="adaln",
    )(cond, w_mod, b_mod.reshape(depth, 1, n))


def _seq_tile(ctx_ref, x_ref, nct):
    return jnp.where(pl.program_id(1) < nct, ctx_ref[0], x_ref[0])


def _seq_specs(t, d, nct):
    return [pl.BlockSpec((1, t, d), lambda bi, ti: (bi, jnp.minimum(ti, nct - 1), 0)),
            pl.BlockSpec((1, t, d), lambda bi, ti: (bi, jnp.maximum(ti - nct, 0), 0))]


def _l0_proj_kernel(ctx_ref, x_ref, mod_ref, g_ref, win_ref, qg_ref, kg_ref, lng_ref, cos_ref, sa_ref, sb_ref,
                    gq_ref, gk_ref, q_ref, k_ref, v_ref, gu_ref, gv_ref, *, nct):
    m = mod_ref[0, 0]
    h = _modulate(_seq_tile(ctx_ref, x_ref, nct), g_ref[...], m[0:1], m[1:2]).astype(BF16)
    z = jnp.dot(h, win_ref[...], preferred_element_type=F32)
    cos, sa, sb = cos_ref[...], sa_ref[...], sb_ref[...]

    def head_norm_rope(t, gsum_ref, gain, reps):
        w = t.shape[1]
        ss = jnp.dot((t * t).astype(BF16), gsum_ref[...], preferred_element_type=F32)
        tn = t * lax.rsqrt(ss * (1.0 / HEAD_DIM) + EPS) * gain
        c, a, b = (jnp.concatenate([tab] * reps, axis=1) if reps > 1 else tab for tab in (cos, sa, sb))
        return tn * c + pltpu.roll(tn, w - AXIS_DIM // 2, 1) * a + pltpu.roll(tn, AXIS_DIM // 2, 1) * b

    q = head_norm_rope(z[:, :ATTN_WIDTH], gq_ref, qg_ref[...], ATTN_WIDTH // LANES)
    q_ref[0] = (q * (HEAD_DIM ** -0.5)).T.astype(BF16)
    k = head_norm_rope(z[:, ATTN_WIDTH:ATTN_WIDTH + KV_WIDTH], gk_ref, kg_ref[...], 1)
    k_ref[0] = k.astype(BF16)
    vt = z[:, ATTN_WIDTH + KV_WIDTH:ATTN_WIDTH + 2 * KV_WIDTH].T
    ones = jnp.ones((V_ONES_ROWS, vt.shape[1]), F32)
    v_ref[0] = jnp.concatenate([part for j in range(N_KV_HEADS)
                                for part in (vt[HEAD_DIM * j:HEAD_DIM * (j + 1)], ones)], axis=0).astype(BF16)
    off = ATTN_WIDTH + 2 * KV_WIDTH
    gu_ref[0] = jax.nn.gelu(z[:, off:off + GMLP_WIDTH], approximate=True).astype(BF16)
    gv = jax.nn.gelu(z[:, off + GMLP_WIDTH:], approximate=True)
    mu = jnp.mean(gv, axis=-1, keepdims=True)
    gc = gv - mu
    gn = gc * lax.rsqrt(jnp.mean(gc * gc, axis=-1, keepdims=True) + EPS) * lng_ref[...]
    gv_ref[0] = gn.astype(BF16)


def _l0_proj(ctx, x, mod, g, w_in, qg, kg, lng, cos, sa, sb, gq, gk, nct):
    b, s, d = x.shape
    l = s + ctx.shape[1]
    t = TOKEN_TILE
    tok = lambda w: pl.BlockSpec((1, t, w), lambda bi, ti: (bi, ti, 0))
    tab = pl.BlockSpec((t, LANES), lambda bi, ti: (ti, 0))
    tok_t = lambda w: pl.BlockSpec((1, w, t), lambda bi, ti: (bi, 0, ti))
    outs = ([jax.ShapeDtypeStruct((b, ATTN_WIDTH, l), BF16), jax.ShapeDtypeStruct((b, l, KV_WIDTH), BF16),
             jax.ShapeDtypeStruct((b, V_AUG_ROWS, l), BF16)]
            + [jax.ShapeDtypeStruct((b, l, GMLP_WIDTH), BF16)] * 2)
    return pl.pallas_call(
        functools.partial(_l0_proj_kernel, nct=nct),
        out_shape=outs,
        grid=(b, l // t),
        in_specs=_seq_specs(t, d, nct) + [
                  pl.BlockSpec((1, 1, 6, d), lambda bi, ti: (bi, jnp.where(ti >= nct, 1, 0), 0, 0)),
                  _full((1, d)), _full((d, MIX_IN_WIDTH)), _full((1, ATTN_WIDTH)), _full((1, KV_WIDTH)),
                  _full((1, GMLP_WIDTH)), tab, tab, tab,
                  _full((ATTN_WIDTH, ATTN_WIDTH)), _full((KV_WIDTH, KV_WIDTH))],
        out_specs=[tok_t(ATTN_WIDTH), tok(KV_WIDTH), tok_t(V_AUG_ROWS), tok(GMLP_WIDTH), tok(GMLP_WIDTH)],
        compiler_params=_params("parallel", "parallel"),
        name="l0_proj",
    )(ctx, x, mod, g, w_in, qg, kg, lng, cos, sa, sb, gq, gk)


def _attn_kernel(qt_ref, k_ref, vt_ref, o_ref, s_ref, p_ref, *, n_ctx, nct):
    ti = pl.program_id(1)
    tq = qt_ref.shape[2]
    row = lax.broadcasted_iota(I32, (LANES, tq), 0)
    low = row < HEAD_DIM
    kb = ATTN_KEY_BLOCK
    va = HEAD_DIM + V_ONES_ROWS

    def attend(nk):
        heads = [(m, j) for m in range(ATTN_WIDTH // LANES) for j in range(2)]
        blocks = [slice(kb * b, kb * (b + 1)) for b in range(nk // kb)]
        state = {}

        def start(h):
            m, j = heads[h]
            qp = qt_ref[0, LANES * m:LANES * (m + 1), :]
            state[h] = dict(qm=jnp.where(low if j == 0 else jnp.logical_not(low), qp, jnp.zeros_like(qp)), mx=None)

        def score_block(h, blk):
            s = jnp.dot(k_ref[0, blk, :], state[h]["qm"], preferred_element_type=F32)
            s_ref[h % 2, blk, :] = s
            bm = jnp.max(s, axis=0, keepdims=True)
            state[h]["mx"] = bm if state[h]["mx"] is None else jnp.maximum(state[h]["mx"], bm)

        def exp_block(h, blk):
            p_ref[h % 2, blk, :] = jnp.exp(s_ref[h % 2, blk, :] - state[h]["mx"]).astype(BF16)

        def finish(h):
            j = heads[h][1]
            pv = jnp.dot(vt_ref[0, va * j:va * (j + 1), :nk], p_ref[h % 2, :nk, :],
                         preferred_element_type=F32)
            return pv[:HEAD_DIM] / pv[HEAD_DIM:HEAD_DIM + 1]

        start(0)
        for blk in blocks:
            score_block(0, blk)
        halves = []
        for h in range(len(heads)):
            if h + 1 < len(heads):
                start(h + 1)
            for blk in blocks:
                if h + 1 < len(heads):
                    score_block(h + 1, blk)
                exp_block(h, blk)
            halves.append(finish(h))
            if heads[h][1] == 1:
                m = heads[h][0]
                o_ref[0, :, LANES * m:LANES * (m + 1)] = jnp.concatenate(halves, axis=0).T.astype(BF16)
                halves = []

    @pl.when(ti < nct)
    def _():
        attend(n_ctx)

    @pl.when(ti >= nct)
    def _():
        attend(k_ref.shape[1])


def _attention(qt, k, vt, n_ctx):
    b, l, _ = k.shape
    t = TOKEN_TILE
    kern = functools.partial(_attn_kernel, n_ctx=n_ctx, nct=n_ctx // t)
    return pl.pallas_call(
        kern,
        out_shape=jax.ShapeDtypeStruct((b, l, ATTN_WIDTH), BF16),
        grid=(b, l // t),
        in_specs=[pl.BlockSpec((1, ATTN_WIDTH, t), lambda bi, ti: (bi, 0, ti)),
                  pl.BlockSpec((1, l, KV_WIDTH), lambda bi, ti: (bi, 0, 0)),
                  pl.BlockSpec((1, V_AUG_ROWS, l), lambda bi, ti: (bi, 0, 0))],
        out_specs=pl.BlockSpec((1, t, ATTN_WIDTH), lambda bi, ti: (bi, ti, 0)),
        scratch_shapes=[pltpu.VMEM((2, l, t), F32), pltpu.VMEM((2, l, t), BF16)],
        compiler_params=_params("parallel", "parallel"),
        name="attention",
    )(qt, k, vt)


def _route_tail(x, y, m, gf, wr_ref, br_ref, base_ref, x_ref, f_ref, ri_ref, rw_ref, cnt_ref, first):
    x1 = x + m[2:3] * y
    x_ref[0] = x1
    f = _modulate(x1, gf, m[3:4], m[4:5])
    _store_row_tiles(f_ref, f)
    t = f.shape[0]
    logits = _dot_split(f, wr_ref[0], wr_ref[1]) + br_ref[...]
    lane = lax.broadcasted_iota(I32, (t, LANES), 1).astype(F32)
    is_grp = lane < N_EXPERT_GROUPS
    gl = jnp.where(is_grp, logits, NEG_BIG)
    gmax = jnp.max(gl, axis=-1, keepdims=True)
    g_idx = jnp.min(jnp.where(gl == gmax, lane, float(LANES)), axis=-1, keepdims=True)
    g_w = 1.0 / jnp.sum(jnp.where(is_grp, jnp.exp(gl - gmax), 0.0), axis=-1, keepdims=True)
    e_lo = N_EXPERT_GROUPS + EXPERTS_PER_GROUP * g_idx
    in_grp = jnp.logical_and(lane >= e_lo, lane < e_lo + EXPERTS_PER_GROUP)
    sel = jnp.where(in_grp, logits, NEG_BIG)
    v1 = jnp.max(sel, axis=-1, keepdims=True)
    i1 = jnp.min(jnp.where(sel == v1, lane, float(LANES)), axis=-1, keepdims=True)
    sel2 = jnp.where(lane == i1, NEG_BIG, sel)
    v2 = jnp.max(sel2, axis=-1, keepdims=True)
    i2 = jnp.min(jnp.where(sel2 == v2, lane, float(LANES)), axis=-1, keepdims=True)
    e = jnp.exp(v2 - v1)
    w1 = g_w / (1.0 + e)
    w2 = g_w * e / (1.0 + e)
    eid1 = i1 - N_EXPERT_GROUPS
    eid2 = i2 - N_EXPERT_GROUPS

    @pl.when(first)
    def _():
        base_ref[...] = jnp.zeros_like(base_ref)

    hit1 = lane == eid1
    hit2 = lane == eid2
    cnt = (hit1.astype(F32) + hit2.astype(F32))
    row = lax.broadcasted_iota(I32, (t, t), 0)
    col = lax.broadcasted_iota(I32, (t, t), 1)
    tril = jnp.where(col < row, 1.0, 0.0).astype(BF16)
    before = jnp.dot(tril, cnt.astype(BF16), preferred_element_type=F32) + base_ref[0:1, :]
    rank1 = jnp.sum(jnp.where(hit1, before, 0.0), axis=-1, keepdims=True)
    rank2 = jnp.sum(jnp.where(hit2, before, 0.0), axis=-1, keepdims=True)
    total = base_ref[0:1, :] + jnp.sum(cnt, axis=0, keepdims=True)
    base_ref[...] = jnp.broadcast_to(total, base_ref.shape)
    cnt_ref[...] = jnp.broadcast_to(total, cnt_ref.shape)

    ri = jnp.where(lane == 0, eid1, jnp.where(lane == 1, eid2, jnp.where(lane == 2, rank1,
                   jnp.where(lane == 3, rank2, 0.0))))
    ri_ref[...] = ri.T[0:8, :].astype(I32)
    l8 = lax.broadcasted_iota(I32, (t, 8), 1)
    rw_ref[...] = jnp.where(l8 == 0, w1, jnp.where(l8 == 1, w2, 0.0))


def _tail_out_shapes(b, n_rows, d):
    t = TOKEN_TILE
    shapes = [jax.ShapeDtypeStruct((b, n_rows, d), F32), jax.ShapeDtypeStruct((b * n_rows * ROW_SUB, LANES), F32),
              jax.ShapeDtypeStruct((8, b * n_rows), I32), jax.ShapeDtypeStruct((b * n_rows, 8), F32),
              jax.ShapeDtypeStruct((8, LANES), F32)]
    nt = n_rows // t
    flat = lambda w: pl.BlockSpec((t, w), lambda bi, ti: (bi * nt + ti, 0))
    specs = [pl.BlockSpec((1, t, d), lambda bi, ti: (bi, ti, 0)),
             pl.BlockSpec((t * ROW_SUB, LANES), lambda bi, ti: (bi * nt + ti, 0)),
             pl.BlockSpec((8, t), lambda bi, ti: (0, bi * nt + ti)), flat(8),
             pl.BlockSpec((8, LANES), lambda bi, ti: (0, 0))]
    return shapes, specs


def _l0_post_kernel(ctx_ref, x_ref, o_ref, gu_ref, gv_ref, wsp_ref, bsp_ref, wout_ref, mod_ref, gf_ref, wr_ref,
                    br_ref, x1_ref, f_ref, ri_ref, rw_ref, cnt_ref, base_ref, *, nct):
    tq = x_ref.shape[1]
    lane = lax.broadcasted_iota(I32, (GMLP_CHUNK, LANES), 1)
    low = lane < GMLP_GROUP_DIM
    gated = []
    for c in range(tq // GMLP_CHUNK):
        rows = slice(c * GMLP_CHUNK, (c + 1) * GMLP_CHUNK)
        mixed = []
        for m in range(GMLP_WIDTH // LANES):
            pair = gv_ref[0, rows, LANES * m:LANES * (m + 1)]
            a = jnp.dot(wsp_ref[2 * m], pair, preferred_element_type=F32)
            b = jnp.dot(wsp_ref[2 * m + 1], pair, preferred_element_type=F32)
            mixed.append(jnp.where(low, a, b))
        mixed = jnp.concatenate(mixed, axis=1) + bsp_ref[...]
        gated.append((gu_ref[0, rows, :].astype(F32) * mixed).astype(BF16))
    gated = jnp.concatenate(gated, axis=0)
    y = (jnp.dot(o_ref[0], wout_ref[:ATTN_WIDTH, :], preferred_element_type=F32)
         + jnp.dot(gated, wout_ref[ATTN_WIDTH:, :], preferred_element_type=F32))
    first = jnp.logical_and(pl.program_id(0) == 0, pl.program_id(1) == 0)
    _route_tail(_seq_tile(ctx_ref, x_ref, nct), y, mod_ref[0, 0], gf_ref[...], wr_ref, br_ref, base_ref,
                x1_ref, f_ref, ri_ref, rw_ref, cnt_ref, first)


def _l0_post(ctx, x, o, gu, gv, wsp, bsp, wout, mod, gf, wr, br, nct):
    b, s, d = x.shape
    l = s + ctx.shape[1]
    t = TOKEN_TILE
    tok = lambda w: pl.BlockSpec((1, t, w), lambda bi, ti: (bi, ti, 0))
    shapes, specs = _tail_out_shapes(b, l, d)
    return pl.pallas_call(
        functools.partial(_l0_post_kernel, nct=nct),
        out_shape=shapes,
        grid=(b, l // t),
        in_specs=_seq_specs(t, d, nct) + [tok(ATTN_WIDTH), tok(GMLP_WIDTH), tok(GMLP_WIDTH),
                  _full((GMLP_GROUPS, GMLP_CHUNK, GMLP_CHUNK)), _full((GMLP_CHUNK, GMLP_WIDTH)),
                  _full((ATTN_WIDTH + GMLP_WIDTH, d)),
                  pl.BlockSpec((1, 1, 6, d), lambda bi, ti: (bi, jnp.where(ti >= nct, 1, 0), 0, 0)),
                  _full((1, d)), _full((2, d, LANES)), _full((1, LANES))],
        out_specs=specs,
        scratch_shapes=[pltpu.VMEM((8, LANES), F32)],
        compiler_params=_params("arbitrary", "arbitrary"),
        name="l0_post",
    )(ctx, x, o, gu, gv, wsp, bsp, wout, mod, gf, wr, br)


def _row_copy(src_ref, src_row, dst_ref, dst_row, sem):
    rows = lambda ref, r: ref.at[pl.ds(pl.multiple_of(r * ROW_SUB, ROW_SUB), ROW_SUB)]
    return pltpu.make_async_copy(rows(src_ref, src_row), rows(dst_ref, dst_row), sem)


def _issue_rows(src_ref, index_of, buf_ref, sem_rows, s):
    n_str, rows = buf_ref.shape[1], buf_ref.shape[2] // ROW_SUB
    slot = s % 2

    def issue(j, carry):
        for half in range(2):
            r = 2 * j + half
            for k in range(n_str):
                _row_copy(src_ref, index_of(k * rows + r), buf_ref.at[slot, k], r,
                          sem_rows.at[slot]).start(priority=(k + half) % 2)
        return carry

    lax.fori_loop(0, rows // 2, issue, 0, unroll=ROW_DMA_UNROLL // 2)


def _wait_rows(src_ref, buf_ref, sem_rows, s):
    n_str, rows = buf_ref.shape[1], buf_ref.shape[2] // ROW_SUB
    slot = s % 2
    for k in range(n_str):
        pltpu.make_async_copy(src_ref.at[pl.ds(0, rows * ROW_SUB)], buf_ref.at[slot, k], sem_rows.at[slot]).wait()
    return slot


def _gather_pipeline(src_ref, table_ref, idx_ref, buf_ref, sem_idx, sem_rows, step, n_steps):
    def idx_copy(s):
        return pltpu.make_async_copy(table_ref.at[s], idx_ref.at[s % 2], sem_idx.at[s % 2])

    def issue_rows(s):
        _issue_rows(src_ref, lambda pos: idx_ref[s % 2, pos], buf_ref, sem_rows, s)

    @pl.when(step == 0)
    def _():
        idx_copy(0).start()
        idx_copy(0).wait()
        issue_rows(0)

        @pl.when(n_steps > 1)
        def _():
            idx_copy(1).start()

    @pl.when(step + 1 < n_steps)
    def _():
        idx_copy(step + 1).wait()
        issue_rows(step + 1)

        @pl.when(step + 2 < n_steps)
        def _():
            idx_copy(step + 2).start()

    return _wait_rows(src_ref, buf_ref, sem_rows, step)


def _gather_scratch(n_str, rows):
    return [pltpu.SMEM((2, n_str * rows), I32), pltpu.VMEM((2, n_str, rows * ROW_SUB, LANES), F32),
            pltpu.SemaphoreType.DMA((2,)), pltpu.SemaphoreType.DMA((2,))]


def _dispatch_kernel(pad_ref, nu_ref, f_ref, slots_ref, xb_ref, idx_ref, zero_ref, sem_idx, sem_rows):
    i = pl.program_id(0)
    t = f_ref.shape[0] // ROW_SUB
    blk = MOE_BLOCK * ROW_SUB
    idx_cp = pltpu.make_async_copy(slots_ref.at[i], idx_ref, sem_idx)
    idx_cp.start()

    @pl.when(i == 0)
    def _():
        zero_ref[...] = jnp.zeros_like(zero_ref)
        n_blocks = xb_ref.shape[0] // blk

        def zero_block(start):
            return pltpu.make_async_copy(zero_ref, xb_ref.at[pl.ds(pl.multiple_of(start * ROW_SUB, blk), blk)],
                                         sem_rows)

        def tail_start(j, carry):
            zero_block(j * MOE_BLOCK).start()
            return carry

        def tail_wait(j, carry):
            zero_block(0).wait()
            return carry

        for e in range(N_EXPERTS):
            zero_block(jnp.maximum(pad_ref[N_EXPERTS + e] - MOE_BLOCK, 0)).start()
        lax.fori_loop(nu_ref[0], n_blocks, tail_start, 0)
        for e in range(N_EXPERTS):
            zero_block(0).wait()
        lax.fori_loop(nu_ref[0], n_blocks, tail_wait, 0)

    idx_cp.wait()

    def issue(r, carry):
        _row_copy(f_ref, r, xb_ref, idx_ref[r], sem_rows).start(priority=0)
        _row_copy(f_ref, r, xb_ref, idx_ref[t + r], sem_rows).start(priority=1)
        return carry

    lax.fori_loop(0, t, issue, 0, unroll=ROW_DMA_UNROLL)
    for _ in range(2):
        pltpu.make_async_copy(f_ref, xb_ref.at[pl.ds(0, t * ROW_SUB)], sem_rows).wait()


def _dispatch(f, slots, pad, n_used, n_slots):
    t = TOKEN_TILE
    n_tok = f.shape[0] // ROW_SUB
    return pl.pallas_call(
        _dispatch_kernel,
        out_shape=jax.ShapeDtypeStruct((n_slots * ROW_SUB, LANES), F32),
        grid_spec=pltpu.PrefetchScalarGridSpec(
            num_scalar_prefetch=2,
            grid=(n_tok // t,),
            in_specs=[pl.BlockSpec((t * ROW_SUB, LANES), lambda i, p, nu: (i, 0)),
                      pl.BlockSpec(memory_space=pltpu.VMEM)],
            out_specs=pl.BlockSpec(memory_space=pl.ANY),
            scratch_shapes=[pltpu.SMEM((2 * t,), I32), pltpu.VMEM((MOE_BLOCK * ROW_SUB, LANES), F32),
                            pltpu.SemaphoreType.DMA, pltpu.SemaphoreType.DMA]),
        compiler_params=_params("arbitrary"),
        name="moe_dispatch",
    )(pad, n_used, f, slots)


def _experts_kernel(be_ref, nu_ref, xb_ref, w1_ref, w3_ref, w2_ref, y_ref, w1b_ref, w3b_ref, w2b_ref):
    i = pl.program_id(0)

    @pl.when(jnp.logical_or(i == 0, be_ref[i] != be_ref[jnp.maximum(i - 1, 0)]))
    def _():
        w1b_ref[...] = w1_ref[...].astype(BF16)
        w3b_ref[...] = w3_ref[...].astype(BF16)
        w2b_ref[...] = w2_ref[...].astype(BF16)

    @pl.when(i < nu_ref[0])
    def _():
        x = _load_row_tiles(xb_ref).astype(BF16)
        h1 = jnp.dot(x, w1b_ref[...], preferred_element_type=F32)
        h3 = jnp.dot(x, w3b_ref[...], preferred_element_type=F32)
        a = (h1 * jax.nn.sigmoid(h1) * h3).astype(BF16)
        _store_row_tiles(y_ref, jnp.dot(a, w2b_ref[...], preferred_element_type=F32))

    @pl.when(i >= nu_ref[0])
    def _():
        y_ref[...] = jnp.zeros_like(y_ref)


def _experts(xb, w1, w3, w2, layer, block_e, n_used):
    n_blocks = block_e.shape[0]
    _, _, d, hid = w1.shape
    rb = MOE_BLOCK * ROW_SUB
    wspec = lambda s: pl.BlockSpec((None, None) + s, lambda i, be, nu: (layer, be[i], 0, 0))
    return pl.pallas_call(
        _experts_kernel,
        out_shape=jax.ShapeDtypeStruct((n_blocks * rb, LANES), F32),
        grid_spec=pltpu.PrefetchScalarGridSpec(
            num_scalar_prefetch=2,
            grid=(n_blocks,),
            in_specs=[pl.BlockSpec((rb, LANES), lambda i, be, nu: (jnp.minimum(i, nu[0] - 1), 0)),
                      wspec((d, hid)), wspec((d, hid)), wspec((hid, d))],
            out_specs=pl.BlockSpec((rb, LANES), lambda i, be, nu: (i, 0)),
            scratch_shapes=[pltpu.VMEM((d, hid), BF16), pltpu.VMEM((d, hid), BF16), pltpu.VMEM((hid, d), BF16)]),
        compiler_params=_params("arbitrary"),
        name="moe_experts",
    )(block_e, n_used, xb, w1, w3, w2)


def _moe_plan(route_i, counts, n_tok):
    eid, rank = route_i[0:2], route_i[2:4]
    cnt = counts[0, :N_EXPERTS].astype(I32)
    padded = (cnt + MOE_BLOCK - 1) // MOE_BLOCK * MOE_BLOCK
    pend = jnp.cumsum(padded)
    pstart = pend - padded
    experts = jnp.arange(N_EXPERTS, dtype=I32)
    slots = rank + jnp.sum(jnp.where(eid[:, :, None] == experts, pstart, 0), axis=-1)
    n_assign = 2 * n_tok
    n_slots = (n_assign + MOE_BLOCK - 1) // MOE_BLOCK * MOE_BLOCK + N_EXPERTS * MOE_BLOCK
    n_blocks = n_slots // MOE_BLOCK
    bstart = jnp.arange(n_blocks, dtype=I32) * MOE_BLOCK
    block_e = jnp.minimum(jnp.sum(bstart[:, None] >= pend[None, :], axis=1), N_EXPERTS - 1).astype(I32)
    n_used = pend[-1] // MOE_BLOCK
    last_e = block_e[jnp.maximum(n_used - 1, 0)]
    block_e = jnp.where(jnp.arange(n_blocks) < n_used, block_e, last_e).astype(I32)
    slots = slots.reshape(2, n_tok // TOKEN_TILE, TOKEN_TILE).transpose(1, 0, 2)
    slots = slots.reshape(n_tok // TOKEN_TILE, 2 * TOKEN_TILE).astype(I32)
    pad = jnp.concatenate([pstart + cnt, pend]).astype(I32)
    return slots, block_e, n_used.reshape(1).astype(I32), pad, n_slots


def _moe(f, route_i, counts, w1, w3, w2, layer):
    n_tok = f.shape[0] // ROW_SUB
    slots, block_e, n_used, pad, n_slots = _moe_plan(route_i, counts, n_tok)
    xb = _dispatch(f, slots, pad, n_used, n_slots)
    y = _experts(xb, w1, w3, w2, layer, block_e, n_used)
    return y, slots


def _combined_rows(y_ref, slots_ref, rw_ref, idx_ref, ybuf_ref, sem_idx, sem_rows):
    slot = _gather_pipeline(y_ref, slots_ref, idx_ref, ybuf_ref, sem_idx, sem_rows,
                            pl.program_id(0), pl.num_programs(0))
    rw = rw_ref[...]
    w1, w2 = rw[:, 0:1], rw[:, 1:2]
    t = ybuf_ref.shape[2] // ROW_SUB
    part = lambda k, s: ybuf_ref[slot, k, pl.ds(s, t, stride=ROW_SUB), :]
    return jnp.concatenate([w1 * part(0, s) + w2 * part(1, s) for s in range(ROW_SUB)], axis=1)


def _l0_combine_kernel(x_ref, rw_ref, slots_ref, y_ref, mod0_ref, mod1_ref, g_ref, win_ref,
                       x2_ref, u_ref, ub_ref, idx_ref, ybuf_ref, sem_idx, sem_rows):
    moe = _combined_rows(y_ref, slots_ref, rw_ref, idx_ref, ybuf_ref, sem_idx, sem_rows)
    x2 = x_ref[...] + mod0_ref[0, 0][5:6] * moe
    x2_ref[...] = x2
    m1 = mod1_ref[0, 0]
    h = _modulate(x2, g_ref[...], m1[0:1], m1[1:2]).astype(BF16)
    u = jnp.dot(h, win_ref[...], preferred_element_type=F32)
    u_ref[...] = u
    ub_ref[...] = u.astype(BF16)


def _l0_combine(x1, rw, slots, y, mod0, mod1, g, win, b, nct):
    n_tok, d = x1.shape
    t = TOKEN_TILE
    nt = n_tok // b // t
    l = n_tok // b
    modspec = pl.BlockSpec((1, 1, 6, d), lambda i: (i // nt, jnp.where(i % nt >= nct, 1, 0), 0, 0))
    return pl.pallas_call(
        _l0_combine_kernel,
        out_shape=[jax.ShapeDtypeStruct((n_tok, d), F32), jax.ShapeDtypeStruct((l, b * S5_WIDTH), F32),
                   jax.ShapeDtypeStruct((l, b * S5_WIDTH), BF16)],
        grid=(n_tok // t,),
        in_specs=[pl.BlockSpec((t, d), lambda i: (i, 0)), pl.BlockSpec((t, 8), lambda i: (i, 0)),
                  pl.BlockSpec(memory_space=pltpu.VMEM), pl.BlockSpec(memory_space=pl.ANY),
                  modspec, modspec, _full((1, d)), _full((d, S5_WIDTH))],
        out_specs=[pl.BlockSpec((t, d), lambda i: (i, 0)),
                   pl.BlockSpec((t, S5_WIDTH), lambda i: (i % nt, i // nt)),
                   pl.BlockSpec((t, S5_WIDTH), lambda i: (i % nt, i // nt))],
        scratch_shapes=_gather_scratch(2, t),
        compiler_params=_params("arbitrary"),
        name="l0_combine",
    )(x1, rw, slots, y, mod0, mod1, g, win)


def _s5_zoh_kernel(are_ref, aim_ref, ldt_ref, bre_ref, bim_ref, lre_ref, lim_ref, bbre_ref, bbim_ref):
    ar, ai = are_ref[...], aim_ref[...]
    dt = jnp.exp(ldt_ref[...])
    mag = jnp.exp(ar * dt)
    lr = mag * jnp.cos(ai * dt)
    li = mag * jnp.sin(ai * dt)
    lre_ref[...] = lr
    lim_ref[...] = li
    nr, ni = lr - 1.0, li
    den = ar * ar + ai * ai
    cr = (nr * ar + ni * ai) / den
    ci = (ni * ar - nr * ai) / den
    br, bi = bre_ref[...], bim_ref[...]
    bbre_ref[...] = cr * br - ci * bi
    bbim_ref[...] = cr * bi + ci * br


def _s5_zoh(a_re, a_im, log_dt, b_re, b_im):
    n_dir, g, n, p = b_re.shape
    rows = n_dir * g * p
    expand = lambda a: jnp.broadcast_to(a[:, :, None, :], (n_dir, g, p, n)).reshape(rows, n)
    ldt = jnp.broadcast_to(log_dt[:, :, None, None], (n_dir, g, p, n)).reshape(rows, n)
    bt = lambda a: jnp.transpose(a, (0, 1, 3, 2)).reshape(rows, n)
    outs = pl.pallas_call(
        _s5_zoh_kernel,
        out_shape=[jax.ShapeDtypeStruct((rows, n), F32)] * 4,
        name="s5_zoh",
    )(expand(a_re), expand(a_im), ldt, bt(b_re), bt(b_im))
    lre, lim, bbre, bbim = (o.reshape(n_dir, g, p, n) for o in outs)
    return lre[:, :, 0, :], lim[:, :, 0, :], bbre, bbim


def _s5_matrices(lre, lim, bbre, bbim, c_re, c_im):
    n_dir = lre.shape[0]
    nj = S5_GROUPS // (2 * S5_TILES)
    shp = (n_dir, nj, S5_TILES, 2, S5_GROUP_DIM, S5_STATE)
    eye_t = jnp.eye(S5_TILES, dtype=F32)
    eye_g = jnp.eye(2, dtype=F32)

    def place(a):
        return jnp.einsum("djtgpn,tu,gh->djtgpuhn", a.reshape(shp), eye_t, eye_g)

    kdim = S5_TILES * 2 * S5_GROUP_DIM
    sdim = S5_TILES * 2 * 2 * S5_STATE
    bm = jnp.stack([place(bbre), place(bbim)], axis=6)
    bm = bm.reshape(n_dir, nj, kdim, sdim).astype(BF16)
    cm = jnp.stack([place(c_re), place(-c_im)], axis=6)
    cm = jnp.transpose(cm, (0, 1, 5, 6, 7, 8, 2, 3, 4)).reshape(n_dir, nj, sdim, kdim).astype(BF16)
    lam = jnp.stack([lre.reshape(n_dir, nj, S5_TILES * LANES), lim.reshape(n_dir, nj, S5_TILES * LANES)], axis=2)
    return bm, cm, lam


def _s5_scan_kernel(uf_ref, ub_ref, bm_ref, cm_ref, lam_ref, yf_ref, yb_ref, v_ref, h_ref, acc_ref, *, tc, nb):
    @pl.when(pl.program_id(0) == 0)
    def _():
        h_ref[...] = jnp.zeros_like(h_ref)

    nj, kdim, sdim = bm_ref.shape[1], bm_ref.shape[2], bm_ref.shape[3]
    u_refs, y_refs = (uf_ref, ub_ref), (yf_ref, yb_ref)
    phases = [(d, jq) for d in range(2) for jq in range(nj)]
    n_sub = 2 * S5_TILES
    col = sdim // S5_TILES
    run = tc // n_sub
    state = {}

    def drive_piece(p, c):
        d, jq = phases[p]
        if c == 0:
            state["uk"] = u_refs[d][:, kdim * jq:kdim * (jq + 1)]
        v_ref[p % 3, :, col * c:col * (c + 1)] = jnp.dot(state["uk"], bm_ref[d, jq, :, col * c:col * (c + 1)],
                                                         preferred_element_type=F32)

    def readout_piece(p, c):
        d, jq = phases[p]
        part = jnp.dot(v_ref[p % 3, :, col * c:col * (c + 1)].astype(BF16), cm_ref[d, jq, col * c:col * (c + 1), :],
                       preferred_element_type=F32)
        if c == 0:
            acc_ref[...] = part
        elif c < S5_TILES - 1:
            acc_ref[...] += part
        else:
            y_refs[d][:, kdim * jq:kdim * (jq + 1)] = (acc_ref[...] + part).astype(BF16)

    def scan_run(p, c):
        d, jq = phases[p]
        if c == 0:
            lam = lam_ref[d, jq]
            state["lr"] = [jnp.broadcast_to(lam[0:1, LANES * t:LANES * (t + 1)], (nb, LANES)) for t in range(S5_TILES)]
            state["li"] = [jnp.broadcast_to(lam[1:2, LANES * t:LANES * (t + 1)], (nb, LANES)) for t in range(S5_TILES)]
            h0 = h_ref[d, jq]
            state["h"] = [h0[:, LANES * i:LANES * (i + 1)] for i in range(2 * S5_TILES)]
        lr, li, h = state["lr"], state["li"], state["h"]
        for s in range(run * c, run * (c + 1)):
            r0 = (s if d == 0 else tc - 1 - s) * nb
            rows = v_ref[p % 3, r0:r0 + nb, :]
            new = []
            for t in range(S5_TILES):
                vr = rows[:, 2 * LANES * t:2 * LANES * t + LANES]
                vi = rows[:, 2 * LANES * t + LANES:2 * LANES * (t + 1)]
                hr, hi = h[2 * t], h[2 * t + 1]
                new.append(lr[t] * hr - li[t] * hi + vr)
                new.append(lr[t] * hi + li[t] * hr + vi)
            v_ref[p % 3, r0:r0 + nb, :] = jnp.concatenate(new, axis=1)
            h = new
        state["h"] = h
        if c == n_sub - 1:
            h_ref[d, jq] = jnp.concatenate(h, axis=1)

    for slot in range(len(phases) + 2):
        for c in range(n_sub):
            if 0 <= slot - 1 < len(phases):
                scan_run(slot - 1, c)
            if c < S5_TILES and slot < len(phases):
                drive_piece(slot, c)
            if c >= S5_TILES and 0 <= slot - 2 < len(phases):
                readout_piece(slot - 2, c - S5_TILES)


def _s5_scan(u_tm, bm, cm, lam, nb, n_ctx):
    rows_total, w = u_tm.shape
    tc = S5_CHUNK
    r = tc * nb
    n_chunks = rows_total // r
    ncc = n_ctx // tc
    bwd = lambda i: (jnp.where(i < ncc, ncc - 1 - i, n_chunks - 1 - (i - ncc)), 0)
    nj = bm.shape[1]
    sdim = bm.shape[3]
    kern = functools.partial(_s5_scan_kernel, tc=tc, nb=nb)
    return pl.pallas_call(
        kern,
        out_shape=[jax.ShapeDtypeStruct((rows_total, w), BF16)] * 2,
        grid=(n_chunks,),
        in_specs=[pl.BlockSpec((r, w), lambda i: (i, 0)), pl.BlockSpec((r, w), bwd),
                  _full(bm.shape), _full(cm.shape), _full(lam.shape)],
        out_specs=[pl.BlockSpec((r, w), lambda i: (i, 0)), pl.BlockSpec((r, w), bwd)],
        scratch_shapes=[pltpu.VMEM((3, r, sdim), F32), pltpu.VMEM((2, nj, nb, sdim), F32),
                        pltpu.VMEM((r, bm.shape[2]), F32)],
        compiler_params=_params("arbitrary"),
        name="s5_scan",
    )(u_tm, u_tm, bm, cm, lam)


def _l1_post_kernel(x_ref, u_ref, yf_ref, yb_ref, dsk_ref, wglu_ref, bglu_ref, wout_ref, mod_ref, gf_ref,
                    wr_ref, br_ref, x3_ref, f_ref, ri_ref, rw_ref, cnt_ref, base_ref):
    y = dsk_ref[...] * u_ref[...] + yf_ref[...].astype(F32) + yb_ref[...].astype(F32)
    g = jax.nn.gelu(y, approximate=True)
    z = jnp.dot(g.astype(BF16), wglu_ref[...], preferred_element_type=F32) + bglu_ref[...]
    o = jnp.dot((g * jax.nn.sigmoid(z)).astype(BF16), wout_ref[...], preferred_element_type=F32)
    first = jnp.logical_and(pl.program_id(0) == 0, pl.program_id(1) == 0)
    _route_tail(x_ref[0], o, mod_ref[0, 0], gf_ref[...], wr_ref, br_ref, base_ref,
                x3_ref, f_ref, ri_ref, rw_ref, cnt_ref, first)


def _l1_post(x2, u_tm, yf, yb, dsk, wglu, bglu, wout, mod, gf, wr, br, n_ctx):
    b, l, d = x2.shape
    t = TOKEN_TILE
    nct = n_ctx // t
    s = l - n_ctx
    tm = pl.BlockSpec((t, S5_WIDTH), lambda bi, ti: (ti + nct, bi))
    shapes, specs = _tail_out_shapes(b, s, d)
    return pl.pallas_call(
        _l1_post_kernel,
        out_shape=shapes,
        grid=(b, s // t),
        in_specs=[pl.BlockSpec((1, t, d), lambda bi, ti: (bi, ti + nct, 0)), tm, tm, tm,
                  _full((1, S5_WIDTH)), _full((S5_WIDTH, S5_WIDTH)), _full((1, S5_WIDTH)), _full((S5_WIDTH, d)),
                  pl.BlockSpec((1, 1, 6, d), lambda bi, ti: (bi, 1, 0, 0)),
                  _full((1, d)), _full((2, d, LANES)), _full((1, LANES))],
        out_specs=specs,
        scratch_shapes=[pltpu.VMEM((8, LANES), F32)],
        compiler_params=_params("arbitrary", "arbitrary"),
        name="l1_post",
    )(x2, u_tm, yf, yb, dsk, wglu, bglu, wout, mod, gf, wr, br)


def _final_kernel(x_ref, rw_ref, slots_ref, y_ref, mod_ref, g_ref, o_ref, idx_ref, ybuf_ref, sem_idx, sem_rows):
    moe = _combined_rows(y_ref, slots_ref, rw_ref, idx_ref, ybuf_ref, sem_idx, sem_rows)
    x4 = x_ref[...] + mod_ref[0, 0][5:6] * moe
    o_ref[...] = _rms(x4) * g_ref[...]


def _final(x3, rw, slots, y, mod, g, b):
    n_tok, d = x3.shape
    t = TOKEN_TILE
    nt = n_tok // b // t
    return pl.pallas_call(
        _final_kernel,
        out_shape=jax.ShapeDtypeStruct((n_tok, d), F32),
        grid=(n_tok // t,),
        in_specs=[pl.BlockSpec((t, d), lambda i: (i, 0)), pl.BlockSpec((t, 8), lambda i: (i, 0)),
                  pl.BlockSpec(memory_space=pltpu.VMEM), pl.BlockSpec(memory_space=pl.ANY),
                  pl.BlockSpec((1, 1, 6, d), lambda i: (i // nt, 1, 0, 0)), _full((1, d))],
        out_specs=pl.BlockSpec((t, d), lambda i: (i, 0)),
        scratch_shapes=_gather_scratch(2, t),
        compiler_params=_params("arbitrary"),
        name="final_combine",
    )(x3, rw, slots, y, mod, g)


def _rope_tables(n_ctx, n_lat):
    pos = jnp.arange(n_lat, dtype=I32)
    row = (pos // GRID_W).astype(F32)
    col = (pos % GRID_W).astype(F32)
    inv_freq = jnp.power(ROPE_THETA, -jnp.arange(0, AXIS_DIM, 2, dtype=F32) / AXIS_DIM)
    lane = np.arange(LANES)
    dd = lane % HEAD_DIM
    use_col = dd >= AXIS_DIM
    first = (dd % AXIS_DIM) < AXIS_DIM // 2
    freq = inv_freq[(dd % AXIS_DIM) % (AXIS_DIM // 2)]
    ang = jnp.where(use_col[None, :], col[:, None], row[:, None]) * freq[None, :]
    cos, sin = jnp.cos(ang), jnp.sin(ang)
    sa = jnp.where(first[None, :], -sin, 0.0)
    sb = jnp.where(first[None, :], 0.0, sin)
    pad = lambda a, v: jnp.concatenate([jnp.full((n_ctx, LANES), v, F32), a], axis=0)
    return pad(cos, 1.0), pad(sa, 0.0), pad(sb, 0.0)


def _group_ones(width, group):
    idx = np.arange(width) // group
    return jnp.asarray(idx[:, None] == idx[None, :], dtype=BF16)


def _router_weights(w_grp, b_grp, w_rt, b_rt):
    d = w_grp.shape[0]
    used = N_EXPERT_GROUPS + N_EXPERTS
    wr = jnp.concatenate([w_grp, w_rt, jnp.zeros((d, LANES - used), F32)], axis=1)
    br = jnp.concatenate([b_grp, b_rt, jnp.zeros((LANES - used,), F32)]).reshape(1, LANES)
    return jnp.stack(_split_bf16(wr)), br


def kernel(x, c, ctx, c_ctx, w_mod, b_mod, norm_mix_g, norm_ffn_g, mix_w_in, mix_w_out, q_norm_g, k_norm_g, gmlp_norm_g, gmlp_w_spatial, gmlp_b_spatial, s5_w_in, s5_a_re, s5_a_im, s5_log_dt, s5_b_re, s5_b_im, s5_c_re, s5_c_im, s5_d, s5_w_glu, s5_b_glu, s5_w_out, moe_w_group, moe_b_group, moe_w_router, moe_b_router, moe_w1, moe_w3, moe_w2, final_norm_g):
    b, s, d = x.shape
    n_ctx = ctx.shape[1]
    l = n_ctx + s
    assert w_mod.shape[0] == 2, "two layers: attention/gMLP then S5"
    assert n_ctx % TOKEN_TILE == 0 and s % TOKEN_TILE == 0 and b % 8 == 0 and d == D_MODEL
    nct = n_ctx // TOKEN_TILE

    r = (b + 1 + 7) // 8 * 8
    cond = jnp.concatenate([c, c_ctx[None, :], jnp.zeros((r - b - 1, d), F32)], axis=0)
    mods = _adaln(cond, w_mod, b_mod)
    lat = mods[:, :b].reshape(2, b, 1, 6, d)
    cxt = jnp.broadcast_to(mods[:, b].reshape(2, 1, 1, 6, d), (2, b, 1, 6, d))
    mod = jnp.concatenate([cxt, lat], axis=2)

    w_in = jnp.concatenate([mix_w_in[0][:, _Q_COL_PERM], mix_w_in[0][:, ATTN_WIDTH:]], axis=1).astype(BF16)
    w_out = jnp.concatenate([mix_w_out[0][_Q_COL_PERM], mix_w_out[0][ATTN_WIDTH:]], axis=0).astype(BF16)
    cos, sa, sb = _rope_tables(n_ctx, s)
    qg = jnp.tile(q_norm_g[0], N_Q_HEADS).reshape(1, ATTN_WIDTH)
    kg = jnp.tile(k_norm_g[0], N_KV_HEADS).reshape(1, KV_WIDTH)
    q, k, v, gu, gv = _l0_proj(ctx, x, mod[0], norm_mix_g[0].reshape(1, d), w_in, qg, kg,
                               gmlp_norm_g[0].reshape(1, GMLP_WIDTH), cos, sa, sb,
                               _group_ones(ATTN_WIDTH, HEAD_DIM), _group_ones(KV_WIDTH, HEAD_DIM), nct)
    o = _attention(q, k, v, n_ctx)
    bsp = jnp.repeat(gmlp_b_spatial[0].T, GMLP_GROUP_DIM, axis=1)
    wr0, br0 = _router_weights(moe_w_group[0], moe_b_group[0], moe_w_router[0], moe_b_router[0])
    x1, f0, ri0, rw0, cnt0 = _l0_post(ctx, x, o, gu, gv, gmlp_w_spatial[0].astype(BF16), bsp, w_out, mod[0],
                                      norm_ffn_g[0].reshape(1, d), wr0, br0, nct)
    y0, slots0 = _moe(f0, ri0, cnt0, moe_w1, moe_w3, moe_w2, 0)
    x2, u_tm, u_bf = _l0_combine(x1.reshape(b * l, d), rw0, slots0, y0, mod[0], mod[1], norm_mix_g[1].reshape(1, d),
                           s5_w_in[0].astype(BF16), b, nct)

    lre, lim, bbre, bbim = _s5_zoh(s5_a_re[0], s5_a_im[0], s5_log_dt[0], s5_b_re[0], s5_b_im[0])
    bm, cm, lam = _s5_matrices(lre, lim, bbre, bbim, s5_c_re[0], s5_c_im[0])
    yf, yb = _s5_scan(u_bf.reshape(l * b, S5_WIDTH), bm, cm, lam, b, n_ctx)
    wr1, br1 = _router_weights(moe_w_group[1], moe_b_group[1], moe_w_router[1], moe_b_router[1])
    tmv = lambda a: a.reshape(l, b * S5_WIDTH)
    x3, f1, ri1, rw1, cnt1 = _l1_post(x2.reshape(b, l, d), u_tm, tmv(yf), tmv(yb), s5_d[0].reshape(1, S5_WIDTH),
                                      s5_w_glu[0].astype(BF16), s5_b_glu[0].reshape(1, S5_WIDTH),
                                      s5_w_out[0].astype(BF16), mod[1], norm_ffn_g[1].reshape(1, d), wr1, br1, n_ctx)
    y1, slots1 = _moe(f1, ri1, cnt1, moe_w1, moe_w3, moe_w2, 1)
    out = _final(x3.reshape(b * s, d), rw1, slots1, y1, mod[1], final_norm_g.reshape(1, d), b)
    return out.reshape(b, s, d)
```

```python
import functools

import numpy as np
import jax
import jax.numpy as jnp
from jax import lax
from jax.experimental import pallas as pl
from jax.experimental.pallas import tpu as pltpu

F32 = jnp.float32
BF16 = jnp.bfloat16
I32 = jnp.int32

EPS = 1e-6
GRID_W = 64
HEAD_DIM = 64
N_Q_HEADS = 8
N_KV_HEADS = 2
ATTN_WIDTH = N_Q_HEADS * HEAD_DIM
KV_WIDTH = N_KV_HEADS * HEAD_DIM
AXIS_DIM = HEAD_DIM // 2
ROPE_THETA = 10000.0
GMLP_GROUPS = 8
GMLP_GROUP_DIM = 64
GMLP_WIDTH = GMLP_GROUPS * GMLP_GROUP_DIM
GMLP_CHUNK = 128
MIX_IN_WIDTH = ATTN_WIDTH + 2 * KV_WIDTH + 2 * GMLP_WIDTH
S5_GROUPS = 32
S5_GROUP_DIM = 16
S5_STATE = 64
S5_WIDTH = S5_GROUPS * S5_GROUP_DIM
N_EXPERT_GROUPS = 4
EXPERTS_PER_GROUP = 8
N_EXPERTS = N_EXPERT_GROUPS * EXPERTS_PER_GROUP
EXPERT_HIDDEN = 512
MOE_BLOCK = 512

D_MODEL = 1024
LANES = 128
ROW_SUB = D_MODEL // LANES
TOKEN_TILE = 256
PAIR = 2
ROW_DMA_UNROLL = 8
ATTN_KEY_BLOCK = 128
V_ONES_ROWS = 16
V_AUG_ROWS = N_KV_HEADS * (HEAD_DIM + V_ONES_ROWS)
S5_CHUNK = 64
S5_TILES = 4
VMEM_LIMIT = 48 * 1024 * 1024
NEG_BIG = -1e30

_Q_HEAD_ORDER = (0, 4, 1, 5, 2, 6, 3, 7)
_Q_COL_PERM = np.concatenate([np.arange(HEAD_DIM) + HEAD_DIM * h for h in _Q_HEAD_ORDER])


def _params(*sem):
    return pltpu.CompilerParams(dimension_semantics=sem, vmem_limit_bytes=VMEM_LIMIT)


def _rms(x):
    return x * lax.rsqrt(jnp.mean(x * x, axis=-1, keepdims=True) + EPS)


def _modulate(x, g, shift, scale):
    return _rms(x) * g * (1.0 + scale) + shift


def _full(shape):
    return pl.BlockSpec(shape, lambda *_: (0,) * len(shape))


def _split_bf16(a):
    hi = a.astype(BF16)
    return hi, (a - hi.astype(F32)).astype(BF16)


def _dot_split(a, w_hi, w_lo):
    a_hi, a_lo = _split_bf16(a)
    dot = functools.partial(jnp.dot, preferred_element_type=F32)
    return dot(a_hi, w_hi) + (dot(a_lo, w_hi) + dot(a_hi, w_lo))


def _store_row_tiles(ref, val):
    n = val.shape[0]
    for s in range(ROW_SUB):
        ref[pl.ds(s, n, stride=ROW_SUB), :] = val[:, LANES * s:LANES * (s + 1)]


def _load_row_tiles(ref):
    n = ref.shape[0] // ROW_SUB
    return jnp.concatenate([ref[pl.ds(s, n, stride=ROW_SUB), :] for s in range(ROW_SUB)], axis=1)


def _adaln_kernel(cond_ref, w_ref, b_ref, o_ref):
    c = cond_ref[...]
    s = c * jax.nn.sigmoid(c)
    o_ref[0] = _dot_split(s, *_split_bf16(w_ref[0])) + b_ref[0]


def _adaln(cond, w_mod, b_mod):
    depth, d, n = w_mod.shape
    r = cond.shape[0]
    tn = 1536
    return pl.pallas_call(
        _adaln_kernel,
        out_shape=jax.ShapeDtypeStruct((depth, r, n), F32),
        grid=(depth, n // tn),
        in_specs=[pl.BlockSpec((r, d), lambda l, j: (0, 0)),
                  pl.BlockSpec((1, d, tn), lambda l, j: (l, 0, j)),
                  pl.BlockSpec((1, 1, tn), lambda l, j: (l, 0, j))],
        out_specs=pl.BlockSpec((1, r, tn), lambda l, j: (l, 0, j)),
        compiler_params=_params("parallel", "parallel"),
        name="adaln",
    )(cond, w_mod, b_mod.reshape(depth, 1, n))


def _seq_tile(ctx_ref, x_ref, nct, j=0):
    return jnp.where(pl.program_id(1) < nct, ctx_ref[j], x_ref[j])


def _seq_specs(t, d, nct, nb=1):
    return [pl.BlockSpec((nb, t, d), lambda bi, ti: (bi, jnp.minimum(ti, nct - 1), 0)),
            pl.BlockSpec((nb, t, d), lambda bi, ti: (bi, jnp.maximum(ti - nct, 0), 0))]


def _l0_proj_kernel(ctx_ref, x_ref, mod_ref, g_ref, win_ref, qg_ref, kg_ref, lng_ref, cos_ref, sa_ref, sb_ref,
                    gq_ref, gk_ref, q_ref, k_ref, v_ref, gu_ref, gv_ref, *, nct):
    cos, sa, sb = cos_ref[...], sa_ref[...], sb_ref[...]

    def head_norm_rope(t, gsum_ref, gain, reps):
        w = t.shape[1]
        ss = jnp.dot((t * t).astype(BF16), gsum_ref[...], preferred_element_type=F32)
        tn = t * lax.rsqrt(ss * (1.0 / HEAD_DIM) + EPS) * gain
        c, a, b = (jnp.concatenate([tab] * reps, axis=1) if reps > 1 else tab for tab in (cos, sa, sb))
        return tn * c + pltpu.roll(tn, w - AXIS_DIM // 2, 1) * a + pltpu.roll(tn, AXIS_DIM // 2, 1) * b

    def prog(j):
        m = mod_ref[j, 0]
        h = _modulate(_seq_tile(ctx_ref, x_ref, nct, j), g_ref[...], m[0:1], m[1:2]).astype(BF16)
        yield
        z = jnp.dot(h, win_ref[...], preferred_element_type=F32)
        yield
        q = head_norm_rope(z[:, :ATTN_WIDTH], gq_ref, qg_ref[...], ATTN_WIDTH // LANES)
        q_ref[j] = (q * (HEAD_DIM ** -0.5)).T.astype(BF16)
        k = head_norm_rope(z[:, ATTN_WIDTH:ATTN_WIDTH + KV_WIDTH], gk_ref, kg_ref[...], 1)
        k_ref[j] = k.astype(BF16)
        vt = z[:, ATTN_WIDTH + KV_WIDTH:ATTN_WIDTH + 2 * KV_WIDTH].T
        ones = jnp.ones((V_ONES_ROWS, vt.shape[1]), F32)
        v_ref[j] = jnp.concatenate([part for kv in range(N_KV_HEADS)
                                    for part in (vt[HEAD_DIM * kv:HEAD_DIM * (kv + 1)], ones)], axis=0).astype(BF16)
        off = ATTN_WIDTH + 2 * KV_WIDTH
        gu_ref[j] = jax.nn.gelu(z[:, off:off + GMLP_WIDTH], approximate=True).astype(BF16)
        gv = jax.nn.gelu(z[:, off + GMLP_WIDTH:], approximate=True)
        mu = jnp.mean(gv, axis=-1, keepdims=True)
        gc = gv - mu
        gn = gc * lax.rsqrt(jnp.mean(gc * gc, axis=-1, keepdims=True) + EPS) * lng_ref[...]
        gv_ref[j] = gn.astype(BF16)

    _interleave(*(prog(j) for j in range(PAIR)))


def _l0_proj(ctx, x, mod, g, w_in, qg, kg, lng, cos, sa, sb, gq, gk, nct):
    b, s, d = x.shape
    l = s + ctx.shape[1]
    t = TOKEN_TILE
    tok = lambda w: pl.BlockSpec((PAIR, t, w), lambda bi, ti: (bi, ti, 0))
    tab = pl.BlockSpec((t, LANES), lambda bi, ti: (ti, 0))
    tok_t = lambda w: pl.BlockSpec((PAIR, w, t), lambda bi, ti: (bi, 0, ti))
    outs = ([jax.ShapeDtypeStruct((b, ATTN_WIDTH, l), BF16), jax.ShapeDtypeStruct((b, l, KV_WIDTH), BF16),
             jax.ShapeDtypeStruct((b, V_AUG_ROWS, l), BF16)]
            + [jax.ShapeDtypeStruct((b, l, GMLP_WIDTH), BF16)] * 2)
    return pl.pallas_call(
        functools.partial(_l0_proj_kernel, nct=nct),
        out_shape=outs,
        grid=(b // PAIR, l // t),
        in_specs=_seq_specs(t, d, nct, PAIR) + [
                  pl.BlockSpec((PAIR, 1, 6, d), lambda bi, ti: (bi, jnp.where(ti >= nct, 1, 0), 0, 0)),
                  _full((1, d)), _full((d, MIX_IN_WIDTH)), _full((1, ATTN_WIDTH)), _full((1, KV_WIDTH)),
                  _full((1, GMLP_WIDTH)), tab, tab, tab,
                  _full((ATTN_WIDTH, ATTN_WIDTH)), _full((KV_WIDTH, KV_WIDTH))],
        out_specs=[tok_t(ATTN_WIDTH), tok(KV_WIDTH), tok_t(V_AUG_ROWS), tok(GMLP_WIDTH), tok(GMLP_WIDTH)],
        compiler_params=_params("parallel", "parallel"),
        name="l0_proj",
    )(ctx, x, mod, g, w_in, qg, kg, lng, cos, sa, sb, gq, gk)


def _attn_kernel(qt_ref, k_ref, vt_ref, o_ref, s_ref, p_ref, *, n_ctx, nct):
    ti = pl.program_id(1)
    tq = qt_ref.shape[2]
    row = lax.broadcasted_iota(I32, (LANES, tq), 0)
    low = row < HEAD_DIM
    kb = ATTN_KEY_BLOCK
    va = HEAD_DIM + V_ONES_ROWS

    def attend(nk):
        heads = [(m, j) for m in range(ATTN_WIDTH // LANES) for j in range(2)]
        blocks = [slice(kb * b, kb * (b + 1)) for b in range(nk // kb)]
        state = {}

        def start(h):
            m, j = heads[h]
            qp = qt_ref[0, LANES * m:LANES * (m + 1), :]
            state[h] = dict(qm=jnp.where(low if j == 0 else jnp.logical_not(low), qp, jnp.zeros_like(qp)), mx=None)

        def score_block(h, blk):
            s = jnp.dot(k_ref[0, blk, :], state[h]["qm"], preferred_element_type=F32)
            s_ref[h % 2, blk, :] = s
            bm = jnp.max(s, axis=0, keepdims=True)
            state[h]["mx"] = bm if state[h]["mx"] is None else jnp.maximum(state[h]["mx"], bm)

        def exp_block(h, blk):
            p_ref[h % 2, blk, :] = jnp.exp(s_ref[h % 2, blk, :] - state[h]["mx"]).astype(BF16)

        def finish(h):
            j = heads[h][1]
            pv = jnp.dot(vt_ref[0, va * j:va * (j + 1), :nk], p_ref[h % 2, :nk, :],
                         preferred_element_type=F32)
            return pv[:HEAD_DIM] / pv[HEAD_DIM:HEAD_DIM + 1]

        start(0)
        for blk in blocks:
            score_block(0, blk)
        halves = []
        for h in range(len(heads)):
            if h + 1 < len(heads):
                start(h + 1)
            for blk in blocks:
                if h + 1 < len(heads):
                    score_block(h + 1, blk)
                exp_block(h, blk)
            halves.append(finish(h))
            if heads[h][1] == 1:
                m = heads[h][0]
                o_ref[0, :, LANES * m:LANES * (m + 1)] = jnp.concatenate(halves, axis=0).T.astype(BF16)
                halves = []

    @pl.when(ti < nct)
    def _():
        attend(n_ctx)

    @pl.when(ti >= nct)
    def _():
        attend(k_ref.shape[1])


def _attention(qt, k, vt, n_ctx):
    b, l, _ = k.shape
    t = TOKEN_TILE
    kern = functools.partial(_attn_kernel, n_ctx=n_ctx, nct=n_ctx // t)
    return pl.pallas_call(
        kern,
        out_shape=jax.ShapeDtypeStruct((b, l, ATTN_WIDTH), BF16),
        grid=(b, l // t),
        in_specs=[pl.BlockSpec((1, ATTN_WIDTH, t), lambda bi, ti: (bi, 0, ti)),
                  pl.BlockSpec((1, l, KV_WIDTH), lambda bi, ti: (bi, 0, 0)),
                  pl.BlockSpec((1, V_AUG_ROWS, l), lambda bi, ti: (bi, 0, 0))],
        out_specs=pl.BlockSpec((1, t, ATTN_WIDTH), lambda bi, ti: (bi, ti, 0)),
        scratch_shapes=[pltpu.VMEM((2, l, t), F32), pltpu.VMEM((2, l, t), BF16)],
        compiler_params=_params("parallel", "parallel"),
        name="attention",
    )(qt, k, vt)


def _interleave(*progs):
    live = dict(enumerate(progs))
    rnd = 0
    while live:
        for k in sorted(live):
            if rnd >= k:
                try:
                    next(live[k])
                except StopIteration:
                    del live[k]
        rnd += 1


def _route_tail(j, x, y, m, gf, wr_ref, br_ref, shared, x_ref, f_ref, ri_ref, rw_ref):
    x1 = x + m[2:3] * y
    x_ref[j] = x1
    f = _modulate(x1, gf, m[3:4], m[4:5])
    _store_row_tiles(f_ref.at[j], f)
    t = f.shape[0]
    yield
    logits = _dot_split(f, wr_ref[0], wr_ref[1]) + br_ref[...]
    yield
    lane = lax.broadcasted_iota(I32, (t, LANES), 1).astype(F32)
    is_grp = lane < N_EXPERT_GROUPS
    gl = jnp.where(is_grp, logits, NEG_BIG)
    gmax = jnp.max(gl, axis=-1, keepdims=True)
    g_idx = jnp.min(jnp.where(gl == gmax, lane, float(LANES)), axis=-1, keepdims=True)
    g_w = 1.0 / jnp.sum(jnp.where(is_grp, jnp.exp(gl - gmax), 0.0), axis=-1, keepdims=True)
    e_lo = N_EXPERT_GROUPS + EXPERTS_PER_GROUP * g_idx
    in_grp = jnp.logical_and(lane >= e_lo, lane < e_lo + EXPERTS_PER_GROUP)
    sel = jnp.where(in_grp, logits, NEG_BIG)
    v1 = jnp.max(sel, axis=-1, keepdims=True)
    i1 = jnp.min(jnp.where(sel == v1, lane, float(LANES)), axis=-1, keepdims=True)
    sel2 = jnp.where(lane == i1, NEG_BIG, sel)
    v2 = jnp.max(sel2, axis=-1, keepdims=True)
    i2 = jnp.min(jnp.where(sel2 == v2, lane, float(LANES)), axis=-1, keepdims=True)
    e = jnp.exp(v2 - v1)
    w1 = g_w / (1.0 + e)
    w2 = g_w * e / (1.0 + e)
    eid1 = i1 - N_EXPERT_GROUPS
    eid2 = i2 - N_EXPERT_GROUPS
    hit1 = lane == eid1
    hit2 = lane == eid2
    cnt = (hit1.astype(F32) + hit2.astype(F32))
    yield

    row = lax.broadcasted_iota(I32, (t, t), 0)
    col = lax.broadcasted_iota(I32, (t, t), 1)
    tril = jnp.where(col < row, 1.0, 0.0).astype(BF16)
    before = jnp.dot(tril, cnt.astype(BF16), preferred_element_type=F32) + shared["base"]
    shared["base"] = shared["base"] + jnp.sum(cnt, axis=0, keepdims=True)
    rank1 = jnp.sum(jnp.where(hit1, before, 0.0), axis=-1, keepdims=True)
    rank2 = jnp.sum(jnp.where(hit2, before, 0.0), axis=-1, keepdims=True)

    ri = jnp.where(lane == 0, eid1, jnp.where(lane == 1, eid2, jnp.where(lane == 2, rank1,
                   jnp.where(lane == 3, rank2, 0.0))))
    ri_ref[j] = ri.T[0:8, :].astype(I32)
    l8 = lax.broadcasted_iota(I32, (t, 8), 1)
    rw_ref[j] = jnp.where(l8 == 0, w1, jnp.where(l8 == 1, w2, 0.0))


def _run_pair(prog, base_ref, cnt_ref):
    @pl.when(jnp.logical_and(pl.program_id(0) == 0, pl.program_id(1) == 0))
    def _():
        base_ref[...] = jnp.zeros_like(base_ref)

    shared = {"base": base_ref[0:1, :]}
    _interleave(*(prog(j, shared) for j in range(PAIR)))
    base_ref[...] = jnp.broadcast_to(shared["base"], base_ref.shape)
    cnt_ref[...] = jnp.broadcast_to(shared["base"], cnt_ref.shape)


def _tail_out_shapes(b, n_rows, d):
    t = TOKEN_TILE
    shapes = [jax.ShapeDtypeStruct((b, n_rows, d), F32), jax.ShapeDtypeStruct((b, n_rows * ROW_SUB, LANES), F32),
              jax.ShapeDtypeStruct((b, 8, n_rows), I32), jax.ShapeDtypeStruct((b, n_rows, 8), F32),
              jax.ShapeDtypeStruct((8, LANES), F32)]
    specs = [pl.BlockSpec((PAIR, t, d), lambda bp, ti: (bp, ti, 0)),
             pl.BlockSpec((PAIR, t * ROW_SUB, LANES), lambda bp, ti: (bp, ti, 0)),
             pl.BlockSpec((PAIR, 8, t), lambda bp, ti: (bp, 0, ti)),
             pl.BlockSpec((PAIR, t, 8), lambda bp, ti: (bp, ti, 0)),
             pl.BlockSpec((8, LANES), lambda bp, ti: (0, 0))]
    return shapes, specs


def _tail_flatten(x1, f, ri, rw):
    b, n_rows, d = x1.shape
    return (x1, f.reshape(b * n_rows * ROW_SUB, LANES), jnp.transpose(ri, (1, 0, 2)).reshape(8, b * n_rows),
            rw.reshape(b * n_rows, 8))


def _l0_post_kernel(ctx_ref, x_ref, o_ref, gu_ref, gv_ref, wsp_ref, bsp_ref, wout_ref, mod_ref, gf_ref, wr_ref,
                    br_ref, x1_ref, f_ref, ri_ref, rw_ref, cnt_ref, base_ref, *, nct):
    tq = x_ref.shape[1]
    lane = lax.broadcasted_iota(I32, (GMLP_CHUNK, LANES), 1)
    low = lane < GMLP_GROUP_DIM

    def prog(j, shared):
        gated = []
        for c in range(tq // GMLP_CHUNK):
            rows = slice(c * GMLP_CHUNK, (c + 1) * GMLP_CHUNK)
            mixed = []
            for m in range(GMLP_WIDTH // LANES):
                pair = gv_ref[j, rows, LANES * m:LANES * (m + 1)]
                a = jnp.dot(wsp_ref[2 * m], pair, preferred_element_type=F32)
                b = jnp.dot(wsp_ref[2 * m + 1], pair, preferred_element_type=F32)
                mixed.append(jnp.where(low, a, b))
            mixed = jnp.concatenate(mixed, axis=1) + bsp_ref[...]
            gated.append((gu_ref[j, rows, :].astype(F32) * mixed).astype(BF16))
        gated = jnp.concatenate(gated, axis=0)
        yield
        y = (jnp.dot(o_ref[j], wout_ref[:ATTN_WIDTH, :], preferred_element_type=F32)
             + jnp.dot(gated, wout_ref[ATTN_WIDTH:, :], preferred_element_type=F32))
        yield
        yield from _route_tail(j, _seq_tile(ctx_ref, x_ref, nct, j), y, mod_ref[j, 0], gf_ref[...], wr_ref, br_ref,
                               shared, x1_ref, f_ref, ri_ref, rw_ref)

    _run_pair(prog, base_ref, cnt_ref)


def _l0_post(ctx, x, o, gu, gv, wsp, bsp, wout, mod, gf, wr, br, nct):
    b, s, d = x.shape
    l = s + ctx.shape[1]
    t = TOKEN_TILE
    tok = lambda w: pl.BlockSpec((PAIR, t, w), lambda bp, ti: (bp, ti, 0))
    shapes, specs = _tail_out_shapes(b, l, d)
    outs = pl.pallas_call(
        functools.partial(_l0_post_kernel, nct=nct),
        out_shape=shapes,
        grid=(b // PAIR, l // t),
        in_specs=_seq_specs(t, d, nct, PAIR) + [tok(ATTN_WIDTH), tok(GMLP_WIDTH), tok(GMLP_WIDTH),
                  _full((GMLP_GROUPS, GMLP_CHUNK, GMLP_CHUNK)), _full((GMLP_CHUNK, GMLP_WIDTH)),
                  _full((ATTN_WIDTH + GMLP_WIDTH, d)),
                  pl.BlockSpec((PAIR, 1, 6, d), lambda bp, ti: (bp, jnp.where(ti >= nct, 1, 0), 0, 0)),
                  _full((1, d)), _full((2, d, LANES)), _full((1, LANES))],
        out_specs=specs,
        scratch_shapes=[pltpu.VMEM((8, LANES), F32)],
        compiler_params=_params("arbitrary", "arbitrary"),
        name="l0_post",
    )(ctx, x, o, gu, gv, wsp, bsp, wout, mod, gf, wr, br)
    return _tail_flatten(*outs[:4]) + (outs[4],)


def _row_copy(src_ref, src_row, dst_ref, dst_row, sem):
    rows = lambda ref, r: ref.at[pl.ds(pl.multiple_of(r * ROW_SUB, ROW_SUB), ROW_SUB)]
    return pltpu.make_async_copy(rows(src_ref, src_row), rows(dst_ref, dst_row), sem)


def _issue_rows(src_ref, index_of, buf_ref, sem_rows, s):
    n_str, rows = buf_ref.shape[1], buf_ref.shape[2] // ROW_SUB
    slot = s % 2

    def issue(j, carry):
        for half in range(2):
            r = 2 * j + half
            for k in range(n_str):
                _row_copy(src_ref, index_of(k * rows + r), buf_ref.at[slot, k], r,
                          sem_rows.at[slot]).start(priority=(k + half) % 2)
        return carry

    lax.fori_loop(0, rows // 2, issue, 0, unroll=ROW_DMA_UNROLL // 2)


def _wait_rows(src_ref, buf_ref, sem_rows, s):
    n_str, rows = buf_ref.shape[1], buf_ref.shape[2] // ROW_SUB
    slot = s % 2
    for k in range(n_str):
        pltpu.make_async_copy(src_ref.at[pl.ds(0, rows * ROW_SUB)], buf_ref.at[slot, k], sem_rows.at[slot]).wait()
    return slot


def _gather_pipeline(src_ref, table_ref, idx_ref, buf_ref, sem_idx, sem_rows, step, n_steps):
    def idx_copy(s):
        return pltpu.make_async_copy(table_ref.at[s], idx_ref.at[s % 2], sem_idx.at[s % 2])

    def issue_rows(s):
        _issue_rows(src_ref, lambda pos: idx_ref[s % 2, pos], buf_ref, sem_rows, s)

    @pl.when(step == 0)
    def _():
        idx_copy(0).start()
        idx_copy(0).wait()
        issue_rows(0)

        @pl.when(n_steps > 1)
        def _():
            idx_copy(1).start()

    @pl.when(step + 1 < n_steps)
    def _():
        idx_copy(step + 1).wait()
        issue_rows(step + 1)

        @pl.when(step + 2 < n_steps)
        def _():
            idx_copy(step + 2).start()

    return _wait_rows(src_ref, buf_ref, sem_rows, step)


def _gather_scratch(n_str, rows):
    return [pltpu.SMEM((2, n_str * rows), I32), pltpu.VMEM((2, n_str, rows * ROW_SUB, LANES), F32),
            pltpu.SemaphoreType.DMA((2,)), pltpu.SemaphoreType.DMA((2,))]


def _dispatch_kernel(pad_ref, nu_ref, f_ref, slots_ref, xb_ref, idx_ref, zero_ref, sem_idx, sem_rows):
    i = pl.program_id(0)
    t = f_ref.shape[0] // ROW_SUB
    blk = MOE_BLOCK * ROW_SUB
    idx_cp = pltpu.make_async_copy(slots_ref.at[i], idx_ref, sem_idx)
    idx_cp.start()

    @pl.when(i == 0)
    def _():
        zero_ref[...] = jnp.zeros_like(zero_ref)
        n_blocks = xb_ref.shape[0] // blk

        def zero_block(start):
            return pltpu.make_async_copy(zero_ref, xb_ref.at[pl.ds(pl.multiple_of(start * ROW_SUB, blk), blk)],
                                         sem_rows)

        def tail_start(j, carry):
            zero_block(j * MOE_BLOCK).start()
            return carry

        def tail_wait(j, carry):
            zero_block(0).wait()
            return carry

        for e in range(N_EXPERTS):
            zero_block(jnp.maximum(pad_ref[N_EXPERTS + e] - MOE_BLOCK, 0)).start()
        lax.fori_loop(nu_ref[0], n_blocks, tail_start, 0)
        for e in range(N_EXPERTS):
            zero_block(0).wait()
        lax.fori_loop(nu_ref[0], n_blocks, tail_wait, 0)

    idx_cp.wait()

    def issue(r, carry):
        _row_copy(f_ref, r, xb_ref, idx_ref[r], sem_rows).start(priority=0)
        _row_copy(f_ref, r, xb_ref, idx_ref[t + r], sem_rows).start(priority=1)
        return carry

    lax.fori_loop(0, t, issue, 0, unroll=ROW_DMA_UNROLL)
    for _ in range(2):
        pltpu.make_async_copy(f_ref, xb_ref.at[pl.ds(0, t * ROW_SUB)], sem_rows).wait()


def _dispatch(f, slots, pad, n_used, n_slots):
    t = TOKEN_TILE
    n_tok = f.shape[0] // ROW_SUB
    return pl.pallas_call(
        _dispatch_kernel,
        out_shape=jax.ShapeDtypeStruct((n_slots * ROW_SUB, LANES), F32),
        grid_spec=pltpu.PrefetchScalarGridSpec(
            num_scalar_prefetch=2,
            grid=(n_tok // t,),
            in_specs=[pl.BlockSpec((t * ROW_SUB, LANES), lambda i, p, nu: (i, 0)),
                      pl.BlockSpec(memory_space=pltpu.VMEM)],
            out_specs=pl.BlockSpec(memory_space=pl.ANY),
            scratch_shapes=[pltpu.SMEM((2 * t,), I32), pltpu.VMEM((MOE_BLOCK * ROW_SUB, LANES), F32),
                            pltpu.SemaphoreType.DMA, pltpu.SemaphoreType.DMA]),
        compiler_params=_params("arbitrary"),
        name="moe_dispatch",
    )(pad, n_used, f, slots)


def _experts_kernel(be_ref, nu_ref, xb_ref, w1_ref, w3_ref, w2_ref, y_ref, w1b_ref, w3b_ref, w2b_ref):
    i = pl.program_id(0)

    @pl.when(jnp.logical_or(i == 0, be_ref[i] != be_ref[jnp.maximum(i - 1, 0)]))
    def _():
        w1b_ref[...] = w1_ref[...].astype(BF16)
        w3b_ref[...] = w3_ref[...].astype(BF16)
        w2b_ref[...] = w2_ref[...].astype(BF16)

    @pl.when(i < nu_ref[0])
    def _():
        x = _load_row_tiles(xb_ref).astype(BF16)
        h1 = jnp.dot(x, w1b_ref[...], preferred_element_type=F32)
        h3 = jnp.dot(x, w3b_ref[...], preferred_element_type=F32)
        a = (h1 * jax.nn.sigmoid(h1) * h3).astype(BF16)
        _store_row_tiles(y_ref, jnp.dot(a, w2b_ref[...], preferred_element_type=F32))

    @pl.when(i >= nu_ref[0])
    def _():
        y_ref[...] = jnp.zeros_like(y_ref)


def _experts(xb, w1, w3, w2, layer, block_e, n_used):
    n_blocks = block_e.shape[0]
    _, _, d, hid = w1.shape
    rb = MOE_BLOCK * ROW_SUB
    wspec = lambda s: pl.BlockSpec((None, None) + s, lambda i, be, nu: (layer, be[i], 0, 0))
    return pl.pallas_call(
        _experts_kernel,
        out_shape=jax.ShapeDtypeStruct((n_blocks * rb, LANES), F32),
        grid_spec=pltpu.PrefetchScalarGridSpec(
            num_scalar_prefetch=2,
            grid=(n_blocks,),
            in_specs=[pl.BlockSpec((rb, LANES), lambda i, be, nu: (jnp.minimum(i, nu[0] - 1), 0)),
                      wspec((d, hid)), wspec((d, hid)), wspec((hid, d))],
            out_specs=pl.BlockSpec((rb, LANES), lambda i, be, nu: (i, 0)),
            scratch_shapes=[pltpu.VMEM((d, hid), BF16), pltpu.VMEM((d, hid), BF16), pltpu.VMEM((hid, d), BF16)]),
        compiler_params=_params("arbitrary"),
        name="moe_experts",
    )(block_e, n_used, xb, w1, w3, w2)


def _moe_plan(route_i, counts, n_tok):
    eid, rank = route_i[0:2], route_i[2:4]
    cnt = counts[0, :N_EXPERTS].astype(I32)
    padded = (cnt + MOE_BLOCK - 1) // MOE_BLOCK * MOE_BLOCK
    pend = jnp.cumsum(padded)
    pstart = pend - padded
    experts = jnp.arange(N_EXPERTS, dtype=I32)
    slots = rank + jnp.sum(jnp.where(eid[:, :, None] == experts, pstart, 0), axis=-1)
    n_assign = 2 * n_tok
    n_slots = (n_assign + MOE_BLOCK - 1) // MOE_BLOCK * MOE_BLOCK + N_EXPERTS * MOE_BLOCK
    n_blocks = n_slots // MOE_BLOCK
    bstart = jnp.arange(n_blocks, dtype=I32) * MOE_BLOCK
    block_e = jnp.minimum(jnp.sum(bstart[:, None] >= pend[None, :], axis=1), N_EXPERTS - 1).astype(I32)
    n_used = pend[-1] // MOE_BLOCK
    last_e = block_e[jnp.maximum(n_used - 1, 0)]
    block_e = jnp.where(jnp.arange(n_blocks) < n_used, block_e, last_e).astype(I32)
    slots = slots.reshape(2, n_tok // TOKEN_TILE, TOKEN_TILE).transpose(1, 0, 2)
    slots = slots.reshape(n_tok // TOKEN_TILE, 2 * TOKEN_TILE).astype(I32)
    pad = jnp.concatenate([pstart + cnt, pend]).astype(I32)
    return slots, block_e, n_used.reshape(1).astype(I32), pad, n_slots


def _moe(f, route_i, counts, w1, w3, w2, layer):
    n_tok = f.shape[0] // ROW_SUB
    slots, block_e, n_used, pad, n_slots = _moe_plan(route_i, counts, n_tok)
    xb = _dispatch(f, slots, pad, n_used, n_slots)
    y = _experts(xb, w1, w3, w2, layer, block_e, n_used)
    return y, slots


def _combined_rows(y_ref, slots_ref, rw_ref, idx_ref, ybuf_ref, sem_idx, sem_rows):
    slot = _gather_pipeline(y_ref, slots_ref, idx_ref, ybuf_ref, sem_idx, sem_rows,
                            pl.program_id(0), pl.num_programs(0))
    rw = rw_ref[...]
    w1, w2 = rw[:, 0:1], rw[:, 1:2]
    t = ybuf_ref.shape[2] // ROW_SUB
    part = lambda k, s: ybuf_ref[slot, k, pl.ds(s, t, stride=ROW_SUB), :]
    return jnp.concatenate([w1 * part(0, s) + w2 * part(1, s) for s in range(ROW_SUB)], axis=1)


def _l0_combine_kernel(x_ref, rw_ref, slots_ref, y_ref, mod0_ref, mod1_ref, g_ref, win_ref,
                       x2_ref, u_ref, ub_ref, idx_ref, ybuf_ref, sem_idx, sem_rows):
    moe = _combined_rows(y_ref, slots_ref, rw_ref, idx_ref, ybuf_ref, sem_idx, sem_rows)
    x2 = x_ref[...] + mod0_ref[0, 0][5:6] * moe
    x2_ref[...] = x2
    m1 = mod1_ref[0, 0]
    h = _modulate(x2, g_ref[...], m1[0:1], m1[1:2]).astype(BF16)
    u = jnp.dot(h, win_ref[...], preferred_element_type=F32)
    u_ref[...] = u
    ub_ref[...] = u.astype(BF16)


def _l0_combine(x1, rw, slots, y, mod0, mod1, g, win, b, nct):
    n_tok, d = x1.shape
    t = TOKEN_TILE
    nt = n_tok // b // t
    l = n_tok // b
    modspec = pl.BlockSpec((1, 1, 6, d), lambda i: (i // nt, jnp.where(i % nt >= nct, 1, 0), 0, 0))
    return pl.pallas_call(
        _l0_combine_kernel,
        out_shape=[jax.ShapeDtypeStruct((n_tok, d), F32), jax.ShapeDtypeStruct((l, b * S5_WIDTH), F32),
                   jax.ShapeDtypeStruct((l, b * S5_WIDTH), BF16)],
        grid=(n_tok // t,),
        in_specs=[pl.BlockSpec((t, d), lambda i: (i, 0)), pl.BlockSpec((t, 8), lambda i: (i, 0)),
                  pl.BlockSpec(memory_space=pltpu.VMEM), pl.BlockSpec(memory_space=pl.ANY),
                  modspec, modspec, _full((1, d)), _full((d, S5_WIDTH))],
        out_specs=[pl.BlockSpec((t, d), lambda i: (i, 0)),
                   pl.BlockSpec((t, S5_WIDTH), lambda i: (i % nt, i // nt)),
                   pl.BlockSpec((t, S5_WIDTH), lambda i: (i % nt, i // nt))],
        scratch_shapes=_gather_scratch(2, t),
        compiler_params=_params("arbitrary"),
        name="l0_combine",
    )(x1, rw, slots, y, mod0, mod1, g, win)


def _s5_zoh_kernel(are_ref, aim_ref, ldt_ref, bre_ref, bim_ref, lre_ref, lim_ref, bbre_ref, bbim_ref):
    ar, ai = are_ref[...], aim_ref[...]
    dt = jnp.exp(ldt_ref[...])
    mag = jnp.exp(ar * dt)
    lr = mag * jnp.cos(ai * dt)
    li = mag * jnp.sin(ai * dt)
    lre_ref[...] = lr
    lim_ref[...] = li
    nr, ni = lr - 1.0, li
    den = ar * ar + ai * ai
    cr = (nr * ar + ni * ai) / den
    ci = (ni * ar - nr * ai) / den
    br, bi = bre_ref[...], bim_ref[...]
    bbre_ref[...] = cr * br - ci * bi
    bbim_ref[...] = cr * bi + ci * br


def _s5_zoh(a_re, a_im, log_dt, b_re, b_im):
    n_dir, g, n, p = b_re.shape
    rows = n_dir * g * p
    expand = lambda a: jnp.broadcast_to(a[:, :, None, :], (n_dir, g, p, n)).reshape(rows, n)
    ldt = jnp.broadcast_to(log_dt[:, :, None, None], (n_dir, g, p, n)).reshape(rows, n)
    bt = lambda a: jnp.transpose(a, (0, 1, 3, 2)).reshape(rows, n)
    outs = pl.pallas_call(
        _s5_zoh_kernel,
        out_shape=[jax.ShapeDtypeStruct((rows, n), F32)] * 4,
        name="s5_zoh",
    )(expand(a_re), expand(a_im), ldt, bt(b_re), bt(b_im))
    lre, lim, bbre, bbim = (o.reshape(n_dir, g, p, n) for o in outs)
    return lre[:, :, 0, :], lim[:, :, 0, :], bbre, bbim


def _s5_matrices(lre, lim, bbre, bbim, c_re, c_im):
    n_dir = lre.shape[0]
    nj = S5_GROUPS // (2 * S5_TILES)
    shp = (n_dir, nj, S5_TILES, 2, S5_GROUP_DIM, S5_STATE)
    eye_t = jnp.eye(S5_TILES, dtype=F32)
    eye_g = jnp.eye(2, dtype=F32)

    def place(a):
        return jnp.einsum("djtgpn,tu,gh->djtgpuhn", a.reshape(shp), eye_t, eye_g)

    kdim = S5_TILES * 2 * S5_GROUP_DIM
    sdim = S5_TILES * 2 * 2 * S5_STATE
    bm = jnp.stack([place(bbre), place(bbim)], axis=6)
    bm = bm.reshape(n_dir, nj, kdim, sdim).astype(BF16)
    cm = jnp.stack([place(c_re), place(-c_im)], axis=6)
    cm = jnp.transpose(cm, (0, 1, 5, 6, 7, 8, 2, 3, 4)).reshape(n_dir, nj, sdim, kdim).astype(BF16)
    lam = jnp.stack([lre.reshape(n_dir, nj, S5_TILES * LANES), lim.reshape(n_dir, nj, S5_TILES * LANES)], axis=2)
    return bm, cm, lam


def _s5_scan_kernel(uf_ref, ub_ref, bm_ref, cm_ref, lam_ref, yf_ref, yb_ref, v_ref, h_ref, acc_ref, *, tc, nb):
    @pl.when(pl.program_id(0) == 0)
    def _():
        h_ref[...] = jnp.zeros_like(h_ref)

    nj, kdim, sdim = bm_ref.shape[1], bm_ref.shape[2], bm_ref.shape[3]
    u_refs, y_refs = (uf_ref, ub_ref), (yf_ref, yb_ref)
    phases = [(d, jq) for d in range(2) for jq in range(nj)]
    n_sub = 2 * S5_TILES
    col = sdim // S5_TILES
    run = tc // n_sub
    state = {}

    def drive_piece(p, c):
        d, jq = phases[p]
        if c == 0:
            state["uk"] = u_refs[d][:, kdim * jq:kdim * (jq + 1)]
        v_ref[p % 3, :, col * c:col * (c + 1)] = jnp.dot(state["uk"], bm_ref[d, jq, :, col * c:col * (c + 1)],
                                                         preferred_element_type=F32)

    def readout_piece(p, c):
        d, jq = phases[p]
        part = jnp.dot(v_ref[p % 3, :, col * c:col * (c + 1)].astype(BF16), cm_ref[d, jq, col * c:col * (c + 1), :],
                       preferred_element_type=F32)
        if c == 0:
            acc_ref[...] = part
        elif c < S5_TILES - 1:
            acc_ref[...] += part
        else:
            y_refs[d][:, kdim * jq:kdim * (jq + 1)] = (acc_ref[...] + part).astype(BF16)

    def scan_run(p, c):
        d, jq = phases[p]
        if c == 0:
            lam = lam_ref[d, jq]
            state["lr"] = [jnp.broadcast_to(lam[0:1, LANES * t:LANES * (t + 1)], (nb, LANES)) for t in range(S5_TILES)]
            state["li"] = [jnp.broadcast_to(lam[1:2, LANES * t:LANES * (t + 1)], (nb, LANES)) for t in range(S5_TILES)]
            h0 = h_ref[d, jq]
            state["h"] = [h0[:, LANES * i:LANES * (i + 1)] for i in range(2 * S5_TILES)]
        lr, li, h = state["lr"], state["li"], state["h"]
        for s in range(run * c, run * (c + 1)):
            r0 = (s if d == 0 else tc - 1 - s) * nb
            rows = v_ref[p % 3, r0:r0 + nb, :]
            new = []
            for t in range(S5_TILES):
                vr = rows[:, 2 * LANES * t:2 * LANES * t + LANES]
                vi = rows[:, 2 * LANES * t + LANES:2 * LANES * (t + 1)]
                hr, hi = h[2 * t], h[2 * t + 1]
                new.append(lr[t] * hr - li[t] * hi + vr)
                new.append(lr[t] * hi + li[t] * hr + vi)
            v_ref[p % 3, r0:r0 + nb, :] = jnp.concatenate(new, axis=1)
            h = new
        state["h"] = h
        if c == n_sub - 1:
            h_ref[d, jq] = jnp.concatenate(h, axis=1)

    for slot in range(len(phases) + 2):
        for c in range(n_sub):
            if 0 <= slot - 1 < len(phases):
                scan_run(slot - 1, c)
            if c < S5_TILES and slot < len(phases):
                drive_piece(slot, c)
            if c >= S5_TILES and 0 <= slot - 2 < len(phases):
                readout_piece(slot - 2, c - S5_TILES)


def _s5_scan(u_tm, bm, cm, lam, nb, n_ctx):
    rows_total, w = u_tm.shape
    tc = S5_CHUNK
    r = tc * nb
    n_chunks = rows_total // r
    ncc = n_ctx // tc
    bwd = lambda i: (jnp.where(i < ncc, ncc - 1 - i, n_chunks - 1 - (i - ncc)), 0)
    nj = bm.shape[1]
    sdim = bm.shape[3]
    kern = functools.partial(_s5_scan_kernel, tc=tc, nb=nb)
    return pl.pallas_call(
        kern,
        out_shape=[jax.ShapeDtypeStruct((rows_total, w), BF16)] * 2,
        grid=(n_chunks,),
        in_specs=[pl.BlockSpec((r, w), lambda i: (i, 0)), pl.BlockSpec((r, w), bwd),
                  _full(bm.shape), _full(cm.shape), _full(lam.shape)],
        out_specs=[pl.BlockSpec((r, w), lambda i: (i, 0)), pl.BlockSpec((r, w), bwd)],
        scratch_shapes=[pltpu.VMEM((3, r, sdim), F32), pltpu.VMEM((2, nj, nb, sdim), F32),
                        pltpu.VMEM((r, bm.shape[2]), F32)],
        compiler_params=_params("arbitrary"),
        name="s5_scan",
    )(u_tm, u_tm, bm, cm, lam)


def _l1_post_kernel(x_ref, u_ref, yf_ref, yb_ref, dsk_ref, wglu_ref, bglu_ref, wout_ref, mod_ref, gf_ref,
                    wr_ref, br_ref, x3_ref, f_ref, ri_ref, rw_ref, cnt_ref, base_ref):
    def prog(j, shared):
        cols = slice(S5_WIDTH * j, S5_WIDTH * (j + 1))
        y = dsk_ref[...] * u_ref[:, cols] + yf_ref[:, cols].astype(F32) + yb_ref[:, cols].astype(F32)
        g = jax.nn.gelu(y, approximate=True)
        yield
        z = jnp.dot(g.astype(BF16), wglu_ref[...], preferred_element_type=F32) + bglu_ref[...]
        yield
        a = (g * jax.nn.sigmoid(z)).astype(BF16)
        yield
        o = jnp.dot(a, wout_ref[...], preferred_element_type=F32)
        yield
        yield from _route_tail(j, x_ref[j], o, mod_ref[j, 0], gf_ref[...], wr_ref, br_ref, shared,
                               x3_ref, f_ref, ri_ref, rw_ref)

    _run_pair(prog, base_ref, cnt_ref)


def _l1_post(x2, u_tm, yf, yb, dsk, wglu, bglu, wout, mod, gf, wr, br, n_ctx):
    b, l, d = x2.shape
    t = TOKEN_TILE
    nct = n_ctx // t
    s = l - n_ctx
    tm = pl.BlockSpec((t, PAIR * S5_WIDTH), lambda bp, ti: (ti + nct, bp))
    shapes, specs = _tail_out_shapes(b, s, d)
    outs = pl.pallas_call(
        _l1_post_kernel,
        out_shape=shapes,
        grid=(b // PAIR, s // t),
        in_specs=[pl.BlockSpec((PAIR, t, d), lambda bp, ti: (bp, ti + nct, 0)), tm, tm, tm,
                  _full((1, S5_WIDTH)), _full((S5_WIDTH, S5_WIDTH)), _full((1, S5_WIDTH)), _full((S5_WIDTH, d)),
                  pl.BlockSpec((PAIR, 1, 6, d), lambda bp, ti: (bp, 1, 0, 0)),
                  _full((1, d)), _full((2, d, LANES)), _full((1, LANES))],
        out_specs=specs,
        scratch_shapes=[pltpu.VMEM((8, LANES), F32)],
        compiler_params=_params("arbitrary", "arbitrary"),
        name="l1_post",
    )(x2, u_tm, yf, yb, dsk, wglu, bglu, wout, mod, gf, wr, br)
    return _tail_flatten(*outs[:4]) + (outs[4],)


def _final_kernel(x_ref, rw_ref, slots_ref, y_ref, mod_ref, g_ref, o_ref, idx_ref, ybuf_ref, sem_idx, sem_rows):
    moe = _combined_rows(y_ref, slots_ref, rw_ref, idx_ref, ybuf_ref, sem_idx, sem_rows)
    x4 = x_ref[...] + mod_ref[0, 0][5:6] * moe
    o_ref[...] = _rms(x4) * g_ref[...]


def _final(x3, rw, slots, y, mod, g, b):
    n_tok, d = x3.shape
    t = TOKEN_TILE
    nt = n_tok // b // t
    return pl.pallas_call(
        _final_kernel,
        out_shape=jax.ShapeDtypeStruct((n_tok, d), F32),
        grid=(n_tok // t,),
        in_specs=[pl.BlockSpec((t, d), lambda i: (i, 0)), pl.BlockSpec((t, 8), lambda i: (i, 0)),
                  pl.BlockSpec(memory_space=pltpu.VMEM), pl.BlockSpec(memory_space=pl.ANY),
                  pl.BlockSpec((1, 1, 6, d), lambda i: (i // nt, 1, 0, 0)), _full((1, d))],
        out_specs=pl.BlockSpec((t, d), lambda i: (i, 0)),
        scratch_shapes=_gather_scratch(2, t),
        compiler_params=_params("arbitrary"),
        name="final_combine",
    )(x3, rw, slots, y, mod, g)


def _rope_tables(n_ctx, n_lat):
    pos = jnp.arange(n_lat, dtype=I32)
    row = (pos // GRID_W).astype(F32)
    col = (pos % GRID_W).astype(F32)
    inv_freq = jnp.power(ROPE_THETA, -jnp.arange(0, AXIS_DIM, 2, dtype=F32) / AXIS_DIM)
    lane = np.arange(LANES)
    dd = lane % HEAD_DIM
    use_col = dd >= AXIS_DIM
    first = (dd % AXIS_DIM) < AXIS_DIM // 2
    freq = inv_freq[(dd % AXIS_DIM) % (AXIS_DIM // 2)]
    ang = jnp.where(use_col[None, :], col[:, None], row[:, None]) * freq[None, :]
    cos, sin = jnp.cos(ang), jnp.sin(ang)
    sa = jnp.where(first[None, :], -sin, 0.0)
    sb = jnp.where(first[None, :], 0.0, sin)
    pad = lambda a, v: jnp.concatenate([jnp.full((n_ctx, LANES), v, F32), a], axis=0)
    return pad(cos, 1.0), pad(sa, 0.0), pad(sb, 0.0)


def _group_ones(width, group):
    idx = np.arange(width) // group
    return jnp.asarray(idx[:, None] == idx[None, :], dtype=BF16)


def _router_weights(w_grp, b_grp, w_rt, b_rt):
    d = w_grp.shape[0]
    used = N_EXPERT_GROUPS + N_EXPERTS
    wr = jnp.concatenate([w_grp, w_rt, jnp.zeros((d, LANES - used), F32)], axis=1)
    br = jnp.concatenate([b_grp, b_rt, jnp.zeros((LANES - used,), F32)]).reshape(1, LANES)
    return jnp.stack(_split_bf16(wr)), br


def kernel(x, c, ctx, c_ctx, w_mod, b_mod, norm_mix_g, norm_ffn_g, mix_w_in, mix_w_out, q_norm_g, k_norm_g, gmlp_norm_g, gmlp_w_spatial, gmlp_b_spatial, s5_w_in, s5_a_re, s5_a_im, s5_log_dt, s5_b_re, s5_b_im, s5_c_re, s5_c_im, s5_d, s5_w_glu, s5_b_glu, s5_w_out, moe_w_group, moe_b_group, moe_w_router, moe_b_router, moe_w1, moe_w3, moe_w2, final_norm_g):
    b, s, d = x.shape
    n_ctx = ctx.shape[1]
    l = n_ctx + s
    assert w_mod.shape[0] == 2, "two layers: attention/gMLP then S5"
    assert n_ctx % TOKEN_TILE == 0 and s % TOKEN_TILE == 0 and b % 8 == 0 and d == D_MODEL
    nct = n_ctx // TOKEN_TILE

    r = (b + 1 + 7) // 8 * 8
    cond = jnp.concatenate([c, c_ctx[None, :], jnp.zeros((r - b - 1, d), F32)], axis=0)
    mods = _adaln(cond, w_mod, b_mod)
    lat = mods[:, :b].reshape(2, b, 1, 6, d)
    cxt = jnp.broadcast_to(mods[:, b].reshape(2, 1, 1, 6, d), (2, b, 1, 6, d))
    mod = jnp.concatenate([cxt, lat], axis=2)

    w_in = jnp.concatenate([mix_w_in[0][:, _Q_COL_PERM], mix_w_in[0][:, ATTN_WIDTH:]], axis=1).astype(BF16)
    w_out = jnp.concatenate([mix_w_out[0][_Q_COL_PERM], mix_w_out[0][ATTN_WIDTH:]], axis=0).astype(BF16)
    cos, sa, sb = _rope_tables(n_ctx, s)
    qg = jnp.tile(q_norm_g[0], N_Q_HEADS).reshape(1, ATTN_WIDTH)
    kg = jnp.tile(k_norm_g[0], N_KV_HEADS).reshape(1, KV_WIDTH)
    q, k, v, gu, gv = _l0_proj(ctx, x, mod[0], norm_mix_g[0].reshape(1, d), w_in, qg, kg,
                               gmlp_norm_g[0].reshape(1, GMLP_WIDTH), cos, sa, sb,
                               _group_ones(ATTN_WIDTH, HEAD_DIM), _group_ones(KV_WIDTH, HEAD_DIM), nct)
    o = _attention(q, k, v, n_ctx)
    bsp = jnp.repeat(gmlp_b_spatial[0].T, GMLP_GROUP_DIM, axis=1)
    wr0, br0 = _router_weights(moe_w_group[0], moe_b_group[0], moe_w_router[0], moe_b_router[0])
    x1, f0, ri0, rw0, cnt0 = _l0_post(ctx, x, o, gu, gv, gmlp_w_spatial[0].astype(BF16), bsp, w_out, mod[0],
                                      norm_ffn_g[0].reshape(1, d), wr0, br0, nct)
    y0, slots0 = _moe(f0, ri0, cnt0, moe_w1, moe_w3, moe_w2, 0)
    x2, u_tm, u_bf = _l0_combine(x1.reshape(b * l, d), rw0, slots0, y0, mod[0], mod[1], norm_mix_g[1].reshape(1, d),
                           s5_w_in[0].astype(BF16), b, nct)

    lre, lim, bbre, bbim = _s5_zoh(s5_a_re[0], s5_a_im[0], s5_log_dt[0], s5_b_re[0], s5_b_im[0])
    bm, cm, lam = _s5_matrices(lre, lim, bbre, bbim, s5_c_re[0], s5_c_im[0])
    yf, yb = _s5_scan(u_bf.reshape(l * b, S5_WIDTH), bm, cm, lam, b, n_ctx)
    wr1, br1 = _router_weights(moe_w_group[1], moe_b_group[1], moe_w_router[1], moe_b_router[1])
    tmv = lambda a: a.reshape(l, b * S5_WIDTH)
    x3, f1, ri1, rw1, cnt1 = _l1_post(x2.reshape(b, l, d), u_tm, tmv(yf), tmv(yb), s5_d[0].reshape(1, S5_WIDTH),
                                      s5_w_glu[0].astype(BF16), s5_b_glu[0].reshape(1, S5_WIDTH),
                                      s5_w_out[0].astype(BF16), mod[1], norm_ffn_g[1].reshape(1, d), wr1, br1, n_ctx)
    y1, slots1 = _moe(f1, ri1, cnt1, moe_w1, moe_w3, moe_w2, 1)
    out = _final(x3.reshape(b * s, d), rw1, slots1, y1, mod[1], final_norm_g.reshape(1, d), b)
    return out.reshape(b, s, d)
```

```python
import functools

import numpy as np
import jax
import jax.numpy as jnp
from jax import lax
from jax.experimental import pallas as pl
from jax.experimental.pallas import tpu as pltpu

F32 = jnp.float32
BF16 = jnp.bfloat16
I32 = jnp.int32

EPS = 1e-6
GRID_W = 64
HEAD_DIM = 64
N_Q_HEADS = 8
N_KV_HEADS = 2
ATTN_WIDTH = N_Q_HEADS * HEAD_DIM
KV_WIDTH = N_KV_HEADS * HEAD_DIM
AXIS_DIM = HEAD_DIM // 2
ROPE_THETA = 10000.0
GMLP_GROUPS = 8
GMLP_GROUP_DIM = 64
GMLP_WIDTH = GMLP_GROUPS * GMLP_GROUP_DIM
GMLP_CHUNK = 128
MIX_IN_WIDTH = ATTN_WIDTH + 2 * KV_WIDTH + 2 * GMLP_WIDTH
S5_GROUPS = 32
S5_GROUP_DIM = 16
S5_STATE = 64
S5_WIDTH = S5_GROUPS * S5_GROUP_DIM
N_EXPERT_GROUPS = 4
EXPERTS_PER_GROUP = 8
N_EXPERTS = N_EXPERT_GROUPS * EXPERTS_PER_GROUP
EXPERT_HIDDEN = 512
MOE_BLOCK = 512

D_MODEL = 1024
LANES = 128
ROW_SUB = D_MODEL // LANES
TOKEN_TILE = 256
PAIR = 4
ROW_DMA_UNROLL = 8
ATTN_KEY_BLOCK = 128
V_ONES_ROWS = 16
V_AUG_ROWS = N_KV_HEADS * (HEAD_DIM + V_ONES_ROWS)
S5_CHUNK = 64
S5_TILES = 4
VMEM_LIMIT = 48 * 1024 * 1024
NEG_BIG = -1e30

_Q_HEAD_ORDER = (0, 4, 1, 5, 2, 6, 3, 7)
_Q_COL_PERM = np.concatenate([np.arange(HEAD_DIM) + HEAD_DIM * h for h in _Q_HEAD_ORDER])


def _params(*sem):
    return pltpu.CompilerParams(dimension_semantics=sem, vmem_limit_bytes=VMEM_LIMIT)


def _rms(x):
    return x * lax.rsqrt(jnp.mean(x * x, axis=-1, keepdims=True) + EPS)


def _modulate(x, g, shift, scale):
    return _rms(x) * g * (1.0 + scale) + shift


def _full(shape):
    return pl.BlockSpec(shape, lambda *_: (0,) * len(shape))


def _split_bf16(a):
    hi = a.astype(BF16)
    return hi, (a - hi.astype(F32)).astype(BF16)


def _dot_split(a, w_hi, w_lo):
    a_hi, a_lo = _split_bf16(a)
    dot = functools.partial(jnp.dot, preferred_element_type=F32)
    return dot(a_hi, w_hi) + (dot(a_lo, w_hi) + dot(a_hi, w_lo))


def _store_row_tiles(ref, val):
    n = val.shape[0]
    for s in range(ROW_SUB):
        ref[pl.ds(s, n, stride=ROW_SUB), :] = val[:, LANES * s:LANES * (s + 1)]


def _load_row_tiles(ref):
    n = ref.shape[0] // ROW_SUB
    return jnp.concatenate([ref[pl.ds(s, n, stride=ROW_SUB), :] for s in range(ROW_SUB)], axis=1)


def _adaln_kernel(cond_ref, w_ref, b_ref, o_ref):
    c = cond_ref[...]
    s = c * jax.nn.sigmoid(c)
    o_ref[0] = _dot_split(s, *_split_bf16(w_ref[0])) + b_ref[0]


def _adaln(cond, w_mod, b_mod):
    depth, d, n = w_mod.shape
    r = cond.shape[0]
    tn = 1536
    return pl.pallas_call(
        _adaln_kernel,
        out_shape=jax.ShapeDtypeStruct((depth, r, n), F32),
        grid=(depth, n // tn),
        in_specs=[pl.BlockSpec((r, d), lambda l, j: (0, 0)),
                  pl.BlockSpec((1, d, tn), lambda l, j: (l, 0, j)),
                  pl.BlockSpec((1, 1, tn), lambda l, j: (l, 0, j))],
        out_specs=pl.BlockSpec((1, r, tn), lambda l, j: (l, 0, j)),
        compiler_params=_params("parallel", "parallel"),
        name="adaln",
    )(cond, w_mod, b_mod.reshape(depth, 1, n))


def _seq_tile(ctx_ref, x_ref, nct, j=0):
    return jnp.where(pl.program_id(1) < nct, ctx_ref[j], x_ref[j])


def _seq_specs(t, d, nct, nb=1):
    return [pl.BlockSpec((nb, t, d), lambda bi, ti: (bi, jnp.minimum(ti, nct - 1), 0)),
            pl.BlockSpec((nb, t, d), lambda bi, ti: (bi, jnp.maximum(ti - nct, 0), 0))]


def _l0_proj_kernel(ctx_ref, x_ref, mod_ref, g_ref, win_ref, qg_ref, kg_ref, lng_ref, cos_ref, sa_ref, sb_ref,
                    gq_ref, gk_ref, q_ref, k_ref, v_ref, gu_ref, gv_ref, *, nct):
    cos, sa, sb = cos_ref[...], sa_ref[...], sb_ref[...]

    def head_norm_rope(t, gsum_ref, gain, reps):
        w = t.shape[1]
        ss = jnp.dot((t * t).astype(BF16), gsum_ref[...], preferred_element_type=F32)
        tn = t * lax.rsqrt(ss * (1.0 / HEAD_DIM) + EPS) * gain
        c, a, b = (jnp.concatenate([tab] * reps, axis=1) if reps > 1 else tab for tab in (cos, sa, sb))
        return tn * c + pltpu.roll(tn, w - AXIS_DIM // 2, 1) * a + pltpu.roll(tn, AXIS_DIM // 2, 1) * b

    def prog(j):
        m = mod_ref[j, 0]
        h = _modulate(_seq_tile(ctx_ref, x_ref, nct, j), g_ref[...], m[0:1], m[1:2]).astype(BF16)
        yield
        z = jnp.dot(h, win_ref[...], preferred_element_type=F32)
        yield
        q = head_norm_rope(z[:, :ATTN_WIDTH], gq_ref, qg_ref[...], ATTN_WIDTH // LANES)
        q_ref[j] = (q * (HEAD_DIM ** -0.5)).T.astype(BF16)
        k = head_norm_rope(z[:, ATTN_WIDTH:ATTN_WIDTH + KV_WIDTH], gk_ref, kg_ref[...], 1)
        k_ref[j] = k.astype(BF16)
        vt = z[:, ATTN_WIDTH + KV_WIDTH:ATTN_WIDTH + 2 * KV_WIDTH].T
        ones = jnp.ones((V_ONES_ROWS, vt.shape[1]), F32)
        v_ref[j] = jnp.concatenate([part for kv in range(N_KV_HEADS)
                                    for part in (vt[HEAD_DIM * kv:HEAD_DIM * (kv + 1)], ones)], axis=0).astype(BF16)
        off = ATTN_WIDTH + 2 * KV_WIDTH
        gu_ref[j] = jax.nn.gelu(z[:, off:off + GMLP_WIDTH], approximate=True).astype(BF16)
        gv = jax.nn.gelu(z[:, off + GMLP_WIDTH:], approximate=True)
        mu = jnp.mean(gv, axis=-1, keepdims=True)
        gc = gv - mu
        gn = gc * lax.rsqrt(jnp.mean(gc * gc, axis=-1, keepdims=True) + EPS) * lng_ref[...]
        gv_ref[j] = gn.astype(BF16)

    _interleave(*(prog(j) for j in range(PAIR)))


def _l0_proj(ctx, x, mod, g, w_in, qg, kg, lng, cos, sa, sb, gq, gk, nct):
    b, s, d = x.shape
    l = s + ctx.shape[1]
    t = TOKEN_TILE
    tok = lambda w: pl.BlockSpec((PAIR, t, w), lambda bi, ti: (bi, ti, 0))
    tab = pl.BlockSpec((t, LANES), lambda bi, ti: (ti, 0))
    tok_t = lambda w: pl.BlockSpec((PAIR, w, t), lambda bi, ti: (bi, 0, ti))
    outs = ([jax.ShapeDtypeStruct((b, ATTN_WIDTH, l), BF16), jax.ShapeDtypeStruct((b, l, KV_WIDTH), BF16),
             jax.ShapeDtypeStruct((b, V_AUG_ROWS, l), BF16)]
            + [jax.ShapeDtypeStruct((b, l, GMLP_WIDTH), BF16)] * 2)
    return pl.pallas_call(
        functools.partial(_l0_proj_kernel, nct=nct),
        out_shape=outs,
        grid=(b // PAIR, l // t),
        in_specs=_seq_specs(t, d, nct, PAIR) + [
                  pl.BlockSpec((PAIR, 1, 6, d), lambda bi, ti: (bi, jnp.where(ti >= nct, 1, 0), 0, 0)),
                  _full((1, d)), _full((d, MIX_IN_WIDTH)), _full((1, ATTN_WIDTH)), _full((1, KV_WIDTH)),
                  _full((1, GMLP_WIDTH)), tab, tab, tab,
                  _full((ATTN_WIDTH, ATTN_WIDTH)), _full((KV_WIDTH, KV_WIDTH))],
        out_specs=[tok_t(ATTN_WIDTH), tok(KV_WIDTH), tok_t(V_AUG_ROWS), tok(GMLP_WIDTH), tok(GMLP_WIDTH)],
        compiler_params=_params("parallel", "parallel"),
        name="l0_proj",
    )(ctx, x, mod, g, w_in, qg, kg, lng, cos, sa, sb, gq, gk)


def _attn_kernel(qt_ref, k_ref, vt_ref, o_ref, s_ref, p_ref, *, n_ctx, nct):
    ti = pl.program_id(1)
    tq = qt_ref.shape[2]
    row = lax.broadcasted_iota(I32, (LANES, tq), 0)
    low = row < HEAD_DIM
    kb = ATTN_KEY_BLOCK
    va = HEAD_DIM + V_ONES_ROWS

    def attend(nk):
        heads = [(m, j) for m in range(ATTN_WIDTH // LANES) for j in range(2)]
        blocks = [slice(kb * b, kb * (b + 1)) for b in range(nk // kb)]
        state = {}

        def start(h):
            m, j = heads[h]
            qp = qt_ref[0, LANES * m:LANES * (m + 1), :]
            state[h] = dict(qm=jnp.where(low if j == 0 else jnp.logical_not(low), qp, jnp.zeros_like(qp)), mx=None)

        def score_block(h, blk):
            s = jnp.dot(k_ref[0, blk, :], state[h]["qm"], preferred_element_type=F32)
            s_ref[h % 2, blk, :] = s
            bm = jnp.max(s, axis=0, keepdims=True)
            state[h]["mx"] = bm if state[h]["mx"] is None else jnp.maximum(state[h]["mx"], bm)

        def exp_block(h, blk):
            p_ref[h % 2, blk, :] = jnp.exp(s_ref[h % 2, blk, :] - state[h]["mx"]).astype(BF16)

        def finish(h):
            j = heads[h][1]
            pv = jnp.dot(vt_ref[0, va * j:va * (j + 1), :nk], p_ref[h % 2, :nk, :],
                         preferred_element_type=F32)
            return pv[:HEAD_DIM] / pv[HEAD_DIM:HEAD_DIM + 1]

        start(0)
        for blk in blocks:
            score_block(0, blk)
        halves = []
        for h in range(len(heads)):
            if h + 1 < len(heads):
                start(h + 1)
            for blk in blocks:
                if h + 1 < len(heads):
                    score_block(h + 1, blk)
                exp_block(h, blk)
            halves.append(finish(h))
            if heads[h][1] == 1:
                m = heads[h][0]
                o_ref[0, :, LANES * m:LANES * (m + 1)] = jnp.concatenate(halves, axis=0).T.astype(BF16)
                halves = []

    @pl.when(ti < nct)
    def _():
        attend(n_ctx)

    @pl.when(ti >= nct)
    def _():
        attend(k_ref.shape[1])


def _attention(qt, k, vt, n_ctx):
    b, l, _ = k.shape
    t = TOKEN_TILE
    kern = functools.partial(_attn_kernel, n_ctx=n_ctx, nct=n_ctx // t)
    return pl.pallas_call(
        kern,
        out_shape=jax.ShapeDtypeStruct((b, l, ATTN_WIDTH), BF16),
        grid=(b, l // t),
        in_specs=[pl.BlockSpec((1, ATTN_WIDTH, t), lambda bi, ti: (bi, 0, ti)),
                  pl.BlockSpec((1, l, KV_WIDTH), lambda bi, ti: (bi, 0, 0)),
                  pl.BlockSpec((1, V_AUG_ROWS, l), lambda bi, ti: (bi, 0, 0))],
        out_specs=pl.BlockSpec((1, t, ATTN_WIDTH), lambda bi, ti: (bi, ti, 0)),
        scratch_shapes=[pltpu.VMEM((2, l, t), F32), pltpu.VMEM((2, l, t), BF16)],
        compiler_params=_params("parallel", "parallel"),
        name="attention",
    )(qt, k, vt)


def _interleave(*progs):
    live = dict(enumerate(progs))
    rnd = 0
    while live:
        for k in sorted(live):
            if rnd >= k:
                try:
                    next(live[k])
                except StopIteration:
                    del live[k]
        rnd += 1


def _route_tail(j, x, y, m, gf, wr_ref, br_ref, shared, x_ref, f_ref, ri_ref, rw_ref):
    x1 = x + m[2:3] * y
    x_ref[j] = x1
    f = _modulate(x1, gf, m[3:4], m[4:5])
    _store_row_tiles(f_ref.at[j], f)
    t = f.shape[0]
    yield
    logits = _dot_split(f, wr_ref[0], wr_ref[1]) + br_ref[...]
    yield
    lane = lax.broadcasted_iota(I32, (t, LANES), 1).astype(F32)
    is_grp = lane < N_EXPERT_GROUPS
    gl = jnp.where(is_grp, logits, NEG_BIG)
    gmax = jnp.max(gl, axis=-1, keepdims=True)
    g_idx = jnp.min(jnp.where(gl == gmax, lane, float(LANES)), axis=-1, keepdims=True)
    g_w = 1.0 / jnp.sum(jnp.where(is_grp, jnp.exp(gl - gmax), 0.0), axis=-1, keepdims=True)
    e_lo = N_EXPERT_GROUPS + EXPERTS_PER_GROUP * g_idx
    in_grp = jnp.logical_and(lane >= e_lo, lane < e_lo + EXPERTS_PER_GROUP)
    sel = jnp.where(in_grp, logits, NEG_BIG)
    v1 = jnp.max(sel, axis=-1, keepdims=True)
    i1 = jnp.min(jnp.where(sel == v1, lane, float(LANES)), axis=-1, keepdims=True)
    sel2 = jnp.where(lane == i1, NEG_BIG, sel)
    v2 = jnp.max(sel2, axis=-1, keepdims=True)
    i2 = jnp.min(jnp.where(sel2 == v2, lane, float(LANES)), axis=-1, keepdims=True)
    e = jnp.exp(v2 - v1)
    w1 = g_w / (1.0 + e)
    w2 = g_w * e / (1.0 + e)
    eid1 = i1 - N_EXPERT_GROUPS
    eid2 = i2 - N_EXPERT_GROUPS
    hit1 = lane == eid1
    hit2 = lane == eid2
    cnt = (hit1.astype(F32) + hit2.astype(F32))
    yield

    row = lax.broadcasted_iota(I32, (t, t), 0)
    col = lax.broadcasted_iota(I32, (t, t), 1)
    tril = jnp.where(col < row, 1.0, 0.0).astype(BF16)
    before = jnp.dot(tril, cnt.astype(BF16), preferred_element_type=F32) + shared["base"]
    shared["base"] = shared["base"] + jnp.sum(cnt, axis=0, keepdims=True)
    rank1 = jnp.sum(jnp.where(hit1, before, 0.0), axis=-1, keepdims=True)
    rank2 = jnp.sum(jnp.where(hit2, before, 0.0), axis=-1, keepdims=True)

    ri = jnp.where(lane == 0, eid1, jnp.where(lane == 1, eid2, jnp.where(lane == 2, rank1,
                   jnp.where(lane == 3, rank2, 0.0))))
    ri_ref[j] = ri.T[0:8, :].astype(I32)
    l8 = lax.broadcasted_iota(I32, (t, 8), 1)
    rw_ref[j] = jnp.where(l8 == 0, w1, jnp.where(l8 == 1, w2, 0.0))


def _run_pair(prog, base_ref, cnt_ref):
    @pl.when(jnp.logical_and(pl.program_id(0) == 0, pl.program_id(1) == 0))
    def _():
        base_ref[...] = jnp.zeros_like(base_ref)

    shared = {"base": base_ref[0:1, :]}
    _interleave(*(prog(j, shared) for j in range(PAIR)))
    base_ref[...] = jnp.broadcast_to(shared["base"], base_ref.shape)
    cnt_ref[...] = jnp.broadcast_to(shared["base"], cnt_ref.shape)


def _tail_out_shapes(b, n_rows, d):
    t = TOKEN_TILE
    shapes = [jax.ShapeDtypeStruct((b, n_rows, d), F32), jax.ShapeDtypeStruct((b, n_rows * ROW_SUB, LANES), F32),
              jax.ShapeDtypeStruct((b, 8, n_rows), I32), jax.ShapeDtypeStruct((b, n_rows, 8), F32),
              jax.ShapeDtypeStruct((8, LANES), F32)]
    specs = [pl.BlockSpec((PAIR, t, d), lambda bp, ti: (bp, ti, 0)),
             pl.BlockSpec((PAIR, t * ROW_SUB, LANES), lambda bp, ti: (bp, ti, 0)),
             pl.BlockSpec((PAIR, 8, t), lambda bp, ti: (bp, 0, ti)),
             pl.BlockSpec((PAIR, t, 8), lambda bp, ti: (bp, ti, 0)),
             pl.BlockSpec((8, LANES), lambda bp, ti: (0, 0))]
    return shapes, specs


def _tail_flatten(x1, f, ri, rw):
    b, n_rows, d = x1.shape
    return (x1, f.reshape(b * n_rows * ROW_SUB, LANES), jnp.transpose(ri, (1, 0, 2)).reshape(8, b * n_rows),
            rw.reshape(b * n_rows, 8))


def _l0_post_kernel(ctx_ref, x_ref, o_ref, gu_ref, gv_ref, wsp_ref, bsp_ref, wout_ref, mod_ref, gf_ref, wr_ref,
                    br_ref, x1_ref, f_ref, ri_ref, rw_ref, cnt_ref, base_ref, *, nct):
    tq = x_ref.shape[1]
    lane = lax.broadcasted_iota(I32, (GMLP_CHUNK, LANES), 1)
    low = lane < GMLP_GROUP_DIM

    def prog(j, shared):
        gated = []
        for c in range(tq // GMLP_CHUNK):
            rows = slice(c * GMLP_CHUNK, (c + 1) * GMLP_CHUNK)
            mixed = []
            for m in range(GMLP_WIDTH // LANES):
                pair = gv_ref[j, rows, LANES * m:LANES * (m + 1)]
                a = jnp.dot(wsp_ref[2 * m], pair, preferred_element_type=F32)
                b = jnp.dot(wsp_ref[2 * m + 1], pair, preferred_element_type=F32)
                mixed.append(jnp.where(low, a, b))
            mixed = jnp.concatenate(mixed, axis=1) + bsp_ref[...]
            gated.append((gu_ref[j, rows, :].astype(F32) * mixed).astype(BF16))
        gated = jnp.concatenate(gated, axis=0)
        yield
        y = (jnp.dot(o_ref[j], wout_ref[:ATTN_WIDTH, :], preferred_element_type=F32)
             + jnp.dot(gated, wout_ref[ATTN_WIDTH:, :], preferred_element_type=F32))
        yield
        yield from _route_tail(j, _seq_tile(ctx_ref, x_ref, nct, j), y, mod_ref[j, 0], gf_ref[...], wr_ref, br_ref,
                               shared, x1_ref, f_ref, ri_ref, rw_ref)

    _run_pair(prog, base_ref, cnt_ref)


def _l0_post(ctx, x, o, gu, gv, wsp, bsp, wout, mod, gf, wr, br, nct):
    b, s, d = x.shape
    l = s + ctx.shape[1]
    t = TOKEN_TILE
    tok = lambda w: pl.BlockSpec((PAIR, t, w), lambda bp, ti: (bp, ti, 0))
    shapes, specs = _tail_out_shapes(b, l, d)
    outs = pl.pallas_call(
        functools.partial(_l0_post_kernel, nct=nct),
        out_shape=shapes,
        grid=(b // PAIR, l // t),
        in_specs=_seq_specs(t, d, nct, PAIR) + [tok(ATTN_WIDTH), tok(GMLP_WIDTH), tok(GMLP_WIDTH),
                  _full((GMLP_GROUPS, GMLP_CHUNK, GMLP_CHUNK)), _full((GMLP_CHUNK, GMLP_WIDTH)),
                  _full((ATTN_WIDTH + GMLP_WIDTH, d)),
                  pl.BlockSpec((PAIR, 1, 6, d), lambda bp, ti: (bp, jnp.where(ti >= nct, 1, 0), 0, 0)),
                  _full((1, d)), _full((2, d, LANES)), _full((1, LANES))],
        out_specs=specs,
        scratch_shapes=[pltpu.VMEM((8, LANES), F32)],
        compiler_params=_params("arbitrary", "arbitrary"),
        name="l0_post",
    )(ctx, x, o, gu, gv, wsp, bsp, wout, mod, gf, wr, br)
    return _tail_flatten(*outs[:4]) + (outs[4],)


def _row_copy(src_ref, src_row, dst_ref, dst_row, sem):
    rows = lambda ref, r: ref.at[pl.ds(pl.multiple_of(r * ROW_SUB, ROW_SUB), ROW_SUB)]
    return pltpu.make_async_copy(rows(src_ref, src_row), rows(dst_ref, dst_row), sem)


def _issue_rows(src_ref, index_of, buf_ref, sem_rows, s):
    n_str, rows = buf_ref.shape[1], buf_ref.shape[2] // ROW_SUB
    slot = s % 2

    def issue(j, carry):
        for half in range(2):
            r = 2 * j + half
            for k in range(n_str):
                _row_copy(src_ref, index_of(k * rows + r), buf_ref.at[slot, k], r,
                          sem_rows.at[slot]).start(priority=(k + half) % 2)
        return carry

    lax.fori_loop(0, rows // 2, issue, 0, unroll=ROW_DMA_UNROLL // 2)


def _wait_rows(src_ref, buf_ref, sem_rows, s):
    n_str, rows = buf_ref.shape[1], buf_ref.shape[2] // ROW_SUB
    slot = s % 2
    for k in range(n_str):
        pltpu.make_async_copy(src_ref.at[pl.ds(0, rows * ROW_SUB)], buf_ref.at[slot, k], sem_rows.at[slot]).wait()
    return slot


def _gather_pipeline(src_ref, table_ref, idx_ref, buf_ref, sem_idx, sem_rows, step, n_steps):
    def idx_copy(s):
        return pltpu.make_async_copy(table_ref.at[s], idx_ref.at[s % 2], sem_idx.at[s % 2])

    def issue_rows(s):
        _issue_rows(src_ref, lambda pos: idx_ref[s % 2, pos], buf_ref, sem_rows, s)

    @pl.when(step == 0)
    def _():
        idx_copy(0).start()
        idx_copy(0).wait()
        issue_rows(0)

        @pl.when(n_steps > 1)
        def _():
            idx_copy(1).start()

    @pl.when(step + 1 < n_steps)
    def _():
        idx_copy(step + 1).wait()
        issue_rows(step + 1)

        @pl.when(step + 2 < n_steps)
        def _():
            idx_copy(step + 2).start()

    return _wait_rows(src_ref, buf_ref, sem_rows, step)


def _gather_scratch(n_str, rows):
    return [pltpu.SMEM((2, n_str * rows), I32), pltpu.VMEM((2, n_str, rows * ROW_SUB, LANES), F32),
            pltpu.SemaphoreType.DMA((2,)), pltpu.SemaphoreType.DMA((2,))]


def _dispatch_kernel(pad_ref, nu_ref, f_ref, slots_ref, xb_ref, idx_ref, zero_ref, sem_idx, sem_rows):
    i = pl.program_id(0)
    t = f_ref.shape[0] // ROW_SUB
    blk = MOE_BLOCK * ROW_SUB
    idx_cp = pltpu.make_async_copy(slots_ref.at[i], idx_ref, sem_idx)
    idx_cp.start()

    @pl.when(i == 0)
    def _():
        zero_ref[...] = jnp.zeros_like(zero_ref)
        n_blocks = xb_ref.shape[0] // blk

        def zero_block(start):
            return pltpu.make_async_copy(zero_ref, xb_ref.at[pl.ds(pl.multiple_of(start * ROW_SUB, blk), blk)],
                                         sem_rows)

        def tail_start(j, carry):
            zero_block(j * MOE_BLOCK).start()
            return carry

        def tail_wait(j, carry):
            zero_block(0).wait()
            return carry

        for e in range(N_EXPERTS):
            zero_block(jnp.maximum(pad_ref[N_EXPERTS + e] - MOE_BLOCK, 0)).start()
        lax.fori_loop(nu_ref[0], n_blocks, tail_start, 0)
        for e in range(N_EXPERTS):
            zero_block(0).wait()
        lax.fori_loop(nu_ref[0], n_blocks, tail_wait, 0)

    idx_cp.wait()

    def issue(r, carry):
        _row_copy(f_ref, r, xb_ref, idx_ref[r], sem_rows).start(priority=0)
        _row_copy(f_ref, r, xb_ref, idx_ref[t + r], sem_rows).start(priority=1)
        return carry

    lax.fori_loop(0, t, issue, 0, unroll=ROW_DMA_UNROLL)
    for _ in range(2):
        pltpu.make_async_copy(f_ref, xb_ref.at[pl.ds(0, t * ROW_SUB)], sem_rows).wait()


def _dispatch(f, slots, pad, n_used, n_slots):
    t = TOKEN_TILE
    n_tok = f.shape[0] // ROW_SUB
    return pl.pallas_call(
        _dispatch_kernel,
        out_shape=jax.ShapeDtypeStruct((n_slots * ROW_SUB, LANES), F32),
        grid_spec=pltpu.PrefetchScalarGridSpec(
            num_scalar_prefetch=2,
            grid=(n_tok // t,),
            in_specs=[pl.BlockSpec((t * ROW_SUB, LANES), lambda i, p, nu: (i, 0)),
                      pl.BlockSpec(memory_space=pltpu.VMEM)],
            out_specs=pl.BlockSpec(memory_space=pl.ANY),
            scratch_shapes=[pltpu.SMEM((2 * t,), I32), pltpu.VMEM((MOE_BLOCK * ROW_SUB, LANES), F32),
                            pltpu.SemaphoreType.DMA, pltpu.SemaphoreType.DMA]),
        compiler_params=_params("arbitrary"),
        name="moe_dispatch",
    )(pad, n_used, f, slots)


def _experts_kernel(be_ref, nu_ref, xb_ref, w1_ref, w3_ref, w2_ref, y_ref, w1b_ref, w3b_ref, w2b_ref):
    i = pl.program_id(0)

    @pl.when(jnp.logical_or(i == 0, be_ref[i] != be_ref[jnp.maximum(i - 1, 0)]))
    def _():
        w1b_ref[...] = w1_ref[...].astype(BF16)
        w3b_ref[...] = w3_ref[...].astype(BF16)
        w2b_ref[...] = w2_ref[...].astype(BF16)

    @pl.when(i < nu_ref[0])
    def _():
        x = _load_row_tiles(xb_ref).astype(BF16)
        h1 = jnp.dot(x, w1b_ref[...], preferred_element_type=F32)
        h3 = jnp.dot(x, w3b_ref[...], preferred_element_type=F32)
        a = (h1 * jax.nn.sigmoid(h1) * h3).astype(BF16)
        _store_row_tiles(y_ref, jnp.dot(a, w2b_ref[...], preferred_element_type=F32))

    @pl.when(i >= nu_ref[0])
    def _():
        y_ref[...] = jnp.zeros_like(y_ref)


def _experts(xb, w1, w3, w2, layer, block_e, n_used):
    n_blocks = block_e.shape[0]
    _, _, d, hid = w1.shape
    rb = MOE_BLOCK * ROW_SUB
    wspec = lambda s: pl.BlockSpec((None, None) + s, lambda i, be, nu: (layer, be[i], 0, 0))
    return pl.pallas_call(
        _experts_kernel,
        out_shape=jax.ShapeDtypeStruct((n_blocks * rb, LANES), F32),
        grid_spec=pltpu.PrefetchScalarGridSpec(
            num_scalar_prefetch=2,
            grid=(n_blocks,),
            in_specs=[pl.BlockSpec((rb, LANES), lambda i, be, nu: (jnp.minimum(i, nu[0] - 1), 0)),
                      wspec((d, hid)), wspec((d, hid)), wspec((hid, d))],
            out_specs=pl.BlockSpec((rb, LANES), lambda i, be, nu: (i, 0)),
            scratch_shapes=[pltpu.VMEM((d, hid), BF16), pltpu.VMEM((d, hid), BF16), pltpu.VMEM((hid, d), BF16)]),
        compiler_params=_params("arbitrary"),
        name="moe_experts",
    )(block_e, n_used, xb, w1, w3, w2)


def _moe_plan(route_i, counts, n_tok):
    eid, rank = route_i[0:2], route_i[2:4]
    cnt = counts[0, :N_EXPERTS].astype(I32)
    padded = (cnt + MOE_BLOCK - 1) // MOE_BLOCK * MOE_BLOCK
    pend = jnp.cumsum(padded)
    pstart = pend - padded
    experts = jnp.arange(N_EXPERTS, dtype=I32)
    slots = rank + jnp.sum(jnp.where(eid[:, :, None] == experts, pstart, 0), axis=-1)
    n_assign = 2 * n_tok
    n_slots = (n_assign + MOE_BLOCK - 1) // MOE_BLOCK * MOE_BLOCK + N_EXPERTS * MOE_BLOCK
    n_blocks = n_slots // MOE_BLOCK
    bstart = jnp.arange(n_blocks, dtype=I32) * MOE_BLOCK
    block_e = jnp.minimum(jnp.sum(bstart[:, None] >= pend[None, :], axis=1), N_EXPERTS - 1).astype(I32)
    n_used = pend[-1] // MOE_BLOCK
    last_e = block_e[jnp.maximum(n_used - 1, 0)]
    block_e = jnp.where(jnp.arange(n_blocks) < n_used, block_e, last_e).astype(I32)
    slots = slots.reshape(2, n_tok // TOKEN_TILE, TOKEN_TILE).transpose(1, 0, 2)
    slots = slots.reshape(n_tok // TOKEN_TILE, 2 * TOKEN_TILE).astype(I32)
    pad = jnp.concatenate([pstart + cnt, pend]).astype(I32)
    return slots, block_e, n_used.reshape(1).astype(I32), pad, n_slots


def _moe(f, route_i, counts, w1, w3, w2, layer):
    n_tok = f.shape[0] // ROW_SUB
    slots, block_e, n_used, pad, n_slots = _moe_plan(route_i, counts, n_tok)
    xb = _dispatch(f, slots, pad, n_used, n_slots)
    y = _experts(xb, w1, w3, w2, layer, block_e, n_used)
    return y, slots


def _combined_rows(y_ref, slots_ref, rw_ref, idx_ref, ybuf_ref, sem_idx, sem_rows):
    slot = _gather_pipeline(y_ref, slots_ref, idx_ref, ybuf_ref, sem_idx, sem_rows,
                            pl.program_id(0), pl.num_programs(0))
    rw = rw_ref[...]
    w1, w2 = rw[:, 0:1], rw[:, 1:2]
    t = ybuf_ref.shape[2] // ROW_SUB
    part = lambda k, s: ybuf_ref[slot, k, pl.ds(s, t, stride=ROW_SUB), :]
    return jnp.concatenate([w1 * part(0, s) + w2 * part(1, s) for s in range(ROW_SUB)], axis=1)


def _l0_combine_kernel(x_ref, rw_ref, slots_ref, y_ref, mod0_ref, mod1_ref, g_ref, win_ref,
                       x2_ref, u_ref, ub_ref, idx_ref, ybuf_ref, sem_idx, sem_rows):
    moe = _combined_rows(y_ref, slots_ref, rw_ref, idx_ref, ybuf_ref, sem_idx, sem_rows)
    x2 = x_ref[...] + mod0_ref[0, 0][5:6] * moe
    x2_ref[...] = x2
    m1 = mod1_ref[0, 0]
    h = _modulate(x2, g_ref[...], m1[0:1], m1[1:2]).astype(BF16)
    u = jnp.dot(h, win_ref[...], preferred_element_type=F32)
    u_ref[...] = u
    ub_ref[...] = u.astype(BF16)


def _l0_combine(x1, rw, slots, y, mod0, mod1, g, win, b, nct):
    n_tok, d = x1.shape
    t = TOKEN_TILE
    nt = n_tok // b // t
    l = n_tok // b
    modspec = pl.BlockSpec((1, 1, 6, d), lambda i: (i // nt, jnp.where(i % nt >= nct, 1, 0), 0, 0))
    return pl.pallas_call(
        _l0_combine_kernel,
        out_shape=[jax.ShapeDtypeStruct((n_tok, d), F32), jax.ShapeDtypeStruct((l, b * S5_WIDTH), F32),
                   jax.ShapeDtypeStruct((l, b * S5_WIDTH), BF16)],
        grid=(n_tok // t,),
        in_specs=[pl.BlockSpec((t, d), lambda i: (i, 0)), pl.BlockSpec((t, 8), lambda i: (i, 0)),
                  pl.BlockSpec(memory_space=pltpu.VMEM), pl.BlockSpec(memory_space=pl.ANY),
                  modspec, modspec, _full((1, d)), _full((d, S5_WIDTH))],
        out_specs=[pl.BlockSpec((t, d), lambda i: (i, 0)),
                   pl.BlockSpec((t, S5_WIDTH), lambda i: (i % nt, i // nt)),
                   pl.BlockSpec((t, S5_WIDTH), lambda i: (i % nt, i // nt))],
        scratch_shapes=_gather_scratch(2, t),
        compiler_params=_params("arbitrary"),
        name="l0_combine",
    )(x1, rw, slots, y, mod0, mod1, g, win)


def _s5_zoh_kernel(are_ref, aim_ref, ldt_ref, bre_ref, bim_ref, lre_ref, lim_ref, bbre_ref, bbim_ref):
    ar, ai = are_ref[...], aim_ref[...]
    dt = jnp.exp(ldt_ref[...])
    mag = jnp.exp(ar * dt)
    lr = mag * jnp.cos(ai * dt)
    li = mag * jnp.sin(ai * dt)
    lre_ref[...] = lr
    lim_ref[...] = li
    nr, ni = lr - 1.0, li
    den = ar * ar + ai * ai
    cr = (nr * ar + ni * ai) / den
    ci = (ni * ar - nr * ai) / den
    br, bi = bre_ref[...], bim_ref[...]
    bbre_ref[...] = cr * br - ci * bi
    bbim_ref[...] = cr * bi + ci * br


def _s5_zoh(a_re, a_im, log_dt, b_re, b_im):
    n_dir, g, n, p = b_re.shape
    rows = n_dir * g * p
    expand = lambda a: jnp.broadcast_to(a[:, :, None, :], (n_dir, g, p, n)).reshape(rows, n)
    ldt = jnp.broadcast_to(log_dt[:, :, None, None], (n_dir, g, p, n)).reshape(rows, n)
    bt = lambda a: jnp.transpose(a, (0, 1, 3, 2)).reshape(rows, n)
    outs = pl.pallas_call(
        _s5_zoh_kernel,
        out_shape=[jax.ShapeDtypeStruct((rows, n), F32)] * 4,
        name="s5_zoh",
    )(expand(a_re), expand(a_im), ldt, bt(b_re), bt(b_im))
    lre, lim, bbre, bbim = (o.reshape(n_dir, g, p, n) for o in outs)
    return lre[:, :, 0, :], lim[:, :, 0, :], bbre, bbim


def _s5_matrices(lre, lim, bbre, bbim, c_re, c_im):
    n_dir = lre.shape[0]
    nj = S5_GROUPS // (2 * S5_TILES)
    shp = (n_dir, nj, S5_TILES, 2, S5_GROUP_DIM, S5_STATE)
    eye_t = jnp.eye(S5_TILES, dtype=F32)
    eye_g = jnp.eye(2, dtype=F32)

    def place(a):
        return jnp.einsum("djtgpn,tu,gh->djtgpuhn", a.reshape(shp), eye_t, eye_g)

    kdim = S5_TILES * 2 * S5_GROUP_DIM
    sdim = S5_TILES * 2 * 2 * S5_STATE
    bm = jnp.stack([place(bbre), place(bbim)], axis=6)
    bm = bm.reshape(n_dir, nj, kdim, sdim).astype(BF16)
    cm = jnp.stack([place(c_re), place(-c_im)], axis=6)
    cm = jnp.transpose(cm, (0, 1, 5, 6, 7, 8, 2, 3, 4)).reshape(n_dir, nj, sdim, kdim).astype(BF16)
    lam = jnp.stack([lre.reshape(n_dir, nj, S5_TILES * LANES), lim.reshape(n_dir, nj, S5_TILES * LANES)], axis=2)
    return bm, cm, lam


def _s5_scan_kernel(uf_ref, ub_ref, bm_ref, cm_ref, lam_ref, yf_ref, yb_ref, v_ref, h_ref, acc_ref, *, tc, nb):
    @pl.when(pl.program_id(0) == 0)
    def _():
        h_ref[...] = jnp.zeros_like(h_ref)

    nj, kdim, sdim = bm_ref.shape[1], bm_ref.shape[2], bm_ref.shape[3]
    u_refs, y_refs = (uf_ref, ub_ref), (yf_ref, yb_ref)
    phases = [(d, jq) for d in range(2) for jq in range(nj)]
    n_sub = 2 * S5_TILES
    col = sdim // S5_TILES
    run = tc // n_sub
    state = {}

    def drive_piece(p, c):
        d, jq = phases[p]
        if c == 0:
            state["uk"] = u_refs[d][:, kdim * jq:kdim * (jq + 1)]
        v_ref[p % 3, :, col * c:col * (c + 1)] = jnp.dot(state["uk"], bm_ref[d, jq, :, col * c:col * (c + 1)],
                                                         preferred_element_type=F32)

    def readout_piece(p, c):
        d, jq = phases[p]
        part = jnp.dot(v_ref[p % 3, :, col * c:col * (c + 1)].astype(BF16), cm_ref[d, jq, col * c:col * (c + 1), :],
                       preferred_element_type=F32)
        if c == 0:
            acc_ref[...] = part
        elif c < S5_TILES - 1:
            acc_ref[...] += part
        else:
            y_refs[d][:, kdim * jq:kdim * (jq + 1)] = (acc_ref[...] + part).astype(BF16)

    def scan_run(p, c):
        d, jq = phases[p]
        if c == 0:
            lam = lam_ref[d, jq]
            state["lr"] = [jnp.broadcast_to(lam[0:1, LANES * t:LANES * (t + 1)], (nb, LANES)) for t in range(S5_TILES)]
            state["li"] = [jnp.broadcast_to(lam[1:2, LANES * t:LANES * (t + 1)], (nb, LANES)) for t in range(S5_TILES)]
            h0 = h_ref[d, jq]
            state["h"] = [h0[:, LANES * i:LANES * (i + 1)] for i in range(2 * S5_TILES)]
        lr, li, h = state["lr"], state["li"], state["h"]
        for s in range(run * c, run * (c + 1)):
            r0 = (s if d == 0 else tc - 1 - s) * nb
            rows = v_ref[p % 3, r0:r0 + nb, :]
            new = []
            for t in range(S5_TILES):
                vr = rows[:, 2 * LANES * t:2 * LANES * t + LANES]
                vi = rows[:, 2 * LANES * t + LANES:2 * LANES * (t + 1)]
                hr, hi = h[2 * t], h[2 * t + 1]
                new.append(lr[t] * hr - li[t] * hi + vr)
                new.append(lr[t] * hi + li[t] * hr + vi)
            v_ref[p % 3, r0:r0 + nb, :] = jnp.concatenate(new, axis=1)
            h = new
        state["h"] = h
        if c == n_sub - 1:
            h_ref[d, jq] = jnp.concatenate(h, axis=1)

    for slot in range(len(phases) + 2):
        for c in range(n_sub):
            if 0 <= slot - 1 < len(phases):
                scan_run(slot - 1, c)
            if c < S5_TILES and slot < len(phases):
                drive_piece(slot, c)
            if c >= S5_TILES and 0 <= slot - 2 < len(phases):
                readout_piece(slot - 2, c - S5_TILES)


def _s5_scan(u_tm, bm, cm, lam, nb, n_ctx):
    rows_total, w = u_tm.shape
    tc = S5_CHUNK
    r = tc * nb
    n_chunks = rows_total // r
    ncc = n_ctx // tc
    bwd = lambda i: (jnp.where(i < ncc, ncc - 1 - i, n_chunks - 1 - (i - ncc)), 0)
    nj = bm.shape[1]
    sdim = bm.shape[3]
    kern = functools.partial(_s5_scan_kernel, tc=tc, nb=nb)
    return pl.pallas_call(
        kern,
        out_shape=[jax.ShapeDtypeStruct((rows_total, w), BF16)] * 2,
        grid=(n_chunks,),
        in_specs=[pl.BlockSpec((r, w), lambda i: (i, 0)), pl.BlockSpec((r, w), bwd),
                  _full(bm.shape), _full(cm.shape), _full(lam.shape)],
        out_specs=[pl.BlockSpec((r, w), lambda i: (i, 0)), pl.BlockSpec((r, w), bwd)],
        scratch_shapes=[pltpu.VMEM((3, r, sdim), F32), pltpu.VMEM((2, nj, nb, sdim), F32),
                        pltpu.VMEM((r, bm.shape[2]), F32)],
        compiler_params=_params("arbitrary"),
        name="s5_scan",
    )(u_tm, u_tm, bm, cm, lam)


def _l1_post_kernel(x_ref, u_ref, yf_ref, yb_ref, dsk_ref, wglu_ref, bglu_ref, wout_ref, mod_ref, gf_ref,
                    wr_ref, br_ref, x3_ref, f_ref, ri_ref, rw_ref, cnt_ref, base_ref):
    def prog(j, shared):
        cols = slice(S5_WIDTH * j, S5_WIDTH * (j + 1))
        y = dsk_ref[...] * u_ref[:, cols] + yf_ref[:, cols].astype(F32) + yb_ref[:, cols].astype(F32)
        g = jax.nn.gelu(y, approximate=True)
        yield
        z = jnp.dot(g.astype(BF16), wglu_ref[...], preferred_element_type=F32) + bglu_ref[...]
        yield
        a = (g * jax.nn.sigmoid(z)).astype(BF16)
        yield
        o = jnp.dot(a, wout_ref[...], preferred_element_type=F32)
        yield
        yield from _route_tail(j, x_ref[j], o, mod_ref[j, 0], gf_ref[...], wr_ref, br_ref, shared,
                               x3_ref, f_ref, ri_ref, rw_ref)

    _run_pair(prog, base_ref, cnt_ref)


def _l1_post(x2, u_tm, yf, yb, dsk, wglu, bglu, wout, mod, gf, wr, br, n_ctx):
    b, l, d = x2.shape
    t = TOKEN_TILE
    nct = n_ctx // t
    s = l - n_ctx
    tm = pl.BlockSpec((t, PAIR * S5_WIDTH), lambda bp, ti: (ti + nct, bp))
    shapes, specs = _tail_out_shapes(b, s, d)
    outs = pl.pallas_call(
        _l1_post_kernel,
        out_shape=shapes,
        grid=(b // PAIR, s // t),
        in_specs=[pl.BlockSpec((PAIR, t, d), lambda bp, ti: (bp, ti + nct, 0)), tm, tm, tm,
                  _full((1, S5_WIDTH)), _full((S5_WIDTH, S5_WIDTH)), _full((1, S5_WIDTH)), _full((S5_WIDTH, d)),
                  pl.BlockSpec((PAIR, 1, 6, d), lambda bp, ti: (bp, 1, 0, 0)),
                  _full((1, d)), _full((2, d, LANES)), _full((1, LANES))],
        out_specs=specs,
        scratch_shapes=[pltpu.VMEM((8, LANES), F32)],
        compiler_params=_params("arbitrary", "arbitrary"),
        name="l1_post",
    )(x2, u_tm, yf, yb, dsk, wglu, bglu, wout, mod, gf, wr, br)
    return _tail_flatten(*outs[:4]) + (outs[4],)


def _final_kernel(x_ref, rw_ref, slots_ref, y_ref, mod_ref, g_ref, o_ref, idx_ref, ybuf_ref, sem_idx, sem_rows):
    moe = _combined_rows(y_ref, slots_ref, rw_ref, idx_ref, ybuf_ref, sem_idx, sem_rows)
    x4 = x_ref[...] + mod_ref[0, 0][5:6] * moe
    o_ref[...] = _rms(x4) * g_ref[...]


def _final(x3, rw, slots, y, mod, g, b):
    n_tok, d = x3.shape
    t = TOKEN_TILE
    nt = n_tok // b // t
    return pl.pallas_call(
        _final_kernel,
        out_shape=jax.ShapeDtypeStruct((n_tok, d), F32),
        grid=(n_tok // t,),
        in_specs=[pl.BlockSpec((t, d), lambda i: (i, 0)), pl.BlockSpec((t, 8), lambda i: (i, 0)),
                  pl.BlockSpec(memory_space=pltpu.VMEM), pl.BlockSpec(memory_space=pl.ANY),
                  pl.BlockSpec((1, 1, 6, d), lambda i: (i // nt, 1, 0, 0)), _full((1, d))],
        out_specs=pl.BlockSpec((t, d), lambda i: (i, 0)),
        scratch_shapes=_gather_scratch(2, t),
        compiler_params=_params("arbitrary"),
        name="final_combine",
    )(x3, rw, slots, y, mod, g)


def _rope_tables(n_ctx, n_lat):
    pos = jnp.arange(n_lat, dtype=I32)
    row = (pos // GRID_W).astype(F32)
    col = (pos % GRID_W).astype(F32)
    inv_freq = jnp.power(ROPE_THETA, -jnp.arange(0, AXIS_DIM, 2, dtype=F32) / AXIS_DIM)
    lane = np.arange(LANES)
    dd = lane % HEAD_DIM
    use_col = dd >= AXIS_DIM
    first = (dd % AXIS_DIM) < AXIS_DIM // 2
    freq = inv_freq[(dd % AXIS_DIM) % (AXIS_DIM // 2)]
    ang = jnp.where(use_col[None, :], col[:, None], row[:, None]) * freq[None, :]
    cos, sin = jnp.cos(ang), jnp.sin(ang)
    sa = jnp.where(first[None, :], -sin, 0.0)
    sb = jnp.where(first[None, :], 0.0, sin)
    pad = lambda a, v: jnp.concatenate([jnp.full((n_ctx, LANES), v, F32), a], axis=0)
    return pad(cos, 1.0), pad(sa, 0.0), pad(sb, 0.0)


def _group_ones(width, group):
    idx = np.arange(width) // group
    return jnp.asarray(idx[:, None] == idx[None, :], dtype=BF16)


def _router_weights(w_grp, b_grp, w_rt, b_rt):
    d = w_grp.shape[0]
    used = N_EXPERT_GROUPS + N_EXPERTS
    wr = jnp.concatenate([w_grp, w_rt, jnp.zeros((d, LANES - used), F32)], axis=1)
    br = jnp.concatenate([b_grp, b_rt, jnp.zeros((LANES - used,), F32)]).reshape(1, LANES)
    return jnp.stack(_split_bf16(wr)), br


def kernel(x, c, ctx, c_ctx, w_mod, b_mod, norm_mix_g, norm_ffn_g, mix_w_in, mix_w_out, q_norm_g, k_norm_g, gmlp_norm_g, gmlp_w_spatial, gmlp_b_spatial, s5_w_in, s5_a_re, s5_a_im, s5_log_dt, s5_b_re, s5_b_im, s5_c_re, s5_c_im, s5_d, s5_w_glu, s5_b_glu, s5_w_out, moe_w_group, moe_b_group, moe_w_router, moe_b_router, moe_w1, moe_w3, moe_w2, final_norm_g):
    b, s, d = x.shape
    n_ctx = ctx.shape[1]
    l = n_ctx + s
    assert w_mod.shape[0] == 2, "two layers: attention/gMLP then S5"
    assert n_ctx % TOKEN_TILE == 0 and s % TOKEN_TILE == 0 and b % 8 == 0 and d == D_MODEL
    nct = n_ctx // TOKEN_TILE

    r = (b + 1 + 7) // 8 * 8
    cond = jnp.concatenate([c, c_ctx[None, :], jnp.zeros((r - b - 1, d), F32)], axis=0)
    mods = _adaln(cond, w_mod, b_mod)
    lat = mods[:, :b].reshape(2, b, 1, 6, d)
    cxt = jnp.broadcast_to(mods[:, b].reshape(2, 1, 1, 6, d), (2, b, 1, 6, d))
    mod = jnp.concatenate([cxt, lat], axis=2)

    w_in = jnp.concatenate([mix_w_in[0][:, _Q_COL_PERM], mix_w_in[0][:, ATTN_WIDTH:]], axis=1).astype(BF16)
    w_out = jnp.concatenate([mix_w_out[0][_Q_COL_PERM], mix_w_out[0][ATTN_WIDTH:]], axis=0).astype(BF16)
    cos, sa, sb = _rope_tables(n_ctx, s)
    qg = jnp.tile(q_norm_g[0], N_Q_HEADS).reshape(1, ATTN_WIDTH)
    kg = jnp.tile(k_norm_g[0], N_KV_HEADS).reshape(1, KV_WIDTH)
    q, k, v, gu, gv = _l0_proj(ctx, x, mod[0], norm_mix_g[0].reshape(1, d), w_in, qg, kg,
                               gmlp_norm_g[0].reshape(1, GMLP_WIDTH), cos, sa, sb,
                               _group_ones(ATTN_WIDTH, HEAD_DIM), _group_ones(KV_WIDTH, HEAD_DIM), nct)
    o = _attention(q, k, v, n_ctx)
    bsp = jnp.repeat(gmlp_b_spatial[0].T, GMLP_GROUP_DIM, axis=1)
    wr0, br0 = _router_weights(moe_w_group[0], moe_b_group[0], moe_w_router[0], moe_b_router[0])
    x1, f0, ri0, rw0, cnt0 = _l0_post(ctx, x, o, gu, gv, gmlp_w_spatial[0].astype(BF16), bsp, w_out, mod[0],
                                      norm_ffn_g[0].reshape(1, d), wr0, br0, nct)
    y0, slots0 = _moe(f0, ri0, cnt0, moe_w1, moe_w3, moe_w2, 0)
    x2, u_tm, u_bf = _l0_combine(x1.reshape(b * l, d), rw0, slots0, y0, mod[0], mod[1], norm_mix_g[1].reshape(1, d),
                           s5_w_in[0].astype(BF16), b, nct)

    lre, lim, bbre, bbim = _s5_zoh(s5_a_re[0], s5_a_im[0], s5_log_dt[0], s5_b_re[0], s5_b_im[0])
    bm, cm, lam = _s5_matrices(lre, lim, bbre, bbim, s5_c_re[0], s5_c_im[0])
    yf, yb = _s5_scan(u_bf.reshape(l * b, S5_WIDTH), bm, cm, lam, b, n_ctx)
    wr1, br1 = _router_weights(moe_w_group[1], moe_b_group[1], moe_w_router[1], moe_b_router[1])
    tmv = lambda a: a.reshape(l, b * S5_WIDTH)
    x3, f1, ri1, rw1, cnt1 = _l1_post(x2.reshape(b, l, d), u_tm, tmv(yf), tmv(yb), s5_d[0].reshape(1, S5_WIDTH),
                                      s5_w_glu[0].astype(BF16), s5_b_glu[0].reshape(1, S5_WIDTH),
                                      s5_w_out[0].astype(BF16), mod[1], norm_ffn_g[1].reshape(1, d), wr1, br1, n_ctx)
    y1, slots1 = _moe(f1, ri1, cnt1, moe_w1, moe_w3, moe_w2, 1)
    out = _final(x3.reshape(b * s, d), rw1, slots1, y1, mod[1], final_norm_g.reshape(1, d), b)
    return out.reshape(b, s, d)
```

```python
import functools

import numpy as np
import jax
import jax.numpy as jnp
from jax import lax
from jax.experimental import pallas as pl
from jax.experimental.pallas import tpu as pltpu

F32 = jnp.float32
BF16 = jnp.bfloat16
I32 = jnp.int32

EPS = 1e-6
GRID_W = 64
HEAD_DIM = 64
N_Q_HEADS = 8
N_KV_HEADS = 2
ATTN_WIDTH = N_Q_HEADS * HEAD_DIM
KV_WIDTH = N_KV_HEADS * HEAD_DIM
AXIS_DIM = HEAD_DIM // 2
ROPE_THETA = 10000.0
GMLP_GROUPS = 8
GMLP_GROUP_DIM = 64
GMLP_WIDTH = GMLP_GROUPS * GMLP_GROUP_DIM
GMLP_CHUNK = 128
MIX_IN_WIDTH = ATTN_WIDTH + 2 * KV_WIDTH + 2 * GMLP_WIDTH
S5_GROUPS = 32
S5_GROUP_DIM = 16
S5_STATE = 64
S5_WIDTH = S5_GROUPS * S5_GROUP_DIM
N_EXPERT_GROUPS = 4
EXPERTS_PER_GROUP = 8
N_EXPERTS = N_EXPERT_GROUPS * EXPERTS_PER_GROUP
EXPERT_HIDDEN = 512
MOE_BLOCK = 512

D_MODEL = 1024
LANES = 128
ROW_SUB = D_MODEL // LANES
TOKEN_TILE = 256
PAIR = 4
ROW_DMA_UNROLL = 8
ATTN_KEY_BLOCK = 128
V_ONES_ROWS = 16
V_AUG_ROWS = N_KV_HEADS * (HEAD_DIM + V_ONES_ROWS)
S5_CHUNK = 64
S5_TILES = 4
VMEM_LIMIT = 48 * 1024 * 1024
NEG_BIG = -1e30

_Q_HEAD_ORDER = (0, 4, 1, 5, 2, 6, 3, 7)
_Q_COL_PERM = np.concatenate([np.arange(HEAD_DIM) + HEAD_DIM * h for h in _Q_HEAD_ORDER])


def _params(*sem):
    return pltpu.CompilerParams(dimension_semantics=sem, vmem_limit_bytes=VMEM_LIMIT)


def _rms(x):
    return x * lax.rsqrt(jnp.mean(x * x, axis=-1, keepdims=True) + EPS)


def _modulate(x, g, shift, scale):
    return _rms(x) * g * (1.0 + scale) + shift


def _full(shape):
    return pl.BlockSpec(shape, lambda *_: (0,) * len(shape))


def _split_bf16(a):
    hi = a.astype(BF16)
    return hi, (a - hi.astype(F32)).astype(BF16)


def _dot_split(a, w_hi, w_lo):
    a_hi, a_lo = _split_bf16(a)
    dot = functools.partial(jnp.dot, preferred_element_type=F32)
    return dot(a_hi, w_hi) + (dot(a_lo, w_hi) + dot(a_hi, w_lo))


def _store_row_tiles(ref, val):
    n = val.shape[0]
    for s in range(ROW_SUB):
        ref[pl.ds(s, n, stride=ROW_SUB), :] = val[:, LANES * s:LANES * (s + 1)]


def _load_row_tiles(ref):
    n = ref.shape[0] // ROW_SUB
    return jnp.concatenate([ref[pl.ds(s, n, stride=ROW_SUB), :] for s in range(ROW_SUB)], axis=1)


def _adaln_kernel(cond_ref, w_ref, b_ref, o_ref):
    c = cond_ref[...]
    s = c * jax.nn.sigmoid(c)
    o_ref[0] = _dot_split(s, *_split_bf16(w_ref[0])) + b_ref[0]


def _adaln(cond, w_mod, b_mod):
    depth, d, n = w_mod.shape
    r = cond.shape[0]
    tn = 1536
    return pl.pallas_call(
        _adaln_kernel,
        out_shape=jax.ShapeDtypeStruct((depth, r, n), F32),
        grid=(depth, n // tn),
        in_specs=[pl.BlockSpec((r, d), lambda l, j: (0, 0)),
                  pl.BlockSpec((1, d, tn), lambda l, j: (l, 0, j)),
                  pl.BlockSpec((1, 1, tn), lambda l, j: (l, 0, j))],
        out_specs=pl.BlockSpec((1, r, tn), lambda l, j: (l, 0, j)),
        compiler_params=_params("parallel", "parallel"),
        name="adaln",
    )(cond, w_mod, b_mod.reshape(depth, 1, n))


def _seq_tile(ctx_ref, x_ref, nct, j=0):
    return jnp.where(pl.program_id(1) < nct, ctx_ref[j], x_ref[j])


def _seq_specs(t, d, nct, nb=1):
    return [pl.BlockSpec((nb, t, d), lambda bi, ti: (bi, jnp.minimum(ti, nct - 1), 0)),
            pl.BlockSpec((nb, t, d), lambda bi, ti: (bi, jnp.maximum(ti - nct, 0), 0))]


def _l0_proj_kernel(ctx_ref, x_ref, mod_ref, g_ref, win_ref, qg_ref, kg_ref, lng_ref, cos_ref, sa_ref, sb_ref,
                    gq_ref, gk_ref, q_ref, k_ref, v_ref, gu_ref, gv_ref, *, nct):
    cos, sa, sb = cos_ref[...], sa_ref[...], sb_ref[...]

    def head_norm_rope(t, gsum_ref, gain, reps):
        w = t.shape[1]
        ss = jnp.dot((t * t).astype(BF16), gsum_ref[...], preferred_element_type=F32)
        tn = t * lax.rsqrt(ss * (1.0 / HEAD_DIM) + EPS) * gain
        c, a, b = (jnp.concatenate([tab] * reps, axis=1) if reps > 1 else tab for tab in (cos, sa, sb))
        return tn * c + pltpu.roll(tn, w - AXIS_DIM // 2, 1) * a + pltpu.roll(tn, AXIS_DIM // 2, 1) * b

    def prog(j):
        m = mod_ref[j, 0]
        h = _modulate(_seq_tile(ctx_ref, x_ref, nct, j), g_ref[...], m[0:1], m[1:2]).astype(BF16)
        yield
        z = jnp.dot(h, win_ref[...], preferred_element_type=F32)
        yield
        q = head_norm_rope(z[:, :ATTN_WIDTH], gq_ref, qg_ref[...], ATTN_WIDTH // LANES)
        q_ref[j] = (q * (HEAD_DIM ** -0.5)).T.astype(BF16)
        k = head_norm_rope(z[:, ATTN_WIDTH:ATTN_WIDTH + KV_WIDTH], gk_ref, kg_ref[...], 1)
        k_ref[j] = k.astype(BF16)
        vt = z[:, ATTN_WIDTH + KV_WIDTH:ATTN_WIDTH + 2 * KV_WIDTH].T
        ones = jnp.ones((V_ONES_ROWS, vt.shape[1]), F32)
        v_ref[j] = jnp.concatenate([part for kv in range(N_KV_HEADS)
                                    for part in (vt[HEAD_DIM * kv:HEAD_DIM * (kv + 1)], ones)], axis=0).astype(BF16)
        off = ATTN_WIDTH + 2 * KV_WIDTH
        gu_ref[j] = jax.nn.gelu(z[:, off:off + GMLP_WIDTH], approximate=True).astype(BF16)
        gv = jax.nn.gelu(z[:, off + GMLP_WIDTH:], approximate=True)
        mu = jnp.mean(gv, axis=-1, keepdims=True)
        gc = gv - mu
        gn = gc * lax.rsqrt(jnp.mean(gc * gc, axis=-1, keepdims=True) + EPS) * lng_ref[...]
        gv_ref[j] = gn.astype(BF16)

    _interleave(*(prog(j) for j in range(PAIR)))


def _l0_proj(ctx, x, mod, g, w_in, qg, kg, lng, cos, sa, sb, gq, gk, nct):
    b, s, d = x.shape
    l = s + ctx.shape[1]
    t = TOKEN_TILE
    tok = lambda w: pl.BlockSpec((PAIR, t, w), lambda bi, ti: (bi, ti, 0))
    tab = pl.BlockSpec((t, LANES), lambda bi, ti: (ti, 0))
    tok_t = lambda w: pl.BlockSpec((PAIR, w, t), lambda bi, ti: (bi, 0, ti))
    outs = ([jax.ShapeDtypeStruct((b, ATTN_WIDTH, l), BF16), jax.ShapeDtypeStruct((b, l, KV_WIDTH), BF16),
             jax.ShapeDtypeStruct((b, V_AUG_ROWS, l), BF16)]
            + [jax.ShapeDtypeStruct((b, l, GMLP_WIDTH), BF16)] * 2)
    return pl.pallas_call(
        functools.partial(_l0_proj_kernel, nct=nct),
        out_shape=outs,
        grid=(b // PAIR, l // t),
        in_specs=_seq_specs(t, d, nct, PAIR) + [
                  pl.BlockSpec((PAIR, 1, 6, d), lambda bi, ti: (bi, jnp.where(ti >= nct, 1, 0), 0, 0)),
                  _full((1, d)), _full((d, MIX_IN_WIDTH)), _full((1, ATTN_WIDTH)), _full((1, KV_WIDTH)),
                  _full((1, GMLP_WIDTH)), tab, tab, tab,
                  _full((ATTN_WIDTH, ATTN_WIDTH)), _full((KV_WIDTH, KV_WIDTH))],
        out_specs=[tok_t(ATTN_WIDTH), tok(KV_WIDTH), tok_t(V_AUG_ROWS), tok(GMLP_WIDTH), tok(GMLP_WIDTH)],
        compiler_params=_params("parallel", "parallel"),
        name="l0_proj",
    )(ctx, x, mod, g, w_in, qg, kg, lng, cos, sa, sb, gq, gk)


def _attn_kernel(qt_ref, k_ref, vt_ref, o_ref, s_ref, p_ref, *, n_ctx, nct):
    ti = pl.program_id(1)
    tq = qt_ref.shape[2]
    row = lax.broadcasted_iota(I32, (LANES, tq), 0)
    low = row < HEAD_DIM
    kb = ATTN_KEY_BLOCK
    va = HEAD_DIM + V_ONES_ROWS

    def attend(nk):
        heads = [(m, j) for m in range(ATTN_WIDTH // LANES) for j in range(2)]
        blocks = [slice(kb * b, kb * (b + 1)) for b in range(nk // kb)]
        state = {}

        def start(h):
            m, j = heads[h]
            qp = qt_ref[0, LANES * m:LANES * (m + 1), :]
            state[h] = dict(qm=jnp.where(low if j == 0 else jnp.logical_not(low), qp, jnp.zeros_like(qp)), mx=None)

        def score_block(h, blk):
            s = jnp.dot(k_ref[0, blk, :], state[h]["qm"], preferred_element_type=F32)
            s_ref[h % 2, blk, :] = s
            bm = jnp.max(s, axis=0, keepdims=True)
            state[h]["mx"] = bm if state[h]["mx"] is None else jnp.maximum(state[h]["mx"], bm)

        def exp_block(h, blk):
            p_ref[h % 2, blk, :] = jnp.exp(s_ref[h % 2, blk, :] - state[h]["mx"]).astype(BF16)

        def finish(h):
            j = heads[h][1]
            pv = jnp.dot(vt_ref[0, va * j:va * (j + 1), :nk], p_ref[h % 2, :nk, :],
                         preferred_element_type=F32)
            return pv[:HEAD_DIM] / pv[HEAD_DIM:HEAD_DIM + 1]

        start(0)
        for blk in blocks:
            score_block(0, blk)
        halves = []
        for h in range(len(heads)):
            if h + 1 < len(heads):
                start(h + 1)
            for blk in blocks:
                if h + 1 < len(heads):
                    score_block(h + 1, blk)
                exp_block(h, blk)
            halves.append(finish(h))
            if heads[h][1] == 1:
                m = heads[h][0]
                o_ref[0, :, LANES * m:LANES * (m + 1)] = jnp.concatenate(halves, axis=0).T.astype(BF16)
                halves = []

    @pl.when(ti < nct)
    def _():
        attend(n_ctx)

    @pl.when(ti >= nct)
    def _():
        attend(k_ref.shape[1])


def _attention(qt, k, vt, n_ctx):
    b, l, _ = k.shape
    t = TOKEN_TILE
    kern = functools.partial(_attn_kernel, n_ctx=n_ctx, nct=n_ctx // t)
    return pl.pallas_call(
        kern,
        out_shape=jax.ShapeDtypeStruct((b, l, ATTN_WIDTH), BF16),
        grid=(b, l // t),
        in_specs=[pl.BlockSpec((1, ATTN_WIDTH, t), lambda bi, ti: (bi, 0, ti)),
                  pl.BlockSpec((1, l, KV_WIDTH), lambda bi, ti: (bi, 0, 0)),
                  pl.BlockSpec((1, V_AUG_ROWS, l), lambda bi, ti: (bi, 0, 0))],
        out_specs=pl.BlockSpec((1, t, ATTN_WIDTH), lambda bi, ti: (bi, ti, 0)),
        scratch_shapes=[pltpu.VMEM((2, l, t), F32), pltpu.VMEM((2, l, t), BF16)],
        compiler_params=_params("parallel", "parallel"),
        name="attention",
    )(qt, k, vt)


def _interleave(*progs):
    live = dict(enumerate(progs))
    rnd = 0
    while live:
        for k in sorted(live):
            if rnd >= k:
                try:
                    next(live[k])
                except StopIteration:
                    del live[k]
        rnd += 1


def _route_tail(j, x, y, m, gf, wr_ref, br_ref, shared, x_ref, f_ref, ri_ref, rw_ref):
    x1 = x + m[2:3] * y
    x_ref[j] = x1
    f = _modulate(x1, gf, m[3:4], m[4:5])
    _store_row_tiles(f_ref.at[j], f)
    t = f.shape[0]
    yield
    logits = _dot_split(f, wr_ref[0], wr_ref[1]) + br_ref[...]
    yield
    lane = lax.broadcasted_iota(I32, (t, LANES), 1).astype(F32)
    is_grp = lane < N_EXPERT_GROUPS
    gl = jnp.where(is_grp, logits, NEG_BIG)
    gmax = jnp.max(gl, axis=-1, keepdims=True)
    g_idx = jnp.min(jnp.where(gl == gmax, lane, float(LANES)), axis=-1, keepdims=True)
    g_w = 1.0 / jnp.sum(jnp.where(is_grp, jnp.exp(gl - gmax), 0.0), axis=-1, keepdims=True)
    e_lo = N_EXPERT_GROUPS + EXPERTS_PER_GROUP * g_idx
    in_grp = jnp.logical_and(lane >= e_lo, lane < e_lo + EXPERTS_PER_GROUP)
    sel = jnp.where(in_grp, logits, NEG_BIG)
    v1 = jnp.max(sel, axis=-1, keepdims=True)
    i1 = jnp.min(jnp.where(sel == v1, lane, float(LANES)), axis=-1, keepdims=True)
    sel2 = jnp.where(lane == i1, NEG_BIG, sel)
    v2 = jnp.max(sel2, axis=-1, keepdims=True)
    i2 = jnp.min(jnp.where(sel2 == v2, lane, float(LANES)), axis=-1, keepdims=True)
    e = jnp.exp(v2 - v1)
    w1 = g_w / (1.0 + e)
    w2 = g_w * e / (1.0 + e)
    eid1 = i1 - N_EXPERT_GROUPS
    eid2 = i2 - N_EXPERT_GROUPS
    hit1 = lane == eid1
    hit2 = lane == eid2
    cnt = (hit1.astype(F32) + hit2.astype(F32))
    yield

    row = lax.broadcasted_iota(I32, (t, t), 0)
    col = lax.broadcasted_iota(I32, (t, t), 1)
    tril = jnp.where(col < row, 1.0, 0.0).astype(BF16)
    before = jnp.dot(tril, cnt.astype(BF16), preferred_element_type=F32) + shared["base"]
    shared["base"] = shared["base"] + jnp.sum(cnt, axis=0, keepdims=True)
    rank1 = jnp.sum(jnp.where(hit1, before, 0.0), axis=-1, keepdims=True)
    rank2 = jnp.sum(jnp.where(hit2, before, 0.0), axis=-1, keepdims=True)

    ri = jnp.where(lane == 0, eid1, jnp.where(lane == 1, eid2, jnp.where(lane == 2, rank1,
                   jnp.where(lane == 3, rank2, 0.0))))
    ri_ref[j] = ri.T[0:8, :].astype(I32)
    l8 = lax.broadcasted_iota(I32, (t, 8), 1)
    rw_ref[j] = jnp.where(l8 == 0, w1, jnp.where(l8 == 1, w2, 0.0))


def _run_pair(prog, base_ref, cnt_ref):
    @pl.when(jnp.logical_and(pl.program_id(0) == 0, pl.program_id(1) == 0))
    def _():
        base_ref[...] = jnp.zeros_like(base_ref)

    shared = {"base": base_ref[0:1, :]}
    _interleave(*(prog(j, shared) for j in range(PAIR)))
    base_ref[...] = jnp.broadcast_to(shared["base"], base_ref.shape)
    cnt_ref[...] = jnp.broadcast_to(shared["base"], cnt_ref.shape)


def _tail_out_shapes(b, n_rows, d):
    t = TOKEN_TILE
    shapes = [jax.ShapeDtypeStruct((b, n_rows, d), F32), jax.ShapeDtypeStruct((b, n_rows * ROW_SUB, LANES), F32),
              jax.ShapeDtypeStruct((b, 8, n_rows), I32), jax.ShapeDtypeStruct((b, n_rows, 8), F32),
              jax.ShapeDtypeStruct((8, LANES), F32)]
    specs = [pl.BlockSpec((PAIR, t, d), lambda bp, ti: (bp, ti, 0)),
             pl.BlockSpec((PAIR, t * ROW_SUB, LANES), lambda bp, ti: (bp, ti, 0)),
             pl.BlockSpec((PAIR, 8, t), lambda bp, ti: (bp, 0, ti)),
             pl.BlockSpec((PAIR, t, 8), lambda bp, ti: (bp, ti, 0)),
             pl.BlockSpec((8, LANES), lambda bp, ti: (0, 0))]
    return shapes, specs


def _tail_flatten(x1, f, ri, rw):
    b, n_rows, d = x1.shape
    return (x1, f.reshape(b * n_rows * ROW_SUB, LANES), jnp.transpose(ri, (1, 0, 2)).reshape(8, b * n_rows),
            rw.reshape(b * n_rows, 8))


def _l0_post_kernel(ctx_ref, x_ref, o_ref, gu_ref, gv_ref, wsp_ref, bsp_ref, wout_ref, mod_ref, gf_ref, wr_ref,
                    br_ref, x1_ref, f_ref, ri_ref, rw_ref, cnt_ref, base_ref, *, nct):
    tq = x_ref.shape[1]
    lane = lax.broadcasted_iota(I32, (GMLP_CHUNK, LANES), 1)
    low = lane < GMLP_GROUP_DIM

    def prog(j, shared):
        gated = []
        for c in range(tq // GMLP_CHUNK):
            rows = slice(c * GMLP_CHUNK, (c + 1) * GMLP_CHUNK)
            mixed = []
            for m in range(GMLP_WIDTH // LANES):
                pair = gv_ref[j, rows, LANES * m:LANES * (m + 1)]
                a = jnp.dot(wsp_ref[2 * m], pair, preferred_element_type=F32)
                b = jnp.dot(wsp_ref[2 * m + 1], pair, preferred_element_type=F32)
                mixed.append(jnp.where(low, a, b))
            mixed = jnp.concatenate(mixed, axis=1) + bsp_ref[...]
            gated.append((gu_ref[j, rows, :].astype(F32) * mixed).astype(BF16))
        gated = jnp.concatenate(gated, axis=0)
        yield
        y = (jnp.dot(o_ref[j], wout_ref[:ATTN_WIDTH, :], preferred_element_type=F32)
             + jnp.dot(gated, wout_ref[ATTN_WIDTH:, :], preferred_element_type=F32))
        yield
        yield from _route_tail(j, _seq_tile(ctx_ref, x_ref, nct, j), y, mod_ref[j, 0], gf_ref[...], wr_ref, br_ref,
                               shared, x1_ref, f_ref, ri_ref, rw_ref)

    _run_pair(prog, base_ref, cnt_ref)


def _l0_post(ctx, x, o, gu, gv, wsp, bsp, wout, mod, gf, wr, br, nct):
    b, s, d = x.shape
    l = s + ctx.shape[1]
    t = TOKEN_TILE
    tok = lambda w: pl.BlockSpec((PAIR, t, w), lambda bp, ti: (bp, ti, 0))
    shapes, specs = _tail_out_shapes(b, l, d)
    outs = pl.pallas_call(
        functools.partial(_l0_post_kernel, nct=nct),
        out_shape=shapes,
        grid=(b // PAIR, l // t),
        in_specs=_seq_specs(t, d, nct, PAIR) + [tok(ATTN_WIDTH), tok(GMLP_WIDTH), tok(GMLP_WIDTH),
                  _full((GMLP_GROUPS, GMLP_CHUNK, GMLP_CHUNK)), _full((GMLP_CHUNK, GMLP_WIDTH)),
                  _full((ATTN_WIDTH + GMLP_WIDTH, d)),
                  pl.BlockSpec((PAIR, 1, 6, d), lambda bp, ti: (bp, jnp.where(ti >= nct, 1, 0), 0, 0)),
                  _full((1, d)), _full((2, d, LANES)), _full((1, LANES))],
        out_specs=specs,
        scratch_shapes=[pltpu.VMEM((8, LANES), F32)],
        compiler_params=_params("arbitrary", "arbitrary"),
        name="l0_post",
    )(ctx, x, o, gu, gv, wsp, bsp, wout, mod, gf, wr, br)
    return _tail_flatten(*outs[:4]) + (outs[4],)


def _row_copy(src_ref, src_row, dst_ref, dst_row, sem):
    rows = lambda ref, r: ref.at[pl.ds(pl.multiple_of(r * ROW_SUB, ROW_SUB), ROW_SUB)]
    return pltpu.make_async_copy(rows(src_ref, src_row), rows(dst_ref, dst_row), sem)


def _issue_rows(src_ref, index_of, buf_ref, sem_rows, s):
    n_str, rows = buf_ref.shape[1], buf_ref.shape[2] // ROW_SUB
    slot = s % 2

    def issue(j, carry):
        for half in range(2):
            r = 2 * j + half
            for k in range(n_str):
                _row_copy(src_ref, index_of(k * rows + r), buf_ref.at[slot, k], r,
                          sem_rows.at[slot]).start(priority=1)
        return carry

    lax.fori_loop(0, rows // 2, issue, 0, unroll=ROW_DMA_UNROLL // 2)


def _wait_rows(src_ref, buf_ref, sem_rows, s):
    n_str, rows = buf_ref.shape[1], buf_ref.shape[2] // ROW_SUB
    slot = s % 2
    for k in range(n_str):
        pltpu.make_async_copy(src_ref.at[pl.ds(0, rows * ROW_SUB)], buf_ref.at[slot, k], sem_rows.at[slot]).wait()
    return slot


def _gather_pipeline(src_ref, table_ref, idx_ref, buf_ref, sem_idx, sem_rows, step, n_steps):
    def idx_copy(s):
        return pltpu.make_async_copy(table_ref.at[s], idx_ref.at[s % 2], sem_idx.at[s % 2])

    def issue_rows(s):
        _issue_rows(src_ref, lambda pos: idx_ref[s % 2, pos], buf_ref, sem_rows, s)

    @pl.when(step == 0)
    def _():
        idx_copy(0).start()
        idx_copy(0).wait()
        issue_rows(0)

        @pl.when(n_steps > 1)
        def _():
            idx_copy(1).start()

    @pl.when(step + 1 < n_steps)
    def _():
        idx_copy(step + 1).wait()
        issue_rows(step + 1)

        @pl.when(step + 2 < n_steps)
        def _():
            idx_copy(step + 2).start()

    return _wait_rows(src_ref, buf_ref, sem_rows, step)


def _gather_scratch(n_str, rows):
    return [pltpu.SMEM((2, n_str * rows), I32), pltpu.VMEM((2, n_str, rows * ROW_SUB, LANES), F32),
            pltpu.SemaphoreType.DMA((2,)), pltpu.SemaphoreType.DMA((2,))]


def _dispatch_kernel(pad_ref, nu_ref, f_ref, slots_ref, xb_ref, idx_ref, zero_ref, sem_idx, sem_rows):
    i = pl.program_id(0)
    t = f_ref.shape[0] // ROW_SUB
    blk = MOE_BLOCK * ROW_SUB
    idx_cp = pltpu.make_async_copy(slots_ref.at[i], idx_ref, sem_idx)
    idx_cp.start()

    @pl.when(i == 0)
    def _():
        zero_ref[...] = jnp.zeros_like(zero_ref)
        n_blocks = xb_ref.shape[0] // blk

        def zero_block(start):
            return pltpu.make_async_copy(zero_ref, xb_ref.at[pl.ds(pl.multiple_of(start * ROW_SUB, blk), blk)],
                                         sem_rows)

        def tail_start(j, carry):
            zero_block(j * MOE_BLOCK).start()
            return carry

        def tail_wait(j, carry):
            zero_block(0).wait()
            return carry

        for e in range(N_EXPERTS):
            zero_block(jnp.maximum(pad_ref[N_EXPERTS + e] - MOE_BLOCK, 0)).start()
        lax.fori_loop(nu_ref[0], n_blocks, tail_start, 0)
        for e in range(N_EXPERTS):
            zero_block(0).wait()
        lax.fori_loop(nu_ref[0], n_blocks, tail_wait, 0)

    idx_cp.wait()

    def issue(r, carry):
        _row_copy(f_ref, r, xb_ref, idx_ref[r], sem_rows).start(priority=0)
        _row_copy(f_ref, r, xb_ref, idx_ref[t + r], sem_rows).start(priority=1)
        return carry

    lax.fori_loop(0, t, issue, 0, unroll=ROW_DMA_UNROLL)
    for _ in range(2):
        pltpu.make_async_copy(f_ref, xb_ref.at[pl.ds(0, t * ROW_SUB)], sem_rows).wait()


def _dispatch(f, slots, pad, n_used, n_slots):
    t = TOKEN_TILE
    n_tok = f.shape[0] // ROW_SUB
    return pl.pallas_call(
        _dispatch_kernel,
        out_shape=jax.ShapeDtypeStruct((n_slots * ROW_SUB, LANES), F32),
        grid_spec=pltpu.PrefetchScalarGridSpec(
            num_scalar_prefetch=2,
            grid=(n_tok // t,),
            in_specs=[pl.BlockSpec((t * ROW_SUB, LANES), lambda i, p, nu: (i, 0)),
                      pl.BlockSpec(memory_space=pltpu.VMEM)],
            out_specs=pl.BlockSpec(memory_space=pl.ANY),
            scratch_shapes=[pltpu.SMEM((2 * t,), I32), pltpu.VMEM((MOE_BLOCK * ROW_SUB, LANES), F32),
                            pltpu.SemaphoreType.DMA, pltpu.SemaphoreType.DMA]),
        compiler_params=_params("arbitrary"),
        name="moe_dispatch",
    )(pad, n_used, f, slots)


def _experts_kernel(be_ref, nu_ref, xb_ref, w1_ref, w3_ref, w2_ref, y_ref, w1b_ref, w3b_ref, w2b_ref):
    i = pl.program_id(0)

    @pl.when(jnp.logical_or(i == 0, be_ref[i] != be_ref[jnp.maximum(i - 1, 0)]))
    def _():
        w1b_ref[...] = w1_ref[...].astype(BF16)
        w3b_ref[...] = w3_ref[...].astype(BF16)
        w2b_ref[...] = w2_ref[...].astype(BF16)

    @pl.when(i < nu_ref[0])
    def _():
        x = _load_row_tiles(xb_ref).astype(BF16)
        h1 = jnp.dot(x, w1b_ref[...], preferred_element_type=F32)
        h3 = jnp.dot(x, w3b_ref[...], preferred_element_type=F32)
        a = (h1 * jax.nn.sigmoid(h1) * h3).astype(BF16)
        _store_row_tiles(y_ref, jnp.dot(a, w2b_ref[...], preferred_element_type=F32))

    @pl.when(i >= nu_ref[0])
    def _():
        y_ref[...] = jnp.zeros_like(y_ref)


def _experts(xb, w1, w3, w2, layer, block_e, n_used):
    n_blocks = block_e.shape[0]
    _, _, d, hid = w1.shape
    rb = MOE_BLOCK * ROW_SUB
    wspec = lambda s: pl.BlockSpec((None, None) + s, lambda i, be, nu: (layer, be[i], 0, 0))
    return pl.pallas_call(
        _experts_kernel,
        out_shape=jax.ShapeDtypeStruct((n_blocks * rb, LANES), F32),
        grid_spec=pltpu.PrefetchScalarGridSpec(
            num_scalar_prefetch=2,
            grid=(n_blocks,),
            in_specs=[pl.BlockSpec((rb, LANES), lambda i, be, nu: (jnp.minimum(i, nu[0] - 1), 0)),
                      wspec((d, hid)), wspec((d, hid)), wspec((hid, d))],
            out_specs=pl.BlockSpec((rb, LANES), lambda i, be, nu: (i, 0)),
            scratch_shapes=[pltpu.VMEM((d, hid), BF16), pltpu.VMEM((d, hid), BF16), pltpu.VMEM((hid, d), BF16)]),
        compiler_params=_params("arbitrary"),
        name="moe_experts",
    )(block_e, n_used, xb, w1, w3, w2)


def _moe_plan(route_i, counts, n_tok):
    eid, rank = route_i[0:2], route_i[2:4]
    cnt = counts[0, :N_EXPERTS].astype(I32)
    padded = (cnt + MOE_BLOCK - 1) // MOE_BLOCK * MOE_BLOCK
    pend = jnp.cumsum(padded)
    pstart = pend - padded
    experts = jnp.arange(N_EXPERTS, dtype=I32)
    slots = rank + jnp.sum(jnp.where(eid[:, :, None] == experts, pstart, 0), axis=-1)
    n_assign = 2 * n_tok
    n_slots = (n_assign + MOE_BLOCK - 1) // MOE_BLOCK * MOE_BLOCK + N_EXPERTS * MOE_BLOCK
    n_blocks = n_slots // MOE_BLOCK
    bstart = jnp.arange(n_blocks, dtype=I32) * MOE_BLOCK
    block_e = jnp.minimum(jnp.sum(bstart[:, None] >= pend[None, :], axis=1), N_EXPERTS - 1).astype(I32)
    n_used = pend[-1] // MOE_BLOCK
    last_e = block_e[jnp.maximum(n_used - 1, 0)]
    block_e = jnp.where(jnp.arange(n_blocks) < n_used, block_e, last_e).astype(I32)
    slots = slots.reshape(2, n_tok // TOKEN_TILE, TOKEN_TILE).transpose(1, 0, 2)
    slots = slots.reshape(n_tok // TOKEN_TILE, 2 * TOKEN_TILE).astype(I32)
    pad = jnp.concatenate([pstart + cnt, pend]).astype(I32)
    return slots, block_e, n_used.reshape(1).astype(I32), pad, n_slots


def _moe(f, route_i, counts, w1, w3, w2, layer):
    n_tok = f.shape[0] // ROW_SUB
    slots, block_e, n_used, pad, n_slots = _moe_plan(route_i, counts, n_tok)
    xb = _dispatch(f, slots, pad, n_used, n_slots)
    y = _experts(xb, w1, w3, w2, layer, block_e, n_used)
    return y, slots


def _combined_rows(y_ref, slots_ref, rw_ref, idx_ref, ybuf_ref, sem_idx, sem_rows):
    slot = _gather_pipeline(y_ref, slots_ref, idx_ref, ybuf_ref, sem_idx, sem_rows,
                            pl.program_id(0), pl.num_programs(0))
    rw = rw_ref[...]
    w1, w2 = rw[:, 0:1], rw[:, 1:2]
    t = ybuf_ref.shape[2] // ROW_SUB
    part = lambda k, s: ybuf_ref[slot, k, pl.ds(s, t, stride=ROW_SUB), :]
    return jnp.concatenate([w1 * part(0, s) + w2 * part(1, s) for s in range(ROW_SUB)], axis=1)


def _l0_combine_kernel(x_ref, rw_ref, slots_ref, y_ref, mod0_ref, mod1_ref, g_ref, win_ref,
                       x2_ref, u_ref, ub_ref, idx_ref, ybuf_ref, sem_idx, sem_rows):
    moe = _combined_rows(y_ref, slots_ref, rw_ref, idx_ref, ybuf_ref, sem_idx, sem_rows)
    x2 = x_ref[...] + mod0_ref[0, 0][5:6] * moe
    x2_ref[...] = x2
    m1 = mod1_ref[0, 0]
    h = _modulate(x2, g_ref[...], m1[0:1], m1[1:2]).astype(BF16)
    u = jnp.dot(h, win_ref[...], preferred_element_type=F32)
    u_ref[...] = u
    ub_ref[...] = u.astype(BF16)


def _l0_combine(x1, rw, slots, y, mod0, mod1, g, win, b, nct):
    n_tok, d = x1.shape
    t = TOKEN_TILE
    nt = n_tok // b // t
    l = n_tok // b
    modspec = pl.BlockSpec((1, 1, 6, d), lambda i: (i // nt, jnp.where(i % nt >= nct, 1, 0), 0, 0))
    return pl.pallas_call(
        _l0_combine_kernel,
        out_shape=[jax.ShapeDtypeStruct((n_tok, d), F32), jax.ShapeDtypeStruct((l, b * S5_WIDTH), F32),
                   jax.ShapeDtypeStruct((l, b * S5_WIDTH), BF16)],
        grid=(n_tok // t,),
        in_specs=[pl.BlockSpec((t, d), lambda i: (i, 0)), pl.BlockSpec((t, 8), lambda i: (i, 0)),
                  pl.BlockSpec(memory_space=pltpu.VMEM), pl.BlockSpec(memory_space=pl.ANY),
                  modspec, modspec, _full((1, d)), _full((d, S5_WIDTH))],
        out_specs=[pl.BlockSpec((t, d), lambda i: (i, 0)),
                   pl.BlockSpec((t, S5_WIDTH), lambda i: (i % nt, i // nt)),
                   pl.BlockSpec((t, S5_WIDTH), lambda i: (i % nt, i // nt))],
        scratch_shapes=_gather_scratch(2, t),
        compiler_params=_params("arbitrary"),
        name="l0_combine",
    )(x1, rw, slots, y, mod0, mod1, g, win)


def _s5_zoh_kernel(are_ref, aim_ref, ldt_ref, bre_ref, bim_ref, lre_ref, lim_ref, bbre_ref, bbim_ref):
    ar, ai = are_ref[...], aim_ref[...]
    dt = jnp.exp(ldt_ref[...])
    mag = jnp.exp(ar * dt)
    lr = mag * jnp.cos(ai * dt)
    li = mag * jnp.sin(ai * dt)
    lre_ref[...] = lr
    lim_ref[...] = li
    nr, ni = lr - 1.0, li
    den = ar * ar + ai * ai
    cr = (nr * ar + ni * ai) / den
    ci = (ni * ar - nr * ai) / den
    br, bi = bre_ref[...], bim_ref[...]
    bbre_ref[...] = cr * br - ci * bi
    bbim_ref[...] = cr * bi + ci * br


def _s5_zoh(a_re, a_im, log_dt, b_re, b_im):
    n_dir, g, n, p = b_re.shape
    rows = n_dir * g * p
    expand = lambda a: jnp.broadcast_to(a[:, :, None, :], (n_dir, g, p, n)).reshape(rows, n)
    ldt = jnp.broadcast_to(log_dt[:, :, None, None], (n_dir, g, p, n)).reshape(rows, n)
    bt = lambda a: jnp.transpose(a, (0, 1, 3, 2)).reshape(rows, n)
    outs = pl.pallas_call(
        _s5_zoh_kernel,
        out_shape=[jax.ShapeDtypeStruct((rows, n), F32)] * 4,
        name="s5_zoh",
    )(expand(a_re), expand(a_im), ldt, bt(b_re), bt(b_im))
    lre, lim, bbre, bbim = (o.reshape(n_dir, g, p, n) for o in outs)
    return lre[:, :, 0, :], lim[:, :, 0, :], bbre, bbim


def _s5_matrices(lre, lim, bbre, bbim, c_re, c_im):
    n_dir = lre.shape[0]
    nj = S5_GROUPS // (2 * S5_TILES)
    shp = (n_dir, nj, S5_TILES, 2, S5_GROUP_DIM, S5_STATE)
    eye_t = jnp.eye(S5_TILES, dtype=F32)
    eye_g = jnp.eye(2, dtype=F32)

    def place(a):
        return jnp.einsum("djtgpn,tu,gh->djtgpuhn", a.reshape(shp), eye_t, eye_g)

    kdim = S5_TILES * 2 * S5_GROUP_DIM
    sdim = S5_TILES * 2 * 2 * S5_STATE
    bm = jnp.stack([place(bbre), place(bbim)], axis=6)
    bm = bm.reshape(n_dir, nj, kdim, sdim).astype(BF16)
    cm = jnp.stack([place(c_re), place(-c_im)], axis=6)
    cm = jnp.transpose(cm, (0, 1, 5, 6, 7, 8, 2, 3, 4)).reshape(n_dir, nj, sdim, kdim).astype(BF16)
    lam = jnp.stack([lre.reshape(n_dir, nj, S5_TILES * LANES), lim.reshape(n_dir, nj, S5_TILES * LANES)], axis=2)
    return bm, cm, lam


def _s5_scan_kernel(uf_ref, ub_ref, bm_ref, cm_ref, lam_ref, yf_ref, yb_ref, v_ref, h_ref, acc_ref, *, tc, nb):
    @pl.when(pl.program_id(0) == 0)
    def _():
        h_ref[...] = jnp.zeros_like(h_ref)

    nj, kdim, sdim = bm_ref.shape[1], bm_ref.shape[2], bm_ref.shape[3]
    u_refs, y_refs = (uf_ref, ub_ref), (yf_ref, yb_ref)
    phases = [(d, jq) for d in range(2) for jq in range(nj)]
    n_sub = 2 * S5_TILES
    col = sdim // S5_TILES
    run = tc // n_sub
    state = {}

    def drive_piece(p, c):
        d, jq = phases[p]
        if c == 0:
            state["uk"] = u_refs[d][:, kdim * jq:kdim * (jq + 1)]
        v_ref[p % 3, :, col * c:col * (c + 1)] = jnp.dot(state["uk"], bm_ref[d, jq, :, col * c:col * (c + 1)],
                                                         preferred_element_type=F32)

    def readout_piece(p, c):
        d, jq = phases[p]
        part = jnp.dot(v_ref[p % 3, :, col * c:col * (c + 1)].astype(BF16), cm_ref[d, jq, col * c:col * (c + 1), :],
                       preferred_element_type=F32)
        if c == 0:
            acc_ref[...] = part
        elif c < S5_TILES - 1:
            acc_ref[...] += part
        else:
            y_refs[d][:, kdim * jq:kdim * (jq + 1)] = (acc_ref[...] + part).astype(BF16)

    def scan_run(p, c):
        d, jq = phases[p]
        if c == 0:
            lam = lam_ref[d, jq]
            state["lr"] = [jnp.broadcast_to(lam[0:1, LANES * t:LANES * (t + 1)], (nb, LANES)) for t in range(S5_TILES)]
            state["li"] = [jnp.broadcast_to(lam[1:2, LANES * t:LANES * (t + 1)], (nb, LANES)) for t in range(S5_TILES)]
            h0 = h_ref[d, jq]
            state["h"] = [h0[:, LANES * i:LANES * (i + 1)] for i in range(2 * S5_TILES)]
        lr, li, h = state["lr"], state["li"], state["h"]
        for s in range(run * c, run * (c + 1)):
            r0 = (s if d == 0 else tc - 1 - s) * nb
            rows = v_ref[p % 3, r0:r0 + nb, :]
            new = []
            for t in range(S5_TILES):
                vr = rows[:, 2 * LANES * t:2 * LANES * t + LANES]
                vi = rows[:, 2 * LANES * t + LANES:2 * LANES * (t + 1)]
                hr, hi = h[2 * t], h[2 * t + 1]
                new.append(lr[t] * hr - li[t] * hi + vr)
                new.append(lr[t] * hi + li[t] * hr + vi)
            v_ref[p % 3, r0:r0 + nb, :] = jnp.concatenate(new, axis=1)
            h = new
        state["h"] = h
        if c == n_sub - 1:
            h_ref[d, jq] = jnp.concatenate(h, axis=1)

    for slot in range(len(phases) + 2):
        for c in range(n_sub):
            if 0 <= slot - 1 < len(phases):
                scan_run(slot - 1, c)
            if c < S5_TILES and slot < len(phases):
                drive_piece(slot, c)
            if c >= S5_TILES and 0 <= slot - 2 < len(phases):
                readout_piece(slot - 2, c - S5_TILES)


def _s5_scan(u_tm, bm, cm, lam, nb, n_ctx):
    rows_total, w = u_tm.shape
    tc = S5_CHUNK
    r = tc * nb
    n_chunks = rows_total // r
    ncc = n_ctx // tc
    bwd = lambda i: (jnp.where(i < ncc, ncc - 1 - i, n_chunks - 1 - (i - ncc)), 0)
    nj = bm.shape[1]
    sdim = bm.shape[3]
    kern = functools.partial(_s5_scan_kernel, tc=tc, nb=nb)
    return pl.pallas_call(
        kern,
        out_shape=[jax.ShapeDtypeStruct((rows_total, w), BF16)] * 2,
        grid=(n_chunks,),
        in_specs=[pl.BlockSpec((r, w), lambda i: (i, 0)), pl.BlockSpec((r, w), bwd),
                  _full(bm.shape), _full(cm.shape), _full(lam.shape)],
        out_specs=[pl.BlockSpec((r, w), lambda i: (i, 0)), pl.BlockSpec((r, w), bwd)],
        scratch_shapes=[pltpu.VMEM((3, r, sdim), F32), pltpu.VMEM((2, nj, nb, sdim), F32),
                        pltpu.VMEM((r, bm.shape[2]), F32)],
        compiler_params=_params("arbitrary"),
        name="s5_scan",
    )(u_tm, u_tm, bm, cm, lam)


def _l1_post_kernel(x_ref, u_ref, yf_ref, yb_ref, dsk_ref, wglu_ref, bglu_ref, wout_ref, mod_ref, gf_ref,
                    wr_ref, br_ref, x3_ref, f_ref, ri_ref, rw_ref, cnt_ref, base_ref):
    def prog(j, shared):
        cols = slice(S5_WIDTH * j, S5_WIDTH * (j + 1))
        y = dsk_ref[...] * u_ref[:, cols] + yf_ref[:, cols].astype(F32) + yb_ref[:, cols].astype(F32)
        g = jax.nn.gelu(y, approximate=True)
        yield
        z = jnp.dot(g.astype(BF16), wglu_ref[...], preferred_element_type=F32) + bglu_ref[...]
        yield
        a = (g * jax.nn.sigmoid(z)).astype(BF16)
        yield
        o = jnp.dot(a, wout_ref[...], preferred_element_type=F32)
        yield
        yield from _route_tail(j, x_ref[j], o, mod_ref[j, 0], gf_ref[...], wr_ref, br_ref, shared,
                               x3_ref, f_ref, ri_ref, rw_ref)

    _run_pair(prog, base_ref, cnt_ref)


def _l1_post(x2, u_tm, yf, yb, dsk, wglu, bglu, wout, mod, gf, wr, br, n_ctx):
    b, l, d = x2.shape
    t = TOKEN_TILE
    nct = n_ctx // t
    s = l - n_ctx
    tm = pl.BlockSpec((t, PAIR * S5_WIDTH), lambda bp, ti: (ti + nct, bp))
    shapes, specs = _tail_out_shapes(b, s, d)
    outs = pl.pallas_call(
        _l1_post_kernel,
        out_shape=shapes,
        grid=(b // PAIR, s // t),
        in_specs=[pl.BlockSpec((PAIR, t, d), lambda bp, ti: (bp, ti + nct, 0)), tm, tm, tm,
                  _full((1, S5_WIDTH)), _full((S5_WIDTH, S5_WIDTH)), _full((1, S5_WIDTH)), _full((S5_WIDTH, d)),
                  pl.BlockSpec((PAIR, 1, 6, d), lambda bp, ti: (bp, 1, 0, 0)),
                  _full((1, d)), _full((2, d, LANES)), _full((1, LANES))],
        out_specs=specs,
        scratch_shapes=[pltpu.VMEM((8, LANES), F32)],
        compiler_params=_params("arbitrary", "arbitrary"),
        name="l1_post",
    )(x2, u_tm, yf, yb, dsk, wglu, bglu, wout, mod, gf, wr, br)
    return _tail_flatten(*outs[:4]) + (outs[4],)


def _final_kernel(x_ref, rw_ref, slots_ref, y_ref, mod_ref, g_ref, o_ref, idx_ref, ybuf_ref, sem_idx, sem_rows):
    moe = _combined_rows(y_ref, slots_ref, rw_ref, idx_ref, ybuf_ref, sem_idx, sem_rows)
    x4 = x_ref[...] + mod_ref[0, 0][5:6] * moe
    o_ref[...] = _rms(x4) * g_ref[...]


def _final(x3, rw, slots, y, mod, g, b):
    n_tok, d = x3.shape
    t = TOKEN_TILE
    nt = n_tok // b // t
    return pl.pallas_call(
        _final_kernel,
        out_shape=jax.ShapeDtypeStruct((n_tok, d), F32),
        grid=(n_tok // t,),
        in_specs=[pl.BlockSpec((t, d), lambda i: (i, 0)), pl.BlockSpec((t, 8), lambda i: (i, 0)),
                  pl.BlockSpec(memory_space=pltpu.VMEM), pl.BlockSpec(memory_space=pl.ANY),
                  pl.BlockSpec((1, 1, 6, d), lambda i: (i // nt, 1, 0, 0)), _full((1, d))],
        out_specs=pl.BlockSpec((t, d), lambda i: (i, 0)),
        scratch_shapes=_gather_scratch(2, t),
        compiler_params=_params("arbitrary"),
        name="final_combine",
    )(x3, rw, slots, y, mod, g)


def _rope_tables(n_ctx, n_lat):
    pos = jnp.arange(n_lat, dtype=I32)
    row = (pos // GRID_W).astype(F32)
    col = (pos % GRID_W).astype(F32)
    inv_freq = jnp.power(ROPE_THETA, -jnp.arange(0, AXIS_DIM, 2, dtype=F32) / AXIS_DIM)
    lane = np.arange(LANES)
    dd = lane % HEAD_DIM
    use_col = dd >= AXIS_DIM
    first = (dd % AXIS_DIM) < AXIS_DIM // 2
    freq = inv_freq[(dd % AXIS_DIM) % (AXIS_DIM // 2)]
    ang = jnp.where(use_col[None, :], col[:, None], row[:, None]) * freq[None, :]
    cos, sin = jnp.cos(ang), jnp.sin(ang)
    sa = jnp.where(first[None, :], -sin, 0.0)
    sb = jnp.where(first[None, :], 0.0, sin)
    pad = lambda a, v: jnp.concatenate([jnp.full((n_ctx, LANES), v, F32), a], axis=0)
    return pad(cos, 1.0), pad(sa, 0.0), pad(sb, 0.0)


def _group_ones(width, group):
    idx = np.arange(width) // group
    return jnp.asarray(idx[:, None] == idx[None, :], dtype=BF16)


def _router_weights(w_grp, b_grp, w_rt, b_rt):
    d = w_grp.shape[0]
    used = N_EXPERT_GROUPS + N_EXPERTS
    wr = jnp.concatenate([w_grp, w_rt, jnp.zeros((d, LANES - used), F32)], axis=1)
    br = jnp.concatenate([b_grp, b_rt, jnp.zeros((LANES - used,), F32)]).reshape(1, LANES)
    return jnp.stack(_split_bf16(wr)), br


def kernel(x, c, ctx, c_ctx, w_mod, b_mod, norm_mix_g, norm_ffn_g, mix_w_in, mix_w_out, q_norm_g, k_norm_g, gmlp_norm_g, gmlp_w_spatial, gmlp_b_spatial, s5_w_in, s5_a_re, s5_a_im, s5_log_dt, s5_b_re, s5_b_im, s5_c_re, s5_c_im, s5_d, s5_w_glu, s5_b_glu, s5_w_out, moe_w_group, moe_b_group, moe_w_router, moe_b_router, moe_w1, moe_w3, moe_w2, final_norm_g):
    b, s, d = x.shape
    n_ctx = ctx.shape[1]
    l = n_ctx + s
    assert w_mod.shape[0] == 2, "two layers: attention/gMLP then S5"
    assert n_ctx % TOKEN_TILE == 0 and s % TOKEN_TILE == 0 and b % 8 == 0 and d == D_MODEL
    nct = n_ctx // TOKEN_TILE

    r = (b + 1 + 7) // 8 * 8
    cond = jnp.concatenate([c, c_ctx[None, :], jnp.zeros((r - b - 1, d), F32)], axis=0)
    mods = _adaln(cond, w_mod, b_mod)
    lat = mods[:, :b].reshape(2, b, 1, 6, d)
    cxt = jnp.broadcast_to(mods[:, b].reshape(2, 1, 1, 6, d), (2, b, 1, 6, d))
    mod = jnp.concatenate([cxt, lat], axis=2)

    w_in = jnp.concatenate([mix_w_in[0][:, _Q_COL_PERM], mix_w_in[0][:, ATTN_WIDTH:]], axis=1).astype(BF16)
    w_out = jnp.concatenate([mix_w_out[0][_Q_COL_PERM], mix_w_out[0][ATTN_WIDTH:]], axis=0).astype(BF16)
    cos, sa, sb = _rope_tables(n_ctx, s)
    qg = jnp.tile(q_norm_g[0], N_Q_HEADS).reshape(1, ATTN_WIDTH)
    kg = jnp.tile(k_norm_g[0], N_KV_HEADS).reshape(1, KV_WIDTH)
    q, k, v, gu, gv = _l0_proj(ctx, x, mod[0], norm_mix_g[0].reshape(1, d), w_in, qg, kg,
                               gmlp_norm_g[0].reshape(1, GMLP_WIDTH), cos, sa, sb,
                               _group_ones(ATTN_WIDTH, HEAD_DIM), _group_ones(KV_WIDTH, HEAD_DIM), nct)
    o = _attention(q, k, v, n_ctx)
    bsp = jnp.repeat(gmlp_b_spatial[0].T, GMLP_GROUP_DIM, axis=1)
    wr0, br0 = _router_weights(moe_w_group[0], moe_b_group[0], moe_w_router[0], moe_b_router[0])
    x1, f0, ri0, rw0, cnt0 = _l0_post(ctx, x, o, gu, gv, gmlp_w_spatial[0].astype(BF16), bsp, w_out, mod[0],
                                      norm_ffn_g[0].reshape(1, d), wr0, br0, nct)
    y0, slots0 = _moe(f0, ri0, cnt0, moe_w1, moe_w3, moe_w2, 0)
    x2, u_tm, u_bf = _l0_combine(x1.reshape(b * l, d), rw0, slots0, y0, mod[0], mod[1], norm_mix_g[1].reshape(1, d),
                           s5_w_in[0].astype(BF16), b, nct)

    lre, lim, bbre, bbim = _s5_zoh(s5_a_re[0], s5_a_im[0], s5_log_dt[0], s5_b_re[0], s5_b_im[0])
    bm, cm, lam = _s5_matrices(lre, lim, bbre, bbim, s5_c_re[0], s5_c_im[0])
    yf, yb = _s5_scan(u_bf.reshape(l * b, S5_WIDTH), bm, cm, lam, b, n_ctx)
    wr1, br1 = _router_weights(moe_w_group[1], moe_b_group[1], moe_w_router[1], moe_b_router[1])
    tmv = lambda a: a.reshape(l, b * S5_WIDTH)
    x3, f1, ri1, rw1, cnt1 = _l1_post(x2.reshape(b, l, d), u_tm, tmv(yf), tmv(yb), s5_d[0].reshape(1, S5_WIDTH),
                                      s5_w_glu[0].astype(BF16), s5_b_glu[0].reshape(1, S5_WIDTH),
                                      s5_w_out[0].astype(BF16), mod[1], norm_ffn_g[1].reshape(1, d), wr1, br1, n_ctx)
    y1, slots1 = _moe(f1, ri1, cnt1, moe_w1, moe_w3, moe_w2, 1)
    out = _final(x3.reshape(b * s, d), rw1, slots1, y1, mod[1], final_norm_g.reshape(1, d), b)
    return out.reshape(b, s, d)
```

```python
import functools

import numpy as np
import jax
import jax.numpy as jnp
from jax import lax
from jax.experimental import pallas as pl
from jax.experimental.pallas import tpu as pltpu

F32 = jnp.float32
BF16 = jnp.bfloat16
I32 = jnp.int32

EPS = 1e-6
GRID_W = 64
HEAD_DIM = 64
N_Q_HEADS = 8
N_KV_HEADS = 2
ATTN_WIDTH = N_Q_HEADS * HEAD_DIM
KV_WIDTH = N_KV_HEADS * HEAD_DIM
AXIS_DIM = HEAD_DIM // 2
ROPE_THETA = 10000.0
GMLP_GROUPS = 8
GMLP_GROUP_DIM = 64
GMLP_WIDTH = GMLP_GROUPS * GMLP_GROUP_DIM
GMLP_CHUNK = 128
MIX_IN_WIDTH = ATTN_WIDTH + 2 * KV_WIDTH + 2 * GMLP_WIDTH
S5_GROUPS = 32
S5_GROUP_DIM = 16
S5_STATE = 64
S5_WIDTH = S5_GROUPS * S5_GROUP_DIM
N_EXPERT_GROUPS = 4
EXPERTS_PER_GROUP = 8
N_EXPERTS = N_EXPERT_GROUPS * EXPERTS_PER_GROUP
EXPERT_HIDDEN = 512
MOE_BLOCK = 512

D_MODEL = 1024
LANES = 128
ROW_SUB = D_MODEL // LANES
TOKEN_TILE = 256
PAIR = 4
ROW_DMA_UNROLL = 8
ATTN_KEY_BLOCK = 128
V_ONES_ROWS = 16
V_AUG_ROWS = N_KV_HEADS * (HEAD_DIM + V_ONES_ROWS)
S5_CHUNK = 64
S5_TILES = 4
VMEM_LIMIT = 48 * 1024 * 1024
NEG_BIG = -1e30

_Q_HEAD_ORDER = (0, 4, 1, 5, 2, 6, 3, 7)
_Q_COL_PERM = np.concatenate([np.arange(HEAD_DIM) + HEAD_DIM * h for h in _Q_HEAD_ORDER])


def _params(*sem):
    return pltpu.CompilerParams(dimension_semantics=sem, vmem_limit_bytes=VMEM_LIMIT)


def _rms(x):
    return x * lax.rsqrt(jnp.mean(x * x, axis=-1, keepdims=True) + EPS)


def _modulate(x, g, shift, scale):
    return _rms(x) * g * (1.0 + scale) + shift


def _full(shape):
    return pl.BlockSpec(shape, lambda *_: (0,) * len(shape))


def _split_bf16(a):
    hi = a.astype(BF16)
    return hi, (a - hi.astype(F32)).astype(BF16)


def _dot_split(a, w_hi, w_lo):
    a_hi, a_lo = _split_bf16(a)
    dot = functools.partial(jnp.dot, preferred_element_type=F32)
    return dot(a_hi, w_hi) + (dot(a_lo, w_hi) + dot(a_hi, w_lo))


def _store_row_tiles(ref, val):
    n = val.shape[0]
    for s in range(ROW_SUB):
        ref[pl.ds(s, n, stride=ROW_SUB), :] = val[:, LANES * s:LANES * (s + 1)]


def _load_row_tiles(ref):
    n = ref.shape[0] // ROW_SUB
    return jnp.concatenate([ref[pl.ds(s, n, stride=ROW_SUB), :] for s in range(ROW_SUB)], axis=1)


def _adaln_kernel(cond_ref, w_ref, b_ref, o_ref):
    c = cond_ref[...]
    s = c * jax.nn.sigmoid(c)
    o_ref[0] = _dot_split(s, *_split_bf16(w_ref[0])) + b_ref[0]


def _adaln(cond, w_mod, b_mod):
    depth, d, n = w_mod.shape
    r = cond.shape[0]
    tn = 1536
    return pl.pallas_call(
        _adaln_kernel,
        out_shape=jax.ShapeDtypeStruct((depth, r, n), F32),
        grid=(depth, n // tn),
        in_specs=[pl.BlockSpec((r, d), lambda l, j: (0, 0)),
                  pl.BlockSpec((1, d, tn), lambda l, j: (l, 0, j)),
                  pl.BlockSpec((1, 1, tn), lambda l, j: (l, 0, j))],
        out_specs=pl.BlockSpec((1, r, tn), lambda l, j: (l, 0, j)),
        compiler_params=_params("parallel", "parallel"),
        name="adaln",
    )(cond, w_mod, b_mod.reshape(depth, 1, n))


def _seq_tile(ctx_ref, x_ref, nct, j=0):
    return jnp.where(pl.program_id(1) < nct, ctx_ref[j], x_ref[j])


def _seq_specs(t, d, nct, nb=1):
    return [pl.BlockSpec((nb, t, d), lambda bi, ti: (bi, jnp.minimum(ti, nct - 1), 0)),
            pl.BlockSpec((nb, t, d), lambda bi, ti: (bi, jnp.maximum(ti - nct, 0), 0))]


def _l0_proj_kernel(ctx_ref, x_ref, mod_ref, g_ref, win_ref, qg_ref, kg_ref, lng_ref, cos_ref, sa_ref, sb_ref,
                    gq_ref, gk_ref, q_ref, k_ref, v_ref, gu_ref, gv_ref, *, nct):
    cos, sa, sb = cos_ref[...], sa_ref[...], sb_ref[...]

    def head_norm_rope(t, gsum_ref, gain, reps):
        w = t.shape[1]
        ss = jnp.dot((t * t).astype(BF16), gsum_ref[...], preferred_element_type=F32)
        tn = t * lax.rsqrt(ss * (1.0 / HEAD_DIM) + EPS) * gain
        c, a, b = (jnp.concatenate([tab] * reps, axis=1) if reps > 1 else tab for tab in (cos, sa, sb))
        return tn * c + pltpu.roll(tn, w - AXIS_DIM // 2, 1) * a + pltpu.roll(tn, AXIS_DIM // 2, 1) * b

    def prog(j):
        m = mod_ref[j, 0]
        h = _modulate(_seq_tile(ctx_ref, x_ref, nct, j), g_ref[...], m[0:1], m[1:2]).astype(BF16)
        yield
        z = jnp.dot(h, win_ref[...], preferred_element_type=F32)
        yield
        q = head_norm_rope(z[:, :ATTN_WIDTH], gq_ref, qg_ref[...], ATTN_WIDTH // LANES)
        q_ref[j] = (q * (HEAD_DIM ** -0.5)).T.astype(BF16)
        k = head_norm_rope(z[:, ATTN_WIDTH:ATTN_WIDTH + KV_WIDTH], gk_ref, kg_ref[...], 1)
        k_ref[j] = k.astype(BF16)
        vt = z[:, ATTN_WIDTH + KV_WIDTH:ATTN_WIDTH + 2 * KV_WIDTH].T
        ones = jnp.ones((V_ONES_ROWS, vt.shape[1]), F32)
        v_ref[j] = jnp.concatenate([part for kv in range(N_KV_HEADS)
                                    for part in (vt[HEAD_DIM * kv:HEAD_DIM * (kv + 1)], ones)], axis=0).astype(BF16)
        off = ATTN_WIDTH + 2 * KV_WIDTH
        gu_ref[j] = jax.nn.gelu(z[:, off:off + GMLP_WIDTH], approximate=True).astype(BF16)
        gv = jax.nn.gelu(z[:, off + GMLP_WIDTH:], approximate=True)
        mu = jnp.mean(gv, axis=-1, keepdims=True)
        gc = gv - mu
        gn = gc * lax.rsqrt(jnp.mean(gc * gc, axis=-1, keepdims=True) + EPS) * lng_ref[...]
        gv_ref[j] = gn.astype(BF16)

    _interleave(*(prog(j) for j in range(PAIR)))


def _l0_proj(ctx, x, mod, g, w_in, qg, kg, lng, cos, sa, sb, gq, gk, nct):
    b, s, d = x.shape
    l = s + ctx.shape[1]
    t = TOKEN_TILE
    tok = lambda w: pl.BlockSpec((PAIR, t, w), lambda bi, ti: (bi, ti, 0))
    tab = pl.BlockSpec((t, LANES), lambda bi, ti: (ti, 0))
    tok_t = lambda w: pl.BlockSpec((PAIR, w, t), lambda bi, ti: (bi, 0, ti))
    outs = ([jax.ShapeDtypeStruct((b, ATTN_WIDTH, l), BF16), jax.ShapeDtypeStruct((b, l, KV_WIDTH), BF16),
             jax.ShapeDtypeStruct((b, V_AUG_ROWS, l), BF16)]
            + [jax.ShapeDtypeStruct((b, l, GMLP_WIDTH), BF16)] * 2)
    return pl.pallas_call(
        functools.partial(_l0_proj_kernel, nct=nct),
        out_shape=outs,
        grid=(b // PAIR, l // t),
        in_specs=_seq_specs(t, d, nct, PAIR) + [
                  pl.BlockSpec((PAIR, 1, 6, d), lambda bi, ti: (bi, jnp.where(ti >= nct, 1, 0), 0, 0)),
                  _full((1, d)), _full((d, MIX_IN_WIDTH)), _full((1, ATTN_WIDTH)), _full((1, KV_WIDTH)),
                  _full((1, GMLP_WIDTH)), tab, tab, tab,
                  _full((ATTN_WIDTH, ATTN_WIDTH)), _full((KV_WIDTH, KV_WIDTH))],
        out_specs=[tok_t(ATTN_WIDTH), tok(KV_WIDTH), tok_t(V_AUG_ROWS), tok(GMLP_WIDTH), tok(GMLP_WIDTH)],
        compiler_params=_params("parallel", "parallel"),
        name="l0_proj",
    )(ctx, x, mod, g, w_in, qg, kg, lng, cos, sa, sb, gq, gk)


def _attn_kernel(qt_ref, k_ref, vt_ref, o_ref, s_ref, p_ref, *, n_ctx, nct):
    ti = pl.program_id(1)
    tq = qt_ref.shape[2]
    row = lax.broadcasted_iota(I32, (LANES, tq), 0)
    low = row < HEAD_DIM
    kb = ATTN_KEY_BLOCK
    va = HEAD_DIM + V_ONES_ROWS

    def attend(nk):
        heads = [(m, j) for m in range(ATTN_WIDTH // LANES) for j in range(2)]
        blocks = [slice(kb * b, kb * (b + 1)) for b in range(nk // kb)]
        state = {}

        def start(h):
            m, j = heads[h]
            qp = qt_ref[0, LANES * m:LANES * (m + 1), :]
            state[h] = dict(qm=jnp.where(low if j == 0 else jnp.logical_not(low), qp, jnp.zeros_like(qp)), mx=None)

        def score_block(h, blk):
            s = jnp.dot(k_ref[0, blk, :], state[h]["qm"], preferred_element_type=F32)
            s_ref[h % 2, blk, :] = s
            bm = jnp.max(s, axis=0, keepdims=True)
            state[h]["mx"] = bm if state[h]["mx"] is None else jnp.maximum(state[h]["mx"], bm)

        def exp_block(h, blk):
            p_ref[h % 2, blk, :] = jnp.exp(s_ref[h % 2, blk, :] - state[h]["mx"]).astype(BF16)

        def finish(h):
            j = heads[h][1]
            pv = jnp.dot(vt_ref[0, va * j:va * (j + 1), :nk], p_ref[h % 2, :nk, :],
                         preferred_element_type=F32)
            return pv[:HEAD_DIM] / pv[HEAD_DIM:HEAD_DIM + 1]

        start(0)
        for blk in blocks:
            score_block(0, blk)
        halves = []
        for h in range(len(heads)):
            if h + 1 < len(heads):
                start(h + 1)
            for blk in blocks:
                if h + 1 < len(heads):
                    score_block(h + 1, blk)
                exp_block(h, blk)
            halves.append(finish(h))
            if heads[h][1] == 1:
                m = heads[h][0]
                o_ref[0, :, LANES * m:LANES * (m + 1)] = jnp.concatenate(halves, axis=0).T.astype(BF16)
                halves = []

    @pl.when(ti < nct)
    def _():
        attend(n_ctx)

    @pl.when(ti >= nct)
    def _():
        attend(k_ref.shape[1])


def _attention(qt, k, vt, n_ctx):
    b, l, _ = k.shape
    t = TOKEN_TILE
    kern = functools.partial(_attn_kernel, n_ctx=n_ctx, nct=n_ctx // t)
    return pl.pallas_call(
        kern,
        out_shape=jax.ShapeDtypeStruct((b, l, ATTN_WIDTH), BF16),
        grid=(b, l // t),
        in_specs=[pl.BlockSpec((1, ATTN_WIDTH, t), lambda bi, ti: (bi, 0, ti)),
                  pl.BlockSpec((1, l, KV_WIDTH), lambda bi, ti: (bi, 0, 0)),
                  pl.BlockSpec((1, V_AUG_ROWS, l), lambda bi, ti: (bi, 0, 0))],
        out_specs=pl.BlockSpec((1, t, ATTN_WIDTH), lambda bi, ti: (bi, ti, 0)),
        scratch_shapes=[pltpu.VMEM((2, l, t), F32), pltpu.VMEM((2, l, t), BF16)],
        compiler_params=_params("parallel", "parallel"),
        name="attention",
    )(qt, k, vt)


def _interleave(*progs):
    live = dict(enumerate(progs))
    rnd = 0
    while live:
        for k in sorted(live):
            if rnd >= k:
                try:
                    next(live[k])
                except StopIteration:
                    del live[k]
        rnd += 1


def _route_tail(j, x, y, m, gf, wr_ref, br_ref, shared, x_ref, f_ref, ri_ref, rw_ref):
    x1 = x + m[2:3] * y
    x_ref[j] = x1
    f = _modulate(x1, gf, m[3:4], m[4:5])
    _store_row_tiles(f_ref.at[j], f)
    t = f.shape[0]
    yield
    logits = _dot_split(f, wr_ref[0], wr_ref[1]) + br_ref[...]
    yield
    lane = lax.broadcasted_iota(I32, (t, LANES), 1).astype(F32)
    is_grp = lane < N_EXPERT_GROUPS
    gl = jnp.where(is_grp, logits, NEG_BIG)
    gmax = jnp.max(gl, axis=-1, keepdims=True)
    g_idx = jnp.min(jnp.where(gl == gmax, lane, float(LANES)), axis=-1, keepdims=True)
    g_w = 1.0 / jnp.sum(jnp.where(is_grp, jnp.exp(gl - gmax), 0.0), axis=-1, keepdims=True)
    e_lo = N_EXPERT_GROUPS + EXPERTS_PER_GROUP * g_idx
    in_grp = jnp.logical_and(lane >= e_lo, lane < e_lo + EXPERTS_PER_GROUP)
    sel = jnp.where(in_grp, logits, NEG_BIG)
    v1 = jnp.max(sel, axis=-1, keepdims=True)
    i1 = jnp.min(jnp.where(sel == v1, lane, float(LANES)), axis=-1, keepdims=True)
    sel2 = jnp.where(lane == i1, NEG_BIG, sel)
    v2 = jnp.max(sel2, axis=-1, keepdims=True)
    i2 = jnp.min(jnp.where(sel2 == v2, lane, float(LANES)), axis=-1, keepdims=True)
    e = jnp.exp(v2 - v1)
    w1 = g_w / (1.0 + e)
    w2 = g_w * e / (1.0 + e)
    eid1 = i1 - N_EXPERT_GROUPS
    eid2 = i2 - N_EXPERT_GROUPS
    hit1 = lane == eid1
    hit2 = lane == eid2
    cnt = (hit1.astype(F32) + hit2.astype(F32))
    yield

    row = lax.broadcasted_iota(I32, (t, t), 0)
    col = lax.broadcasted_iota(I32, (t, t), 1)
    tril = jnp.where(col < row, 1.0, 0.0).astype(BF16)
    before = jnp.dot(tril, cnt.astype(BF16), preferred_element_type=F32) + shared["base"]
    shared["base"] = shared["base"] + jnp.sum(cnt, axis=0, keepdims=True)
    rank1 = jnp.sum(jnp.where(hit1, before, 0.0), axis=-1, keepdims=True)
    rank2 = jnp.sum(jnp.where(hit2, before, 0.0), axis=-1, keepdims=True)

    ri = jnp.where(lane == 0, eid1, jnp.where(lane == 1, eid2, jnp.where(lane == 2, rank1,
                   jnp.where(lane == 3, rank2, 0.0))))
    ri_ref[j] = ri.T[0:8, :].astype(I32)
    l8 = lax.broadcasted_iota(I32, (t, 8), 1)
    rw_ref[j] = jnp.where(l8 == 0, w1, jnp.where(l8 == 1, w2, 0.0))


def _run_pair(prog, base_ref, cnt_ref):
    @pl.when(jnp.logical_and(pl.program_id(0) == 0, pl.program_id(1) == 0))
    def _():
        base_ref[...] = jnp.zeros_like(base_ref)

    shared = {"base": base_ref[0:1, :]}
    _interleave(*(prog(j, shared) for j in range(PAIR)))
    base_ref[...] = jnp.broadcast_to(shared["base"], base_ref.shape)
    cnt_ref[...] = jnp.broadcast_to(shared["base"], cnt_ref.shape)


def _tail_out_shapes(b, n_rows, d):
    t = TOKEN_TILE
    shapes = [jax.ShapeDtypeStruct((b, n_rows, d), F32), jax.ShapeDtypeStruct((b, n_rows * ROW_SUB, LANES), F32),
              jax.ShapeDtypeStruct((b, 8, n_rows), I32), jax.ShapeDtypeStruct((b, n_rows, 8), F32),
              jax.ShapeDtypeStruct((8, LANES), F32)]
    specs = [pl.BlockSpec((PAIR, t, d), lambda bp, ti: (bp, ti, 0)),
             pl.BlockSpec((PAIR, t * ROW_SUB, LANES), lambda bp, ti: (bp, ti, 0)),
             pl.BlockSpec((PAIR, 8, t), lambda bp, ti: (bp, 0, ti)),
             pl.BlockSpec((PAIR, t, 8), lambda bp, ti: (bp, ti, 0)),
             pl.BlockSpec((8, LANES), lambda bp, ti: (0, 0))]
    return shapes, specs


def _tail_flatten(x1, f, ri, rw):
    b, n_rows, d = x1.shape
    return (x1, f.reshape(b * n_rows * ROW_SUB, LANES), jnp.transpose(ri, (1, 0, 2)).reshape(8, b * n_rows),
            rw.reshape(b * n_rows, 8))


def _l0_post_kernel(ctx_ref, x_ref, o_ref, gu_ref, gv_ref, wsp_ref, bsp_ref, wout_ref, mod_ref, gf_ref, wr_ref,
                    br_ref, x1_ref, f_ref, ri_ref, rw_ref, cnt_ref, base_ref, *, nct):
    tq = x_ref.shape[1]
    lane = lax.broadcasted_iota(I32, (GMLP_CHUNK, LANES), 1)
    low = lane < GMLP_GROUP_DIM

    def prog(j, shared):
        gated = []
        for c in range(tq // GMLP_CHUNK):
            rows = slice(c * GMLP_CHUNK, (c + 1) * GMLP_CHUNK)
            mixed = []
            for m in range(GMLP_WIDTH // LANES):
                pair = gv_ref[j, rows, LANES * m:LANES * (m + 1)]
                a = jnp.dot(wsp_ref[2 * m], pair, preferred_element_type=F32)
                b = jnp.dot(wsp_ref[2 * m + 1], pair, preferred_element_type=F32)
                mixed.append(jnp.where(low, a, b))
            mixed = jnp.concatenate(mixed, axis=1) + bsp_ref[...]
            gated.append((gu_ref[j, rows, :].astype(F32) * mixed).astype(BF16))
        gated = jnp.concatenate(gated, axis=0)
        yield
        y = (jnp.dot(o_ref[j], wout_ref[:ATTN_WIDTH, :], preferred_element_type=F32)
             + jnp.dot(gated, wout_ref[ATTN_WIDTH:, :], preferred_element_type=F32))
        yield
        yield from _route_tail(j, _seq_tile(ctx_ref, x_ref, nct, j), y, mod_ref[j, 0], gf_ref[...], wr_ref, br_ref,
                               shared, x1_ref, f_ref, ri_ref, rw_ref)

    _run_pair(prog, base_ref, cnt_ref)


def _l0_post(ctx, x, o, gu, gv, wsp, bsp, wout, mod, gf, wr, br, nct):
    b, s, d = x.shape
    l = s + ctx.shape[1]
    t = TOKEN_TILE
    tok = lambda w: pl.BlockSpec((PAIR, t, w), lambda bp, ti: (bp, ti, 0))
    shapes, specs = _tail_out_shapes(b, l, d)
    outs = pl.pallas_call(
        functools.partial(_l0_post_kernel, nct=nct),
        out_shape=shapes,
        grid=(b // PAIR, l // t),
        in_specs=_seq_specs(t, d, nct, PAIR) + [tok(ATTN_WIDTH), tok(GMLP_WIDTH), tok(GMLP_WIDTH),
                  _full((GMLP_GROUPS, GMLP_CHUNK, GMLP_CHUNK)), _full((GMLP_CHUNK, GMLP_WIDTH)),
                  _full((ATTN_WIDTH + GMLP_WIDTH, d)),
                  pl.BlockSpec((PAIR, 1, 6, d), lambda bp, ti: (bp, jnp.where(ti >= nct, 1, 0), 0, 0)),
                  _full((1, d)), _full((2, d, LANES)), _full((1, LANES))],
        out_specs=specs,
        scratch_shapes=[pltpu.VMEM((8, LANES), F32)],
        compiler_params=_params("arbitrary", "arbitrary"),
        name="l0_post",
    )(ctx, x, o, gu, gv, wsp, bsp, wout, mod, gf, wr, br)
    return _tail_flatten(*outs[:4]) + (outs[4],)


def _row_copy(src_ref, src_row, dst_ref, dst_row, sem):
    rows = lambda ref, r: ref.at[pl.ds(pl.multiple_of(r * ROW_SUB, ROW_SUB), ROW_SUB)]
    return pltpu.make_async_copy(rows(src_ref, src_row), rows(dst_ref, dst_row), sem)


def _issue_rows(src_ref, index_of, buf_ref, sem_rows, s):
    n_str, rows = buf_ref.shape[1], buf_ref.shape[2] // ROW_SUB
    slot = s % 2

    def issue(j, carry):
        for half in range(2):
            r = 2 * j + half
            for k in range(n_str):
                _row_copy(src_ref, index_of(k * rows + r), buf_ref.at[slot, k], r,
                          sem_rows.at[slot]).start(priority=(k + half) % 2)
        return carry

    lax.fori_loop(0, rows // 2, issue, 0, unroll=ROW_DMA_UNROLL // 2)


def _wait_rows(src_ref, buf_ref, sem_rows, s):
    n_str, rows = buf_ref.shape[1], buf_ref.shape[2] // ROW_SUB
    slot = s % 2
    for k in range(n_str):
        pltpu.make_async_copy(src_ref.at[pl.ds(0, rows * ROW_SUB)], buf_ref.at[slot, k], sem_rows.at[slot]).wait()
    return slot


def _gather_pipeline(src_ref, table_ref, idx_ref, buf_ref, sem_idx, sem_rows, step, n_steps):
    def idx_copy(s):
        return pltpu.make_async_copy(table_ref.at[s], idx_ref.at[s % 2], sem_idx.at[s % 2])

    def issue_rows(s):
        _issue_rows(src_ref, lambda pos: idx_ref[s % 2, pos], buf_ref, sem_rows, s)

    @pl.when(step == 0)
    def _():
        idx_copy(0).start()
        idx_copy(0).wait()
        issue_rows(0)

        @pl.when(n_steps > 1)
        def _():
            idx_copy(1).start()

    @pl.when(step + 1 < n_steps)
    def _():
        idx_copy(step + 1).wait()
        issue_rows(step + 1)

        @pl.when(step + 2 < n_steps)
        def _():
            idx_copy(step + 2).start()

    return _wait_rows(src_ref, buf_ref, sem_rows, step)


def _gather_scratch(n_str, rows):
    return [pltpu.SMEM((2, n_str * rows), I32), pltpu.VMEM((2, n_str, rows * ROW_SUB, LANES), F32),
            pltpu.SemaphoreType.DMA((2,)), pltpu.SemaphoreType.DMA((2,))]


def _dispatch_kernel(pad_ref, nu_ref, f_ref, slots_ref, xb_ref, idx_ref, zero_ref, sem_idx, sem_rows):
    i = pl.program_id(0)
    t = f_ref.shape[0] // ROW_SUB
    blk = MOE_BLOCK * ROW_SUB
    idx_cp = pltpu.make_async_copy(slots_ref.at[i], idx_ref, sem_idx)
    idx_cp.start()

    @pl.when(i == 0)
    def _():
        zero_ref[...] = jnp.zeros_like(zero_ref)
        n_blocks = xb_ref.shape[0] // blk

        def zero_block(start):
            return pltpu.make_async_copy(zero_ref, xb_ref.at[pl.ds(pl.multiple_of(start * ROW_SUB, blk), blk)],
                                         sem_rows)

        def tail_start(j, carry):
            zero_block(j * MOE_BLOCK).start()
            return carry

        def tail_wait(j, carry):
            zero_block(0).wait()
            return carry

        for e in range(N_EXPERTS):
            zero_block(jnp.maximum(pad_ref[N_EXPERTS + e] - MOE_BLOCK, 0)).start()
        lax.fori_loop(nu_ref[0], n_blocks, tail_start, 0)
        for e in range(N_EXPERTS):
            zero_block(0).wait()
        lax.fori_loop(nu_ref[0], n_blocks, tail_wait, 0)

    idx_cp.wait()

    def issue(r, carry):
        _row_copy(f_ref, r, xb_ref, idx_ref[r], sem_rows).start(priority=0)
        _row_copy(f_ref, r, xb_ref, idx_ref[t + r], sem_rows).start(priority=1)
        return carry

    lax.fori_loop(0, t, issue, 0, unroll=ROW_DMA_UNROLL)
    for _ in range(2):
        pltpu.make_async_copy(f_ref, xb_ref.at[pl.ds(0, t * ROW_SUB)], sem_rows).wait()


def _dispatch(f, slots, pad, n_used, n_slots):
    t = TOKEN_TILE
    n_tok = f.shape[0] // ROW_SUB
    return pl.pallas_call(
        _dispatch_kernel,
        out_shape=jax.ShapeDtypeStruct((n_slots * ROW_SUB, LANES), F32),
        grid_spec=pltpu.PrefetchScalarGridSpec(
            num_scalar_prefetch=2,
            grid=(n_tok // t,),
            in_specs=[pl.BlockSpec((t * ROW_SUB, LANES), lambda i, p, nu: (i, 0)),
                      pl.BlockSpec(memory_space=pltpu.VMEM)],
            out_specs=pl.BlockSpec(memory_space=pl.ANY),
            scratch_shapes=[pltpu.SMEM((2 * t,), I32), pltpu.VMEM((MOE_BLOCK * ROW_SUB, LANES), F32),
                            pltpu.SemaphoreType.DMA, pltpu.SemaphoreType.DMA]),
        compiler_params=_params("arbitrary"),
        name="moe_dispatch",
    )(pad, n_used, f, slots)


def _experts_kernel(be_ref, nu_ref, xb_ref, w1_ref, w3_ref, w2_ref, y_ref, w1b_ref, w3b_ref, w2b_ref):
    i = pl.program_id(0)

    @pl.when(jnp.logical_or(i == 0, be_ref[i] != be_ref[jnp.maximum(i - 1, 0)]))
    def _():
        w1b_ref[...] = w1_ref[...].astype(BF16)
        w3b_ref[...] = w3_ref[...].astype(BF16)
        w2b_ref[...] = w2_ref[...].astype(BF16)

    @pl.when(i < nu_ref[0])
    def _():
        x = _load_row_tiles(xb_ref).astype(BF16)
        h1 = jnp.dot(x, w1b_ref[...], preferred_element_type=F32)
        h3 = jnp.dot(x, w3b_ref[...], preferred_element_type=F32)
        a = (h1 * jax.nn.sigmoid(h1) * h3).astype(BF16)
        _store_row_tiles(y_ref, jnp.dot(a, w2b_ref[...], preferred_element_type=F32))

    @pl.when(i >= nu_ref[0])
    def _():
        y_ref[...] = jnp.zeros_like(y_ref)


def _experts(xb, w1, w3, w2, layer, block_e, n_used):
    n_blocks = block_e.shape[0]
    _, _, d, hid = w1.shape
    rb = MOE_BLOCK * ROW_SUB
    wspec = lambda s: pl.BlockSpec((None, None) + s, lambda i, be, nu: (layer, be[i], 0, 0))
    return pl.pallas_call(
        _experts_kernel,
        out_shape=jax.ShapeDtypeStruct((n_blocks * rb, LANES), F32),
        grid_spec=pltpu.PrefetchScalarGridSpec(
            num_scalar_prefetch=2,
            grid=(n_blocks,),
            in_specs=[pl.BlockSpec((rb, LANES), lambda i, be, nu: (jnp.minimum(i, nu[0] - 1), 0)),
                      wspec((d, hid)), wspec((d, hid)), wspec((hid, d))],
            out_specs=pl.BlockSpec((rb, LANES), lambda i, be, nu: (i, 0)),
            scratch_shapes=[pltpu.VMEM((d, hid), BF16), pltpu.VMEM((d, hid), BF16), pltpu.VMEM((hid, d), BF16)]),
        compiler_params=_params("arbitrary"),
        name="moe_experts",
    )(block_e, n_used, xb, w1, w3, w2)


def _moe_plan(route_i, counts, n_tok):
    eid, rank = route_i[0:2], route_i[2:4]
    cnt = counts[0, :N_EXPERTS].astype(I32)
    padded = (cnt + MOE_BLOCK - 1) // MOE_BLOCK * MOE_BLOCK
    pend = jnp.cumsum(padded)
    pstart = pend - padded
    experts = jnp.arange(N_EXPERTS, dtype=I32)
    slots = rank + jnp.sum(jnp.where(eid[:, :, None] == experts, pstart, 0), axis=-1)
    n_assign = 2 * n_tok
    n_slots = (n_assign + MOE_BLOCK - 1) // MOE_BLOCK * MOE_BLOCK + N_EXPERTS * MOE_BLOCK
    n_blocks = n_slots // MOE_BLOCK
    bstart = jnp.arange(n_blocks, dtype=I32) * MOE_BLOCK
    block_e = jnp.minimum(jnp.sum(bstart[:, None] >= pend[None, :], axis=1), N_EXPERTS - 1).astype(I32)
    n_used = pend[-1] // MOE_BLOCK
    last_e = block_e[jnp.maximum(n_used - 1, 0)]
    block_e = jnp.where(jnp.arange(n_blocks) < n_used, block_e, last_e).astype(I32)
    slots = slots.reshape(2, n_tok // TOKEN_TILE, TOKEN_TILE).transpose(1, 0, 2)
    slots = slots.reshape(n_tok // TOKEN_TILE, 2 * TOKEN_TILE).astype(I32)
    pad = jnp.concatenate([pstart + cnt, pend]).astype(I32)
    return slots, block_e, n_used.reshape(1).astype(I32), pad, n_slots


def _moe(f, route_i, counts, w1, w3, w2, layer):
    n_tok = f.shape[0] // ROW_SUB
    slots, block_e, n_used, pad, n_slots = _moe_plan(route_i, counts, n_tok)
    xb = _dispatch(f, slots, pad, n_used, n_slots)
    y = _experts(xb, w1, w3, w2, layer, block_e, n_used)
    return y, slots


def _combined_rows(y_ref, slots_ref, rw_ref, idx_ref, ybuf_ref, sem_idx, sem_rows):
    slot = _gather_pipeline(y_ref, slots_ref, idx_ref, ybuf_ref, sem_idx, sem_rows,
                            pl.program_id(0), pl.num_programs(0))
    rw = rw_ref[...]
    w1, w2 = rw[:, 0:1], rw[:, 1:2]
    t = ybuf_ref.shape[2] // ROW_SUB
    part = lambda k, s: ybuf_ref[slot, k, pl.ds(s, t, stride=ROW_SUB), :]
    return jnp.concatenate([w1 * part(0, s) + w2 * part(1, s) for s in range(ROW_SUB)], axis=1)


def _l0_combine_kernel(x_ref, rw_ref, slots_ref, y_ref, mod0_ref, mod1_ref, g_ref, win_ref,
                       x2_ref, u_ref, idx_ref, ybuf_ref, sem_idx, sem_rows):
    moe = _combined_rows(y_ref, slots_ref, rw_ref, idx_ref, ybuf_ref, sem_idx, sem_rows)
    x2 = x_ref[...] + mod0_ref[0, 0][5:6] * moe
    x2_ref[...] = x2
    m1 = mod1_ref[0, 0]
    h = _modulate(x2, g_ref[...], m1[0:1], m1[1:2]).astype(BF16)
    u_ref[...] = jnp.dot(h, win_ref[...], preferred_element_type=F32)


def _l0_combine(x1, rw, slots, y, mod0, mod1, g, win, b, nct):
    n_tok, d = x1.shape
    t = TOKEN_TILE
    nt = n_tok // b // t
    l = n_tok // b
    modspec = pl.BlockSpec((1, 1, 6, d), lambda i: (i // nt, jnp.where(i % nt >= nct, 1, 0), 0, 0))
    return pl.pallas_call(
        _l0_combine_kernel,
        out_shape=[jax.ShapeDtypeStruct((n_tok, d), F32), jax.ShapeDtypeStruct((l, b * S5_WIDTH), F32)],
        grid=(n_tok // t,),
        in_specs=[pl.BlockSpec((t, d), lambda i: (i, 0)), pl.BlockSpec((t, 8), lambda i: (i, 0)),
                  pl.BlockSpec(memory_space=pltpu.VMEM), pl.BlockSpec(memory_space=pl.ANY),
                  modspec, modspec, _full((1, d)), _full((d, S5_WIDTH))],
        out_specs=[pl.BlockSpec((t, d), lambda i: (i, 0)),
                   pl.BlockSpec((t, S5_WIDTH), lambda i: (i % nt, i // nt))],
        scratch_shapes=_gather_scratch(2, t),
        compiler_params=_params("arbitrary"),
        name="l0_combine",
    )(x1, rw, slots, y, mod0, mod1, g, win)


def _s5_zoh_kernel(are_ref, aim_ref, ldt_ref, bre_ref, bim_ref, lre_ref, lim_ref, bbre_ref, bbim_ref):
    ar, ai = are_ref[...], aim_ref[...]
    dt = jnp.exp(ldt_ref[...])
    mag = jnp.exp(ar * dt)
    lr = mag * jnp.cos(ai * dt)
    li = mag * jnp.sin(ai * dt)
    lre_ref[...] = lr
    lim_ref[...] = li
    nr, ni = lr - 1.0, li
    den = ar * ar + ai * ai
    cr = (nr * ar + ni * ai) / den
    ci = (ni * ar - nr * ai) / den
    br, bi = bre_ref[...], bim_ref[...]
    bbre_ref[...] = cr * br - ci * bi
    bbim_ref[...] = cr * bi + ci * br


def _s5_zoh(a_re, a_im, log_dt, b_re, b_im):
    n_dir, g, n, p = b_re.shape
    rows = n_dir * g * p
    expand = lambda a: jnp.broadcast_to(a[:, :, None, :], (n_dir, g, p, n)).reshape(rows, n)
    ldt = jnp.broadcast_to(log_dt[:, :, None, None], (n_dir, g, p, n)).reshape(rows, n)
    bt = lambda a: jnp.transpose(a, (0, 1, 3, 2)).reshape(rows, n)
    outs = pl.pallas_call(
        _s5_zoh_kernel,
        out_shape=[jax.ShapeDtypeStruct((rows, n), F32)] * 4,
        name="s5_zoh",
    )(expand(a_re), expand(a_im), ldt, bt(b_re), bt(b_im))
    lre, lim, bbre, bbim = (o.reshape(n_dir, g, p, n) for o in outs)
    return lre[:, :, 0, :], lim[:, :, 0, :], bbre, bbim


def _s5_matrices(lre, lim, bbre, bbim, c_re, c_im):
    n_dir = lre.shape[0]
    nj = S5_GROUPS // (2 * S5_TILES)
    shp = (n_dir, nj, S5_TILES, 2, S5_GROUP_DIM, S5_STATE)
    kdim = S5_TILES * 2 * S5_GROUP_DIM
    sdim = S5_TILES * 2 * 2 * S5_STATE
    lane_h = (np.arange(LANES) // S5_STATE).reshape(1, 1, 1, 1, 1, LANES)
    own_group = jnp.asarray(lane_h == np.arange(2).reshape(1, 1, 1, 2, 1, 1), F32)
    own_tile = jnp.eye(S5_TILES, dtype=F32).reshape(1, 1, S5_TILES, 1, 1, S5_TILES, 1, 1)

    def place(re, im):
        parts = [jnp.concatenate([a.reshape(shp)] * 2, axis=-1) * own_group for a in (re, im)]
        a = jnp.stack(parts, axis=5)[:, :, :, :, :, None] * own_tile
        return a.reshape(n_dir, nj, kdim, sdim)

    bm = place(bbre, bbim).astype(BF16)
    cm = jnp.swapaxes(place(c_re, -c_im), 2, 3).astype(BF16)
    lam = jnp.stack([lre.reshape(n_dir, nj, S5_TILES * LANES), lim.reshape(n_dir, nj, S5_TILES * LANES)], axis=2)
    return bm, cm, lam


def _s5_scan_kernel(uf_ref, ub_ref, bm_ref, cm_ref, lam_ref, yf_ref, yb_ref, v_ref, h_ref, acc_ref, uk_ref,
                    *, tc, nb):
    @pl.when(pl.program_id(0) == 0)
    def _():
        h_ref[...] = jnp.zeros_like(h_ref)

    nj, kdim, sdim = bm_ref.shape[1], bm_ref.shape[2], bm_ref.shape[3]
    u_refs, y_refs = (uf_ref, ub_ref), (yf_ref, yb_ref)
    phases = [(d, jq) for d in range(2) for jq in range(nj)]
    n_sub = 2 * S5_TILES
    col = sdim // S5_TILES
    run = tc // n_sub
    state = {}

    def drive_piece(p, c):
        d, jq = phases[p]
        if c == 0:
            for bb in range(nb):
                lo = bb * S5_WIDTH + kdim * jq
                uk_ref[pl.ds(bb, tc, stride=nb), :] = u_refs[d][:, lo:lo + kdim]
            state["uk"] = uk_ref[...].astype(BF16)
        v_ref[p % 3, :, col * c:col * (c + 1)] = jnp.dot(state["uk"], bm_ref[d, jq, :, col * c:col * (c + 1)],
                                                         preferred_element_type=F32)

    def readout_piece(p, c):
        d, jq = phases[p]
        part = jnp.dot(v_ref[p % 3, :, col * c:col * (c + 1)].astype(BF16), cm_ref[d, jq, col * c:col * (c + 1), :],
                       preferred_element_type=F32)
        if c == 0:
            acc_ref[...] = part
        elif c < S5_TILES - 1:
            acc_ref[...] += part
        else:
            acc_ref[...] += part
            for bb in range(nb):
                lo = bb * S5_WIDTH + kdim * jq
                y_refs[d][:, lo:lo + kdim] = acc_ref[pl.ds(bb, tc, stride=nb), :]

    def scan_run(p, c):
        d, jq = phases[p]
        if c == 0:
            lam = lam_ref[d, jq]
            state["lr"] = [jnp.broadcast_to(lam[0:1, LANES * t:LANES * (t + 1)], (nb, LANES)) for t in range(S5_TILES)]
            state["li"] = [jnp.broadcast_to(lam[1:2, LANES * t:LANES * (t + 1)], (nb, LANES)) for t in range(S5_TILES)]
            h0 = h_ref[d, jq]
            state["h"] = [h0[:, LANES * i:LANES * (i + 1)] for i in range(2 * S5_TILES)]
        lr, li, h = state["lr"], state["li"], state["h"]
        for s in range(run * c, run * (c + 1)):
            r0 = (s if d == 0 else tc - 1 - s) * nb
            rows = v_ref[p % 3, r0:r0 + nb, :]
            new = []
            for t in range(S5_TILES):
                vr = rows[:, 2 * LANES * t:2 * LANES * t + LANES]
                vi = rows[:, 2 * LANES * t + LANES:2 * LANES * (t + 1)]
                hr, hi = h[2 * t], h[2 * t + 1]
                new.append(lr[t] * hr - li[t] * hi + vr)
                new.append(lr[t] * hi + li[t] * hr + vi)
            v_ref[p % 3, r0:r0 + nb, :] = jnp.concatenate(new, axis=1)
            h = new
        state["h"] = h
        if c == n_sub - 1:
            h_ref[d, jq] = jnp.concatenate(h, axis=1)

    for slot in range(len(phases) + 2):
        for c in range(n_sub):
            if 0 <= slot - 1 < len(phases):
                scan_run(slot - 1, c)
            if c < S5_TILES and slot < len(phases):
                drive_piece(slot, c)
            if c >= S5_TILES and 0 <= slot - 2 < len(phases):
                readout_piece(slot - 2, c - S5_TILES)


def _s5_scan(u_tm, bm, cm, lam, nb, n_ctx):
    l, w = u_tm.shape
    tc = S5_CHUNK
    r = tc * nb
    n_chunks = l // tc
    ncc = n_ctx // tc
    bwd = lambda i: (jnp.where(i < ncc, ncc - 1 - i, n_chunks - 1 - (i - ncc)), 0)
    nj = bm.shape[1]
    sdim = bm.shape[3]
    kern = functools.partial(_s5_scan_kernel, tc=tc, nb=nb)
    return pl.pallas_call(
        kern,
        out_shape=[jax.ShapeDtypeStruct((l, w), F32)] * 2,
        grid=(n_chunks,),
        in_specs=[pl.BlockSpec((tc, w), lambda i: (i, 0)), pl.BlockSpec((tc, w), bwd),
                  _full(bm.shape), _full(cm.shape), _full(lam.shape)],
        out_specs=[pl.BlockSpec((tc, w), lambda i: (i, 0)), pl.BlockSpec((tc, w), bwd)],
        scratch_shapes=[pltpu.VMEM((3, r, sdim), F32), pltpu.VMEM((2, nj, nb, sdim), F32),
                        pltpu.VMEM((r, bm.shape[2]), F32), pltpu.VMEM((r, bm.shape[2]), F32)],
        compiler_params=_params("arbitrary"),
        name="s5_scan",
    )(u_tm, u_tm, bm, cm, lam)


def _l1_post_kernel(x_ref, u_ref, yf_ref, yb_ref, dsk_ref, wglu_ref, bglu_ref, wout_ref, mod_ref, gf_ref,
                    wr_ref, br_ref, x3_ref, f_ref, ri_ref, rw_ref, cnt_ref, base_ref):
    def prog(j, shared):
        cols = slice(S5_WIDTH * j, S5_WIDTH * (j + 1))
        y = dsk_ref[...] * u_ref[:, cols] + yf_ref[:, cols].astype(F32) + yb_ref[:, cols].astype(F32)
        g = jax.nn.gelu(y, approximate=True)
        yield
        z = jnp.dot(g.astype(BF16), wglu_ref[...], preferred_element_type=F32) + bglu_ref[...]
        yield
        a = (g * jax.nn.sigmoid(z)).astype(BF16)
        yield
        o = jnp.dot(a, wout_ref[...], preferred_element_type=F32)
        yield
        yield from _route_tail(j, x_ref[j], o, mod_ref[j, 0], gf_ref[...], wr_ref, br_ref, shared,
                               x3_ref, f_ref, ri_ref, rw_ref)

    _run_pair(prog, base_ref, cnt_ref)


def _l1_post(x2, u_tm, yf, yb, dsk, wglu, bglu, wout, mod, gf, wr, br, n_ctx):
    b, l, d = x2.shape
    t = TOKEN_TILE
    nct = n_ctx // t
    s = l - n_ctx
    tm = pl.BlockSpec((t, PAIR * S5_WIDTH), lambda bp, ti: (ti + nct, bp))
    shapes, specs = _tail_out_shapes(b, s, d)
    outs = pl.pallas_call(
        _l1_post_kernel,
        out_shape=shapes,
        grid=(b // PAIR, s // t),
        in_specs=[pl.BlockSpec((PAIR, t, d), lambda bp, ti: (bp, ti + nct, 0)), tm, tm, tm,
                  _full((1, S5_WIDTH)), _full((S5_WIDTH, S5_WIDTH)), _full((1, S5_WIDTH)), _full((S5_WIDTH, d)),
                  pl.BlockSpec((PAIR, 1, 6, d), lambda bp, ti: (bp, 1, 0, 0)),
                  _full((1, d)), _full((2, d, LANES)), _full((1, LANES))],
        out_specs=specs,
        scratch_shapes=[pltpu.VMEM((8, LANES), F32)],
        compiler_params=_params("arbitrary", "arbitrary"),
        name="l1_post",
    )(x2, u_tm, yf, yb, dsk, wglu, bglu, wout, mod, gf, wr, br)
    return _tail_flatten(*outs[:4]) + (outs[4],)


def _final_kernel(x_ref, rw_ref, slots_ref, y_ref, mod_ref, g_ref, o_ref, idx_ref, ybuf_ref, sem_idx, sem_rows):
    moe = _combined_rows(y_ref, slots_ref, rw_ref, idx_ref, ybuf_ref, sem_idx, sem_rows)
    x4 = x_ref[...] + mod_ref[0, 0][5:6] * moe
    o_ref[...] = _rms(x4) * g_ref[...]


def _final(x3, rw, slots, y, mod, g, b):
    n_tok, d = x3.shape
    t = TOKEN_TILE
    nt = n_tok // b // t
    return pl.pallas_call(
        _final_kernel,
        out_shape=jax.ShapeDtypeStruct((n_tok, d), F32),
        grid=(n_tok // t,),
        in_specs=[pl.BlockSpec((t, d), lambda i: (i, 0)), pl.BlockSpec((t, 8), lambda i: (i, 0)),
                  pl.BlockSpec(memory_space=pltpu.VMEM), pl.BlockSpec(memory_space=pl.ANY),
                  pl.BlockSpec((1, 1, 6, d), lambda i: (i // nt, 1, 0, 0)), _full((1, d))],
        out_specs=pl.BlockSpec((t, d), lambda i: (i, 0)),
        scratch_shapes=_gather_scratch(2, t),
        compiler_params=_params("arbitrary"),
        name="final_combine",
    )(x3, rw, slots, y, mod, g)


def _rope_tables(n_ctx, n_lat):
    pos = jnp.arange(n_lat, dtype=I32)
    row = (pos // GRID_W).astype(F32)
    col = (pos % GRID_W).astype(F32)
    inv_freq = jnp.power(ROPE_THETA, -jnp.arange(0, AXIS_DIM, 2, dtype=F32) / AXIS_DIM)
    lane = np.arange(LANES)
    dd = lane % HEAD_DIM
    use_col = dd >= AXIS_DIM
    first = (dd % AXIS_DIM) < AXIS_DIM // 2
    freq = inv_freq[(dd % AXIS_DIM) % (AXIS_DIM // 2)]
    ang = jnp.where(use_col[None, :], col[:, None], row[:, None]) * freq[None, :]
    cos, sin = jnp.cos(ang), jnp.sin(ang)
    sa = jnp.where(first[None, :], -sin, 0.0)
    sb = jnp.where(first[None, :], 0.0, sin)
    pad = lambda a, v: jnp.concatenate([jnp.full((n_ctx, LANES), v, F32), a], axis=0)
    return pad(cos, 1.0), pad(sa, 0.0), pad(sb, 0.0)


def _group_ones(width, group):
    idx = np.arange(width) // group
    return jnp.asarray(idx[:, None] == idx[None, :], dtype=BF16)


def _router_weights(w_grp, b_grp, w_rt, b_rt):
    d = w_grp.shape[0]
    used = N_EXPERT_GROUPS + N_EXPERTS
    wr = jnp.concatenate([w_grp, w_rt, jnp.zeros((d, LANES - used), F32)], axis=1)
    br = jnp.concatenate([b_grp, b_rt, jnp.zeros((LANES - used,), F32)]).reshape(1, LANES)
    return jnp.stack(_split_bf16(wr)), br


def kernel(x, c, ctx, c_ctx, w_mod, b_mod, norm_mix_g, norm_ffn_g, mix_w_in, mix_w_out, q_norm_g, k_norm_g, gmlp_norm_g, gmlp_w_spatial, gmlp_b_spatial, s5_w_in, s5_a_re, s5_a_im, s5_log_dt, s5_b_re, s5_b_im, s5_c_re, s5_c_im, s5_d, s5_w_glu, s5_b_glu, s5_w_out, moe_w_group, moe_b_group, moe_w_router, moe_b_router, moe_w1, moe_w3, moe_w2, final_norm_g):
    b, s, d = x.shape
    n_ctx = ctx.shape[1]
    l = n_ctx + s
    assert w_mod.shape[0] == 2, "two layers: attention/gMLP then S5"
    assert n_ctx % TOKEN_TILE == 0 and s % TOKEN_TILE == 0 and b % 8 == 0 and d == D_MODEL
    nct = n_ctx // TOKEN_TILE

    r = (b + 1 + 7) // 8 * 8
    cond = jnp.concatenate([c, c_ctx[None, :], jnp.zeros((r - b - 1, d), F32)], axis=0)
    mods = _adaln(cond, w_mod, b_mod)
    lat = mods[:, :b].reshape(2, b, 1, 6, d)
    cxt = jnp.broadcast_to(mods[:, b].reshape(2, 1, 1, 6, d), (2, b, 1, 6, d))
    mod = jnp.concatenate([cxt, lat], axis=2)

    w_in = jnp.concatenate([mix_w_in[0][:, _Q_COL_PERM], mix_w_in[0][:, ATTN_WIDTH:]], axis=1).astype(BF16)
    w_out = jnp.concatenate([mix_w_out[0][_Q_COL_PERM], mix_w_out[0][ATTN_WIDTH:]], axis=0).astype(BF16)
    cos, sa, sb = _rope_tables(n_ctx, s)
    qg = jnp.tile(q_norm_g[0], N_Q_HEADS).reshape(1, ATTN_WIDTH)
    kg = jnp.tile(k_norm_g[0], N_KV_HEADS).reshape(1, KV_WIDTH)
    q, k, v, gu, gv = _l0_proj(ctx, x, mod[0], norm_mix_g[0].reshape(1, d), w_in, qg, kg,
                               gmlp_norm_g[0].reshape(1, GMLP_WIDTH), cos, sa, sb,
                               _group_ones(ATTN_WIDTH, HEAD_DIM), _group_ones(KV_WIDTH, HEAD_DIM), nct)
    o = _attention(q, k, v, n_ctx)
    bsp = jnp.repeat(gmlp_b_spatial[0].T, GMLP_GROUP_DIM, axis=1)
    wr0, br0 = _router_weights(moe_w_group[0], moe_b_group[0], moe_w_router[0], moe_b_router[0])
    x1, f0, ri0, rw0, cnt0 = _l0_post(ctx, x, o, gu, gv, gmlp_w_spatial[0].astype(BF16), bsp, w_out, mod[0],
                                      norm_ffn_g[0].reshape(1, d), wr0, br0, nct)
    y0, slots0 = _moe(f0, ri0, cnt0, moe_w1, moe_w3, moe_w2, 0)
    x2, u_tm = _l0_combine(x1.reshape(b * l, d), rw0, slots0, y0, mod[0], mod[1], norm_mix_g[1].reshape(1, d),
                           s5_w_in[0].astype(BF16), b, nct)

    lre, lim, bbre, bbim = _s5_zoh(s5_a_re[0], s5_a_im[0], s5_log_dt[0], s5_b_re[0], s5_b_im[0])
    bm, cm, lam = _s5_matrices(lre, lim, bbre, bbim, s5_c_re[0], s5_c_im[0])
    yf, yb = _s5_scan(u_tm, bm, cm, lam, b, n_ctx)
    wr1, br1 = _router_weights(moe_w_group[1], moe_b_group[1], moe_w_router[1], moe_b_router[1])
    x3, f1, ri1, rw1, cnt1 = _l1_post(x2.reshape(b, l, d), u_tm, yf, yb, s5_d[0].reshape(1, S5_WIDTH),
                                      s5_w_glu[0].astype(BF16), s5_b_glu[0].reshape(1, S5_WIDTH),
                                      s5_w_out[0].astype(BF16), mod[1], norm_ffn_g[1].reshape(1, d), wr1, br1, n_ctx)
    y1, slots1 = _moe(f1, ri1, cnt1, moe_w1, moe_w3, moe_w2, 1)
    out = _final(x3.reshape(b * s, d), rw1, slots1, y1, mod[1], final_norm_g.reshape(1, d), b)
    return out.reshape(b, s, d)
```

```python
import functools

import numpy as np
import jax
import jax.numpy as jnp
from jax import lax
from jax.experimental import pallas as pl
from jax.experimental.pallas import tpu as pltpu

F32 = jnp.float32
BF16 = jnp.bfloat16
I32 = jnp.int32

EPS = 1e-6
GRID_W = 64
HEAD_DIM = 64
N_Q_HEADS = 8
N_KV_HEADS = 2
ATTN_WIDTH = N_Q_HEADS * HEAD_DIM
KV_WIDTH = N_KV_HEADS * HEAD_DIM
AXIS_DIM = HEAD_DIM // 2
ROPE_THETA = 10000.0
GMLP_GROUPS = 8
GMLP_GROUP_DIM = 64
GMLP_WIDTH = GMLP_GROUPS * GMLP_GROUP_DIM
GMLP_CHUNK = 128
MIX_IN_WIDTH = ATTN_WIDTH + 2 * KV_WIDTH + 2 * GMLP_WIDTH
S5_GROUPS = 32
S5_GROUP_DIM = 16
S5_STATE = 64
S5_WIDTH = S5_GROUPS * S5_GROUP_DIM
N_EXPERT_GROUPS = 4
EXPERTS_PER_GROUP = 8
N_EXPERTS = N_EXPERT_GROUPS * EXPERTS_PER_GROUP
EXPERT_HIDDEN = 512
MOE_BLOCK = 512

D_MODEL = 1024
LANES = 128
ROW_SUB = D_MODEL // LANES
TOKEN_TILE = 256
PAIR = 4
ROW_DMA_UNROLL = 8
ATTN_KEY_BLOCK = 128
V_ONES_ROWS = 16
V_AUG_ROWS = N_KV_HEADS * (HEAD_DIM + V_ONES_ROWS)
S5_CHUNK = 64
S5_TILES = 4
VMEM_LIMIT = 48 * 1024 * 1024
NEG_BIG = -1e30

_Q_HEAD_ORDER = (0, 4, 1, 5, 2, 6, 3, 7)
_Q_COL_PERM = np.concatenate([np.arange(HEAD_DIM) + HEAD_DIM * h for h in _Q_HEAD_ORDER])


def _params(*sem):
    return pltpu.CompilerParams(dimension_semantics=sem, vmem_limit_bytes=VMEM_LIMIT)


def _rms(x):
    return x * lax.rsqrt(jnp.mean(x * x, axis=-1, keepdims=True) + EPS)


def _modulate(x, g, shift, scale):
    return _rms(x) * g * (1.0 + scale) + shift


def _full(shape):
    return pl.BlockSpec(shape, lambda *_: (0,) * len(shape))


def _split_bf16(a):
    hi = a.astype(BF16)
    return hi, (a - hi.astype(F32)).astype(BF16)


def _dot_split(a, w_hi, w_lo):
    a_hi, a_lo = _split_bf16(a)
    dot = functools.partial(jnp.dot, preferred_element_type=F32)
    return dot(a_hi, w_hi) + (dot(a_lo, w_hi) + dot(a_hi, w_lo))


def _store_row_tiles(ref, val):
    n = val.shape[0]
    for s in range(ROW_SUB):
        ref[pl.ds(s, n, stride=ROW_SUB), :] = val[:, LANES * s:LANES * (s + 1)]


def _load_row_tiles(ref):
    n = ref.shape[0] // ROW_SUB
    return jnp.concatenate([ref[pl.ds(s, n, stride=ROW_SUB), :] for s in range(ROW_SUB)], axis=1)


def _adaln_kernel(cond_ref, w_ref, b_ref, o_ref):
    c = cond_ref[...]
    s = c * jax.nn.sigmoid(c)
    o_ref[0] = _dot_split(s, *_split_bf16(w_ref[0])) + b_ref[0]


def _adaln(cond, w_mod, b_mod):
    depth, d, n = w_mod.shape
    r = cond.shape[0]
    tn = 1536
    return pl.pallas_call(
        _adaln_kernel,
        out_shape=jax.ShapeDtypeStruct((depth, r, n), F32),
        grid=(depth, n // tn),
        in_specs=[pl.BlockSpec((r, d), lambda l, j: (0, 0)),
                  pl.BlockSpec((1, d, tn), lambda l, j: (l, 0, j)),
                  pl.BlockSpec((1, 1, tn), lambda l, j: (l, 0, j))],
        out_specs=pl.BlockSpec((1, r, tn), lambda l, j: (l, 0, j)),
        compiler_params=_params("parallel", "parallel"),
        name="adaln",
    )(cond, w_mod, b_mod.reshape(depth, 1, n))


def _seq_tile(ctx_ref, x_ref, nct, j=0):
    return jnp.where(pl.program_id(1) < nct, ctx_ref[j], x_ref[j])


def _seq_specs(t, d, nct, nb=1):
    return [pl.BlockSpec((nb, t, d), lambda bi, ti: (bi, jnp.minimum(ti, nct - 1), 0)),
            pl.BlockSpec((nb, t, d), lambda bi, ti: (bi, jnp.maximum(ti - nct, 0), 0))]


def _l0_proj_kernel(ctx_ref, x_ref, mod_ref, g_ref, win_ref, qg_ref, kg_ref, lng_ref, cos_ref, sa_ref, sb_ref,
                    gq_ref, gk_ref, q_ref, k_ref, v_ref, gu_ref, gv_ref, *, nct):
    cos, sa, sb = cos_ref[...], sa_ref[...], sb_ref[...]

    def head_norm_rope(t, gsum_ref, gain, reps):
        w = t.shape[1]
        ss = jnp.dot((t * t).astype(BF16), gsum_ref[...], preferred_element_type=F32)
        tn = t * lax.rsqrt(ss * (1.0 / HEAD_DIM) + EPS) * gain
        c, a, b = (jnp.concatenate([tab] * reps, axis=1) if reps > 1 else tab for tab in (cos, sa, sb))
        return tn * c + pltpu.roll(tn, w - AXIS_DIM // 2, 1) * a + pltpu.roll(tn, AXIS_DIM // 2, 1) * b

    def prog(j):
        m = mod_ref[j, 0]
        h = _modulate(_seq_tile(ctx_ref, x_ref, nct, j), g_ref[...], m[0:1], m[1:2]).astype(BF16)
        yield
        z = jnp.dot(h, win_ref[...], preferred_element_type=F32)
        yield
        q = head_norm_rope(z[:, :ATTN_WIDTH], gq_ref, qg_ref[...], ATTN_WIDTH // LANES)
        q_ref[j] = (q * (HEAD_DIM ** -0.5)).T.astype(BF16)
        k = head_norm_rope(z[:, ATTN_WIDTH:ATTN_WIDTH + KV_WIDTH], gk_ref, kg_ref[...], 1)
        k_ref[j] = k.astype(BF16)
        vt = z[:, ATTN_WIDTH + KV_WIDTH:ATTN_WIDTH + 2 * KV_WIDTH].T
        ones = jnp.ones((V_ONES_ROWS, vt.shape[1]), F32)
        v_ref[j] = jnp.concatenate([part for kv in range(N_KV_HEADS)
                                    for part in (vt[HEAD_DIM * kv:HEAD_DIM * (kv + 1)], ones)], axis=0).astype(BF16)
        off = ATTN_WIDTH + 2 * KV_WIDTH
        gu_ref[j] = jax.nn.gelu(z[:, off:off + GMLP_WIDTH], approximate=True).astype(BF16)
        gv = jax.nn.gelu(z[:, off + GMLP_WIDTH:], approximate=True)
        mu = jnp.mean(gv, axis=-1, keepdims=True)
        gc = gv - mu
        gn = gc * lax.rsqrt(jnp.mean(gc * gc, axis=-1, keepdims=True) + EPS) * lng_ref[...]
        gv_ref[j] = gn.astype(BF16)

    _interleave(*(prog(j) for j in range(PAIR)))


def _l0_proj(ctx, x, mod, g, w_in, qg, kg, lng, cos, sa, sb, gq, gk, nct):
    b, s, d = x.shape
    l = s + ctx.shape[1]
    t = TOKEN_TILE
    tok = lambda w: pl.BlockSpec((PAIR, t, w), lambda bi, ti: (bi, ti, 0))
    tab = pl.BlockSpec((t, LANES), lambda bi, ti: (ti, 0))
    tok_t = lambda w: pl.BlockSpec((PAIR, w, t), lambda bi, ti: (bi, 0, ti))
    outs = ([jax.ShapeDtypeStruct((b, ATTN_WIDTH, l), BF16), jax.ShapeDtypeStruct((b, l, KV_WIDTH), BF16),
             jax.ShapeDtypeStruct((b, V_AUG_ROWS, l), BF16)]
            + [jax.ShapeDtypeStruct((b, l, GMLP_WIDTH), BF16)] * 2)
    return pl.pallas_call(
        functools.partial(_l0_proj_kernel, nct=nct),
        out_shape=outs,
        grid=(b // PAIR, l // t),
        in_specs=_seq_specs(t, d, nct, PAIR) + [
                  pl.BlockSpec((PAIR, 1, 6, d), lambda bi, ti: (bi, jnp.where(ti >= nct, 1, 0), 0, 0)),
                  _full((1, d)), _full((d, MIX_IN_WIDTH)), _full((1, ATTN_WIDTH)), _full((1, KV_WIDTH)),
                  _full((1, GMLP_WIDTH)), tab, tab, tab,
                  _full((ATTN_WIDTH, ATTN_WIDTH)), _full((KV_WIDTH, KV_WIDTH))],
        out_specs=[tok_t(ATTN_WIDTH), tok(KV_WIDTH), tok_t(V_AUG_ROWS), tok(GMLP_WIDTH), tok(GMLP_WIDTH)],
        compiler_params=_params("parallel", "parallel"),
        name="l0_proj",
    )(ctx, x, mod, g, w_in, qg, kg, lng, cos, sa, sb, gq, gk)


def _attn_kernel(qt_ref, k_ref, vt_ref, o_ref, s_ref, p_ref, *, n_ctx, nct):
    ti = pl.program_id(1)
    tq = qt_ref.shape[2]
    row = lax.broadcasted_iota(I32, (LANES, tq), 0)
    low = row < HEAD_DIM
    kb = ATTN_KEY_BLOCK
    va = HEAD_DIM + V_ONES_ROWS

    def attend(nk):
        heads = [(m, j) for m in range(ATTN_WIDTH // LANES) for j in range(2)]
        blocks = [slice(kb * b, kb * (b + 1)) for b in range(nk // kb)]
        state = {}

        def start(h):
            m, j = heads[h]
            qp = qt_ref[0, LANES * m:LANES * (m + 1), :]
            state[h] = dict(qm=jnp.where(low if j == 0 else jnp.logical_not(low), qp, jnp.zeros_like(qp)), mx=None)

        def score_block(h, blk):
            s = jnp.dot(k_ref[0, blk, :], state[h]["qm"], preferred_element_type=F32)
            s_ref[h % 2, blk, :] = s
            bm = jnp.max(s, axis=0, keepdims=True)
            state[h]["mx"] = bm if state[h]["mx"] is None else jnp.maximum(state[h]["mx"], bm)

        def exp_block(h, blk):
            p_ref[h % 2, blk, :] = jnp.exp(s_ref[h % 2, blk, :] - state[h]["mx"]).astype(BF16)

        def finish(h):
            j = heads[h][1]
            pv = jnp.dot(vt_ref[0, va * j:va * (j + 1), :nk], p_ref[h % 2, :nk, :],
                         preferred_element_type=F32)
            return pv[:HEAD_DIM] / pv[HEAD_DIM:HEAD_DIM + 1]

        start(0)
        for blk in blocks:
            score_block(0, blk)
        halves = []
        for h in range(len(heads)):
            if h + 1 < len(heads):
                start(h + 1)
            for blk in blocks:
                if h + 1 < len(heads):
                    score_block(h + 1, blk)
                exp_block(h, blk)
            halves.append(finish(h))
            if heads[h][1] == 1:
                m = heads[h][0]
                o_ref[0, :, LANES * m:LANES * (m + 1)] = jnp.concatenate(halves, axis=0).T.astype(BF16)
                halves = []

    @pl.when(ti < nct)
    def _():
        attend(n_ctx)

    @pl.when(ti >= nct)
    def _():
        attend(k_ref.shape[1])


def _attention(qt, k, vt, n_ctx):
    b, l, _ = k.shape
    t = TOKEN_TILE
    kern = functools.partial(_attn_kernel, n_ctx=n_ctx, nct=n_ctx // t)
    return pl.pallas_call(
        kern,
        out_shape=jax.ShapeDtypeStruct((b, l, ATTN_WIDTH), BF16),
        grid=(b, l // t),
        in_specs=[pl.BlockSpec((1, ATTN_WIDTH, t), lambda bi, ti: (bi, 0, ti)),
                  pl.BlockSpec((1, l, KV_WIDTH), lambda bi, ti: (bi, 0, 0)),
                  pl.BlockSpec((1, V_AUG_ROWS, l), lambda bi, ti: (bi, 0, 0))],
        out_specs=pl.BlockSpec((1, t, ATTN_WIDTH), lambda bi, ti: (bi, ti, 0)),
        scratch_shapes=[pltpu.VMEM((2, l, t), F32), pltpu.VMEM((2, l, t), BF16)],
        compiler_params=_params("parallel", "parallel"),
        name="attention",
    )(qt, k, vt)


def _interleave(*progs):
    live = dict(enumerate(progs))
    rnd = 0
    while live:
        for k in sorted(live):
            if rnd >= k:
                try:
                    next(live[k])
                except StopIteration:
                    del live[k]
        rnd += 1


def _route_tail(j, x, y, m, gf, wr_ref, br_ref, shared, x_ref, f_ref, ri_ref, rw_ref):
    x1 = x + m[2:3] * y
    x_ref[j] = x1
    f = _modulate(x1, gf, m[3:4], m[4:5])
    _store_row_tiles(f_ref.at[j], f)
    t = f.shape[0]
    yield
    logits = _dot_split(f, wr_ref[0], wr_ref[1]) + br_ref[...]
    yield
    lane = lax.broadcasted_iota(I32, (t, LANES), 1).astype(F32)
    is_grp = lane < N_EXPERT_GROUPS
    gl = jnp.where(is_grp, logits, NEG_BIG)
    gmax = jnp.max(gl, axis=-1, keepdims=True)
    g_idx = jnp.min(jnp.where(gl == gmax, lane, float(LANES)), axis=-1, keepdims=True)
    g_w = 1.0 / jnp.sum(jnp.where(is_grp, jnp.exp(gl - gmax), 0.0), axis=-1, keepdims=True)
    e_lo = N_EXPERT_GROUPS + EXPERTS_PER_GROUP * g_idx
    in_grp = jnp.logical_and(lane >= e_lo, lane < e_lo + EXPERTS_PER_GROUP)
    sel = jnp.where(in_grp, logits, NEG_BIG)
    v1 = jnp.max(sel, axis=-1, keepdims=True)
    i1 = jnp.min(jnp.where(sel == v1, lane, float(LANES)), axis=-1, keepdims=True)
    sel2 = jnp.where(lane == i1, NEG_BIG, sel)
    v2 = jnp.max(sel2, axis=-1, keepdims=True)
    i2 = jnp.min(jnp.where(sel2 == v2, lane, float(LANES)), axis=-1, keepdims=True)
    e = jnp.exp(v2 - v1)
    w1 = g_w / (1.0 + e)
    w2 = g_w * e / (1.0 + e)
    eid1 = i1 - N_EXPERT_GROUPS
    eid2 = i2 - N_EXPERT_GROUPS
    hit1 = lane == eid1
    hit2 = lane == eid2
    cnt = (hit1.astype(F32) + hit2.astype(F32))
    yield

    row = lax.broadcasted_iota(I32, (t, t), 0)
    col = lax.broadcasted_iota(I32, (t, t), 1)
    tril = jnp.where(col < row, 1.0, 0.0).astype(BF16)
    before = jnp.dot(tril, cnt.astype(BF16), preferred_element_type=F32) + shared["base"]
    shared["base"] = shared["base"] + jnp.sum(cnt, axis=0, keepdims=True)
    rank1 = jnp.sum(jnp.where(hit1, before, 0.0), axis=-1, keepdims=True)
    rank2 = jnp.sum(jnp.where(hit2, before, 0.0), axis=-1, keepdims=True)

    ri = jnp.where(lane == 0, eid1, jnp.where(lane == 1, eid2, jnp.where(lane == 2, rank1,
                   jnp.where(lane == 3, rank2, 0.0))))
    ri_ref[j] = ri.T[0:8, :].astype(I32)
    l8 = lax.broadcasted_iota(I32, (t, 8), 1)
    rw_ref[j] = jnp.where(l8 == 0, w1, jnp.where(l8 == 1, w2, 0.0))


def _run_pair(prog, base_ref, cnt_ref):
    @pl.when(jnp.logical_and(pl.program_id(0) == 0, pl.program_id(1) == 0))
    def _():
        base_ref[...] = jnp.zeros_like(base_ref)

    shared = {"base": base_ref[0:1, :]}
    _interleave(*(prog(j, shared) for j in range(PAIR)))
    base_ref[...] = jnp.broadcast_to(shared["base"], base_ref.shape)
    cnt_ref[...] = jnp.broadcast_to(shared["base"], cnt_ref.shape)


def _tail_out_shapes(b, n_rows, d):
    t = TOKEN_TILE
    shapes = [jax.ShapeDtypeStruct((b, n_rows, d), F32), jax.ShapeDtypeStruct((b, n_rows * ROW_SUB, LANES), F32),
              jax.ShapeDtypeStruct((b, 8, n_rows), I32), jax.ShapeDtypeStruct((b, n_rows, 8), F32),
              jax.ShapeDtypeStruct((8, LANES), F32)]
    specs = [pl.BlockSpec((PAIR, t, d), lambda bp, ti: (bp, ti, 0)),
             pl.BlockSpec((PAIR, t * ROW_SUB, LANES), lambda bp, ti: (bp, ti, 0)),
             pl.BlockSpec((PAIR, 8, t), lambda bp, ti: (bp, 0, ti)),
             pl.BlockSpec((PAIR, t, 8), lambda bp, ti: (bp, ti, 0)),
             pl.BlockSpec((8, LANES), lambda bp, ti: (0, 0))]
    return shapes, specs


def _tail_flatten(x1, f, ri, rw):
    b, n_rows, d = x1.shape
    return (x1, f.reshape(b * n_rows * ROW_SUB, LANES), jnp.transpose(ri, (1, 0, 2)).reshape(8, b * n_rows),
            rw.reshape(b * n_rows, 8))


def _l0_post_kernel(ctx_ref, x_ref, o_ref, gu_ref, gv_ref, wsp_ref, bsp_ref, wout_ref, mod_ref, gf_ref, wr_ref,
                    br_ref, x1_ref, f_ref, ri_ref, rw_ref, cnt_ref, base_ref, *, nct):
    tq = x_ref.shape[1]
    lane = lax.broadcasted_iota(I32, (GMLP_CHUNK, LANES), 1)
    low = lane < GMLP_GROUP_DIM

    def prog(j, shared):
        gated = []
        for c in range(tq // GMLP_CHUNK):
            rows = slice(c * GMLP_CHUNK, (c + 1) * GMLP_CHUNK)
            mixed = []
            for m in range(GMLP_WIDTH // LANES):
                pair = gv_ref[j, rows, LANES * m:LANES * (m + 1)]
                a = jnp.dot(wsp_ref[2 * m], pair, preferred_element_type=F32)
                b = jnp.dot(wsp_ref[2 * m + 1], pair, preferred_element_type=F32)
                mixed.append(jnp.where(low, a, b))
            mixed = jnp.concatenate(mixed, axis=1) + bsp_ref[...]
            gated.append((gu_ref[j, rows, :].astype(F32) * mixed).astype(BF16))
        gated = jnp.concatenate(gated, axis=0)
        yield
        y = (jnp.dot(o_ref[j], wout_ref[:ATTN_WIDTH, :], preferred_element_type=F32)
             + jnp.dot(gated, wout_ref[ATTN_WIDTH:, :], preferred_element_type=F32))
        yield
        yield from _route_tail(j, _seq_tile(ctx_ref, x_ref, nct, j), y, mod_ref[j, 0], gf_ref[...], wr_ref, br_ref,
                               shared, x1_ref, f_ref, ri_ref, rw_ref)

    _run_pair(prog, base_ref, cnt_ref)


def _l0_post(ctx, x, o, gu, gv, wsp, bsp, wout, mod, gf, wr, br, nct):
    b, s, d = x.shape
    l = s + ctx.shape[1]
    t = TOKEN_TILE
    tok = lambda w: pl.BlockSpec((PAIR, t, w), lambda bp, ti: (bp, ti, 0))
    shapes, specs = _tail_out_shapes(b, l, d)
    outs = pl.pallas_call(
        functools.partial(_l0_post_kernel, nct=nct),
        out_shape=shapes,
        grid=(b // PAIR, l // t),
        in_specs=_seq_specs(t, d, nct, PAIR) + [tok(ATTN_WIDTH), tok(GMLP_WIDTH), tok(GMLP_WIDTH),
                  _full((GMLP_GROUPS, GMLP_CHUNK, GMLP_CHUNK)), _full((GMLP_CHUNK, GMLP_WIDTH)),
                  _full((ATTN_WIDTH + GMLP_WIDTH, d)),
                  pl.BlockSpec((PAIR, 1, 6, d), lambda bp, ti: (bp, jnp.where(ti >= nct, 1, 0), 0, 0)),
                  _full((1, d)), _full((2, d, LANES)), _full((1, LANES))],
        out_specs=specs,
        scratch_shapes=[pltpu.VMEM((8, LANES), F32)],
        compiler_params=_params("arbitrary", "arbitrary"),
        name="l0_post",
    )(ctx, x, o, gu, gv, wsp, bsp, wout, mod, gf, wr, br)
    return _tail_flatten(*outs[:4]) + (outs[4],)


def _row_copy(src_ref, src_row, dst_ref, dst_row, sem):
    rows = lambda ref, r: ref.at[pl.ds(pl.multiple_of(r * ROW_SUB, ROW_SUB), ROW_SUB)]
    return pltpu.make_async_copy(rows(src_ref, src_row), rows(dst_ref, dst_row), sem)


def _issue_rows(src_ref, index_of, buf_ref, sem_rows, s):
    n_str, rows = buf_ref.shape[1], buf_ref.shape[2] // ROW_SUB
    slot = s % 2

    def issue(j, carry):
        for half in range(2):
            r = 2 * j + half
            for k in range(n_str):
                _row_copy(src_ref, index_of(k * rows + r), buf_ref.at[slot, k], r,
                          sem_rows.at[slot]).start(priority=(k + half) % 2)
        return carry

    lax.fori_loop(0, rows // 2, issue, 0, unroll=ROW_DMA_UNROLL // 2)


def _wait_rows(src_ref, buf_ref, sem_rows, s):
    n_str, rows = buf_ref.shape[1], buf_ref.shape[2] // ROW_SUB
    slot = s % 2
    for k in range(n_str):
        pltpu.make_async_copy(src_ref.at[pl.ds(0, rows * ROW_SUB)], buf_ref.at[slot, k], sem_rows.at[slot]).wait()
    return slot


def _gather_pipeline(src_ref, table_ref, idx_ref, buf_ref, sem_idx, sem_rows, step, n_steps):
    def idx_copy(s):
        return pltpu.make_async_copy(table_ref.at[s], idx_ref.at[s % 2], sem_idx.at[s % 2])

    def issue_rows(s):
        _issue_rows(src_ref, lambda pos: idx_ref[s % 2, pos], buf_ref, sem_rows, s)

    @pl.when(step == 0)
    def _():
        idx_copy(0).start()
        idx_copy(0).wait()
        issue_rows(0)

        @pl.when(n_steps > 1)
        def _():
            idx_copy(1).start()

    @pl.when(step + 1 < n_steps)
    def _():
        idx_copy(step + 1).wait()
        issue_rows(step + 1)

        @pl.when(step + 2 < n_steps)
        def _():
            idx_copy(step + 2).start()

    return _wait_rows(src_ref, buf_ref, sem_rows, step)


def _gather_scratch(n_str, rows):
    return [pltpu.SMEM((2, n_str * rows), I32), pltpu.VMEM((2, n_str, rows * ROW_SUB, LANES), F32),
            pltpu.SemaphoreType.DMA((2,)), pltpu.SemaphoreType.DMA((2,))]


def _dispatch_kernel(pad_ref, nu_ref, f_ref, slots_ref, xb_ref, idx_ref, zero_ref, sem_idx, sem_rows):
    i = pl.program_id(0)
    t = f_ref.shape[0] // ROW_SUB
    blk = MOE_BLOCK * ROW_SUB
    idx_cp = pltpu.make_async_copy(slots_ref.at[i], idx_ref, sem_idx)
    idx_cp.start()

    @pl.when(i == 0)
    def _():
        zero_ref[...] = jnp.zeros_like(zero_ref)
        n_blocks = xb_ref.shape[0] // blk

        def zero_block(start):
            return pltpu.make_async_copy(zero_ref, xb_ref.at[pl.ds(pl.multiple_of(start * ROW_SUB, blk), blk)],
                                         sem_rows)

        def tail_start(j, carry):
            zero_block(j * MOE_BLOCK).start()
            return carry

        def tail_wait(j, carry):
            zero_block(0).wait()
            return carry

        for e in range(N_EXPERTS):
            zero_block(jnp.maximum(pad_ref[N_EXPERTS + e] - MOE_BLOCK, 0)).start()
        lax.fori_loop(nu_ref[0], n_blocks, tail_start, 0)
        for e in range(N_EXPERTS):
            zero_block(0).wait()
        lax.fori_loop(nu_ref[0], n_blocks, tail_wait, 0)

    idx_cp.wait()

    def issue(r, carry):
        _row_copy(f_ref, r, xb_ref, idx_ref[r], sem_rows).start(priority=0)
        _row_copy(f_ref, r, xb_ref, idx_ref[t + r], sem_rows).start(priority=1)
        return carry

    lax.fori_loop(0, t, issue, 0, unroll=ROW_DMA_UNROLL)
    for _ in range(2):
        pltpu.make_async_copy(f_ref, xb_ref.at[pl.ds(0, t * ROW_SUB)], sem_rows).wait()


def _dispatch(f, slots, pad, n_used, n_slots):
    t = TOKEN_TILE
    n_tok = f.shape[0] // ROW_SUB
    return pl.pallas_call(
        _dispatch_kernel,
        out_shape=jax.ShapeDtypeStruct((n_slots * ROW_SUB, LANES), F32),
        grid_spec=pltpu.PrefetchScalarGridSpec(
            num_scalar_prefetch=2,
            grid=(n_tok // t,),
            in_specs=[pl.BlockSpec((t * ROW_SUB, LANES), lambda i, p, nu: (i, 0)),
                      pl.BlockSpec(memory_space=pltpu.VMEM)],
            out_specs=pl.BlockSpec(memory_space=pl.ANY),
            scratch_shapes=[pltpu.SMEM((2 * t,), I32), pltpu.VMEM((MOE_BLOCK * ROW_SUB, LANES), F32),
                            pltpu.SemaphoreType.DMA, pltpu.SemaphoreType.DMA]),
        compiler_params=_params("arbitrary"),
        name="moe_dispatch",
    )(pad, n_used, f, slots)


def _experts_kernel(be_ref, nu_ref, xb_ref, w1_ref, w3_ref, w2_ref, y_ref, w1b_ref, w3b_ref, w2b_ref,
                    ring_ref, ring_sem):
    i = pl.program_id(0)
    n_used = nu_ref[0]
    rb = ring_ref.shape[1]

    def block_copy(s):
        slot = s % 3
        return pltpu.make_async_copy(xb_ref.at[pl.ds(pl.multiple_of(s * rb, rb), rb)],
                                     ring_ref.at[slot], ring_sem.at[slot])

    @pl.when(i == 0)
    def _():
        block_copy(0).start()

        @pl.when(n_used > 1)
        def _():
            block_copy(1).start()

    @pl.when(i + 2 < n_used)
    def _():
        block_copy(i + 2).start()

    @pl.when(jnp.logical_or(i == 0, be_ref[i] != be_ref[jnp.maximum(i - 1, 0)]))
    def _():
        w1b_ref[...] = w1_ref[...].astype(BF16)
        w3b_ref[...] = w3_ref[...].astype(BF16)
        w2b_ref[...] = w2_ref[...].astype(BF16)

    @pl.when(i < n_used)
    def _():
        block_copy(i).wait()
        x = _load_row_tiles(ring_ref.at[i % 3]).astype(BF16)
        h1 = jnp.dot(x, w1b_ref[...], preferred_element_type=F32)
        h3 = jnp.dot(x, w3b_ref[...], preferred_element_type=F32)
        a = (h1 * jax.nn.sigmoid(h1) * h3).astype(BF16)
        _store_row_tiles(y_ref, jnp.dot(a, w2b_ref[...], preferred_element_type=F32))

    @pl.when(i >= n_used)
    def _():
        y_ref[...] = jnp.zeros_like(y_ref)


def _experts(xb, w1, w3, w2, layer, block_e, n_used):
    n_blocks = block_e.shape[0]
    _, _, d, hid = w1.shape
    rb = MOE_BLOCK * ROW_SUB
    wspec = lambda s: pl.BlockSpec((None, None) + s, lambda i, be, nu: (layer, be[i], 0, 0))
    return pl.pallas_call(
        _experts_kernel,
        out_shape=jax.ShapeDtypeStruct((n_blocks * rb, LANES), F32),
        grid_spec=pltpu.PrefetchScalarGridSpec(
            num_scalar_prefetch=2,
            grid=(n_blocks,),
            in_specs=[pl.BlockSpec(memory_space=pl.ANY),
                      wspec((d, hid)), wspec((d, hid)), wspec((hid, d))],
            out_specs=pl.BlockSpec((rb, LANES), lambda i, be, nu: (i, 0)),
            scratch_shapes=[pltpu.VMEM((d, hid), BF16), pltpu.VMEM((d, hid), BF16), pltpu.VMEM((hid, d), BF16),
                            pltpu.VMEM((3, rb, LANES), F32), pltpu.SemaphoreType.DMA((3,))]),
        compiler_params=_params("arbitrary"),
        name="moe_experts",
    )(block_e, n_used, xb, w1, w3, w2)


def _moe_plan(route_i, counts, n_tok):
    eid, rank = route_i[0:2], route_i[2:4]
    cnt = counts[0, :N_EXPERTS].astype(I32)
    padded = (cnt + MOE_BLOCK - 1) // MOE_BLOCK * MOE_BLOCK
    pend = jnp.cumsum(padded)
    pstart = pend - padded
    experts = jnp.arange(N_EXPERTS, dtype=I32)
    slots = rank + jnp.sum(jnp.where(eid[:, :, None] == experts, pstart, 0), axis=-1)
    n_assign = 2 * n_tok
    n_slots = (n_assign + MOE_BLOCK - 1) // MOE_BLOCK * MOE_BLOCK + N_EXPERTS * MOE_BLOCK
    n_blocks = n_slots // MOE_BLOCK
    bstart = jnp.arange(n_blocks, dtype=I32) * MOE_BLOCK
    block_e = jnp.minimum(jnp.sum(bstart[:, None] >= pend[None, :], axis=1), N_EXPERTS - 1).astype(I32)
    n_used = pend[-1] // MOE_BLOCK
    last_e = block_e[jnp.maximum(n_used - 1, 0)]
    block_e = jnp.where(jnp.arange(n_blocks) < n_used, block_e, last_e).astype(I32)
    slots = slots.reshape(2, n_tok // TOKEN_TILE, TOKEN_TILE).transpose(1, 0, 2)
    slots = slots.reshape(n_tok // TOKEN_TILE, 2 * TOKEN_TILE).astype(I32)
    pad = jnp.concatenate([pstart + cnt, pend]).astype(I32)
    return slots, block_e, n_used.reshape(1).astype(I32), pad, n_slots


def _moe(f, route_i, counts, w1, w3, w2, layer):
    n_tok = f.shape[0] // ROW_SUB
    slots, block_e, n_used, pad, n_slots = _moe_plan(route_i, counts, n_tok)
    xb = _dispatch(f, slots, pad, n_used, n_slots)
    y = _experts(xb, w1, w3, w2, layer, block_e, n_used)
    return y, slots


def _combined_rows(y_ref, slots_ref, rw_ref, idx_ref, ybuf_ref, sem_idx, sem_rows):
    slot = _gather_pipeline(y_ref, slots_ref, idx_ref, ybuf_ref, sem_idx, sem_rows,
                            pl.program_id(0), pl.num_programs(0))
    rw = rw_ref[...]
    w1, w2 = rw[:, 0:1], rw[:, 1:2]
    t = ybuf_ref.shape[2] // ROW_SUB
    part = lambda k, s: ybuf_ref[slot, k, pl.ds(s, t, stride=ROW_SUB), :]
    return jnp.concatenate([w1 * part(0, s) + w2 * part(1, s) for s in range(ROW_SUB)], axis=1)


def _l0_combine_kernel(x_ref, rw_ref, slots_ref, y_ref, mod0_ref, mod1_ref, g_ref, win_ref,
                       x2_ref, u_ref, idx_ref, ybuf_ref, sem_idx, sem_rows):
    moe = _combined_rows(y_ref, slots_ref, rw_ref, idx_ref, ybuf_ref, sem_idx, sem_rows)
    x2 = x_ref[...] + mod0_ref[0, 0][5:6] * moe
    x2_ref[...] = x2
    m1 = mod1_ref[0, 0]
    h = _modulate(x2, g_ref[...], m1[0:1], m1[1:2]).astype(BF16)
    u_ref[...] = jnp.dot(h, win_ref[...], preferred_element_type=F32)


def _l0_combine(x1, rw, slots, y, mod0, mod1, g, win, b, nct):
    n_tok, d = x1.shape
    t = TOKEN_TILE
    nt = n_tok // b // t
    l = n_tok // b
    modspec = pl.BlockSpec((1, 1, 6, d), lambda i: (i // nt, jnp.where(i % nt >= nct, 1, 0), 0, 0))
    return pl.pallas_call(
        _l0_combine_kernel,
        out_shape=[jax.ShapeDtypeStruct((n_tok, d), F32), jax.ShapeDtypeStruct((l, b * S5_WIDTH), F32)],
        grid=(n_tok // t,),
        in_specs=[pl.BlockSpec((t, d), lambda i: (i, 0)), pl.BlockSpec((t, 8), lambda i: (i, 0)),
                  pl.BlockSpec(memory_space=pltpu.VMEM), pl.BlockSpec(memory_space=pl.ANY),
                  modspec, modspec, _full((1, d)), _full((d, S5_WIDTH))],
        out_specs=[pl.BlockSpec((t, d), lambda i: (i, 0)),
                   pl.BlockSpec((t, S5_WIDTH), lambda i: (i % nt, i // nt))],
        scratch_shapes=_gather_scratch(2, t),
        compiler_params=_params("arbitrary"),
        name="l0_combine",
    )(x1, rw, slots, y, mod0, mod1, g, win)


def _s5_zoh_kernel(are_ref, aim_ref, ldt_ref, bre_ref, bim_ref, lre_ref, lim_ref, bbre_ref, bbim_ref):
    ar, ai = are_ref[...], aim_ref[...]
    dt = jnp.exp(ldt_ref[...])
    mag = jnp.exp(ar * dt)
    lr = mag * jnp.cos(ai * dt)
    li = mag * jnp.sin(ai * dt)
    lre_ref[...] = lr
    lim_ref[...] = li
    nr, ni = lr - 1.0, li
    den = ar * ar + ai * ai
    cr = (nr * ar + ni * ai) / den
    ci = (ni * ar - nr * ai) / den
    br, bi = bre_ref[...], bim_ref[...]
    bbre_ref[...] = cr * br - ci * bi
    bbim_ref[...] = cr * bi + ci * br


def _s5_zoh(a_re, a_im, log_dt, b_re, b_im):
    n_dir, g, n, p = b_re.shape
    rows = n_dir * g * p
    expand = lambda a: jnp.broadcast_to(a[:, :, None, :], (n_dir, g, p, n)).reshape(rows, n)
    ldt = jnp.broadcast_to(log_dt[:, :, None, None], (n_dir, g, p, n)).reshape(rows, n)
    bt = lambda a: jnp.transpose(a, (0, 1, 3, 2)).reshape(rows, n)
    outs = pl.pallas_call(
        _s5_zoh_kernel,
        out_shape=[jax.ShapeDtypeStruct((rows, n), F32)] * 4,
        name="s5_zoh",
    )(expand(a_re), expand(a_im), ldt, bt(b_re), bt(b_im))
    lre, lim, bbre, bbim = (o.reshape(n_dir, g, p, n) for o in outs)
    return lre[:, :, 0, :], lim[:, :, 0, :], bbre, bbim


def _s5_matrices(lre, lim, bbre, bbim, c_re, c_im):
    n_dir = lre.shape[0]
    nj = S5_GROUPS // (2 * S5_TILES)
    shp = (n_dir, nj, S5_TILES, 2, S5_GROUP_DIM, S5_STATE)
    kdim = S5_TILES * 2 * S5_GROUP_DIM
    sdim = S5_TILES * 2 * 2 * S5_STATE
    lane_h = (np.arange(LANES) // S5_STATE).reshape(1, 1, 1, 1, 1, LANES)
    own_group = jnp.asarray(lane_h == np.arange(2).reshape(1, 1, 1, 2, 1, 1), F32)
    own_tile = jnp.eye(S5_TILES, dtype=F32).reshape(1, 1, S5_TILES, 1, 1, S5_TILES, 1, 1)

    def place(re, im):
        parts = [jnp.concatenate([a.reshape(shp)] * 2, axis=-1) * own_group for a in (re, im)]
        a = jnp.stack(parts, axis=5)[:, :, :, :, :, None] * own_tile
        return a.reshape(n_dir, nj, kdim, sdim)

    bm = place(bbre, bbim).astype(BF16)
    cm = jnp.swapaxes(place(c_re, -c_im), 2, 3).astype(BF16)
    lam = jnp.stack([lre.reshape(n_dir, nj, S5_TILES * LANES), lim.reshape(n_dir, nj, S5_TILES * LANES)], axis=2)
    return bm, cm, lam


def _s5_scan_kernel(uf_ref, ub_ref, bm_ref, cm_ref, lam_ref, yf_ref, yb_ref, v_ref, h_ref, acc_ref, uk_ref,
                    *, tc, nb):
    @pl.when(pl.program_id(0) == 0)
    def _():
        h_ref[...] = jnp.zeros_like(h_ref)

    nj, kdim, sdim = bm_ref.shape[1], bm_ref.shape[2], bm_ref.shape[3]
    u_refs, y_refs = (uf_ref, ub_ref), (yf_ref, yb_ref)
    phases = [(d, jq) for d in range(2) for jq in range(nj)]
    n_sub = 2 * S5_TILES
    col = sdim // S5_TILES
    run = tc // n_sub
    state = {}

    def drive_piece(p, c):
        d, jq = phases[p]
        if c == 0:
            for bb in range(nb):
                lo = bb * S5_WIDTH + kdim * jq
                uk_ref[pl.ds(bb, tc, stride=nb), :] = u_refs[d][:, lo:lo + kdim]
            state["uk"] = uk_ref[...].astype(BF16)
        v_ref[p % 3, :, col * c:col * (c + 1)] = jnp.dot(state["uk"], bm_ref[d, jq, :, col * c:col * (c + 1)],
                                                         preferred_element_type=F32)

    def readout_piece(p, c):
        d, jq = phases[p]
        part = jnp.dot(v_ref[p % 3, :, col * c:col * (c + 1)].astype(BF16), cm_ref[d, jq, col * c:col * (c + 1), :],
                       preferred_element_type=F32)
        if c == 0:
            acc_ref[...] = part
        elif c < S5_TILES - 1:
            acc_ref[...] += part
        else:
            acc_ref[...] += part
            for bb in range(nb):
                lo = bb * S5_WIDTH + kdim * jq
                y_refs[d][:, lo:lo + kdim] = acc_ref[pl.ds(bb, tc, stride=nb), :]

    def scan_run(p, c):
        d, jq = phases[p]
        if c == 0:
            lam = lam_ref[d, jq]
            state["lr"] = [jnp.broadcast_to(lam[0:1, LANES * t:LANES * (t + 1)], (nb, LANES)) for t in range(S5_TILES)]
            state["li"] = [jnp.broadcast_to(lam[1:2, LANES * t:LANES * (t + 1)], (nb, LANES)) for t in range(S5_TILES)]
            h0 = h_ref[d, jq]
            state["h"] = [h0[:, LANES * i:LANES * (i + 1)] for i in range(2 * S5_TILES)]
        lr, li, h = state["lr"], state["li"], state["h"]
        for s in range(run * c, run * (c + 1)):
            r0 = (s if d == 0 else tc - 1 - s) * nb
            rows = v_ref[p % 3, r0:r0 + nb, :]
            new = []
            for t in range(S5_TILES):
                vr = rows[:, 2 * LANES * t:2 * LANES * t + LANES]
                vi = rows[:, 2 * LANES * t + LANES:2 * LANES * (t + 1)]
                hr, hi = h[2 * t], h[2 * t + 1]
                new.append(lr[t] * hr - li[t] * hi + vr)
                new.append(lr[t] * hi + li[t] * hr + vi)
            v_ref[p % 3, r0:r0 + nb, :] = jnp.concatenate(new, axis=1)
            h = new
        state["h"] = h
        if c == n_sub - 1:
            h_ref[d, jq] = jnp.concatenate(h, axis=1)

    for slot in range(len(phases) + 2):
        for c in range(n_sub):
            if 0 <= slot - 1 < len(phases):
                scan_run(slot - 1, c)
            if c < S5_TILES and slot < len(phases):
                drive_piece(slot, c)
            if c >= S5_TILES and 0 <= slot - 2 < len(phases):
                readout_piece(slot - 2, c - S5_TILES)


def _s5_scan(u_tm, bm, cm, lam, nb, n_ctx):
    l, w = u_tm.shape
    tc = S5_CHUNK
    r = tc * nb
    n_chunks = l // tc
    ncc = n_ctx // tc
    bwd = lambda i: (jnp.where(i < ncc, ncc - 1 - i, n_chunks - 1 - (i - ncc)), 0)
    nj = bm.shape[1]
    sdim = bm.shape[3]
    kern = functools.partial(_s5_scan_kernel, tc=tc, nb=nb)
    return pl.pallas_call(
        kern,
        out_shape=[jax.ShapeDtypeStruct((l, w), F32)] * 2,
        grid=(n_chunks,),
        in_specs=[pl.BlockSpec((tc, w), lambda i: (i, 0)), pl.BlockSpec((tc, w), bwd),
                  _full(bm.shape), _full(cm.shape), _full(lam.shape)],
        out_specs=[pl.BlockSpec((tc, w), lambda i: (i, 0)), pl.BlockSpec((tc, w), bwd)],
        scratch_shapes=[pltpu.VMEM((3, r, sdim), F32), pltpu.VMEM((2, nj, nb, sdim), F32),
                        pltpu.VMEM((r, bm.shape[2]), F32), pltpu.VMEM((r, bm.shape[2]), F32)],
        compiler_params=_params("arbitrary"),
        name="s5_scan",
    )(u_tm, u_tm, bm, cm, lam)


def _l1_post_kernel(x_ref, u_ref, yf_ref, yb_ref, dsk_ref, wglu_ref, bglu_ref, wout_ref, mod_ref, gf_ref,
                    wr_ref, br_ref, x3_ref, f_ref, ri_ref, rw_ref, cnt_ref, base_ref):
    def prog(j, shared):
        cols = slice(S5_WIDTH * j, S5_WIDTH * (j + 1))
        y = dsk_ref[...] * u_ref[:, cols] + yf_ref[:, cols].astype(F32) + yb_ref[:, cols].astype(F32)
        g = jax.nn.gelu(y, approximate=True)
        yield
        z = jnp.dot(g.astype(BF16), wglu_ref[...], preferred_element_type=F32) + bglu_ref[...]
        yield
        a = (g * jax.nn.sigmoid(z)).astype(BF16)
        yield
        o = jnp.dot(a, wout_ref[...], preferred_element_type=F32)
        yield
        yield from _route_tail(j, x_ref[j], o, mod_ref[j, 0], gf_ref[...], wr_ref, br_ref, shared,
                               x3_ref, f_ref, ri_ref, rw_ref)

    _run_pair(prog, base_ref, cnt_ref)


def _l1_post(x2, u_tm, yf, yb, dsk, wglu, bglu, wout, mod, gf, wr, br, n_ctx):
    b, l, d = x2.shape
    t = TOKEN_TILE
    nct = n_ctx // t
    s = l - n_ctx
    tm = pl.BlockSpec((t, PAIR * S5_WIDTH), lambda bp, ti: (ti + nct, bp))
    shapes, specs = _tail_out_shapes(b, s, d)
    outs = pl.pallas_call(
        _l1_post_kernel,
        out_shape=shapes,
        grid=(b // PAIR, s // t),
        in_specs=[pl.BlockSpec((PAIR, t, d), lambda bp, ti: (bp, ti + nct, 0)), tm, tm, tm,
                  _full((1, S5_WIDTH)), _full((S5_WIDTH, S5_WIDTH)), _full((1, S5_WIDTH)), _full((S5_WIDTH, d)),
                  pl.BlockSpec((PAIR, 1, 6, d), lambda bp, ti: (bp, 1, 0, 0)),
                  _full((1, d)), _full((2, d, LANES)), _full((1, LANES))],
        out_specs=specs,
        scratch_shapes=[pltpu.VMEM((8, LANES), F32)],
        compiler_params=_params("arbitrary", "arbitrary"),
        name="l1_post",
    )(x2, u_tm, yf, yb, dsk, wglu, bglu, wout, mod, gf, wr, br)
    return _tail_flatten(*outs[:4]) + (outs[4],)


def _final_kernel(x_ref, rw_ref, slots_ref, y_ref, mod_ref, g_ref, o_ref, idx_ref, ybuf_ref, sem_idx, sem_rows):
    moe = _combined_rows(y_ref, slots_ref, rw_ref, idx_ref, ybuf_ref, sem_idx, sem_rows)
    x4 = x_ref[...] + mod_ref[0, 0][5:6] * moe
    o_ref[...] = _rms(x4) * g_ref[...]


def _final(x3, rw, slots, y, mod, g, b):
    n_tok, d = x3.shape
    t = TOKEN_TILE
    nt = n_tok // b // t
    return pl.pallas_call(
        _final_kernel,
        out_shape=jax.ShapeDtypeStruct((n_tok, d), F32),
        grid=(n_tok // t,),
        in_specs=[pl.BlockSpec((t, d), lambda i: (i, 0)), pl.BlockSpec((t, 8), lambda i: (i, 0)),
                  pl.BlockSpec(memory_space=pltpu.VMEM), pl.BlockSpec(memory_space=pl.ANY),
                  pl.BlockSpec((1, 1, 6, d), lambda i: (i // nt, 1, 0, 0)), _full((1, d))],
        out_specs=pl.BlockSpec((t, d), lambda i: (i, 0)),
        scratch_shapes=_gather_scratch(2, t),
        compiler_params=_params("arbitrary"),
        name="final_combine",
    )(x3, rw, slots, y, mod, g)


def _rope_tables(n_ctx, n_lat):
    pos = jnp.arange(n_lat, dtype=I32)
    row = (pos // GRID_W).astype(F32)
    col = (pos % GRID_W).astype(F32)
    inv_freq = jnp.power(ROPE_THETA, -jnp.arange(0, AXIS_DIM, 2, dtype=F32) / AXIS_DIM)
    lane = np.arange(LANES)
    dd = lane % HEAD_DIM
    use_col = dd >= AXIS_DIM
    first = (dd % AXIS_DIM) < AXIS_DIM // 2
    freq = inv_freq[(dd % AXIS_DIM) % (AXIS_DIM // 2)]
    ang = jnp.where(use_col[None, :], col[:, None], row[:, None]) * freq[None, :]
    cos, sin = jnp.cos(ang), jnp.sin(ang)
    sa = jnp.where(first[None, :], -sin, 0.0)
    sb = jnp.where(first[None, :], 0.0, sin)
    pad = lambda a, v: jnp.concatenate([jnp.full((n_ctx, LANES), v, F32), a], axis=0)
    return pad(cos, 1.0), pad(sa, 0.0), pad(sb, 0.0)


def _group_ones(width, group):
    idx = np.arange(width) // group
    return jnp.asarray(idx[:, None] == idx[None, :], dtype=BF16)


def _router_weights(w_grp, b_grp, w_rt, b_rt):
    d = w_grp.shape[0]
    used = N_EXPERT_GROUPS + N_EXPERTS
    wr = jnp.concatenate([w_grp, w_rt, jnp.zeros((d, LANES - used), F32)], axis=1)
    br = jnp.concatenate([b_grp, b_rt, jnp.zeros((LANES - used,), F32)]).reshape(1, LANES)
    return jnp.stack(_split_bf16(wr)), br


def kernel(x, c, ctx, c_ctx, w_mod, b_mod, norm_mix_g, norm_ffn_g, mix_w_in, mix_w_out, q_norm_g, k_norm_g, gmlp_norm_g, gmlp_w_spatial, gmlp_b_spatial, s5_w_in, s5_a_re, s5_a_im, s5_log_dt, s5_b_re, s5_b_im, s5_c_re, s5_c_im, s5_d, s5_w_glu, s5_b_glu, s5_w_out, moe_w_group, moe_b_group, moe_w_router, moe_b_router, moe_w1, moe_w3, moe_w2, final_norm_g):
    b, s, d = x.shape
    n_ctx = ctx.shape[1]
    l = n_ctx + s
    assert w_mod.shape[0] == 2, "two layers: attention/gMLP then S5"
    assert n_ctx % TOKEN_TILE == 0 and s % TOKEN_TILE == 0 and b % 8 == 0 and d == D_MODEL
    nct = n_ctx // TOKEN_TILE

    r = (b + 1 + 7) // 8 * 8
    cond = jnp.concatenate([c, c_ctx[None, :], jnp.zeros((r - b - 1, d), F32)], axis=0)
    mods = _adaln(cond, w_mod, b_mod)
    lat = mods[:, :b].reshape(2, b, 1, 6, d)
    cxt = jnp.broadcast_to(mods[:, b].reshape(2, 1, 1, 6, d), (2, b, 1, 6, d))
    mod = jnp.concatenate([cxt, lat], axis=2)

    w_in = jnp.concatenate([mix_w_in[0][:, _Q_COL_PERM], mix_w_in[0][:, ATTN_WIDTH:]], axis=1).astype(BF16)
    w_out = jnp.concatenate([mix_w_out[0][_Q_COL_PERM], mix_w_out[0][ATTN_WIDTH:]], axis=0).astype(BF16)
    cos, sa, sb = _rope_tables(n_ctx, s)
    qg = jnp.tile(q_norm_g[0], N_Q_HEADS).reshape(1, ATTN_WIDTH)
    kg = jnp.tile(k_norm_g[0], N_KV_HEADS).reshape(1, KV_WIDTH)
    q, k, v, gu, gv = _l0_proj(ctx, x, mod[0], norm_mix_g[0].reshape(1, d), w_in, qg, kg,
                               gmlp_norm_g[0].reshape(1, GMLP_WIDTH), cos, sa, sb,
                               _group_ones(ATTN_WIDTH, HEAD_DIM), _group_ones(KV_WIDTH, HEAD_DIM), nct)
    o = _attention(q, k, v, n_ctx)
    bsp = jnp.repeat(gmlp_b_spatial[0].T, GMLP_GROUP_DIM, axis=1)
    wr0, br0 = _router_weights(moe_w_group[0], moe_b_group[0], moe_w_router[0], moe_b_router[0])
    x1, f0, ri0, rw0, cnt0 = _l0_post(ctx, x, o, gu, gv, gmlp_w_spatial[0].astype(BF16), bsp, w_out, mod[0],
                                      norm_ffn_g[0].reshape(1, d), wr0, br0, nct)
    y0, slots0 = _moe(f0, ri0, cnt0, moe_w1, moe_w3, moe_w2, 0)
    x2, u_tm = _l0_combine(x1.reshape(b * l, d), rw0, slots0, y0, mod[0], mod[1], norm_mix_g[1].reshape(1, d),
                           s5_w_in[0].astype(BF16), b, nct)

    lre, lim, bbre, bbim = _s5_zoh(s5_a_re[0], s5_a_im[0], s5_log_dt[0], s5_b_re[0], s5_b_im[0])
    bm, cm, lam = _s5_matrices(lre, lim, bbre, bbim, s5_c_re[0], s5_c_im[0])
    yf, yb = _s5_scan(u_tm, bm, cm, lam, b, n_ctx)
    wr1, br1 = _router_weights(moe_w_group[1], moe_b_group[1], moe_w_router[1], moe_b_router[1])
    x3, f1, ri1, rw1, cnt1 = _l1_post(x2.reshape(b, l, d), u_tm, yf, yb, s5_d[0].reshape(1, S5_WIDTH),
                                      s5_w_glu[0].astype(BF16), s5_b_glu[0].reshape(1, S5_WIDTH),
                                      s5_w_out[0].astype(BF16), mod[1], norm_ffn_g[1].reshape(1, d), wr1, br1, n_ctx)
    y1, slots1 = _moe(f1, ri1, cnt1, moe_w1, moe_w3, moe_w2, 1)
    out = _final(x3.reshape(b * s, d), rw1, slots1, y1, mod[1], final_norm_g.reshape(1, d), b)
    return out.reshape(b, s, d)
```
